```python
import math
import jax, jax.numpy as jnp
from jax import lax
import numpy as np

D_MODEL = 4096
BATCH = 2
SEQ = 4096
DEPTH = 1
DEC_BATCH = 128
DEC_SEQ = 1
PAST_LEN = 2048
PAGE_SIZE = 128

HEAD_DIM = 128
NSA_HEADS = 16
NSA_KV_HEADS = 4
NSA_GROUP = NSA_HEADS // NSA_KV_HEADS
CMP_LEN = 32
CMP_STRIDE = 16
CMP_HIDDEN = 256
SLC_LEN = 64
SLC_TOP_N = 16
SLC_LOCAL = 2
WINDOW = 512
WIN_BLOCK = 128
Q_BLOCK = 64
GDN_HEADS = 16
GDN_DK = 128
GDN_DV = 128
GDN_CONV = 4
GDN_CHUNK = 64
GDN_CONV_DIM = GDN_HEADS * (2 * GDN_DK + GDN_DV)
D_FF = 11008
FFN_CONV = 3
N_BRANCH = 2
ROPE_THETA = 10000.0
EPS = 1e-6
PROJ_SIZES = (NSA_HEADS * HEAD_DIM, 6 * NSA_KV_HEADS * HEAD_DIM, 3 * NSA_HEADS, GDN_CONV_DIM,
              GDN_HEADS * GDN_DV, 2 * GDN_HEADS, N_BRANCH * D_MODEL)
D_IN = sum(PROJ_SIZES)

kernel_name = 'nsa_gdn_convffn_hybrid_step'


def _rmsnorm(x, w):
    xf = x.astype(jnp.float32)
    y = xf * lax.rsqrt(jnp.mean(xf * xf, axis=-1, keepdims=True) + EPS)
    return (y * w.astype(jnp.float32)).astype(x.dtype)


def _l2norm(x):
    return x * lax.rsqrt(jnp.sum(x * x, axis=-1, keepdims=True) + EPS)


def _rope(x, pos):
    half = x.shape[-1] // 2
    inv = ROPE_THETA ** (-jnp.arange(half, dtype=jnp.float32) / half)
    ang = pos.astype(jnp.float32)[:, None] * inv[None, :]
    cos = jnp.cos(ang)[None, :, None, :]
    sin = jnp.sin(ang)[None, :, None, :]
    xf = x.astype(jnp.float32)
    x1, x2 = xf[..., :half], xf[..., half:]
    return jnp.concatenate([x1 * cos - x2 * sin, x2 * cos + x1 * sin], axis=-1).astype(x.dtype)


def _masked_softmax(s, mask):
    s = jnp.where(mask, s, -jnp.inf)
    m = jnp.max(s, axis=-1, keepdims=True)
    m = jnp.where(jnp.isfinite(m), m, 0.0)
    e = jnp.exp(s - m)
    return e / jnp.maximum(jnp.sum(e, axis=-1, keepdims=True), 1e-30)


def _causal_dwconv(x_full, w):
    K = w.shape[0]
    T = x_full.shape[1] - K + 1
    out = x_full[:, K - 1:] * w[K - 1]
    for j in range(K - 1):
        out = out + x_full[:, j:j + T] * w[j]
    return out


def _compress(kv, pe, w1, w2):
    B, L, G, D = kv.shape
    f32 = jnp.float32
    r = CMP_LEN // CMP_STRIDE
    n_chunk = L // CMP_STRIDE
    n_cmp = n_chunk - r + 1
    chunks = kv[:, :n_chunk * CMP_STRIDE].astype(f32).reshape(B, n_chunk, CMP_STRIDE, G, D)
    w1f = w1.astype(f32)
    part = jnp.einsum('bnsgd,rsdf->bnrgf', chunks, w1f.reshape(r, CMP_STRIDE, D, CMP_HIDDEN))
    hid = jnp.einsum('ld,ldf->f', pe.astype(f32), w1f)
    for i in range(r):
        hid = hid + part[:, i:i + n_cmp, i]
    return jax.nn.silu(hid) @ w2.astype(f32)


def _overlap_matrix(n_cmp, n_slc):
    start = np.arange(n_cmp)[:, None] * CMP_STRIDE
    blk = np.arange(n_slc)[None, :] * SLC_LEN
    return ((start < blk + SLC_LEN) & (start + CMP_LEN > blk)).astype(np.float32)


def _sweep_queries(fn, q, qpos):
    B, T = q.shape[:2]
    if T <= Q_BLOCK or T % Q_BLOCK:
        return fn(q, qpos)
    nb = T // Q_BLOCK
    qs = jnp.moveaxis(q.reshape(B, nb, Q_BLOCK, *q.shape[2:]), 1, 0)
    ps = jnp.moveaxis(qpos.reshape(B, nb, Q_BLOCK), 1, 0)
    outs = lax.map(lambda a: fn(a[0], a[1]), (qs, ps))
    return tuple(jnp.moveaxis(o, 0, 1).reshape(B, T, *o.shape[3:]) for o in outs)


def _nsa_global(q, qpos, kv4, cmp_pe, cmp_w1, cmp_w2):
    B, L = kv4.shape[:2]
    G = NSA_KV_HEADS
    f32 = jnp.float32
    kc = _compress(kv4[:, :, 0], cmp_pe[0], cmp_w1[0], cmp_w2[0])
    vc = _compress(kv4[:, :, 1], cmp_pe[1], cmp_w1[1], cmp_w2[1])
    n_cmp = kc.shape[1]
    cmp_end = jnp.arange(n_cmp) * CMP_STRIDE + CMP_LEN - 1
    n_slc = -(-L // SLC_LEN)
    pad = n_slc * SLC_LEN - L

    def blocks(a):
        a = jnp.pad(a.astype(f32), ((0, 0), (0, pad), (0, 0), (0, 0)))
        return a.reshape(B, n_slc, SLC_LEN, G, HEAD_DIM).transpose(0, 3, 1, 2, 4)

    ks, vs = blocks(kv4[:, :, 2]), blocks(kv4[:, :, 3])
    overlap = jnp.asarray(_overlap_matrix(n_cmp, n_slc))
    top_n = min(SLC_TOP_N, n_slc)
    blk = jnp.arange(n_slc)
    b_idx = jnp.arange(B)[:, None, None, None]
    g_idx = jnp.arange(G)[None, None, :, None]
    scale = HEAD_DIM ** -0.5

    def attend(qb, pb):
        tq = qb.shape[1]
        qg = qb.astype(f32).reshape(B, tq, G, NSA_GROUP, HEAD_DIM) * scale
        s = jnp.einsum('btghd,bngd->btghn', qg, kc)
        p = _masked_softmax(s, (cmp_end[None, None, :] <= pb[:, :, None])[:, :, None, None, :])
        o_c = jnp.einsum('btghn,bngd->btghd', p, vc)
        imp = jnp.einsum('btghn,nj->btgj', p, overlap)
        pq = pb[:, :, None, None]
        cur = pq // SLC_LEN
        forced = (blk == 0) | ((blk <= cur) & (blk > cur - SLC_LOCAL))
        score = jnp.where(blk * SLC_LEN > pq, -jnp.inf, jnp.where(forced, jnp.inf, imp))
        _, idx = lax.top_k(score, top_n)
        kg = ks[b_idx, g_idx, idx]
        vg = vs[b_idx, g_idx, idx]
        kpos = idx[..., None] * SLC_LEN + jnp.arange(SLC_LEN)
        valid = (kpos <= pb[:, :, None, None, None]).reshape(B, tq, G, 1, top_n * SLC_LEN)
        s2 = jnp.einsum('btghd,btgmd->btghm', qg, kg.reshape(B, tq, G, top_n * SLC_LEN, HEAD_DIM))
        p2 = _masked_softmax(s2, valid)
        o_s = jnp.einsum('btghm,btgmd->btghd', p2, vg.reshape(B, tq, G, top_n * SLC_LEN, HEAD_DIM))
        return (o_c.reshape(B, tq, NSA_HEADS, HEAD_DIM), o_s.reshape(B, tq, NSA_HEADS, HEAD_DIM))

    return _sweep_queries(attend, q, qpos)


def _window_attention(q, pos0, kv_win, n_past):
    B, T = q.shape[:2]
    G = NSA_KV_HEADS
    f32 = jnp.float32
    blk = WIN_BLOCK if T % WIN_BLOCK == 0 else T
    nc = T // blk
    kvp = jnp.pad(kv_win, ((0, 0), (WINDOW - n_past, 0), (0, 0), (0, 0), (0, 0)))
    idx = np.arange(nc)[:, None] * blk + np.arange(WINDOW + blk)[None, :]
    kpos = pos0 - WINDOW + idx
    qpos = pos0 + np.arange(T).reshape(nc, blk)
    diff = qpos[:, :, None] - kpos[:, None, :]
    mask = (diff >= 0) & (diff < WINDOW) & (kpos[:, None, :] >= 0)
    kb = kvp[:, idx].astype(f32)
    qb = q.astype(f32).reshape(B, nc, blk, G, NSA_GROUP, HEAD_DIM) * HEAD_DIM ** -0.5
    s = jnp.einsum('bcqghd,bckgd->bcghqk', qb, kb[:, :, :, 0])
    p = _masked_softmax(s, jnp.asarray(mask)[None, :, None, None])
    o = jnp.einsum('bcghqk,bckgd->bcqghd', p, kb[:, :, :, 1])
    return o.reshape(B, T, NSA_HEADS, HEAD_DIM)


def _chunk_gated_delta_rule(q, k, v, beta, g, s0):
    B, T, H, _ = q.shape
    dv = v.shape[-1]
    C = min(GDN_CHUNK, T)
    N = -(-T // C)
    pad = N * C - T

    def chunks(a):
        a = jnp.pad(a, ((0, 0), (0, pad)) + ((0, 0),) * (a.ndim - 2))
        return jnp.moveaxis(a.reshape(B, N, C, *a.shape[2:]), 3, 1)

    q, k, v, beta, g = chunks(q), chunks(k), chunks(v), chunks(beta), chunks(g)
    gc = jnp.cumsum(g, axis=-1)
    causal = jnp.tril(jnp.ones((C, C), dtype=bool))
    strict = jnp.tril(jnp.ones((C, C), dtype=bool), -1)
    decay = jnp.exp(jnp.where(causal, gc[..., :, None] - gc[..., None, :], -jnp.inf))
    kb = k * beta[..., None]
    lmat = jnp.where(strict, jnp.einsum('bhncd,bhnsd->bhncs', kb, k) * decay, 0.0)
    eye = jnp.eye(C, dtype=jnp.float32)
    tmat = lax.linalg.triangular_solve(eye + lmat, jnp.broadcast_to(eye, lmat.shape), left_side=True, lower=True)
    w_v = tmat @ (v * beta[..., None])
    w_k = tmat @ (kb * jnp.exp(gc)[..., None])
    a_qk = jnp.einsum('bhncd,bhnsd->bhncs', q, k) * decay
    q_dec = q * jnp.exp(gc)[..., None]
    k_dec = k * jnp.exp(gc[..., -1:] - gc)[..., None]
    g_last = jnp.exp(gc[..., -1])

    def step(S, xs):
        wv, wk, aqk, qd, kd, gl = xs
        v_new = wv - wk @ S
        o = qd @ S + aqk @ v_new
        S = S * gl[..., None, None] + jnp.swapaxes(kd, -1, -2) @ v_new
        return S, o

    xs = tuple(jnp.moveaxis(a, 2, 0) for a in (w_v, w_k, a_qk, q_dec, k_dec, g_last))
    S, o = lax.scan(step, s0, xs)
    o = jnp.moveaxis(o, 0, 2).transpose(0, 2, 3, 1, 4).reshape(B, N * C, H, dv)[:, :T]
    return o, S


def _gated_deltanet(qkv, z, ba, conv_buf, s0, conv_w, a_log, dt_bias, norm_w):
    B, T = qkv.shape[:2]
    f32 = jnp.float32
    full = jnp.concatenate([conv_buf, qkv], axis=1)
    h = jax.nn.silu(_causal_dwconv(full, conv_w)).astype(f32)
    qg, kg, vg = jnp.split(h, [GDN_HEADS * GDN_DK, 2 * GDN_HEADS * GDN_DK], axis=-1)
    q = _l2norm(qg.reshape(B, T, GDN_HEADS, GDN_DK)) * GDN_DK ** -0.5
    k = _l2norm(kg.reshape(B, T, GDN_HEADS, GDN_DK))
    v = vg.reshape(B, T, GDN_HEADS, GDN_DV)
    b, a = jnp.split(ba.astype(f32), 2, axis=-1)
    beta = jax.nn.sigmoid(b)
    g = -jnp.exp(a_log.astype(f32)) * jax.nn.softplus(a + dt_bias.astype(f32))
    o, s_new = _chunk_gated_delta_rule(q, k, v, beta, g, s0.astype(f32))
    o = _rmsnorm(o, norm_w) * jax.nn.silu(z.astype(f32).reshape(B, T, GDN_HEADS, GDN_DV))
    return o.reshape(B, T, GDN_HEADS * GDN_DV).astype(qkv.dtype), s_new.astype(s0.dtype), full[:, -(GDN_CONV - 1):]


def _block(x, pos0, nsa_past, win_past, gdn_s0, gdn_buf, ffn_buf,
           norm_mix, w_in, cmp_pe, cmp_w1, cmp_w2, gdn_conv_w, gdn_a_log, gdn_dt_bias, gdn_norm,
           w_nsa_out, w_gdn_out, w_o, norm_ffn, w_up, ffn_conv_w, w_down, norm_final):
    B, T, _ = x.shape
    G = NSA_KV_HEADS
    f32 = jnp.float32
    pos = pos0 + jnp.arange(T, dtype=jnp.int32)
    u = _rmsnorm(x, norm_mix) @ w_in
    offsets = np.cumsum(PROJ_SIZES)[:-1].tolist()
    q, kv, ng, qkv, z, ba, mg = jnp.split(u, offsets, axis=-1)
    q = _rope(q.reshape(B, T, NSA_HEADS, HEAD_DIM), pos)
    kv = kv.reshape(B, T, 3, 2, G, HEAD_DIM)
    k = _rope(kv[:, :, :, 0].reshape(B, T, 3 * G, HEAD_DIM), pos).reshape(B, T, 3, G, HEAD_DIM)
    kv = jnp.stack([k, kv[:, :, :, 1]], axis=3).reshape(B, T, 6, G, HEAD_DIM)
    nsa_rows = kv[:, :, :4]
    nsa_full = jnp.concatenate([nsa_past, nsa_rows], axis=1)
    qpos = jnp.broadcast_to(pos[None], (B, T))
    o_cmp, o_slc = _nsa_global(q, qpos, nsa_full, cmp_pe, cmp_w1, cmp_w2)
    n_past = win_past.shape[1]
    win_all = jnp.concatenate([win_past, kv[:, :, 4:]], axis=1)
    o_win = _window_attention(q, pos0, win_all, n_past)
    win_new = win_all[:, -min(WINDOW, n_past + T):]
    gates = jax.nn.sigmoid(ng.astype(f32)).reshape(B, T, 3, NSA_HEADS, 1)
    o_nsa = gates[:, :, 0] * o_cmp + gates[:, :, 1] * o_slc + gates[:, :, 2] * o_win
    u_nsa = o_nsa.reshape(B, T, NSA_HEADS * HEAD_DIM).astype(x.dtype) @ w_nsa_out
    o_gdn, s_new, gdn_buf_new = _gated_deltanet(qkv, z, ba, gdn_buf, gdn_s0, gdn_conv_w, gdn_a_log, gdn_dt_bias, gdn_norm)
    u_gdn = o_gdn @ w_gdn_out
    mg = jax.nn.sigmoid(mg.astype(f32)).reshape(B, T, N_BRANCH, D_MODEL)
    mix = (mg[:, :, 0] * u_nsa + mg[:, :, 1] * u_gdn).astype(x.dtype) @ w_o
    h = x + mix
    up = _rmsnorm(h, norm_ffn) @ w_up
    up_full = jnp.concatenate([ffn_buf, up], axis=1)
    gate, val = jnp.split(_causal_dwconv(up_full, ffn_conv_w), 2, axis=-1)
    y = _rmsnorm(h + (jax.nn.silu(gate) * val) @ w_down, norm_final)
    return y, nsa_rows, win_new, s_new, gdn_buf_new, up_full[:, -(FFN_CONV - 1):]


def setup_inputs(seed: int = 0) -> dict:
    key = jax.random.key(seed)
    ks = jax.random.split(key, 28)
    f32 = jnp.float32

    def nrm(k, shape, scale=1.0):
        return jax.random.normal(k, shape, f32) * scale

    G, hd = NSA_KV_HEADS, HEAD_DIM
    n_pages = PAST_LEN // PAGE_SIZE
    n_used = DEC_BATCH * n_pages
    n_pool = n_used + (n_used + 3) // 4
    page_table = jax.random.permutation(ks[3], n_pool)[:n_used].reshape(DEC_BATCH, n_pages).astype(jnp.int32)
    win_buf = min(WINDOW, PAST_LEN)
    dt = jnp.exp(jax.random.uniform(ks[12], (GDN_HEADS,), f32, math.log(1e-3), math.log(1e-1)))
    return {
        'x_prompt': nrm(ks[0], (BATCH, SEQ, D_MODEL)),
        'x_sample': nrm(ks[1], (DEC_BATCH, DEC_SEQ, D_MODEL)),
        'cache_nsa_kv': nrm(ks[2], (n_pool, PAGE_SIZE, 4, G, hd)),
        'page_table': page_table,
        'state_win_kv': nrm(ks[4], (DEC_BATCH, win_buf, 2, G, hd)),
        'state_gdn': nrm(ks[5], (DEC_BATCH, GDN_HEADS, GDN_DK, GDN_DV), 0.1),
        'state_gdn_conv': nrm(ks[6], (DEC_BATCH, GDN_CONV - 1, GDN_CONV_DIM)),
        'state_ffn_conv': nrm(ks[7], (DEC_BATCH, FFN_CONV - 1, 2 * D_FF)),
        'norm_mix': 1.0 + nrm(ks[8], (D_MODEL,), 0.02),
        'w_in': nrm(ks[9], (D_MODEL, D_IN), D_MODEL ** -0.5),
        'cmp_pe': nrm(ks[10], (2, CMP_LEN, hd), 0.02),
        'cmp_w1': nrm(ks[11], (2, CMP_LEN, hd, CMP_HIDDEN), (CMP_LEN * hd) ** -0.5),
        'cmp_w2': nrm(ks[13], (2, CMP_HIDDEN, hd), CMP_HIDDEN ** -0.5),
        'gdn_conv_w': nrm(ks[14], (GDN_CONV, GDN_CONV_DIM), GDN_CONV ** -0.5),
        'gdn_a_log': jnp.log(jax.random.uniform(ks[15], (GDN_HEADS,), f32, 1.0, 16.0)),
        'gdn_dt_bias': dt + jnp.log(-jnp.expm1(-dt)),
        'gdn_norm': 1.0 + nrm(ks[16], (GDN_DV,), 0.02),
        'w_nsa_out': nrm(ks[17], (NSA_HEADS * hd, D_MODEL), (NSA_HEADS * hd) ** -0.5),
        'w_gdn_out': nrm(ks[18], (GDN_HEADS * GDN_DV, D_MODEL), (GDN_HEADS * GDN_DV) ** -0.5),
        'w_o': nrm(ks[19], (D_MODEL, D_MODEL), D_MODEL ** -0.5),
        'norm_ffn': 1.0 + nrm(ks[20], (D_MODEL,), 0.02),
        'w_up': nrm(ks[21], (D_MODEL, 2 * D_FF), D_MODEL ** -0.5),
        'ffn_conv_w': nrm(ks[22], (FFN_CONV, 2 * D_FF), FFN_CONV ** -0.5),
        'w_down': nrm(ks[23], (D_FF, D_MODEL), D_FF ** -0.5),
        'norm_final': 1.0 + nrm(ks[24], (D_MODEL,), 0.02),
    }


def reference(x_prompt, x_sample, cache_nsa_kv, page_table, state_win_kv, state_gdn, state_gdn_conv, state_ffn_conv,
              norm_mix, w_in, cmp_pe, cmp_w1, cmp_w2, gdn_conv_w, gdn_a_log, gdn_dt_bias, gdn_norm,
              w_nsa_out, w_gdn_out, w_o, norm_ffn, w_up, ffn_conv_w, w_down, norm_final):
    params = (norm_mix, w_in, cmp_pe, cmp_w1, cmp_w2, gdn_conv_w, gdn_a_log, gdn_dt_bias, gdn_norm,
              w_nsa_out, w_gdn_out, w_o, norm_ffn, w_up, ffn_conv_w, w_down, norm_final)
    G = NSA_KV_HEADS
    B = x_prompt.shape[0]
    dt = x_prompt.dtype
    y_prompt, kv_p, win_p, s_p, conv_p, ffn_p = _block(
        x_prompt, 0,
        jnp.zeros((B, 0, 4, G, HEAD_DIM), dt), jnp.zeros((B, 0, 2, G, HEAD_DIM), dt),
        jnp.zeros((B, GDN_HEADS, GDN_DK, GDN_DV), state_gdn.dtype),
        jnp.zeros((B, GDN_CONV - 1, GDN_CONV_DIM), dt), jnp.zeros((B, FFN_CONV - 1, 2 * D_FF), dt),
        *params)
    n_pages = page_table.shape[1]
    past = cache_nsa_kv[page_table].reshape(x_sample.shape[0], n_pages * PAGE_SIZE, 4, G, HEAD_DIM)
    y_sample, kv_s, win_s, s_s, conv_s, ffn_s = _block(
        x_sample, n_pages * PAGE_SIZE, past, state_win_kv, state_gdn, state_gdn_conv, state_ffn_conv, *params)
    return (y_prompt, y_sample, kv_p, win_p, s_p, conv_p, ffn_p, kv_s, win_s, s_s, conv_s, ffn_s)
```

```python
import functools
import math

import jax
import jax.numpy as jnp
import numpy as np
from jax import lax
from jax.experimental import pallas as pl
from jax.experimental.pallas import tpu as pltpu

F32 = jnp.float32
BF16 = jnp.bfloat16
HI = lax.Precision.HIGHEST

LANES = 128
SUBLANES = 8
VMEM_LIMIT_BYTES = 56 * 1024 * 1024

HEAD_DIM = 128
NSA_HEADS = 16
NSA_KV_HEADS = 4
NSA_GROUP = NSA_HEADS // NSA_KV_HEADS
CMP_LEN = 32
CMP_STRIDE = 16
CMP_HIDDEN = 256
SLC_LEN = 64
SLC_TOP_N = 16
SLC_LOCAL = 2
WINDOW = 512
GDN_HEADS = 16
GDN_DK = 128
GDN_DV = 128
GDN_CONV = 4
GDN_CHUNK = 64
FFN_CONV = 3
ROPE_THETA = 10000.0
EPS = 1e-6


def _cparams(*sem):
    return pltpu.CompilerParams(dimension_semantics=sem, vmem_limit_bytes=VMEM_LIMIT_BYTES)


def _dot(a, b, precision=None):
    return jnp.dot(a, b, preferred_element_type=F32, precision=precision)


def _dot_nt(a, b, precision=None):
    return lax.dot_general(a, b, (((1,), (1,)), ((), ())), preferred_element_type=F32, precision=precision)


def _dot_tn(a, b, precision=None):
    return lax.dot_general(a, b, (((0,), (0,)), ((), ())), preferred_element_type=F32, precision=precision)


def _sigmoid(x):
    return 1.0 / (1.0 + jnp.exp(-x))


def _silu(x):
    return x * _sigmoid(x)


def _rmsnorm_body(x_ref, w_ref, o_ref):
    x = x_ref[...]
    y = x * lax.rsqrt(jnp.mean(x * x, axis=-1, keepdims=True) + EPS)
    o_ref[...] = (y * w_ref[...]).astype(o_ref.dtype)


def rmsnorm(x, w, out_dtype, tm):
    M, D = x.shape
    return pl.pallas_call(
        _rmsnorm_body,
        grid=(M // tm,),
        in_specs=[pl.BlockSpec((tm, D), lambda i: (i, 0)), pl.BlockSpec((1, D), lambda i: (0, 0))],
        out_specs=pl.BlockSpec((tm, D), lambda i: (i, 0)),
        out_shape=jax.ShapeDtypeStruct((M, D), out_dtype),
        compiler_params=_cparams("parallel"),
        name="rmsnorm",
    )(x, w.reshape(1, D))


def _mm_body(x_ref, w_ref, o_ref):
    o_ref[...] = _dot(x_ref[...], w_ref[...]).astype(o_ref.dtype)


def matmul(x, w, tm, tn, out_dtype=F32):
    M, K = x.shape
    N = w.shape[1]
    return pl.pallas_call(
        _mm_body,
        grid=(M // tm, N // tn),
        in_specs=[pl.BlockSpec((tm, K), lambda i, j: (i, 0)), pl.BlockSpec((K, tn), lambda i, j: (0, j))],
        out_specs=pl.BlockSpec((tm, tn), lambda i, j: (i, j)),
        out_shape=jax.ShapeDtypeStruct((M, N), out_dtype),
        compiler_params=_cparams("parallel", "arbitrary"),
        name="matmul",
    )(x, w)


def _mm_res_body(x_ref, w_ref, r_ref, o_ref):
    o_ref[...] = r_ref[...] + _dot(x_ref[...], w_ref[...])


def matmul_residual(x, w, res, tm, tn):
    M, K = x.shape
    N = w.shape[1]
    return pl.pallas_call(
        _mm_res_body,
        grid=(M // tm, N // tn),
        in_specs=[pl.BlockSpec((tm, K), lambda i, j: (i, 0)), pl.BlockSpec((K, tn), lambda i, j: (0, j)),
                  pl.BlockSpec((tm, tn), lambda i, j: (i, j))],
        out_specs=pl.BlockSpec((tm, tn), lambda i, j: (i, j)),
        out_shape=jax.ShapeDtypeStruct((M, N), F32),
        compiler_params=_cparams("parallel", "arbitrary"),
        name="matmul_residual",
    )(x, w, res)


def _mm_mix_body(a_ref, b_ref, wa_ref, wb_ref, ga_ref, gb_ref, o_ref):
    ua = _dot(a_ref[...], wa_ref[...])
    ub = _dot(b_ref[...], wb_ref[...])
    o_ref[...] = (_sigmoid(ga_ref[...]) * ua + _sigmoid(gb_ref[...]) * ub).astype(o_ref.dtype)


def matmul_mix(a, b, wa, wb, gates, tm, tn):
    M, K = a.shape
    N = wa.shape[1]
    nj = N // tn
    return pl.pallas_call(
        _mm_mix_body,
        grid=(M // tm, nj),
        in_specs=[pl.BlockSpec((tm, K), lambda i, j: (i, 0)), pl.BlockSpec((tm, K), lambda i, j: (i, 0)),
                  pl.BlockSpec((K, tn), lambda i, j: (0, j)), pl.BlockSpec((K, tn), lambda i, j: (0, j)),
                  pl.BlockSpec((tm, tn), lambda i, j: (i, j)), pl.BlockSpec((tm, tn), lambda i, j: (i, j + nj))],
        out_specs=pl.BlockSpec((tm, tn), lambda i, j: (i, j)),
        out_shape=jax.ShapeDtypeStruct((M, N), BF16),
        compiler_params=_cparams("parallel", "arbitrary"),
        name="matmul_mix",
    )(a, b, wa, wb, gates, gates)


def _shift_rows(x, hist, sh, row):
    s = pltpu.roll(x, sh, axis=0)
    for r in range(sh):
        s = jnp.where(row == r, hist[SUBLANES - sh + r:SUBLANES - sh + r + 1], s)
    return s


def _ffn_up_seq_body(x_ref, wg_ref, wv_ref, cg_ref, cv_ref, act_ref, tg_ref, tv_ref, hg_ref, hv_ref, *, tiles_per_seq):
    i = pl.program_id(0)
    j = pl.program_id(1)
    first = (i % tiles_per_seq) == 0
    x = x_ref[...]

    def branch(w_ref, c_ref, hist_ref, tail_ref):
        up = _dot(x, w_ref[...])
        tm = up.shape[0]
        hist = jnp.where(first, 0.0, hist_ref[j])
        row = lax.broadcasted_iota(jnp.int32, up.shape, 0)
        c = c_ref[...]
        out = up * c[2:3] + _shift_rows(up, hist, 2, row) * c[0:1] + _shift_rows(up, hist, 1, row) * c[1:2]
        tail = up[tm - SUBLANES:tm]
        hist_ref[j] = tail
        tail_ref[0] = tail
        return out

    gate = branch(wg_ref, cg_ref, hg_ref, tg_ref)
    val = branch(wv_ref, cv_ref, hv_ref, tv_ref)
    act_ref[...] = (_silu(gate) * val).astype(act_ref.dtype)


def ffn_up_seq(x, w_up, conv_w, n_seq, tm, tn):
    M, K = x.shape
    F = w_up.shape[1] // 2
    nj = F // tn
    tiles_per_seq = M // n_seq // tm
    body = functools.partial(_ffn_up_seq_body, tiles_per_seq=tiles_per_seq)
    return pl.pallas_call(
        body,
        grid=(M // tm, nj),
        in_specs=[pl.BlockSpec((tm, K), lambda i, j: (i, 0)),
                  pl.BlockSpec((K, tn), lambda i, j: (0, j)), pl.BlockSpec((K, tn), lambda i, j: (0, j + nj)),
                  pl.BlockSpec((FFN_CONV, tn), lambda i, j: (0, j)), pl.BlockSpec((FFN_CONV, tn), lambda i, j: (0, j + nj))],
        out_specs=[pl.BlockSpec((tm, tn), lambda i, j: (i, j)),
                   pl.BlockSpec((1, SUBLANES, tn), lambda i, j: (i, 0, j)),
                   pl.BlockSpec((1, SUBLANES, tn), lambda i, j: (i, 0, j))],
        out_shape=[jax.ShapeDtypeStruct((M, F), BF16),
                   jax.ShapeDtypeStruct((M // tm, SUBLANES, F), F32), jax.ShapeDtypeStruct((M // tm, SUBLANES, F), F32)],
        scratch_shapes=[pltpu.VMEM((nj, SUBLANES, tn), F32), pltpu.VMEM((nj, SUBLANES, tn), F32)],
        compiler_params=_cparams("arbitrary", "arbitrary"),
        name="ffn_up_seq",
    )(x, w_up, w_up, conv_w, conv_w)


def _ffn_up_step_body(x_ref, wg_ref, wv_ref, cg_ref, cv_ref, g0_ref, g1_ref, v0_ref, v1_ref, act_ref, ug_ref, uv_ref):
    x = x_ref[...]

    def branch(w_ref, c_ref, h0_ref, h1_ref, up_ref):
        up = _dot(x, w_ref[...])
        up_ref[...] = up
        c = c_ref[...]
        return up * c[2:3] + h0_ref[...] * c[0:1] + h1_ref[...] * c[1:2]

    gate = branch(wg_ref, cg_ref, g0_ref, g1_ref, ug_ref)
    val = branch(wv_ref, cv_ref, v0_ref, v1_ref, uv_ref)
    act_ref[...] = (_silu(gate) * val).astype(act_ref.dtype)


def ffn_up_step(x, w_up, conv_w, hist, tn):
    M, K = x.shape
    F2 = w_up.shape[1]
    F = F2 // 2
    nj = F // tn
    h2 = hist.reshape(M, 2 * F2)
    return pl.pallas_call(
        _ffn_up_step_body,
        grid=(nj,),
        in_specs=[pl.BlockSpec((M, K), lambda j: (0, 0)),
                  pl.BlockSpec((K, tn), lambda j: (0, j)), pl.BlockSpec((K, tn), lambda j: (0, j + nj)),
                  pl.BlockSpec((FFN_CONV, tn), lambda j: (0, j)), pl.BlockSpec((FFN_CONV, tn), lambda j: (0, j + nj)),
                  pl.BlockSpec((M, tn), lambda j: (0, j)), pl.BlockSpec((M, tn), lambda j: (0, j + 2 * nj)),
                  pl.BlockSpec((M, tn), lambda j: (0, j + nj)), pl.BlockSpec((M, tn), lambda j: (0, j + 3 * nj))],
        out_specs=[pl.BlockSpec((M, tn), lambda j: (0, j))] * 3,
        out_shape=[jax.ShapeDtypeStruct((M, F), BF16), jax.ShapeDtypeStruct((M, F), F32), jax.ShapeDtypeStruct((M, F), F32)],
        compiler_params=_cparams("arbitrary"),
        name="ffn_up_step",
    )(x, w_up, w_up, conv_w, conv_w, h2, h2, h2, h2)


def _rope_body(u_ref, cos_ref, sin_ref, q_ref, kv_ref, win_ref):
    cos = cos_ref[...]
    sin = sin_ref[...]
    scale = HEAD_DIM ** -0.5

    def rot(x):
        return x * cos + pltpu.roll(x, HEAD_DIM // 2, axis=1) * sin

    for h in range(NSA_HEADS):
        sl = slice(h * HEAD_DIM, (h + 1) * HEAD_DIM)
        q_ref[:, sl] = (rot(u_ref[:, sl]) * scale).astype(q_ref.dtype)
    base = NSA_HEADS * HEAD_DIM
    n_glob = 4 * NSA_KV_HEADS
    for slot in range(6 * NSA_KV_HEADS):
        x = u_ref[:, base + slot * HEAD_DIM:base + (slot + 1) * HEAD_DIM]
        if (slot // NSA_KV_HEADS) % 2 == 0:
            x = rot(x)
        if slot < n_glob:
            kv_ref[:, slot * HEAD_DIM:(slot + 1) * HEAD_DIM] = x
        else:
            win_ref[:, (slot - n_glob) * HEAD_DIM:(slot - n_glob + 1) * HEAD_DIM] = x


def rope_split(u, cos, sin, tm):
    M, W = u.shape
    nt = cos.shape[0] // tm
    nq = NSA_HEADS * HEAD_DIM
    ng = 4 * NSA_KV_HEADS * HEAD_DIM
    nw = 2 * NSA_KV_HEADS * HEAD_DIM
    return pl.pallas_call(
        _rope_body,
        grid=(M // tm,),
        in_specs=[pl.BlockSpec((tm, W), lambda i: (i, 0)),
                  pl.BlockSpec((tm, HEAD_DIM), lambda i: (i % nt, 0)), pl.BlockSpec((tm, HEAD_DIM), lambda i: (i % nt, 0))],
        out_specs=[pl.BlockSpec((tm, nq), lambda i: (i, 0)), pl.BlockSpec((tm, ng), lambda i: (i, 0)),
                   pl.BlockSpec((tm, nw), lambda i: (i, 0))],
        out_shape=[jax.ShapeDtypeStruct((M, nq), BF16), jax.ShapeDtypeStruct((M, ng), F32), jax.ShapeDtypeStruct((M, nw), F32)],
        compiler_params=_cparams("parallel"),
        name="rope_split",
    )(u, cos, sin)


GDN_HB = 4
GDN_COL_B = 0
GDN_COL_A = 8
NSA_COL_G = 16


def _softplus(x):
    return jnp.maximum(x, 0.0) + jnp.log1p(jnp.exp(-jnp.abs(x)))


def _l2norm(x):
    return x * lax.rsqrt(jnp.sum(x * x, axis=-1, keepdims=True) + EPS)


def _solve_unit_lower(a, r, order):
    steps = int(math.log2(order))
    m, n = a.shape[1], r.shape[1]
    for i in range(steps):
        a_b = a.astype(BF16)
        r_hi, r_lo = _split_bf16(r)
        last = i == steps - 1
        prod = _dot(a_b, jnp.concatenate(([] if last else [a_b]) + [r_hi, r_lo], axis=1))
        if not last:
            a, prod = prod[:, :m], prod[:, m:]
        r = r + (prod[:, :n] + prod[:, n:])
    return r


def _block_diag(blocks):
    z = jnp.zeros_like(blocks[0])
    nb = len(blocks)
    return jnp.concatenate([jnp.concatenate([blocks[h] if j == h else z for j in range(nb)], axis=1)
                            for h in range(nb)], axis=0)


def _gdn_chunk_group(q, k, v, beta, gc, gcr, s):
    nh = len(q)
    C, d = v[0].shape
    W = nh * C
    stack = lambda xs: jnp.concatenate(xs, axis=0)
    bd = lambda xs: _block_diag([x.astype(BF16) for x in xs])
    ri = lax.broadcasted_iota(jnp.int32, (W, W), 0)
    ci = lax.broadcasted_iota(jnp.int32, (W, W), 1)
    same = (ri // C) == (ci // C)
    gcs = stack(gc)
    dmat = jnp.where(same & (ri >= ci), jnp.exp(jnp.minimum(gcs - jnp.concatenate(gcr, axis=1), 0.0)), 0.0)
    eg = [jnp.exp(g) for g in gc]
    kb = [k[h] * beta[h] for h in range(nh)]
    xk = _dot_nt(jnp.concatenate([bd(kb), bd(q)], axis=0), bd(k))
    a = jnp.where(same & (ri > ci), -(xk[:W] * dmat), 0.0)
    r = jnp.concatenate([stack([v[h] * beta[h] for h in range(nh)]), stack([kb[h] * eg[h] for h in range(nh)])], axis=1)
    w = _solve_unit_lower(a, r, C)
    wk = [w[h * C:(h + 1) * C, d:] for h in range(nh)]
    xs = _dot(jnp.concatenate([bd(wk), bd([q[h] * eg[h] for h in range(nh)])], axis=0), s.astype(BF16))
    v_new = (w[:, :d] - xs[:W]).astype(BF16)
    o = xs[W:] + _dot((xk[W:] * dmat).astype(BF16), v_new)
    g_last = [g[C - 1:C, :] for g in gc]
    kd = [k[h] * jnp.exp(g_last[h] - gc[h]) for h in range(nh)]
    keep = stack([jnp.broadcast_to(jnp.exp(g_last[h]), (d, 1)) for h in range(nh)])
    s_new = s * keep + _dot_tn(bd(kd), v_new)
    return o, s_new


def _gdn_seq_body(q_ref, k_ref, v_ref, z_ref, ba_ref, cq_ref, ck_ref, cv_ref, alog_ref, dtb_ref, nw_ref,
                  o_ref, s_ref, tq_ref, tk_ref, tv_ref, qs_ref, ks_ref, vs_ref, gs_ref, bs_ref):
    t_idx = pl.program_id(2)
    tt = q_ref.shape[0]
    hb = q_ref.shape[1] // GDN_DK
    C = GDN_CHUNK

    @pl.when(t_idx == 0)
    def _():
        s_ref[...] = jnp.zeros_like(s_ref)
        tq_ref[...] = jnp.zeros_like(tq_ref)
        tk_ref[...] = jnp.zeros_like(tk_ref)
        tv_ref[...] = jnp.zeros_like(tv_ref)

    def conv(x_ref, c_ref, tail_ref):
        x = x_ref[...]
        hist = tail_ref[...]
        row = lax.broadcasted_iota(jnp.int32, x.shape, 0)
        c = c_ref[...]
        out = x * c[GDN_CONV - 1:GDN_CONV]
        for j in range(GDN_CONV - 1):
            out = out + _shift_rows(x, hist, GDN_CONV - 1 - j, row) * c[j:j + 1]
        tail_ref[...] = x[tt - SUBLANES:tt]
        return _silu(out)

    qa = conv(q_ref, cq_ref, tq_ref)
    ka = conv(k_ref, ck_ref, tk_ref)
    vs_ref[...] = conv(v_ref, cv_ref, tv_ref)
    for h in range(hb):
        sl = slice(h * GDN_DK, (h + 1) * GDN_DK)
        qs_ref[:, sl] = _l2norm(qa[:, sl]) * GDN_DK ** -0.5
        ks_ref[:, sl] = _l2norm(ka[:, sl])
    ba = ba_ref[...]
    bs_ref[...] = _sigmoid(ba)
    gs_ref[...] = -jnp.exp(alog_ref[0, 0:1, :]) * _softplus(ba + dtb_ref[0, 0:1, :])

    tri = (lax.broadcasted_iota(jnp.int32, (C, C), 0) >= lax.broadcasted_iota(jnp.int32, (C, C), 1)).astype(F32)
    sel = (lax.broadcasted_iota(jnp.int32, (2 * SUBLANES, LANES), 0)
           == lax.broadcasted_iota(jnp.int32, (2 * SUBLANES, LANES), 1)).astype(F32)
    nw = nw_ref[...]

    for c in range(tt // C):
        rows = slice(c * C, (c + 1) * C)
        gc_all = _dot(tri, gs_ref[rows, :], HI)
        gc_t = _dot_nt(sel, gc_all, HI)
        beta_all = bs_ref[rows, :]
        heads = [slice(h * GDN_DK, (h + 1) * GDN_DK) for h in range(hb)]
        o, s_new = _gdn_chunk_group(
            [qs_ref[rows, sl] for sl in heads], [ks_ref[rows, sl] for sl in heads], [vs_ref[rows, sl] for sl in heads],
            [beta_all[:, GDN_COL_B + h:GDN_COL_B + h + 1] for h in range(hb)],
            [gc_all[:, GDN_COL_A + h:GDN_COL_A + h + 1] for h in range(hb)],
            [gc_t[GDN_COL_A + h:GDN_COL_A + h + 1, :] for h in range(hb)], s_ref[0])
        s_ref[0] = s_new
        on = o * lax.rsqrt(jnp.mean(o * o, axis=-1, keepdims=True) + EPS) * nw
        for h, sl in enumerate(heads):
            o_ref[rows, sl] = (on[h * C:(h + 1) * C] * _silu(z_ref[rows, sl])).astype(o_ref.dtype)


def gdn_seq(u_qkv, u_z, u_small, conv_w, alog_rows, dtb_rows, norm_w, n_seq, tt):
    M = u_qkv.shape[0]
    T = M // n_seq
    nt = T // tt
    hb = GDN_HB
    nhb = GDN_HEADS // hb
    wb = hb * GDN_DK
    row_blk = lambda off: pl.BlockSpec((tt, wb), lambda b, h, t: (b * nt + t, h + off))
    cw_blk = lambda off: pl.BlockSpec((GDN_CONV, wb), lambda b, h, t: (0, h + off))
    par_blk = pl.BlockSpec((1, SUBLANES, LANES), lambda b, h, t: (h, 0, 0))
    return pl.pallas_call(
        _gdn_seq_body,
        grid=(n_seq, nhb, nt),
        in_specs=[row_blk(0), row_blk(nhb), row_blk(2 * nhb), row_blk(0),
                  pl.BlockSpec((tt, LANES), lambda b, h, t: (b * nt + t, h)),
                  cw_blk(0), cw_blk(nhb), cw_blk(2 * nhb), par_blk, par_blk,
                  pl.BlockSpec((1, GDN_DV), lambda b, h, t: (0, 0))],
        out_specs=[row_blk(0), pl.BlockSpec((1, hb * GDN_DK, GDN_DV), lambda b, h, t: (b, h, 0))],
        out_shape=[jax.ShapeDtypeStruct((M, GDN_HEADS * GDN_DV), BF16),
                   jax.ShapeDtypeStruct((n_seq, GDN_HEADS * GDN_DK, GDN_DV), F32)],
        scratch_shapes=[pltpu.VMEM((SUBLANES, wb), F32)] * 3 + [pltpu.VMEM((tt, wb), F32)] * 3
                       + [pltpu.VMEM((tt, LANES), F32)] * 2,
        compiler_params=_cparams("parallel", "parallel", "arbitrary"),
        name="gdn_seq",
    )(u_qkv, u_qkv, u_qkv, u_z, u_small, conv_w, conv_w, conv_w, alog_rows, dtb_rows, norm_w.reshape(1, GDN_DV))


def gdn_param_rows(gdn_a_log, gdn_dt_bias):
    nhb = GDN_HEADS // GDN_HB

    def rows(p):
        r = _place_lanes(((GDN_COL_A, p.reshape(nhb, GDN_HB)),), (nhb,))
        return jnp.broadcast_to(r[:, None, :], (nhb, SUBLANES, LANES))

    return rows(gdn_a_log), rows(gdn_dt_bias)


CMP_RATIO = CMP_LEN // CMP_STRIDE
CMP_FEAT = CMP_STRIDE * HEAD_DIM


def _compress_mlp(part, pe_ref, w1_ref, w2_ref):
    n = part.shape[0]
    pe_part = _dot(pe_ref[...], w1_ref[...])
    hid0 = pe_part[0:1, :CMP_HIDDEN] + pe_part[1:2, CMP_HIDDEN:]
    hid = hid0 + part[:, :CMP_HIDDEN]
    hid = hid + pltpu.roll(part[:, CMP_HIDDEN:], n - 1, axis=0)
    return _dot(_silu(hid).astype(BF16), w2_ref[...])


def _compress_seq_body(x_ref, pe_ref, w1_ref, w2_ref, o_ref, xc_ref):
    n = o_ref.shape[3]
    for s in range(CMP_STRIDE):
        xc_ref[:, s * HEAD_DIM:(s + 1) * HEAD_DIM] = x_ref[pl.ds(s, n, stride=CMP_STRIDE), :].astype(BF16)
    part = _dot(xc_ref[...], w1_ref[0])
    o_ref[0, 0, 0] = _compress_mlp(part, pe_ref.at[0], w1_ref.at[0], w2_ref.at[0])


def compress_seq(kv_rows, pe2, w1c, w2c, n_seq):
    M = kv_rows.shape[0]
    T = M // n_seq
    n = T // CMP_STRIDE
    G = NSA_KV_HEADS
    return pl.pallas_call(
        _compress_seq_body,
        grid=(n_seq, 2, G),
        in_specs=[pl.BlockSpec((T, HEAD_DIM), lambda b, c, g: (b, c * G + g)),
                  pl.BlockSpec((1, SUBLANES, CMP_FEAT), lambda b, c, g: (c, 0, 0)),
                  pl.BlockSpec((1, CMP_FEAT, 2 * CMP_HIDDEN), lambda b, c, g: (c, 0, 0)),
                  pl.BlockSpec((1, CMP_HIDDEN, HEAD_DIM), lambda b, c, g: (c, 0, 0))],
        out_specs=pl.BlockSpec((1, 1, 1, n, HEAD_DIM), lambda b, c, g: (b, c, g, 0, 0)),
        out_shape=jax.ShapeDtypeStruct((n_seq, 2, G, n, HEAD_DIM), F32),
        scratch_shapes=[pltpu.VMEM((n, CMP_FEAT), BF16)],
        compiler_params=_cparams("parallel", "parallel", "parallel"),
        name="compress_seq",
    )(kv_rows, pe2, w1c, w2c)


def compress_params(cmp_pe, cmp_w1, cmp_w2):
    w1 = cmp_w1.reshape(2, CMP_RATIO, CMP_FEAT, CMP_HIDDEN)
    w1c = jnp.concatenate([w1[:, r] for r in range(CMP_RATIO)], axis=-1).astype(BF16)
    pe = cmp_pe.reshape(2, CMP_RATIO, CMP_FEAT)
    pe2 = jnp.zeros((2, SUBLANES, CMP_FEAT), F32).at[:, :CMP_RATIO].set(pe).astype(BF16)
    return pe2, w1c, cmp_w2.astype(BF16)


def _masked_softmax(s, mask, axis=-1):
    s = jnp.where(mask, s, -jnp.inf)
    m = jnp.max(s, axis=axis, keepdims=True)
    m = jnp.where(m > -jnp.inf, m, 0.0)
    e = jnp.exp(s - m)
    return e / jnp.maximum(jnp.sum(e, axis=axis, keepdims=True), 1e-30)


def _split_bf16(x):
    hi = x.astype(BF16)
    return hi, (x - hi.astype(F32)).astype(BF16)


def _select_blocks_t(imp_t, pos_row, n_blocks):
    J = imp_t.shape[0]
    j = lax.broadcasted_iota(jnp.int32, imp_t.shape, 0)
    cur = pos_row // SLC_LEN
    forced = (j == 0) | ((j <= cur) & (j > cur - SLC_LOCAL))
    score = jnp.where(j * SLC_LEN > pos_row, -jnp.inf, jnp.where(forced, jnp.inf, imp_t))
    score = jnp.where(j < n_blocks, score, -jnp.inf)
    rank = jnp.zeros(imp_t.shape, F32)
    for jp in range(n_blocks):
        row = score[jp:jp + 1, :]
        ahead = (row > score) | ((row == score) & (j > jp))
        rank = rank + jnp.where(ahead, 1.0, 0.0)
    keep = (rank < float(min(SLC_TOP_N, n_blocks))) & (j < n_blocks)
    return jnp.where(keep, 1.0, 0.0)


def _nsa_seq_body(q_ref, kc_ref, vc_ref, ks_ref, vs_ref, kw_ref, vw_ref, gl_ref, ovl_ref, o_ref, *, kblk):
    i = pl.program_id(2)
    tq = q_ref.shape[0]
    T = ks_ref.shape[0]
    H = NSA_GROUP
    D = HEAD_DIM
    n_slc = T // SLC_LEN
    qb = q_ref[...]
    q4 = jnp.concatenate([qb[:, h * D:(h + 1) * D] for h in range(H)], axis=0)
    t0 = i * tq
    pos1 = t0 + lax.broadcasted_iota(jnp.int32, (tq, 1), 0)
    pos = jnp.concatenate([pos1] * H, axis=0)

    kc = kc_ref[0, 0, 0].astype(BF16)
    vc = vc_ref[0, 0, 0].astype(BF16)
    n_c = kc.shape[0]
    s = _dot_nt(q4, kc)
    cmp_end = lax.broadcasted_iota(jnp.int32, (1, n_c), 1) * CMP_STRIDE + (CMP_LEN - 1)
    p = _masked_softmax(s, cmp_end <= pos)
    o_c = _dot(p.astype(BF16), vc)

    psum = p[0:tq]
    for h in range(1, H):
        psum = psum + p[h * tq:(h + 1) * tq]
    p_hi, p_lo = _split_bf16(psum)
    ovl = ovl_ref[...]
    imp_t = _dot_nt(ovl, p_hi) + _dot_nt(ovl, p_lo)
    pos_row = t0 + lax.broadcasted_iota(jnp.int32, (1, tq), 1)
    sel_t = _select_blocks_t(imp_t, pos_row, n_slc)
    eye = (lax.broadcasted_iota(jnp.int32, (tq, tq), 0) == lax.broadcasted_iota(jnp.int32, (tq, tq), 1))
    sel = _dot_nt(jnp.where(eye, 1.0, 0.0).astype(BF16), sel_t.astype(BF16)).astype(BF16)

    n_kb = (t0 + tq + kblk - 1) // kblk

    def kv_step(kb, carry):
        m, l, acc = carry
        k0 = pl.multiple_of(kb * kblk, kblk)
        kk = ks_ref[pl.ds(k0, kblk), :].astype(BF16)
        vv = vs_ref[pl.ds(k0, kblk), :].astype(BF16)
        s2 = _dot_nt(q4, kk)
        kpos = k0 + lax.broadcasted_iota(jnp.int32, (1, kblk), 1)
        expand = (lax.broadcasted_iota(jnp.int32, (n_slc, kblk), 0) == kpos // SLC_LEN)
        chosen = _dot(sel, jnp.where(expand, 1.0, 0.0).astype(BF16))
        chosen = jnp.concatenate([chosen] * H, axis=0)
        valid = (chosen > 0.5) & (kpos <= pos)
        s2 = jnp.where(valid, s2, -jnp.inf)
        m_new = jnp.maximum(m, jnp.max(s2, axis=-1, keepdims=True))
        alpha = jnp.exp(m - m_new)
        e = jnp.exp(s2 - m_new)
        l = alpha * l + jnp.sum(e, axis=-1, keepdims=True)
        acc = alpha * acc + _dot(e.astype(BF16), vv)
        return m_new, l, acc

    init = (jnp.full((H * tq, 1), -jnp.inf, F32), jnp.zeros((H * tq, 1), F32), jnp.zeros((H * tq, D), F32))
    m, l, acc = lax.fori_loop(0, n_kb, kv_step, init)
    o_s = acc / jnp.maximum(l, 1e-30)

    band = WINDOW + tq
    w0 = pl.multiple_of(jnp.maximum(t0 - WINDOW, 0), tq)
    kw = kw_ref[pl.ds(w0, band), :].astype(BF16)
    vw = vw_ref[pl.ds(w0, band), :].astype(BF16)
    sw = _dot_nt(q4, kw)
    diff = pos - (w0 + lax.broadcasted_iota(jnp.int32, (1, band), 1))
    pw = _masked_softmax(sw, (diff >= 0) & (diff < WINDOW))
    o_w = _dot(pw.astype(BF16), vw)

    gates = _sigmoid(gl_ref[...])
    for h in range(H):
        rows = slice(h * tq, (h + 1) * tq)
        g = lambda br: gates[:, NSA_COL_G + H * br + h:NSA_COL_G + H * br + h + 1]
        o_ref[:, h * D:(h + 1) * D] = (g(0) * o_c[rows] + g(1) * o_s[rows] + g(2) * o_w[rows]).astype(o_ref.dtype)


def _overlap_t(n_cmp_rows, n_slc):
    start = np.arange(n_cmp_rows)[None, :] * CMP_STRIDE
    blk = np.arange(n_slc)[:, None] * SLC_LEN
    return ((start < blk + SLC_LEN) & (start + CMP_LEN > blk)).astype(np.float32)


def nsa_seq(q, kv_rows, win_rows, cmp_kv, u_small, n_seq, tq=128, kblk=256):
    M = q.shape[0]
    T = M // n_seq
    nt = T // tq
    G = NSA_KV_HEADS
    gw = NSA_GROUP * HEAD_DIM
    n_c = cmp_kv.shape[3]
    ovl = jnp.asarray(_overlap_t(n_c, T // SLC_LEN), BF16)
    body = functools.partial(_nsa_seq_body, kblk=kblk)
    seq_blk = lambda off: pl.BlockSpec((T, HEAD_DIM), lambda b, g, i: (b, g + off))
    cmp_blk = lambda c: pl.BlockSpec((1, 1, 1, n_c, HEAD_DIM), lambda b, g, i: (b, c, g, 0, 0))
    return pl.pallas_call(
        body,
        grid=(n_seq, G, nt),
        in_specs=[pl.BlockSpec((tq, gw), lambda b, g, i: (b * nt + i, g)),
                  cmp_blk(0), cmp_blk(1), seq_blk(2 * G), seq_blk(3 * G), seq_blk(0), seq_blk(G),
                  pl.BlockSpec((tq, LANES), lambda b, g, i: (b * nt + i, g)),
                  pl.BlockSpec(ovl.shape, lambda b, g, i: (0, 0))],
        out_specs=pl.BlockSpec((tq, gw), lambda b, g, i: (b * nt + i, g)),
        out_shape=jax.ShapeDtypeStruct((M, NSA_HEADS * HEAD_DIM), BF16),
        compiler_params=_cparams("parallel", "parallel", "arbitrary"),
        name="nsa_seq",
    )(q, cmp_kv, cmp_kv, kv_rows, kv_rows, win_rows, win_rows, u_small, ovl)


STEP_COL_G = 0
STEP_COL_B = 3 * NSA_HEADS
STEP_COL_A = STEP_COL_B + GDN_HEADS


def _gdn_step_prep_body(x_ref, h_ref, c_ref, us_ref, alog_ref, dtb_ref, q_ref, k_ref, v_ref, b_ref, e_ref):
    W = x_ref.shape[1]
    c = c_ref[...]
    x = x_ref[...] * c[GDN_CONV - 1:GDN_CONV]
    for j in range(GDN_CONV - 1):
        x = x + h_ref[:, j * W:(j + 1) * W] * c[j:j + 1]
    x = _silu(x)
    us = us_ref[...]
    beta = _sigmoid(us)
    eg = jnp.exp(-jnp.exp(alog_ref[...]) * _softplus(us + dtb_ref[...]))
    nq = GDN_HEADS * GDN_DK
    for h in range(GDN_HEADS):
        sl = slice(h * GDN_DK, (h + 1) * GDN_DK)
        q_ref[:, sl] = _l2norm(x[:, sl]) * GDN_DK ** -0.5
        k_ref[:, sl] = _l2norm(x[:, nq + h * GDN_DK:nq + (h + 1) * GDN_DK])
        b_ref[:, sl] = jnp.broadcast_to(beta[:, STEP_COL_B + h:STEP_COL_B + h + 1], (x.shape[0], GDN_DK))
        e_ref[:, sl] = jnp.broadcast_to(eg[:, STEP_COL_A + h:STEP_COL_A + h + 1], (x.shape[0], GDN_DK))
    v_ref[...] = x[:, 2 * nq:]


def _gdn_step_body(q_ref, k_ref, v_ref, b_ref, e_ref, z_ref, nw_ref, s_ref, o_ref, so_ref):
    bb = q_ref.shape[0]
    eye = jnp.where(lax.broadcasted_iota(jnp.int32, (GDN_DK, GDN_DK), 0)
                    == lax.broadcasted_iota(jnp.int32, (GDN_DK, GDN_DK), 1), 1.0, 0.0)
    nw = nw_ref[...]

    def per_seq(bi, carry):
        q, k, v, beta, eg, z = q_ref[bi], k_ref[bi], v_ref[bi], b_ref[bi], e_ref[bi], z_ref[bi]
        k_t = _dot_nt(eye, k, HI)
        q_t = _dot_nt(eye, q, HI)
        outs = []
        for h in range(GDN_HEADS):
            r = slice(h, h + 1)
            s = s_ref[bi, h]
            kcol = k_t[:, h:h + 1]
            k_s = jnp.sum(kcol * s, axis=0, keepdims=True)
            q_s = jnp.sum(q_t[:, h:h + 1] * s, axis=0, keepdims=True)
            v_new = v[r] * beta[r] - (beta[r] * eg[r]) * k_s
            qk = jnp.sum(q[r] * k[r], axis=-1, keepdims=True)
            o = eg[r] * q_s + qk * v_new
            so_ref[bi, h] = s * eg[r] + kcol * v_new
            on = o * lax.rsqrt(jnp.mean(o * o, axis=-1, keepdims=True) + EPS) * nw
            outs.append(on * _silu(z[r]))
        o_ref[bi] = jnp.concatenate(outs, axis=0).astype(o_ref.dtype)
        return carry

    lax.fori_loop(0, bb, per_seq, 0)


def gdn_step(u_qkv, u_z, u_small, hist, state, conv_w, a_log, dt_bias, norm_w, bb=4):
    Bd, W = u_qkv.shape
    H = GDN_HEADS
    nq = H * GDN_DK
    row = lambda p: _place_lanes(((STEP_COL_A, p.reshape(1, H)),), (1,))
    full = lambda shape: pl.BlockSpec(shape, lambda i: (0,) * len(shape))
    outs = pl.pallas_call(
        _gdn_step_prep_body,
        grid=(1,),
        in_specs=[full((Bd, W)), full((Bd, (GDN_CONV - 1) * W)), full((GDN_CONV, W)), full((Bd, LANES)),
                  full((1, LANES)), full((1, LANES))],
        out_specs=[full((Bd, nq))] * 5,
        out_shape=[jax.ShapeDtypeStruct((Bd, nq), F32)] * 5,
        compiler_params=_cparams("arbitrary"),
        name="gdn_step_prep",
    )(u_qkv, hist.reshape(Bd, (GDN_CONV - 1) * W), conv_w, u_small, row(a_log), row(dt_bias))
    heads = lambda a: a.reshape(Bd, H, GDN_DK)
    vec_blk = pl.BlockSpec((bb, H, GDN_DK), lambda i: (i, 0, 0))
    st_blk = pl.BlockSpec((bb, H, GDN_DK, GDN_DV), lambda i: (i, 0, 0, 0))
    o, s_new = pl.pallas_call(
        _gdn_step_body,
        grid=(Bd // bb,),
        in_specs=[vec_blk] * 6 + [pl.BlockSpec((1, GDN_DV), lambda i: (0, 0)), st_blk],
        out_specs=[vec_blk, st_blk],
        out_shape=[jax.ShapeDtypeStruct((Bd, H, GDN_DV), BF16), jax.ShapeDtypeStruct(state.shape, F32)],
        compiler_params=_cparams("parallel"),
        name="gdn_step",
    )(*[heads(a) for a in outs], heads(u_z), norm_w.reshape(1, GDN_DV), state)
    return o.reshape(Bd, H * GDN_DV), s_new


def _softmax_with_new(s_past, valid, s_new):
    s_past = jnp.where(valid, s_past, -jnp.inf)
    m = jnp.maximum(jnp.max(s_past, axis=-1, keepdims=True), s_new)
    e = jnp.exp(s_past - m)
    e_new = jnp.exp(s_new - m)
    return e, e_new, 1.0 / (jnp.sum(e, axis=-1, keepdims=True) + e_new)


def _nsa_step_body(pt_ref, *refs, n_pages, page, n_past):
    pages = refs[:n_pages]
    (q_ref, new_ref, win_ref, wnew_ref, gl_ref, pe_ref, w1_ref, w2_ref, ovl_ref, o_ref, xc_ref) = refs[n_pages:]
    G, H, D = NSA_KV_HEADS, NSA_HEADS, HEAD_DIM
    pos = n_past
    cpp = page // CMP_STRIDE
    n_chunk = n_pages * cpp
    L = n_pages * page
    q = q_ref[0]
    qf = q.astype(F32)
    new = new_ref[0]
    row_g = lax.broadcasted_iota(jnp.int32, (H, 1), 0) // NSA_GROUP

    def slot_rows(pg, slot, first, count, step):
        return pg[0, pl.ds(first * 4 * G + slot, count, stride=step * 4 * G), :]

    def per_head_rows(rows):
        return jnp.concatenate([jnp.broadcast_to(rows[g:g + 1], (NSA_GROUP, rows.shape[1])) for g in range(G)], axis=0)

    ckv = []
    for c in range(2):
        for s in range(CMP_STRIDE):
            for g in range(G):
                piece = jnp.concatenate([slot_rows(pg, c * G + g, s, cpp, CMP_STRIDE) for pg in pages], axis=0)
                xc_ref[g * n_chunk:(g + 1) * n_chunk, s * D:(s + 1) * D] = piece.astype(BF16)
        part = _dot(xc_ref[...], w1_ref[c])
        ckv.append(_compress_mlp(part, pe_ref.at[c], w1_ref.at[c], w2_ref.at[c]).astype(BF16))

    n_all = G * n_chunk
    s = _dot_nt(q, ckv[0])
    col = lax.broadcasted_iota(jnp.int32, (1, n_all), 1)
    blk = col % n_chunk
    ok = (col // n_chunk == row_g) & (blk * CMP_STRIDE + (CMP_LEN - 1) <= pos) & (blk < n_chunk - CMP_RATIO + 1)
    p = _masked_softmax(s, ok)
    o_c = _dot(p.astype(BF16), ckv[1])

    gsum = jnp.where(lax.broadcasted_iota(jnp.int32, (LANES, H), 0) == lax.broadcasted_iota(jnp.int32, (LANES, H), 1) // NSA_GROUP, 1.0, 0.0)
    psum = _dot(gsum, p, HI)
    p_hi, p_lo = _split_bf16(psum)
    ovl = ovl_ref[...]
    imp_t = _dot_nt(ovl, p_hi) + _dot_nt(ovl, p_lo)
    n_slc = -(-(L + 1) // SLC_LEN)
    sel_t = _select_blocks_t(imp_t, jnp.full((1, LANES), pos, jnp.int32), n_slc)
    J = sel_t.shape[0]
    eye8 = jnp.where(lax.broadcasted_iota(jnp.int32, (SUBLANES, LANES), 0) == lax.broadcasted_iota(jnp.int32, (SUBLANES, LANES), 1), 1.0, 0.0)
    sel = _dot_nt(eye8.astype(BF16), sel_t.astype(BF16)).astype(BF16)
    kpos = lax.broadcasted_iota(jnp.int32, (1, L), 1)
    expand = lax.broadcasted_iota(jnp.int32, (J, L), 0) == kpos // SLC_LEN
    chosen = _dot(sel, jnp.where(expand, 1.0, 0.0).astype(BF16))
    chosen = per_head_rows(chosen)

    def past_scores(slot0):
        sc = jnp.zeros((H, L), F32)
        for g in range(G):
            sg = jnp.concatenate([_dot_nt(q, slot_rows(pg, slot0 + g, 0, page, 1).astype(BF16)) for pg in pages], axis=1)
            sc = jnp.where(row_g == g, sg, sc)
        return sc

    def past_values(e, slot0):
        acc = jnp.zeros((H, D), F32)
        for g in range(G):
            eg = jnp.where(row_g == g, e, 0.0).astype(BF16)
            for j, pg in enumerate(pages):
                acc = acc + _dot(eg[:, j * page:(j + 1) * page], slot_rows(pg, slot0 + g, 0, page, 1).astype(BF16))
        return acc

    s_new = jnp.sum(qf * per_head_rows(new[2 * G:3 * G]), axis=-1, keepdims=True)
    e, e_new, inv = _softmax_with_new(past_scores(2 * G), (chosen > 0.5) & (kpos <= pos), s_new)
    o_s = (past_values(e, 3 * G) + e_new * per_head_rows(new[3 * G:4 * G])) * inv

    n_win = win_ref.shape[1] // (2 * G)
    win_rows = lambda slot: win_ref[0, pl.ds(slot, n_win, stride=2 * G), :].astype(BF16)
    wnew = wnew_ref[0]
    sw = jnp.zeros((H, n_win), F32)
    for g in range(G):
        sg = _dot_nt(q, win_rows(g))
        sw = jnp.where(row_g == g, sg, sw)
    wpos = pos - n_win + lax.broadcasted_iota(jnp.int32, (1, n_win), 1)
    sw_new = jnp.sum(qf * per_head_rows(wnew[0:G]), axis=-1, keepdims=True)
    ew, ew_new, winv = _softmax_with_new(sw, (pos - wpos < WINDOW) & (wpos >= 0), sw_new)
    acc = jnp.zeros((H, D), F32)
    for g in range(G):
        eg = jnp.where(row_g == g, ew, 0.0).astype(BF16)
        acc = acc + _dot(eg, win_rows(G + g))
    o_w = (acc + ew_new * per_head_rows(wnew[G:2 * G])) * winv

    gl = jnp.broadcast_to(_sigmoid(gl_ref[0]), (SUBLANES, LANES))
    hh = lax.broadcasted_iota(jnp.int32, (H, LANES), 0)
    cc = lax.broadcasted_iota(jnp.int32, (H, LANES), 1)
    gate = lambda br: _dot_nt(jnp.where(cc == STEP_COL_G + br * H + hh, 1.0, 0.0), gl, HI)[:, 0:1]
    o_ref[0] = (gate(0) * o_c + gate(1) * o_s + gate(2) * o_w).astype(o_ref.dtype)


def nsa_step(q, new_rows, win_new, cache, page_table, win_state, u_small, pe2, w1c, w2c):
    Bd = q.shape[0]
    n_pool, page = cache.shape[:2]
    n_pages = page_table.shape[1]
    G, H, D = NSA_KV_HEADS, NSA_HEADS, HEAD_DIM
    n_past = n_pages * page
    n_chunk = n_past // CMP_STRIDE
    n_win = win_state.shape[1]
    assert n_past % SLC_LEN == 0 and n_win <= n_past and page % CMP_STRIDE == 0
    n_slc = -(-(n_past + 1) // SLC_LEN)
    J = -(-n_slc // (2 * SUBLANES)) * (2 * SUBLANES)
    ovl = np.zeros((J, G * n_chunk), np.float32)
    ovl[:n_slc] = np.tile(_overlap_t(n_chunk, n_slc), (1, G))
    ovl = jnp.asarray(ovl, BF16)
    body = functools.partial(_nsa_step_body, n_pages=n_pages, page=page, n_past=n_past)
    page_spec = lambda j: pl.BlockSpec((1, page * 4 * G, D), lambda b, pt: (pt[b, j], 0, 0))
    const = lambda a: pl.BlockSpec(a.shape, lambda b, pt: (0,) * a.ndim, pipeline_mode=pl.Buffered(1))
    per_b = lambda shape: pl.BlockSpec((1,) + shape, lambda b, pt: (b,) + (0,) * len(shape))
    cache3 = cache.reshape(n_pool, page * 4 * G, D)
    out = pl.pallas_call(
        body,
        grid_spec=pltpu.PrefetchScalarGridSpec(
            num_scalar_prefetch=1,
            grid=(Bd,),
            in_specs=[page_spec(j) for j in range(n_pages)]
                     + [per_b((H, D)), per_b((4 * G, D)), per_b((n_win * 2 * G, D)), per_b((2 * G, D)), per_b((1, LANES)),
                        const(pe2), const(w1c), const(w2c), const(ovl)],
            out_specs=per_b((H, D)),
            scratch_shapes=[pltpu.VMEM((G * n_chunk, CMP_FEAT), BF16)],
        ),
        out_shape=jax.ShapeDtypeStruct((Bd, H, D), BF16),
        compiler_params=_cparams("arbitrary"),
        name="nsa_step",
    )(page_table, *([cache3] * n_pages), q.reshape(Bd, H, D), new_rows.reshape(Bd, 4 * G, D),
      win_state.reshape(Bd, n_win * 2 * G, D), win_new.reshape(Bd, 2 * G, D), u_small.reshape(Bd, 1, LANES),
      pe2, w1c, w2c, ovl)
    return out.reshape(Bd, H * D)


def rope_tables(pos):
    half = HEAD_DIM // 2
    inv = ROPE_THETA ** (-jnp.arange(half, dtype=F32) / half)
    ang = pos.astype(F32)[:, None] * inv[None, :]
    cos, sin = jnp.cos(ang), jnp.sin(ang)
    return jnp.concatenate([cos, cos], axis=-1), jnp.concatenate([-sin, sin], axis=-1)


def _split_w_in(w_in):
    H, G = NSA_HEADS, NSA_KV_HEADS
    sizes = (H * HEAD_DIM, 6 * G * HEAD_DIM, 3 * H, GDN_HEADS * (2 * GDN_DK + GDN_DV), GDN_HEADS * GDN_DV, 2 * GDN_HEADS)
    o = np.cumsum((0,) + sizes).tolist()
    K = w_in.shape[0]
    w_qkv = w_in[:, o[0]:o[2]].astype(BF16)
    w_gqkv = w_in[:, o[3]:o[4]].astype(BF16)
    w_z = w_in[:, o[4]:o[5]].astype(BF16)
    w_mg = w_in[:, o[6]:].astype(BF16)
    w_ng = w_in[:, o[2]:o[3]]
    w_ba = w_in[:, o[5]:o[6]]
    nhb = GDN_HEADS // GDN_HB
    ng = w_ng.reshape(K, 3, G, NSA_GROUP).transpose(0, 2, 1, 3).reshape(K, G, 3 * NSA_GROUP)
    pieces = ((GDN_COL_B, w_ba[:, :GDN_HEADS].reshape(K, nhb, GDN_HB)),
              (GDN_COL_A, w_ba[:, GDN_HEADS:].reshape(K, nhb, GDN_HB)), (NSA_COL_G, ng))
    w_small_seq = _place_lanes(pieces, (K, nhb)).reshape(K, nhb * LANES).astype(BF16)
    w_small_step = _place_lanes(((STEP_COL_G, w_ng), (STEP_COL_B, w_ba)), (K,)).astype(BF16)
    return w_qkv, w_gqkv, w_z, w_mg, w_small_seq, w_small_step


def _place_lanes(pieces, lead):
    out, at = [], 0
    for lane, vals in pieces:
        out += [jnp.zeros(lead + (lane - at,), F32), vals.astype(F32)]
        at = lane + vals.shape[-1]
    return jnp.concatenate(out + [jnp.zeros(lead + (LANES - at,), F32)], axis=-1)


def kernel(x_prompt, x_sample, cache_nsa_kv, page_table, state_win_kv, state_gdn, state_gdn_conv, state_ffn_conv, norm_mix, w_in, cmp_pe, cmp_w1, cmp_w2, gdn_conv_w, gdn_a_log, gdn_dt_bias, gdn_norm, w_nsa_out, w_gdn_out, w_o, norm_ffn, w_up, ffn_conv_w, w_down, norm_final):
    B, T, D = x_prompt.shape
    Bd = x_sample.shape[0]
    G = NSA_KV_HEADS
    n_past = page_table.shape[1] * cache_nsa_kv.shape[1]
    F = w_down.shape[0]

    w_qkv, w_gqkv, w_z, w_mg, w_small_seq, w_small_step = _split_w_in(w_in)
    w_nsa_b, w_gdn_b, w_o_b = w_nsa_out.astype(BF16), w_gdn_out.astype(BF16), w_o.astype(BF16)
    w_up_b, w_down_b = w_up.astype(BF16), w_down.astype(BF16)
    pe2, w1c, w2c = compress_params(cmp_pe, cmp_w1, cmp_w2)

    def tail(x, mixed_nsa, mixed_gdn, u_mg, ffn_up, tm):
        mixin = matmul_mix(mixed_nsa, mixed_gdn, w_nsa_b, w_gdn_b, u_mg, tm, 512)
        h = matmul_residual(mixin, w_o_b, x, tm, 512)
        hn = rmsnorm(h, norm_ffn, BF16, min(tm, 256))
        act, extra = ffn_up(hn)
        h2 = matmul_residual(act, w_down_b, h, min(tm, 512), 256)
        return rmsnorm(h2, norm_final, F32, min(tm, 256)), extra

    xp = x_prompt.reshape(B * T, D)
    xn = rmsnorm(xp, norm_mix, BF16, 256)
    tm = 1024
    u_qkv = matmul(xn, w_qkv, tm, 512)
    u_gqkv = matmul(xn, w_gqkv, tm, 512)
    u_z = matmul(xn, w_z, tm, 512)
    u_mg = matmul(xn, w_mg, tm, 512)
    u_small = matmul(xn, w_small_seq, tm, w_small_seq.shape[1])
    cos, sin = rope_tables(jnp.arange(T, dtype=jnp.int32))
    q, kv_rows, win_rows = rope_split(u_qkv, cos, sin, 256)
    cmp_kv = compress_seq(kv_rows, pe2, w1c, w2c, B)
    o_nsa = nsa_seq(q, kv_rows, win_rows, cmp_kv, u_small, B)
    a_rows, d_rows = gdn_param_rows(gdn_a_log, gdn_dt_bias)
    o_gdn, s_p = gdn_seq(u_gqkv, u_z, u_small, gdn_conv_w, a_rows, d_rows, gdn_norm, B, 256)
    s_p = s_p.reshape(B, GDN_HEADS, GDN_DK, GDN_DV)

    def ffn_up_p(hn):
        act, tg, tv = ffn_up_seq(hn, w_up_b, ffn_conv_w, B, tm, 256)
        last = lambda a: a[T // tm - 1::T // tm, SUBLANES - (FFN_CONV - 1):]
        return act, jnp.concatenate([last(tg), last(tv)], axis=-1)

    y_p, ffn_p = tail(xp, o_nsa, o_gdn, u_mg, ffn_up_p, tm)
    n_win_p = min(WINDOW, T)
    kv_p = kv_rows.reshape(B, T, 4, G, HEAD_DIM)
    win_p = win_rows.reshape(B, T, 2, G, HEAD_DIM)[:, T - n_win_p:]
    conv_p = u_gqkv.reshape(B, T, -1)[:, T - (GDN_CONV - 1):]

    xs = x_sample.reshape(Bd, D)
    xn = rmsnorm(xs, norm_mix, BF16, Bd)
    u_qkv = matmul(xn, w_qkv, Bd, 512)
    u_gqkv = matmul(xn, w_gqkv, Bd, 512)
    u_z = matmul(xn, w_z, Bd, 512)
    u_mg = matmul(xn, w_mg, Bd, 512)
    u_small = matmul(xn, w_small_step, Bd, LANES)
    cos, sin = rope_tables(jnp.full((Bd,), n_past, jnp.int32))
    q, new_rows, win_new = rope_split(u_qkv, cos, sin, Bd)
    o_nsa = nsa_step(q, new_rows, win_new, cache_nsa_kv, page_table, state_win_kv, u_small, pe2, w1c, w2c)
    o_gdn, s_s = gdn_step(u_gqkv, u_z, u_small, state_gdn_conv, state_gdn, gdn_conv_w, gdn_a_log, gdn_dt_bias, gdn_norm)

    def ffn_up_s(hn):
        act, ug, uv = ffn_up_step(hn, w_up_b, ffn_conv_w, state_ffn_conv, 256)
        return act, jnp.concatenate([ug, uv], axis=-1)

    y_s, up_new = tail(xs, o_nsa, o_gdn, u_mg, ffn_up_s, Bd)
    kv_s = new_rows.reshape(Bd, 1, 4, G, HEAD_DIM)
    win_s = jnp.concatenate([state_win_kv, win_new.reshape(Bd, 1, 2, G, HEAD_DIM)], axis=1)
    win_s = win_s[:, win_s.shape[1] - min(WINDOW, win_s.shape[1]):]
    conv_s = jnp.concatenate([state_gdn_conv[:, 1:], u_gqkv[:, None]], axis=1)
    ffn_s = jnp.concatenate([state_ffn_conv[:, 1:], up_new[:, None]], axis=1)

    return (y_p.reshape(B, T, D), y_s.reshape(Bd, 1, D), kv_p, win_p, s_p, conv_p, ffn_p,
            kv_s, win_s, s_s, conv_s, ffn_s)
```

```python
import functools
import math

import jax
import jax.numpy as jnp
import numpy as np
from jax import lax
from jax.experimental import pallas as pl
from jax.experimental.pallas import tpu as pltpu

F32 = jnp.float32
BF16 = jnp.bfloat16
HI = lax.Precision.HIGHEST

LANES = 128
SUBLANES = 8
VMEM_LIMIT_BYTES = 56 * 1024 * 1024

HEAD_DIM = 128
NSA_HEADS = 16
NSA_KV_HEADS = 4
NSA_GROUP = NSA_HEADS // NSA_KV_HEADS
CMP_LEN = 32
CMP_STRIDE = 16
CMP_HIDDEN = 256
SLC_LEN = 64
SLC_TOP_N = 16
SLC_LOCAL = 2
WINDOW = 512
GDN_HEADS = 16
GDN_DK = 128
GDN_DV = 128
GDN_CONV = 4
GDN_CHUNK = 64
FFN_CONV = 3
ROPE_THETA = 10000.0
EPS = 1e-6


def _cparams(*sem):
    return pltpu.CompilerParams(dimension_semantics=sem, vmem_limit_bytes=VMEM_LIMIT_BYTES)


def _dot(a, b, precision=None):
    return jnp.dot(a, b, preferred_element_type=F32, precision=precision)


def _dot_nt(a, b, precision=None):
    return lax.dot_general(a, b, (((1,), (1,)), ((), ())), preferred_element_type=F32, precision=precision)


def _dot_tn(a, b, precision=None):
    return lax.dot_general(a, b, (((0,), (0,)), ((), ())), preferred_element_type=F32, precision=precision)


def _sigmoid(x):
    return 1.0 / (1.0 + jnp.exp(-x))


def _silu(x):
    return x * _sigmoid(x)


def _rmsnorm_body(x_ref, w_ref, o_ref):
    x = x_ref[...]
    y = x * lax.rsqrt(jnp.mean(x * x, axis=-1, keepdims=True) + EPS)
    o_ref[...] = (y * w_ref[...]).astype(o_ref.dtype)


def rmsnorm(x, w, out_dtype, tm):
    M, D = x.shape
    return pl.pallas_call(
        _rmsnorm_body,
        grid=(M // tm,),
        in_specs=[pl.BlockSpec((tm, D), lambda i: (i, 0)), pl.BlockSpec((1, D), lambda i: (0, 0))],
        out_specs=pl.BlockSpec((tm, D), lambda i: (i, 0)),
        out_shape=jax.ShapeDtypeStruct((M, D), out_dtype),
        compiler_params=_cparams("parallel"),
        name="rmsnorm",
    )(x, w.reshape(1, D))


def _mm_body(x_ref, w_ref, o_ref):
    o_ref[...] = _dot(x_ref[...], w_ref[...]).astype(o_ref.dtype)


def matmul(x, w, tm, tn, out_dtype=F32):
    M, K = x.shape
    N = w.shape[1]
    return pl.pallas_call(
        _mm_body,
        grid=(M // tm, N // tn),
        in_specs=[pl.BlockSpec((tm, K), lambda i, j: (i, 0)), pl.BlockSpec((K, tn), lambda i, j: (0, j))],
        out_specs=pl.BlockSpec((tm, tn), lambda i, j: (i, j)),
        out_shape=jax.ShapeDtypeStruct((M, N), out_dtype),
        compiler_params=_cparams("parallel", "arbitrary"),
        name="matmul",
    )(x, w)


def _mm_res_body(x_ref, w_ref, r_ref, o_ref):
    o_ref[...] = r_ref[...] + _dot(x_ref[...], w_ref[...])


def matmul_residual(x, w, res, tm, tn):
    M, K = x.shape
    N = w.shape[1]
    return pl.pallas_call(
        _mm_res_body,
        grid=(M // tm, N // tn),
        in_specs=[pl.BlockSpec((tm, K), lambda i, j: (i, 0)), pl.BlockSpec((K, tn), lambda i, j: (0, j)),
                  pl.BlockSpec((tm, tn), lambda i, j: (i, j))],
        out_specs=pl.BlockSpec((tm, tn), lambda i, j: (i, j)),
        out_shape=jax.ShapeDtypeStruct((M, N), F32),
        compiler_params=_cparams("parallel", "arbitrary"),
        name="matmul_residual",
    )(x, w, res)


def _mm_mix_body(a_ref, b_ref, wa_ref, wb_ref, ga_ref, gb_ref, o_ref):
    ua = _dot(a_ref[...], wa_ref[...])
    ub = _dot(b_ref[...], wb_ref[...])
    o_ref[...] = (_sigmoid(ga_ref[...]) * ua + _sigmoid(gb_ref[...]) * ub).astype(o_ref.dtype)


def matmul_mix(a, b, wa, wb, gates, tm, tn):
    M, K = a.shape
    N = wa.shape[1]
    nj = N // tn
    return pl.pallas_call(
        _mm_mix_body,
        grid=(M // tm, nj),
        in_specs=[pl.BlockSpec((tm, K), lambda i, j: (i, 0)), pl.BlockSpec((tm, K), lambda i, j: (i, 0)),
                  pl.BlockSpec((K, tn), lambda i, j: (0, j)), pl.BlockSpec((K, tn), lambda i, j: (0, j)),
                  pl.BlockSpec((tm, tn), lambda i, j: (i, j)), pl.BlockSpec((tm, tn), lambda i, j: (i, j + nj))],
        out_specs=pl.BlockSpec((tm, tn), lambda i, j: (i, j)),
        out_shape=jax.ShapeDtypeStruct((M, N), BF16),
        compiler_params=_cparams("parallel", "arbitrary"),
        name="matmul_mix",
    )(a, b, wa, wb, gates, gates)


def _shift_rows(x, hist, sh, row):
    s = pltpu.roll(x, sh, axis=0)
    for r in range(sh):
        s = jnp.where(row == r, hist[SUBLANES - sh + r:SUBLANES - sh + r + 1], s)
    return s


def _ffn_up_seq_body(x_ref, wg_ref, wv_ref, cg_ref, cv_ref, act_ref, tg_ref, tv_ref, hg_ref, hv_ref, *, tiles_per_seq):
    i = pl.program_id(0)
    j = pl.program_id(1)
    first = (i % tiles_per_seq) == 0
    x = x_ref[...]

    def branch(w_ref, c_ref, hist_ref, tail_ref):
        up = _dot(x, w_ref[...])
        tm = up.shape[0]
        hist = jnp.where(first, 0.0, hist_ref[j])
        row = lax.broadcasted_iota(jnp.int32, up.shape, 0)
        c = c_ref[...]
        out = up * c[2:3] + _shift_rows(up, hist, 2, row) * c[0:1] + _shift_rows(up, hist, 1, row) * c[1:2]
        tail = up[tm - SUBLANES:tm]
        hist_ref[j] = tail
        tail_ref[0] = tail
        return out

    gate = branch(wg_ref, cg_ref, hg_ref, tg_ref)
    val = branch(wv_ref, cv_ref, hv_ref, tv_ref)
    act_ref[...] = (_silu(gate) * val).astype(act_ref.dtype)


def ffn_up_seq(x, w_up, conv_w, n_seq, tm, tn):
    M, K = x.shape
    F = w_up.shape[1] // 2
    nj = F // tn
    tiles_per_seq = M // n_seq // tm
    body = functools.partial(_ffn_up_seq_body, tiles_per_seq=tiles_per_seq)
    return pl.pallas_call(
        body,
        grid=(M // tm, nj),
        in_specs=[pl.BlockSpec((tm, K), lambda i, j: (i, 0)),
                  pl.BlockSpec((K, tn), lambda i, j: (0, j)), pl.BlockSpec((K, tn), lambda i, j: (0, j + nj)),
                  pl.BlockSpec((FFN_CONV, tn), lambda i, j: (0, j)), pl.BlockSpec((FFN_CONV, tn), lambda i, j: (0, j + nj))],
        out_specs=[pl.BlockSpec((tm, tn), lambda i, j: (i, j)),
                   pl.BlockSpec((1, SUBLANES, tn), lambda i, j: (i, 0, j)),
                   pl.BlockSpec((1, SUBLANES, tn), lambda i, j: (i, 0, j))],
        out_shape=[jax.ShapeDtypeStruct((M, F), BF16),
                   jax.ShapeDtypeStruct((M // tm, SUBLANES, F), F32), jax.ShapeDtypeStruct((M // tm, SUBLANES, F), F32)],
        scratch_shapes=[pltpu.VMEM((nj, SUBLANES, tn), F32), pltpu.VMEM((nj, SUBLANES, tn), F32)],
        compiler_params=_cparams("arbitrary", "arbitrary"),
        name="ffn_up_seq",
    )(x, w_up, w_up, conv_w, conv_w)


def _ffn_up_step_body(x_ref, wg_ref, wv_ref, cg_ref, cv_ref, g0_ref, g1_ref, v0_ref, v1_ref, act_ref, ug_ref, uv_ref):
    x = x_ref[...]

    def branch(w_ref, c_ref, h0_ref, h1_ref, up_ref):
        up = _dot(x, w_ref[...])
        up_ref[...] = up
        c = c_ref[...]
        return up * c[2:3] + h0_ref[...] * c[0:1] + h1_ref[...] * c[1:2]

    gate = branch(wg_ref, cg_ref, g0_ref, g1_ref, ug_ref)
    val = branch(wv_ref, cv_ref, v0_ref, v1_ref, uv_ref)
    act_ref[...] = (_silu(gate) * val).astype(act_ref.dtype)


def ffn_up_step(x, w_up, conv_w, hist, tn):
    M, K = x.shape
    F2 = w_up.shape[1]
    F = F2 // 2
    nj = F // tn
    h2 = hist.reshape(M, 2 * F2)
    return pl.pallas_call(
        _ffn_up_step_body,
        grid=(nj,),
        in_specs=[pl.BlockSpec((M, K), lambda j: (0, 0)),
                  pl.BlockSpec((K, tn), lambda j: (0, j)), pl.BlockSpec((K, tn), lambda j: (0, j + nj)),
                  pl.BlockSpec((FFN_CONV, tn), lambda j: (0, j)), pl.BlockSpec((FFN_CONV, tn), lambda j: (0, j + nj)),
                  pl.BlockSpec((M, tn), lambda j: (0, j)), pl.BlockSpec((M, tn), lambda j: (0, j + 2 * nj)),
                  pl.BlockSpec((M, tn), lambda j: (0, j + nj)), pl.BlockSpec((M, tn), lambda j: (0, j + 3 * nj))],
        out_specs=[pl.BlockSpec((M, tn), lambda j: (0, j))] * 3,
        out_shape=[jax.ShapeDtypeStruct((M, F), BF16), jax.ShapeDtypeStruct((M, F), F32), jax.ShapeDtypeStruct((M, F), F32)],
        compiler_params=_cparams("arbitrary"),
        name="ffn_up_step",
    )(x, w_up, w_up, conv_w, conv_w, h2, h2, h2, h2)


def _rope_body(u_ref, cos_ref, sin_ref, q_ref, kv_ref, win_ref):
    cos = cos_ref[...]
    sin = sin_ref[...]
    scale = HEAD_DIM ** -0.5

    def rot(x):
        return x * cos + pltpu.roll(x, HEAD_DIM // 2, axis=1) * sin

    for h in range(NSA_HEADS):
        sl = slice(h * HEAD_DIM, (h + 1) * HEAD_DIM)
        q_ref[:, sl] = (rot(u_ref[:, sl]) * scale).astype(q_ref.dtype)
    base = NSA_HEADS * HEAD_DIM
    n_glob = 4 * NSA_KV_HEADS
    for slot in range(6 * NSA_KV_HEADS):
        x = u_ref[:, base + slot * HEAD_DIM:base + (slot + 1) * HEAD_DIM]
        if (slot // NSA_KV_HEADS) % 2 == 0:
            x = rot(x)
        if slot < n_glob:
            kv_ref[:, slot * HEAD_DIM:(slot + 1) * HEAD_DIM] = x
        else:
            win_ref[:, (slot - n_glob) * HEAD_DIM:(slot - n_glob + 1) * HEAD_DIM] = x


def rope_split(u, cos, sin, tm):
    M, W = u.shape
    nt = cos.shape[0] // tm
    nq = NSA_HEADS * HEAD_DIM
    ng = 4 * NSA_KV_HEADS * HEAD_DIM
    nw = 2 * NSA_KV_HEADS * HEAD_DIM
    return pl.pallas_call(
        _rope_body,
        grid=(M // tm,),
        in_specs=[pl.BlockSpec((tm, W), lambda i: (i, 0)),
                  pl.BlockSpec((tm, HEAD_DIM), lambda i: (i % nt, 0)), pl.BlockSpec((tm, HEAD_DIM), lambda i: (i % nt, 0))],
        out_specs=[pl.BlockSpec((tm, nq), lambda i: (i, 0)), pl.BlockSpec((tm, ng), lambda i: (i, 0)),
                   pl.BlockSpec((tm, nw), lambda i: (i, 0))],
        out_shape=[jax.ShapeDtypeStruct((M, nq), BF16), jax.ShapeDtypeStruct((M, ng), F32), jax.ShapeDtypeStruct((M, nw), F32)],
        compiler_params=_cparams("parallel"),
        name="rope_split",
    )(u, cos, sin)


GDN_HB = 4
GDN_COL_B = 0
GDN_COL_A = 8
NSA_COL_G = 16


def _softplus(x):
    return jnp.maximum(x, 0.0) + jnp.log1p(jnp.exp(-jnp.abs(x)))


def _l2norm(x):
    return x * lax.rsqrt(jnp.sum(x * x, axis=-1, keepdims=True) + EPS)


def _solve_unit_lower(a, r, order):
    steps = int(math.log2(order))
    m, n = a.shape[1], r.shape[1]
    for i in range(steps):
        a_b = a.astype(BF16)
        r_hi, r_lo = _split_bf16(r)
        last = i == steps - 1
        prod = _dot(a_b, jnp.concatenate(([] if last else [a_b]) + [r_hi, r_lo], axis=1))
        if not last:
            a, prod = prod[:, :m], prod[:, m:]
        r = r + (prod[:, :n] + prod[:, n:])
    return r


def _block_diag(blocks):
    z = jnp.zeros_like(blocks[0])
    nb = len(blocks)
    return jnp.concatenate([jnp.concatenate([blocks[h] if j == h else z for j in range(nb)], axis=1)
                            for h in range(nb)], axis=0)


def _gdn_chunk_group(q, k, v, beta, gc, gcr, s):
    nh = len(q)
    C, d = v[0].shape
    W = nh * C
    stack = lambda xs: jnp.concatenate(xs, axis=0)
    bd = lambda xs: _block_diag([x.astype(BF16) for x in xs])
    ri = lax.broadcasted_iota(jnp.int32, (W, W), 0)
    ci = lax.broadcasted_iota(jnp.int32, (W, W), 1)
    same = (ri // C) == (ci // C)
    gcs = stack(gc)
    dmat = jnp.where(same & (ri >= ci), jnp.exp(jnp.minimum(gcs - jnp.concatenate(gcr, axis=1), 0.0)), 0.0)
    eg = [jnp.exp(g) for g in gc]
    kb = [k[h] * beta[h] for h in range(nh)]
    xk = _dot_nt(jnp.concatenate([bd(kb), bd(q)], axis=0), bd(k))
    a = jnp.where(same & (ri > ci), -(xk[:W] * dmat), 0.0)
    r = jnp.concatenate([stack([v[h] * beta[h] for h in range(nh)]), stack([kb[h] * eg[h] for h in range(nh)])], axis=1)
    w = _solve_unit_lower(a, r, C)
    wk = [w[h * C:(h + 1) * C, d:] for h in range(nh)]
    xs = _dot(jnp.concatenate([bd(wk), bd([q[h] * eg[h] for h in range(nh)])], axis=0), s.astype(BF16))
    v_new = (w[:, :d] - xs[:W]).astype(BF16)
    o = xs[W:] + _dot((xk[W:] * dmat).astype(BF16), v_new)
    g_last = [g[C - 1:C, :] for g in gc]
    kd = [k[h] * jnp.exp(g_last[h] - gc[h]) for h in range(nh)]
    keep = stack([jnp.broadcast_to(jnp.exp(g_last[h]), (d, 1)) for h in range(nh)])
    s_new = s * keep + _dot_tn(bd(kd), v_new)
    return o, s_new


def _gdn_seq_body(q_ref, k_ref, v_ref, z_ref, ba_ref, cq_ref, ck_ref, cv_ref, alog_ref, dtb_ref, nw_ref,
                  o_ref, s_ref, tq_ref, tk_ref, tv_ref, qs_ref, ks_ref, vs_ref, gs_ref, bs_ref):
    t_idx = pl.program_id(2)
    tt = q_ref.shape[0]
    hb = q_ref.shape[1] // GDN_DK
    C = GDN_CHUNK

    @pl.when(t_idx == 0)
    def _():
        s_ref[...] = jnp.zeros_like(s_ref)
        tq_ref[...] = jnp.zeros_like(tq_ref)
        tk_ref[...] = jnp.zeros_like(tk_ref)
        tv_ref[...] = jnp.zeros_like(tv_ref)

    def conv(x_ref, c_ref, tail_ref):
        x = x_ref[...]
        hist = tail_ref[...]
        row = lax.broadcasted_iota(jnp.int32, x.shape, 0)
        c = c_ref[...]
        out = x * c[GDN_CONV - 1:GDN_CONV]
        for j in range(GDN_CONV - 1):
            out = out + _shift_rows(x, hist, GDN_CONV - 1 - j, row) * c[j:j + 1]
        tail_ref[...] = x[tt - SUBLANES:tt]
        return _silu(out)

    qa = conv(q_ref, cq_ref, tq_ref)
    ka = conv(k_ref, ck_ref, tk_ref)
    vs_ref[...] = conv(v_ref, cv_ref, tv_ref)
    for h in range(hb):
        sl = slice(h * GDN_DK, (h + 1) * GDN_DK)
        qs_ref[:, sl] = _l2norm(qa[:, sl]) * GDN_DK ** -0.5
        ks_ref[:, sl] = _l2norm(ka[:, sl])
    ba = ba_ref[...]
    bs_ref[...] = _sigmoid(ba)
    gs_ref[...] = -jnp.exp(alog_ref[0, 0:1, :]) * _softplus(ba + dtb_ref[0, 0:1, :])

    tri = (lax.broadcasted_iota(jnp.int32, (C, C), 0) >= lax.broadcasted_iota(jnp.int32, (C, C), 1)).astype(F32)
    sel = (lax.broadcasted_iota(jnp.int32, (2 * SUBLANES, LANES), 0)
           == lax.broadcasted_iota(jnp.int32, (2 * SUBLANES, LANES), 1)).astype(F32)
    nw = nw_ref[...]

    for c in range(tt // C):
        rows = slice(c * C, (c + 1) * C)
        gc_all = _dot(tri, gs_ref[rows, :], HI)
        gc_t = _dot_nt(sel, gc_all, HI)
        beta_all = bs_ref[rows, :]
        heads = [slice(h * GDN_DK, (h + 1) * GDN_DK) for h in range(hb)]
        o, s_new = _gdn_chunk_group(
            [qs_ref[rows, sl] for sl in heads], [ks_ref[rows, sl] for sl in heads], [vs_ref[rows, sl] for sl in heads],
            [beta_all[:, GDN_COL_B + h:GDN_COL_B + h + 1] for h in range(hb)],
            [gc_all[:, GDN_COL_A + h:GDN_COL_A + h + 1] for h in range(hb)],
            [gc_t[GDN_COL_A + h:GDN_COL_A + h + 1, :] for h in range(hb)], s_ref[0])
        s_ref[0] = s_new
        on = o * lax.rsqrt(jnp.mean(o * o, axis=-1, keepdims=True) + EPS) * nw
        for h, sl in enumerate(heads):
            o_ref[rows, sl] = (on[h * C:(h + 1) * C] * _silu(z_ref[rows, sl])).astype(o_ref.dtype)


def gdn_seq(u_qkv, u_z, u_small, conv_w, alog_rows, dtb_rows, norm_w, n_seq, tt):
    M = u_qkv.shape[0]
    T = M // n_seq
    nt = T // tt
    hb = GDN_HB
    nhb = GDN_HEADS // hb
    wb = hb * GDN_DK
    row_blk = lambda off: pl.BlockSpec((tt, wb), lambda b, h, t: (b * nt + t, h + off))
    cw_blk = lambda off: pl.BlockSpec((GDN_CONV, wb), lambda b, h, t: (0, h + off))
    par_blk = pl.BlockSpec((1, SUBLANES, LANES), lambda b, h, t: (h, 0, 0))
    return pl.pallas_call(
        _gdn_seq_body,
        grid=(n_seq, nhb, nt),
        in_specs=[row_blk(0), row_blk(nhb), row_blk(2 * nhb), row_blk(0),
                  pl.BlockSpec((tt, LANES), lambda b, h, t: (b * nt + t, h)),
                  cw_blk(0), cw_blk(nhb), cw_blk(2 * nhb), par_blk, par_blk,
                  pl.BlockSpec((1, GDN_DV), lambda b, h, t: (0, 0))],
        out_specs=[row_blk(0), pl.BlockSpec((1, hb * GDN_DK, GDN_DV), lambda b, h, t: (b, h, 0))],
        out_shape=[jax.ShapeDtypeStruct((M, GDN_HEADS * GDN_DV), BF16),
                   jax.ShapeDtypeStruct((n_seq, GDN_HEADS * GDN_DK, GDN_DV), F32)],
        scratch_shapes=[pltpu.VMEM((SUBLANES, wb), F32)] * 3 + [pltpu.VMEM((tt, wb), F32)] * 3
                       + [pltpu.VMEM((tt, LANES), F32)] * 2,
        compiler_params=_cparams("parallel", "parallel", "arbitrary"),
        name="gdn_seq",
    )(u_qkv, u_qkv, u_qkv, u_z, u_small, conv_w, conv_w, conv_w, alog_rows, dtb_rows, norm_w.reshape(1, GDN_DV))


def gdn_param_rows(gdn_a_log, gdn_dt_bias):
    nhb = GDN_HEADS // GDN_HB

    def rows(p):
        r = _place_lanes(((GDN_COL_A, p.reshape(nhb, GDN_HB)),), (nhb,))
        return jnp.broadcast_to(r[:, None, :], (nhb, SUBLANES, LANES))

    return rows(gdn_a_log), rows(gdn_dt_bias)


CMP_RATIO = CMP_LEN // CMP_STRIDE
CMP_FEAT = CMP_STRIDE * HEAD_DIM
MASKED = -1e30


def _compress_mlp(part, pe_ref, w1_ref, w2_ref, chunk_step=1):
    n = part.shape[0]
    pe_part = _dot(pe_ref[...], w1_ref[...])
    hid0 = pe_part[0:1, :CMP_HIDDEN] + pe_part[1:2, CMP_HIDDEN:]
    hid = hid0 + part[:, :CMP_HIDDEN]
    hid = hid + pltpu.roll(part[:, CMP_HIDDEN:], n - chunk_step, axis=0)
    return _dot(_silu(hid).astype(BF16), w2_ref[...])


def _compress_seq_body(x_ref, pe_ref, w1_ref, w2_ref, o_ref, xc_ref):
    n = o_ref.shape[3]
    for s in range(CMP_STRIDE):
        xc_ref[:, s * HEAD_DIM:(s + 1) * HEAD_DIM] = x_ref[pl.ds(s, n, stride=CMP_STRIDE), :].astype(BF16)
    part = _dot(xc_ref[...], w1_ref[0])
    o_ref[0, 0, 0] = _compress_mlp(part, pe_ref.at[0], w1_ref.at[0], w2_ref.at[0])


def compress_seq(kv_rows, pe2, w1c, w2c, n_seq):
    M = kv_rows.shape[0]
    T = M // n_seq
    n = T // CMP_STRIDE
    G = NSA_KV_HEADS
    return pl.pallas_call(
        _compress_seq_body,
        grid=(n_seq, 2, G),
        in_specs=[pl.BlockSpec((T, HEAD_DIM), lambda b, c, g: (b, c * G + g)),
                  pl.BlockSpec((1, SUBLANES, CMP_FEAT), lambda b, c, g: (c, 0, 0)),
                  pl.BlockSpec((1, CMP_FEAT, 2 * CMP_HIDDEN), lambda b, c, g: (c, 0, 0)),
                  pl.BlockSpec((1, CMP_HIDDEN, HEAD_DIM), lambda b, c, g: (c, 0, 0))],
        out_specs=pl.BlockSpec((1, 1, 1, n, HEAD_DIM), lambda b, c, g: (b, c, g, 0, 0)),
        out_shape=jax.ShapeDtypeStruct((n_seq, 2, G, n, HEAD_DIM), F32),
        scratch_shapes=[pltpu.VMEM((n, CMP_FEAT), BF16)],
        compiler_params=_cparams("parallel", "parallel", "parallel"),
        name="compress_seq",
    )(kv_rows, pe2, w1c, w2c)


def compress_params(cmp_pe, cmp_w1, cmp_w2):
    w1 = cmp_w1.reshape(2, CMP_RATIO, CMP_FEAT, CMP_HIDDEN)
    w1c = jnp.concatenate([w1[:, r] for r in range(CMP_RATIO)], axis=-1).astype(BF16)
    pe = cmp_pe.reshape(2, CMP_RATIO, CMP_FEAT)
    pe2 = jnp.zeros((2, SUBLANES, CMP_FEAT), F32).at[:, :CMP_RATIO].set(pe).astype(BF16)
    return pe2, w1c, cmp_w2.astype(BF16)


def _masked_softmax(s, mask, axis=-1):
    s = jnp.where(mask, s, -jnp.inf)
    m = jnp.max(s, axis=axis, keepdims=True)
    m = jnp.where(m > -jnp.inf, m, 0.0)
    e = jnp.exp(s - m)
    return e / jnp.maximum(jnp.sum(e, axis=axis, keepdims=True), 1e-30)


def _masked_softmax_heads(s, mask, n_heads):
    t = mask.shape[0]
    return jnp.concatenate([_masked_softmax(s[h * t:(h + 1) * t], mask) for h in range(n_heads)], axis=0)


def _split_bf16(x):
    hi = x.astype(BF16)
    return hi, (x - hi.astype(F32)).astype(BF16)


def _select_blocks_t(imp_t, pos_row, n_blocks):
    J = imp_t.shape[0]
    j = lax.broadcasted_iota(jnp.int32, imp_t.shape, 0)
    cur = pos_row // SLC_LEN
    forced = (j == 0) | ((j <= cur) & (j > cur - SLC_LOCAL))
    score = jnp.where(j * SLC_LEN > pos_row, -jnp.inf, jnp.where(forced, jnp.inf, imp_t))
    score = jnp.where(j < n_blocks, score, -jnp.inf)
    tiles = [score[r:r + SUBLANES] for r in range(0, J, SUBLANES)]
    ranks = [jnp.zeros(t.shape, F32) for t in tiles]
    for jp in range(n_blocks):
        row = score[jp:jp + 1, :]
        for i, t in enumerate(tiles):
            lo_j = i * SUBLANES
            ge, gt = jnp.where(row >= t, 1.0, 0.0), jnp.where(row > t, 1.0, 0.0)
            if lo_j > jp:
                ahead = ge
            elif lo_j + SUBLANES - 1 <= jp:
                ahead = gt
            else:
                ahead = jnp.where(lax.broadcasted_iota(jnp.int32, t.shape, 0) + lo_j > jp, ge, gt)
            ranks[i] = ranks[i] + ahead
    rank = jnp.concatenate(ranks, axis=0)
    keep = (rank < float(min(SLC_TOP_N, n_blocks))) & (j < n_blocks)
    return jnp.where(keep, 1.0, 0.0)


def _nsa_seq_body(q_ref, kc_ref, vc_ref, ks_ref, vs_ref, kw_ref, vw_ref, gl_ref, ovl_ref, o_ref, *, kblk):
    i = pl.program_id(2)
    tq = q_ref.shape[0]
    T = ks_ref.shape[0]
    H = NSA_GROUP
    D = HEAD_DIM
    n_slc = T // SLC_LEN
    qb = q_ref[...]
    q4 = jnp.concatenate([qb[:, h * D:(h + 1) * D] for h in range(H)], axis=0)
    t0 = i * tq
    pos1 = t0 + lax.broadcasted_iota(jnp.int32, (tq, 1), 0)

    kc = kc_ref[0, 0, 0].astype(BF16)
    vc = vc_ref[0, 0, 0].astype(BF16)
    n_c = kc.shape[0]
    s = _dot_nt(q4, kc)
    cmp_end = lax.broadcasted_iota(jnp.int32, (1, n_c), 1) * CMP_STRIDE + (CMP_LEN - 1)
    p = _masked_softmax_heads(s, cmp_end <= pos1, H)
    o_c = _dot(p.astype(BF16), vc)

    psum = p[0:tq]
    for h in range(1, H):
        psum = psum + p[h * tq:(h + 1) * tq]
    p_hi, p_lo = _split_bf16(psum)
    ovl = ovl_ref[...]
    imp_t = _dot_nt(ovl, p_hi) + _dot_nt(ovl, p_lo)
    pos_row = t0 + lax.broadcasted_iota(jnp.int32, (1, tq), 1)
    sel_t = _select_blocks_t(imp_t, pos_row, n_slc)
    eye = (lax.broadcasted_iota(jnp.int32, (tq, tq), 0) == lax.broadcasted_iota(jnp.int32, (tq, tq), 1))
    sel = _dot_nt(jnp.where(eye, 1.0, 0.0).astype(BF16), sel_t.astype(BF16)).astype(BF16)

    n_kb = (t0 + tq + kblk - 1) // kblk

    def kv_step(kb, carry):
        ms, ls, accs = carry
        k0 = pl.multiple_of(kb * kblk, kblk)
        kk = ks_ref[pl.ds(k0, kblk), :].astype(BF16)
        vv = vs_ref[pl.ds(k0, kblk), :].astype(BF16)
        s2 = _dot_nt(q4, kk)
        kpos = k0 + lax.broadcasted_iota(jnp.int32, (1, kblk), 1)
        expand = (lax.broadcasted_iota(jnp.int32, (n_slc, kblk), 0) == kpos // SLC_LEN)
        chosen = _dot(sel, jnp.where(expand, 1.0, 0.0).astype(BF16))
        bias = jnp.where((chosen > 0.5) & (kpos <= pos1), 0.0, MASKED)
        m_out, l_out, alphas, es = [], [], [], []
        for h in range(H):
            sh = s2[h * tq:(h + 1) * tq] + bias
            m_new = jnp.maximum(ms[h], jnp.max(sh, axis=-1, keepdims=True))
            alpha = jnp.exp(ms[h] - m_new)
            e = jnp.exp(sh - m_new)
            m_out.append(m_new)
            l_out.append(alpha * ls[h] + jnp.sum(e, axis=-1, keepdims=True))
            alphas.append(alpha)
            es.append(e.astype(BF16))
        pv = _dot(jnp.concatenate(es, axis=0), vv)
        acc_out = [alphas[h] * accs[h] + pv[h * tq:(h + 1) * tq] for h in range(H)]
        return tuple(m_out), tuple(l_out), tuple(acc_out)

    init = (tuple(jnp.full((tq, 1), MASKED, F32) for _ in range(H)), tuple(jnp.zeros((tq, 1), F32) for _ in range(H)),
            tuple(jnp.zeros((tq, D), F32) for _ in range(H)))
    ms, ls, accs = lax.fori_loop(0, n_kb, kv_step, init)
    o_s = jnp.concatenate([accs[h] / jnp.maximum(ls[h], 1e-30) for h in range(H)], axis=0)

    band = WINDOW + tq
    w0 = pl.multiple_of(jnp.maximum(t0 - WINDOW, 0), tq)
    kw = kw_ref[pl.ds(w0, band), :].astype(BF16)
    vw = vw_ref[pl.ds(w0, band), :].astype(BF16)
    sw = _dot_nt(q4, kw)
    diff = pos1 - (w0 + lax.broadcasted_iota(jnp.int32, (1, band), 1))
    pw = _masked_softmax_heads(sw, (diff >= 0) & (diff < WINDOW), H)
    o_w = _dot(pw.astype(BF16), vw)

    gates = _sigmoid(gl_ref[...])
    for h in range(H):
        rows = slice(h * tq, (h + 1) * tq)
        g = lambda br: gates[:, NSA_COL_G + H * br + h:NSA_COL_G + H * br + h + 1]
        o_ref[:, h * D:(h + 1) * D] = (g(0) * o_c[rows] + g(1) * o_s[rows] + g(2) * o_w[rows]).astype(o_ref.dtype)


def _overlap_t(n_cmp_rows, n_slc):
    start = np.arange(n_cmp_rows)[None, :] * CMP_STRIDE
    blk = np.arange(n_slc)[:, None] * SLC_LEN
    return ((start < blk + SLC_LEN) & (start + CMP_LEN > blk)).astype(np.float32)


def nsa_seq(q, kv_rows, win_rows, cmp_kv, u_small, n_seq, tq=128, kblk=512):
    M = q.shape[0]
    T = M // n_seq
    nt = T // tq
    G = NSA_KV_HEADS
    gw = NSA_GROUP * HEAD_DIM
    n_c = cmp_kv.shape[3]
    ovl = jnp.asarray(_overlap_t(n_c, T // SLC_LEN), BF16)
    body = functools.partial(_nsa_seq_body, kblk=kblk)
    seq_blk = lambda off: pl.BlockSpec((T, HEAD_DIM), lambda b, g, i: (b, g + off))
    cmp_blk = lambda c: pl.BlockSpec((1, 1, 1, n_c, HEAD_DIM), lambda b, g, i: (b, c, g, 0, 0))
    return pl.pallas_call(
        body,
        grid=(n_seq, G, nt),
        in_specs=[pl.BlockSpec((tq, gw), lambda b, g, i: (b * nt + i, g)),
                  cmp_blk(0), cmp_blk(1), seq_blk(2 * G), seq_blk(3 * G), seq_blk(0), seq_blk(G),
                  pl.BlockSpec((tq, LANES), lambda b, g, i: (b * nt + i, g)),
                  pl.BlockSpec(ovl.shape, lambda b, g, i: (0, 0))],
        out_specs=pl.BlockSpec((tq, gw), lambda b, g, i: (b * nt + i, g)),
        out_shape=jax.ShapeDtypeStruct((M, NSA_HEADS * HEAD_DIM), BF16),
        compiler_params=_cparams("parallel", "parallel", "arbitrary"),
        name="nsa_seq",
    )(q, cmp_kv, cmp_kv, kv_rows, kv_rows, win_rows, win_rows, u_small, ovl)


STEP_COL_G = 0
STEP_COL_B = 3 * NSA_HEADS
STEP_COL_A = STEP_COL_B + GDN_HEADS


def _gdn_step_prep_body(x_ref, h_ref, c_ref, us_ref, alog_ref, dtb_ref, q_ref, k_ref, v_ref, b_ref, e_ref):
    W = x_ref.shape[1]
    c = c_ref[...]
    x = x_ref[...] * c[GDN_CONV - 1:GDN_CONV]
    for j in range(GDN_CONV - 1):
        x = x + h_ref[:, j * W:(j + 1) * W] * c[j:j + 1]
    x = _silu(x)
    us = us_ref[...]
    beta = _sigmoid(us)
    eg = jnp.exp(-jnp.exp(alog_ref[...]) * _softplus(us + dtb_ref[...]))
    nq = GDN_HEADS * GDN_DK
    for h in range(GDN_HEADS):
        sl = slice(h * GDN_DK, (h + 1) * GDN_DK)
        q_ref[:, sl] = _l2norm(x[:, sl]) * GDN_DK ** -0.5
        k_ref[:, sl] = _l2norm(x[:, nq + h * GDN_DK:nq + (h + 1) * GDN_DK])
        b_ref[:, sl] = jnp.broadcast_to(beta[:, STEP_COL_B + h:STEP_COL_B + h + 1], (x.shape[0], GDN_DK))
        e_ref[:, sl] = jnp.broadcast_to(eg[:, STEP_COL_A + h:STEP_COL_A + h + 1], (x.shape[0], GDN_DK))
    v_ref[...] = x[:, 2 * nq:]


def _gdn_step_body(q_ref, k_ref, v_ref, b_ref, e_ref, z_ref, nw_ref, s_ref, o_ref, so_ref):
    bb = q_ref.shape[0]
    eye = jnp.where(lax.broadcasted_iota(jnp.int32, (GDN_DK, GDN_DK), 0)
                    == lax.broadcasted_iota(jnp.int32, (GDN_DK, GDN_DK), 1), 1.0, 0.0)
    nw = nw_ref[...]

    def per_seq(bi, carry):
        q, k, v, beta, eg, z = q_ref[bi], k_ref[bi], v_ref[bi], b_ref[bi], e_ref[bi], z_ref[bi]
        k_t = _dot_nt(eye, k, HI)
        q_t = _dot_nt(eye, q, HI)
        outs = []
        for h in range(GDN_HEADS):
            r = slice(h, h + 1)
            s = s_ref[bi, h]
            kcol = k_t[:, h:h + 1]
            k_s = jnp.sum(kcol * s, axis=0, keepdims=True)
            q_s = jnp.sum(q_t[:, h:h + 1] * s, axis=0, keepdims=True)
            v_new = v[r] * beta[r] - (beta[r] * eg[r]) * k_s
            qk = jnp.sum(q[r] * k[r], axis=-1, keepdims=True)
            o = eg[r] * q_s + qk * v_new
            so_ref[bi, h] = s * eg[r] + kcol * v_new
            on = o * lax.rsqrt(jnp.mean(o * o, axis=-1, keepdims=True) + EPS) * nw
            outs.append(on * _silu(z[r]))
        o_ref[bi] = jnp.concatenate(outs, axis=0).astype(o_ref.dtype)
        return carry

    lax.fori_loop(0, bb, per_seq, 0)


def gdn_step(u_qkv, u_z, u_small, hist, state, conv_w, a_log, dt_bias, norm_w, bb=4):
    Bd, W = u_qkv.shape
    H = GDN_HEADS
    nq = H * GDN_DK
    row = lambda p: _place_lanes(((STEP_COL_A, p.reshape(1, H)),), (1,))
    full = lambda shape: pl.BlockSpec(shape, lambda i: (0,) * len(shape))
    outs = pl.pallas_call(
        _gdn_step_prep_body,
        grid=(1,),
        in_specs=[full((Bd, W)), full((Bd, (GDN_CONV - 1) * W)), full((GDN_CONV, W)), full((Bd, LANES)),
                  full((1, LANES)), full((1, LANES))],
        out_specs=[full((Bd, nq))] * 5,
        out_shape=[jax.ShapeDtypeStruct((Bd, nq), F32)] * 5,
        compiler_params=_cparams("arbitrary"),
        name="gdn_step_prep",
    )(u_qkv, hist.reshape(Bd, (GDN_CONV - 1) * W), conv_w, u_small, row(a_log), row(dt_bias))
    heads = lambda a: a.reshape(Bd, H, GDN_DK)
    vec_blk = pl.BlockSpec((bb, H, GDN_DK), lambda i: (i, 0, 0))
    st_blk = pl.BlockSpec((bb, H, GDN_DK, GDN_DV), lambda i: (i, 0, 0, 0))
    o, s_new = pl.pallas_call(
        _gdn_step_body,
        grid=(Bd // bb,),
        in_specs=[vec_blk] * 6 + [pl.BlockSpec((1, GDN_DV), lambda i: (0, 0)), st_blk],
        out_specs=[vec_blk, st_blk],
        out_shape=[jax.ShapeDtypeStruct((Bd, H, GDN_DV), BF16), jax.ShapeDtypeStruct(state.shape, F32)],
        compiler_params=_cparams("parallel"),
        name="gdn_step",
    )(*[heads(a) for a in outs], heads(u_z), norm_w.reshape(1, GDN_DV), state)
    return o.reshape(Bd, H * GDN_DV), s_new


def _softmax_with_new(s_past, valid, s_new):
    s_past = jnp.where(valid, s_past, -jnp.inf)
    m = jnp.maximum(jnp.max(s_past, axis=-1, keepdims=True), s_new)
    e = jnp.exp(s_past - m)
    e_new = jnp.exp(s_new - m)
    return e, e_new, 1.0 / (jnp.sum(e, axis=-1, keepdims=True) + e_new)


def _nsa_step_body(pt_ref, *refs, n_pages, page, n_past):
    lo = refs[:n_pages]
    hi = refs[n_pages:2 * n_pages]
    (q_ref, new_ref, win_ref, wnew_ref, gl_ref, pe_ref, w1_ref, w2_ref, ovl_ref, o_ref, wout_ref, xk_ref, xv_ref) = refs[2 * n_pages:]
    G, H, D = NSA_KV_HEADS, NSA_HEADS, HEAD_DIM
    HS = 2 * G
    pos = n_past
    cpp = page // CMP_STRIDE
    n_chunk = n_pages * cpp
    L = n_pages * page
    q = q_ref[0]
    qf = q.astype(F32)
    new = new_ref[0]
    row_g = lax.broadcasted_iota(jnp.int32, (H, 1), 0) // NSA_GROUP

    def per_head_rows(rows):
        return jnp.concatenate([jnp.broadcast_to(rows[g:g + 1], (NSA_GROUP, rows.shape[1])) for g in range(G)], axis=0)

    first_half = lax.broadcasted_iota(jnp.int32, (SUBLANES, D), 0) < G
    for j in range(n_pages):
        for s in range(CMP_STRIDE):
            for m in range(cpp // 4):
                ks, vs = [], []
                for pair in range(2):
                    ta = lo[j][0, (4 * m + 2 * pair) * CMP_STRIDE + s]
                    tb = lo[j][0, (4 * m + 2 * pair + 1) * CMP_STRIDE + s]
                    ks.append(jnp.where(first_half, ta, pltpu.roll(tb, G, axis=0)))
                    vs.append(jnp.where(first_half, pltpu.roll(ta, G, axis=0), tb))
                r0 = (j * cpp + 4 * m) * G
                xk_ref[r0:r0 + 4 * G, s * D:(s + 1) * D] = jnp.concatenate(ks, axis=0).astype(BF16)
                xv_ref[r0:r0 + 4 * G, s * D:(s + 1) * D] = jnp.concatenate(vs, axis=0).astype(BF16)
    ckv = []
    for c, x_ref in enumerate((xk_ref, xv_ref)):
        part = _dot(x_ref[...], w1_ref[c])
        ckv.append(_compress_mlp(part, pe_ref.at[c], w1_ref.at[c], w2_ref.at[c], G).astype(BF16))

    n_all = G * n_chunk
    s = _dot_nt(q, ckv[0])
    col = lax.broadcasted_iota(jnp.int32, (1, n_all), 1)
    blk = col // G
    ok = (col % G == row_g) & (blk * CMP_STRIDE + (CMP_LEN - 1) <= pos) & (blk < n_chunk - CMP_RATIO + 1)
    p = _masked_softmax(s, ok)
    o_c = _dot(p.astype(BF16), ckv[1])

    gsum = jnp.where(lax.broadcasted_iota(jnp.int32, (LANES, H), 0) == lax.broadcasted_iota(jnp.int32, (LANES, H), 1) // NSA_GROUP, 1.0, 0.0)
    psum = _dot(gsum, p, HI)
    p_hi, p_lo = _split_bf16(psum)
    ovl = ovl_ref[...]
    imp_t = _dot_nt(ovl, p_hi) + _dot_nt(ovl, p_lo)
    n_slc = -(-(L + 1) // SLC_LEN)
    sel_t = _select_blocks_t(imp_t, jnp.full((1, LANES), pos, jnp.int32), n_slc)
    J = sel_t.shape[0]
    eye8 = jnp.where(lax.broadcasted_iota(jnp.int32, (SUBLANES, LANES), 0) == lax.broadcasted_iota(jnp.int32, (SUBLANES, LANES), 1), 1.0, 0.0)
    sel = _dot_nt(eye8.astype(BF16), sel_t.astype(BF16)).astype(BF16)

    def slot_softmax(sc, tok_ok, s_new):
        slot = lax.broadcasted_iota(jnp.int32, (1, sc.shape[1]), 1) % HS
        return _softmax_with_new(sc, (slot == row_g) & tok_ok, s_new)

    tok = lax.broadcasted_iota(jnp.int32, (1, L * HS), 1) // HS
    expand = lax.broadcasted_iota(jnp.int32, (J, L * HS), 0) == tok // SLC_LEN
    chosen = per_head_rows(_dot(sel, jnp.where(expand, 1.0, 0.0).astype(BF16)))
    kv_hi = [hi[j][0].reshape(page * HS, D).astype(BF16) for j in range(n_pages)]
    sc = jnp.concatenate([_dot_nt(q, kv) for kv in kv_hi], axis=1)
    s_new = jnp.sum(qf * per_head_rows(new[2 * G:3 * G]), axis=-1, keepdims=True)
    e, e_new, inv = slot_softmax(sc, (chosen > 0.5) & (tok <= pos), s_new)
    e = pltpu.roll(e, G, axis=1).astype(BF16)
    acc = jnp.zeros((H, D), F32)
    for j, kv in enumerate(kv_hi):
        acc = acc + _dot(e[:, j * page * HS:(j + 1) * page * HS], kv)
    o_s = (acc + e_new * per_head_rows(new[3 * G:4 * G])) * inv

    n_win = win_ref.shape[1] // HS
    wnew = wnew_ref[0]
    win = win_ref[0].astype(BF16)
    wtok = pos - n_win + lax.broadcasted_iota(jnp.int32, (1, n_win * HS), 1) // HS
    sw_new = jnp.sum(qf * per_head_rows(wnew[0:G]), axis=-1, keepdims=True)
    ew, ew_new, winv = slot_softmax(_dot_nt(q, win), (pos - wtok < WINDOW) & (wtok >= 0), sw_new)
    o_w = (_dot(pltpu.roll(ew, G, axis=1).astype(BF16), win) + ew_new * per_head_rows(wnew[G:2 * G])) * winv

    gl = jnp.broadcast_to(_sigmoid(gl_ref[0]), (SUBLANES, LANES))
    hh = lax.broadcasted_iota(jnp.int32, (H, LANES), 0)
    cc = lax.broadcasted_iota(jnp.int32, (H, LANES), 1)
    gate = lambda br: _dot_nt(jnp.where(cc == STEP_COL_G + br * H + hh, 1.0, 0.0), gl, HI)[:, 0:1]
    o_ref[0] = (gate(0) * o_c + gate(1) * o_s + gate(2) * o_w).astype(o_ref.dtype)

    wout_ref[0, :(n_win - 1) * HS] = win_ref[0, HS:]
    wout_ref[0, (n_win - 1) * HS:] = wnew


def nsa_step(q, new_rows, win_new, cache, page_table, win_state, u_small, pe2, w1c, w2c):
    Bd = q.shape[0]
    n_pool, page = cache.shape[:2]
    n_pages = page_table.shape[1]
    G, H, D = NSA_KV_HEADS, NSA_HEADS, HEAD_DIM
    n_past = n_pages * page
    n_chunk = n_past // CMP_STRIDE
    n_win = win_state.shape[1]
    assert n_past % SLC_LEN == 0 and n_win == WINDOW <= n_past and page % (4 * CMP_STRIDE) == 0 and 2 * G == SUBLANES
    n_slc = -(-(n_past + 1) // SLC_LEN)
    J = -(-n_slc // (2 * SUBLANES)) * (2 * SUBLANES)
    ovl = np.zeros((J, n_chunk * G), np.float32)
    ovl[:n_slc] = np.repeat(_overlap_t(n_chunk, n_slc), G, axis=1)
    ovl = jnp.asarray(ovl, BF16)
    body = functools.partial(_nsa_step_body, n_pages=n_pages, page=page, n_past=n_past)
    half_spec = lambda j, half: pl.BlockSpec((1, page, None, 2 * G, D), lambda b, pt: (pt[b, j], 0, half, 0, 0))
    const = lambda a: pl.BlockSpec(a.shape, lambda b, pt: (0,) * a.ndim, pipeline_mode=pl.Buffered(1))
    per_b = lambda shape: pl.BlockSpec((1,) + shape, lambda b, pt: (b,) + (0,) * len(shape))
    cache5 = cache.reshape(n_pool, page, 2, 2 * G, D)
    out, win_out = pl.pallas_call(
        body,
        grid_spec=pltpu.PrefetchScalarGridSpec(
            num_scalar_prefetch=1,
            grid=(Bd,),
            in_specs=[half_spec(j, 0) for j in range(n_pages)] + [half_spec(j, 1) for j in range(n_pages)]
                     + [per_b((H, D)), per_b((4 * G, D)), per_b((n_win * 2 * G, D)), per_b((2 * G, D)), per_b((1, LANES)),
                        const(pe2), const(w1c), const(w2c), const(ovl)],
            out_specs=[per_b((H, D)), per_b((n_win * 2 * G, D))],
            scratch_shapes=[pltpu.VMEM((G * n_chunk, CMP_FEAT), BF16)] * 2,
        ),
        out_shape=[jax.ShapeDtypeStruct((Bd, H, D), BF16), jax.ShapeDtypeStruct((Bd, n_win * 2 * G, D), F32)],
        compiler_params=_cparams("arbitrary"),
        name="nsa_step",
    )(page_table, *([cache5] * (2 * n_pages)), q.reshape(Bd, H, D), new_rows.reshape(Bd, 4 * G, D),
      win_state.reshape(Bd, n_win * 2 * G, D), win_new.reshape(Bd, 2 * G, D), u_small.reshape(Bd, 1, LANES),
      pe2, w1c, w2c, ovl)
    return out.reshape(Bd, H * D), win_out.reshape(win_state.shape)


def rope_tables(pos):
    half = HEAD_DIM // 2
    inv = ROPE_THETA ** (-jnp.arange(half, dtype=F32) / half)
    ang = pos.astype(F32)[:, None] * inv[None, :]
    cos, sin = jnp.cos(ang), jnp.sin(ang)
    return jnp.concatenate([cos, cos], axis=-1), jnp.concatenate([-sin, sin], axis=-1)


def _split_w_in(w_in):
    H, G = NSA_HEADS, NSA_KV_HEADS
    sizes = (H * HEAD_DIM, 6 * G * HEAD_DIM, 3 * H, GDN_HEADS * (2 * GDN_DK + GDN_DV), GDN_HEADS * GDN_DV, 2 * GDN_HEADS)
    o = np.cumsum((0,) + sizes).tolist()
    K = w_in.shape[0]
    w_qkv = w_in[:, o[0]:o[2]].astype(BF16)
    w_gqkv = w_in[:, o[3]:o[4]].astype(BF16)
    w_z = w_in[:, o[4]:o[5]].astype(BF16)
    w_mg = w_in[:, o[6]:].astype(BF16)
    w_ng = w_in[:, o[2]:o[3]]
    w_ba = w_in[:, o[5]:o[6]]
    nhb = GDN_HEADS // GDN_HB
    ng = w_ng.reshape(K, 3, G, NSA_GROUP).transpose(0, 2, 1, 3).reshape(K, G, 3 * NSA_GROUP)
    pieces = ((GDN_COL_B, w_ba[:, :GDN_HEADS].reshape(K, nhb, GDN_HB)),
              (GDN_COL_A, w_ba[:, GDN_HEADS:].reshape(K, nhb, GDN_HB)), (NSA_COL_G, ng))
    w_small_seq = _place_lanes(pieces, (K, nhb)).reshape(K, nhb * LANES).astype(BF16)
    w_small_step = _place_lanes(((STEP_COL_G, w_ng), (STEP_COL_B, w_ba)), (K,)).astype(BF16)
    return w_qkv, w_gqkv, w_z, w_mg, w_small_seq, w_small_step


def _place_lanes(pieces, lead):
    out, at = [], 0
    for lane, vals in pieces:
        out += [jnp.zeros(lead + (lane - at,), F32), vals.astype(F32)]
        at = lane + vals.shape[-1]
    return jnp.concatenate(out + [jnp.zeros(lead + (LANES - at,), F32)], axis=-1)


def kernel(x_prompt, x_sample, cache_nsa_kv, page_table, state_win_kv, state_gdn, state_gdn_conv, state_ffn_conv, norm_mix, w_in, cmp_pe, cmp_w1, cmp_w2, gdn_conv_w, gdn_a_log, gdn_dt_bias, gdn_norm, w_nsa_out, w_gdn_out, w_o, norm_ffn, w_up, ffn_conv_w, w_down, norm_final):
    B, T, D = x_prompt.shape
    Bd = x_sample.shape[0]
    G = NSA_KV_HEADS
    n_past = page_table.shape[1] * cache_nsa_kv.shape[1]
    F = w_down.shape[0]

    w_qkv, w_gqkv, w_z, w_mg, w_small_seq, w_small_step = _split_w_in(w_in)
    w_nsa_b, w_gdn_b, w_o_b = w_nsa_out.astype(BF16), w_gdn_out.astype(BF16), w_o.astype(BF16)
    w_up_b, w_down_b = w_up.astype(BF16), w_down.astype(BF16)
    pe2, w1c, w2c = compress_params(cmp_pe, cmp_w1, cmp_w2)

    def tail(x, mixed_nsa, mixed_gdn, u_mg, ffn_up, tm):
        mixin = matmul_mix(mixed_nsa, mixed_gdn, w_nsa_b, w_gdn_b, u_mg, tm, 512)
        h = matmul_residual(mixin, w_o_b, x, tm, 512)
        hn = rmsnorm(h, norm_ffn, BF16, min(tm, 256))
        act, extra = ffn_up(hn)
        h2 = matmul_residual(act, w_down_b, h, min(tm, 512), 256)
        return rmsnorm(h2, norm_final, F32, min(tm, 256)), extra

    xp = x_prompt.reshape(B * T, D)
    xn = rmsnorm(xp, norm_mix, BF16, 256)
    tm = 1024
    u_qkv = matmul(xn, w_qkv, tm, 512)
    u_gqkv = matmul(xn, w_gqkv, tm, 512)
    u_z = matmul(xn, w_z, tm, 512)
    u_mg = matmul(xn, w_mg, tm, 512)
    u_small = matmul(xn, w_small_seq, tm, w_small_seq.shape[1])
    cos, sin = rope_tables(jnp.arange(T, dtype=jnp.int32))
    q, kv_rows, win_rows = rope_split(u_qkv, cos, sin, 256)
    cmp_kv = compress_seq(kv_rows, pe2, w1c, w2c, B)
    o_nsa = nsa_seq(q, kv_rows, win_rows, cmp_kv, u_small, B)
    a_rows, d_rows = gdn_param_rows(gdn_a_log, gdn_dt_bias)
    o_gdn, s_p = gdn_seq(u_gqkv, u_z, u_small, gdn_conv_w, a_rows, d_rows, gdn_norm, B, 256)
    s_p = s_p.reshape(B, GDN_HEADS, GDN_DK, GDN_DV)

    def ffn_up_p(hn):
        act, tg, tv = ffn_up_seq(hn, w_up_b, ffn_conv_w, B, tm, 256)
        last = lambda a: a[T // tm - 1::T // tm, SUBLANES - (FFN_CONV - 1):]
        return act, jnp.concatenate([last(tg), last(tv)], axis=-1)

    y_p, ffn_p = tail(xp, o_nsa, o_gdn, u_mg, ffn_up_p, tm)
    n_win_p = min(WINDOW, T)
    kv_p = kv_rows.reshape(B, T, 4, G, HEAD_DIM)
    win_p = win_rows.reshape(B, T, 2, G, HEAD_DIM)[:, T - n_win_p:]
    conv_p = u_gqkv.reshape(B, T, -1)[:, T - (GDN_CONV - 1):]

    xs = x_sample.reshape(Bd, D)
    xn = rmsnorm(xs, norm_mix, BF16, Bd)
    u_qkv = matmul(xn, w_qkv, Bd, 512)
    u_gqkv = matmul(xn, w_gqkv, Bd, 512)
    u_z = matmul(xn, w_z, Bd, 512)
    u_mg = matmul(xn, w_mg, Bd, 512)
    u_small = matmul(xn, w_small_step, Bd, LANES)
    cos, sin = rope_tables(jnp.full((Bd,), n_past, jnp.int32))
    q, new_rows, win_new = rope_split(u_qkv, cos, sin, Bd)
    o_nsa, win_s = nsa_step(q, new_rows, win_new, cache_nsa_kv, page_table, state_win_kv, u_small, pe2, w1c, w2c)
    o_gdn, s_s = gdn_step(u_gqkv, u_z, u_small, state_gdn_conv, state_gdn, gdn_conv_w, gdn_a_log, gdn_dt_bias, gdn_norm)

    def ffn_up_s(hn):
        act, ug, uv = ffn_up_step(hn, w_up_b, ffn_conv_w, state_ffn_conv, 256)
        return act, jnp.concatenate([ug, uv], axis=-1)

    y_s, up_new = tail(xs, o_nsa, o_gdn, u_mg, ffn_up_s, Bd)
    kv_s = new_rows.reshape(Bd, 1, 4, G, HEAD_DIM)
    conv_s = jnp.concatenate([state_gdn_conv[:, 1:], u_gqkv[:, None]], axis=1)
    ffn_s = jnp.concatenate([state_ffn_conv[:, 1:], up_new[:, None]], axis=1)

    return (y_p.reshape(B, T, D), y_s.reshape(Bd, 1, D), kv_p, win_p, s_p, conv_p, ffn_p,
            kv_s, win_s, s_s, conv_s, ffn_s)
```

```python
import functools
import math

import jax
import jax.numpy as jnp
import numpy as np
from jax import lax
from jax.experimental import pallas as pl
from jax.experimental.pallas import tpu as pltpu

F32 = jnp.float32
BF16 = jnp.bfloat16
HI = lax.Precision.HIGHEST

LANES = 128
SUBLANES = 8
VMEM_LIMIT_BYTES = 56 * 1024 * 1024

HEAD_DIM = 128
NSA_HEADS = 16
NSA_KV_HEADS = 4
NSA_GROUP = NSA_HEADS // NSA_KV_HEADS
CMP_LEN = 32
CMP_STRIDE = 16
CMP_HIDDEN = 256
SLC_LEN = 64
SLC_TOP_N = 16
SLC_LOCAL = 2
WINDOW = 512
GDN_HEADS = 16
GDN_DK = 128
GDN_DV = 128
GDN_CONV = 4
GDN_CHUNK = 64
FFN_CONV = 3
ROPE_THETA = 10000.0
EPS = 1e-6


def _cparams(*sem):
    return pltpu.CompilerParams(dimension_semantics=sem, vmem_limit_bytes=VMEM_LIMIT_BYTES)


def _dot(a, b, precision=None):
    return jnp.dot(a, b, preferred_element_type=F32, precision=precision)


def _dot_nt(a, b, precision=None):
    return lax.dot_general(a, b, (((1,), (1,)), ((), ())), preferred_element_type=F32, precision=precision)


def _dot_tn(a, b, precision=None):
    return lax.dot_general(a, b, (((0,), (0,)), ((), ())), preferred_element_type=F32, precision=precision)


def _sigmoid(x):
    return 1.0 / (1.0 + jnp.exp(-x))


def _silu(x):
    return x * _sigmoid(x)


def _rmsnorm_body(x_ref, w_ref, o_ref):
    x = x_ref[...]
    y = x * lax.rsqrt(jnp.mean(x * x, axis=-1, keepdims=True) + EPS)
    o_ref[...] = (y * w_ref[...]).astype(o_ref.dtype)


def rmsnorm(x, w, out_dtype, tm):
    M, D = x.shape
    return pl.pallas_call(
        _rmsnorm_body,
        grid=(M // tm,),
        in_specs=[pl.BlockSpec((tm, D), lambda i: (i, 0)), pl.BlockSpec((1, D), lambda i: (0, 0))],
        out_specs=pl.BlockSpec((tm, D), lambda i: (i, 0)),
        out_shape=jax.ShapeDtypeStruct((M, D), out_dtype),
        compiler_params=_cparams("parallel"),
        name="rmsnorm",
    )(x, w.reshape(1, D))


def _mm_body(x_ref, w_ref, o_ref):
    o_ref[...] = _dot(x_ref[...], w_ref[...]).astype(o_ref.dtype)


def matmul(x, w, tm, tn, out_dtype=F32):
    M, K = x.shape
    N = w.shape[1]
    return pl.pallas_call(
        _mm_body,
        grid=(M // tm, N // tn),
        in_specs=[pl.BlockSpec((tm, K), lambda i, j: (i, 0)), pl.BlockSpec((K, tn), lambda i, j: (0, j))],
        out_specs=pl.BlockSpec((tm, tn), lambda i, j: (i, j)),
        out_shape=jax.ShapeDtypeStruct((M, N), out_dtype),
        compiler_params=_cparams("parallel", "arbitrary"),
        name="matmul",
    )(x, w)


def _mm_res_body(x_ref, w_ref, r_ref, o_ref):
    o_ref[...] = r_ref[...] + _dot(x_ref[...], w_ref[...])


def matmul_residual(x, w, res, tm, tn):
    M, K = x.shape
    N = w.shape[1]
    return pl.pallas_call(
        _mm_res_body,
        grid=(M // tm, N // tn),
        in_specs=[pl.BlockSpec((tm, K), lambda i, j: (i, 0)), pl.BlockSpec((K, tn), lambda i, j: (0, j)),
                  pl.BlockSpec((tm, tn), lambda i, j: (i, j))],
        out_specs=pl.BlockSpec((tm, tn), lambda i, j: (i, j)),
        out_shape=jax.ShapeDtypeStruct((M, N), F32),
        compiler_params=_cparams("parallel", "arbitrary"),
        name="matmul_residual",
    )(x, w, res)


def _mm_mix_body(a_ref, b_ref, wa_ref, wb_ref, ga_ref, gb_ref, o_ref):
    ua = _dot(a_ref[...], wa_ref[...])
    ub = _dot(b_ref[...], wb_ref[...])
    o_ref[...] = (_sigmoid(ga_ref[...]) * ua + _sigmoid(gb_ref[...]) * ub).astype(o_ref.dtype)


def matmul_mix(a, b, wa, wb, gates, tm, tn):
    M, K = a.shape
    N = wa.shape[1]
    nj = N // tn
    return pl.pallas_call(
        _mm_mix_body,
        grid=(M // tm, nj),
        in_specs=[pl.BlockSpec((tm, K), lambda i, j: (i, 0)), pl.BlockSpec((tm, K), lambda i, j: (i, 0)),
                  pl.BlockSpec((K, tn), lambda i, j: (0, j)), pl.BlockSpec((K, tn), lambda i, j: (0, j)),
                  pl.BlockSpec((tm, tn), lambda i, j: (i, j)), pl.BlockSpec((tm, tn), lambda i, j: (i, j + nj))],
        out_specs=pl.BlockSpec((tm, tn), lambda i, j: (i, j)),
        out_shape=jax.ShapeDtypeStruct((M, N), BF16),
        compiler_params=_cparams("parallel", "arbitrary"),
        name="matmul_mix",
    )(a, b, wa, wb, gates, gates)


def _shift_rows(x, hist, sh, row):
    s = pltpu.roll(x, sh, axis=0)
    for r in range(sh):
        s = jnp.where(row == r, hist[SUBLANES - sh + r:SUBLANES - sh + r + 1], s)
    return s


def _ffn_up_seq_body(x_ref, wg_ref, wv_ref, cg_ref, cv_ref, act_ref, tg_ref, tv_ref, hg_ref, hv_ref, *, tiles_per_seq):
    i = pl.program_id(0)
    j = pl.program_id(1)
    first = (i % tiles_per_seq) == 0
    x = x_ref[...]

    def branch(w_ref, c_ref, hist_ref, tail_ref):
        up = _dot(x, w_ref[...])
        tm = up.shape[0]
        hist = jnp.where(first, 0.0, hist_ref[j])
        row = lax.broadcasted_iota(jnp.int32, up.shape, 0)
        c = c_ref[...]
        out = up * c[2:3] + _shift_rows(up, hist, 2, row) * c[0:1] + _shift_rows(up, hist, 1, row) * c[1:2]
        tail = up[tm - SUBLANES:tm]
        hist_ref[j] = tail
        tail_ref[0] = tail
        return out

    gate = branch(wg_ref, cg_ref, hg_ref, tg_ref)
    val = branch(wv_ref, cv_ref, hv_ref, tv_ref)
    act_ref[...] = (_silu(gate) * val).astype(act_ref.dtype)


def ffn_up_seq(x, w_up, conv_w, n_seq, tm, tn):
    M, K = x.shape
    F = w_up.shape[1] // 2
    nj = F // tn
    tiles_per_seq = M // n_seq // tm
    body = functools.partial(_ffn_up_seq_body, tiles_per_seq=tiles_per_seq)
    return pl.pallas_call(
        body,
        grid=(M // tm, nj),
        in_specs=[pl.BlockSpec((tm, K), lambda i, j: (i, 0)),
                  pl.BlockSpec((K, tn), lambda i, j: (0, j)), pl.BlockSpec((K, tn), lambda i, j: (0, j + nj)),
                  pl.BlockSpec((FFN_CONV, tn), lambda i, j: (0, j)), pl.BlockSpec((FFN_CONV, tn), lambda i, j: (0, j + nj))],
        out_specs=[pl.BlockSpec((tm, tn), lambda i, j: (i, j)),
                   pl.BlockSpec((1, SUBLANES, tn), lambda i, j: (i, 0, j)),
                   pl.BlockSpec((1, SUBLANES, tn), lambda i, j: (i, 0, j))],
        out_shape=[jax.ShapeDtypeStruct((M, F), BF16),
                   jax.ShapeDtypeStruct((M // tm, SUBLANES, F), F32), jax.ShapeDtypeStruct((M // tm, SUBLANES, F), F32)],
        scratch_shapes=[pltpu.VMEM((nj, SUBLANES, tn), F32), pltpu.VMEM((nj, SUBLANES, tn), F32)],
        compiler_params=_cparams("arbitrary", "arbitrary"),
        name="ffn_up_seq",
    )(x, w_up, w_up, conv_w, conv_w)


def _ffn_up_step_body(x_ref, wg_ref, wv_ref, cg_ref, cv_ref, g0_ref, g1_ref, v0_ref, v1_ref, act_ref, ug_ref, uv_ref):
    x = x_ref[...]

    def branch(w_ref, c_ref, h0_ref, h1_ref, up_ref):
        up = _dot(x, w_ref[...])
        up_ref[...] = up
        c = c_ref[...]
        return up * c[2:3] + h0_ref[...] * c[0:1] + h1_ref[...] * c[1:2]

    gate = branch(wg_ref, cg_ref, g0_ref, g1_ref, ug_ref)
    val = branch(wv_ref, cv_ref, v0_ref, v1_ref, uv_ref)
    act_ref[...] = (_silu(gate) * val).astype(act_ref.dtype)


def ffn_up_step(x, w_up, conv_w, hist, tn):
    M, K = x.shape
    F2 = w_up.shape[1]
    F = F2 // 2
    nj = F // tn
    h2 = hist.reshape(M, 2 * F2)
    return pl.pallas_call(
        _ffn_up_step_body,
        grid=(nj,),
        in_specs=[pl.BlockSpec((M, K), lambda j: (0, 0)),
                  pl.BlockSpec((K, tn), lambda j: (0, j)), pl.BlockSpec((K, tn), lambda j: (0, j + nj)),
                  pl.BlockSpec((FFN_CONV, tn), lambda j: (0, j)), pl.BlockSpec((FFN_CONV, tn), lambda j: (0, j + nj)),
                  pl.BlockSpec((M, tn), lambda j: (0, j)), pl.BlockSpec((M, tn), lambda j: (0, j + 2 * nj)),
                  pl.BlockSpec((M, tn), lambda j: (0, j + nj)), pl.BlockSpec((M, tn), lambda j: (0, j + 3 * nj))],
        out_specs=[pl.BlockSpec((M, tn), lambda j: (0, j))] * 3,
        out_shape=[jax.ShapeDtypeStruct((M, F), BF16), jax.ShapeDtypeStruct((M, F), F32), jax.ShapeDtypeStruct((M, F), F32)],
        compiler_params=_cparams("arbitrary"),
        name="ffn_up_step",
    )(x, w_up, w_up, conv_w, conv_w, h2, h2, h2, h2)


def _rope_body(u_ref, cos_ref, sin_ref, q_ref, kv_ref, win_ref):
    cos = cos_ref[...]
    sin = sin_ref[...]
    scale = HEAD_DIM ** -0.5

    def rot(x):
        return x * cos + pltpu.roll(x, HEAD_DIM // 2, axis=1) * sin

    for h in range(NSA_HEADS):
        sl = slice(h * HEAD_DIM, (h + 1) * HEAD_DIM)
        q_ref[:, sl] = (rot(u_ref[:, sl]) * scale).astype(q_ref.dtype)
    base = NSA_HEADS * HEAD_DIM
    n_glob = 4 * NSA_KV_HEADS
    for slot in range(6 * NSA_KV_HEADS):
        x = u_ref[:, base + slot * HEAD_DIM:base + (slot + 1) * HEAD_DIM]
        if (slot // NSA_KV_HEADS) % 2 == 0:
            x = rot(x)
        if slot < n_glob:
            kv_ref[:, slot * HEAD_DIM:(slot + 1) * HEAD_DIM] = x
        else:
            win_ref[:, (slot - n_glob) * HEAD_DIM:(slot - n_glob + 1) * HEAD_DIM] = x


def rope_split(u, cos, sin, tm):
    M, W = u.shape
    nt = cos.shape[0] // tm
    nq = NSA_HEADS * HEAD_DIM
    ng = 4 * NSA_KV_HEADS * HEAD_DIM
    nw = 2 * NSA_KV_HEADS * HEAD_DIM
    return pl.pallas_call(
        _rope_body,
        grid=(M // tm,),
        in_specs=[pl.BlockSpec((tm, W), lambda i: (i, 0)),
                  pl.BlockSpec((tm, HEAD_DIM), lambda i: (i % nt, 0)), pl.BlockSpec((tm, HEAD_DIM), lambda i: (i % nt, 0))],
        out_specs=[pl.BlockSpec((tm, nq), lambda i: (i, 0)), pl.BlockSpec((tm, ng), lambda i: (i, 0)),
                   pl.BlockSpec((tm, nw), lambda i: (i, 0))],
        out_shape=[jax.ShapeDtypeStruct((M, nq), BF16), jax.ShapeDtypeStruct((M, ng), F32), jax.ShapeDtypeStruct((M, nw), F32)],
        compiler_params=_cparams("parallel"),
        name="rope_split",
    )(u, cos, sin)


GDN_HB = 4
GDN_COL_B = 0
GDN_COL_A = 8
NSA_COL_G = 16
GDN_STEP_GROUPS = 2


def _softplus(x):
    return jnp.maximum(x, 0.0) + jnp.log1p(jnp.exp(-jnp.abs(x)))


def _l2norm(x):
    return x * lax.rsqrt(jnp.sum(x * x, axis=-1, keepdims=True) + EPS)


def _solve_unit_lower(a_list, r_list, order):
    steps = int(math.log2(order))
    a_list, r_list = list(a_list), list(r_list)
    m, n = a_list[0].shape[1], r_list[0].shape[1]
    for i in range(steps):
        last = i == steps - 1
        for c in range(len(a_list)):
            a_b = a_list[c].astype(BF16)
            r_hi, r_lo = _split_bf16(r_list[c])
            prod = _dot(a_b, jnp.concatenate(([] if last else [a_b]) + [r_hi, r_lo], axis=1))
            if not last:
                a_list[c], prod = prod[:, :m], prod[:, m:]
            r_list[c] = r_list[c] + (prod[:, :n] + prod[:, n:])
    return r_list


def _block_diag(blocks):
    z = jnp.zeros_like(blocks[0])
    nb = len(blocks)
    return jnp.concatenate([jnp.concatenate([blocks[h] if j == h else z for j in range(nb)], axis=1)
                            for h in range(nb)], axis=0)


def _gdn_chunks_groups(groups, states):
    nh = len(groups[0][0][0])
    C, d = groups[0][0][2][0].shape
    W = nh * C
    stack = lambda xs: jnp.concatenate(xs, axis=0)
    bd = lambda xs: _block_diag([x.astype(BF16) for x in xs])
    ri = lax.broadcasted_iota(jnp.int32, (W, W), 0)
    ci = lax.broadcasted_iota(jnp.int32, (W, W), 1)
    same = (ri // C) == (ci // C)

    a_list, r_list, pre = [], [], []
    for chunks in groups:
        for q, k, v, beta, gc, gcr in chunks:
            dmat = jnp.where(same & (ri >= ci), jnp.exp(jnp.minimum(stack(gc) - jnp.concatenate(gcr, axis=1), 0.0)), 0.0)
            eg = [jnp.exp(g) for g in gc]
            kb = [k[h] * beta[h] for h in range(nh)]
            xk = _dot_nt(jnp.concatenate([bd(kb), bd(q)], axis=0), bd(k))
            a_list.append(jnp.where(same & (ri > ci), -(xk[:W] * dmat), 0.0))
            r_list.append(jnp.concatenate([stack([v[h] * beta[h] for h in range(nh)]),
                                           stack([kb[h] * eg[h] for h in range(nh)])], axis=1))
            g_last = [g[C - 1:C, :] for g in gc]
            pre.append(((xk[W:] * dmat).astype(BF16), bd([q[h] * eg[h] for h in range(nh)]),
                        bd([k[h] * jnp.exp(g_last[h] - gc[h]) for h in range(nh)]),
                        stack([jnp.broadcast_to(jnp.exp(g_last[h]), (d, 1)) for h in range(nh)])))
    w_list = _solve_unit_lower(a_list, r_list, C)

    n_chunks = len(groups[0])
    states = list(states)
    outs = [[] for _ in groups]
    for c in range(n_chunks):
        for g in range(len(groups)):
            w, (aqk, q_eg, kd, keep) = w_list[g * n_chunks + c], pre[g * n_chunks + c]
            wk = bd([w[h * C:(h + 1) * C, d:] for h in range(nh)])
            xs = _dot(jnp.concatenate([wk, q_eg], axis=0), states[g].astype(BF16))
            v_new = (w[:, :d] - xs[:W]).astype(BF16)
            outs[g].append(xs[W:] + _dot(aqk, v_new))
            states[g] = states[g] * keep + _dot_tn(kd, v_new)
    return outs, states


def _gdn_seq_body(q_ref, k_ref, v_ref, z_ref, ba_ref, cq_ref, ck_ref, cv_ref, alog_ref, dtb_ref, nw_ref,
                  o_ref, s_ref, tq_ref, tk_ref, tv_ref, qs_ref, ks_ref, vs_ref, gs_ref, bs_ref):
    t_idx = pl.program_id(2)
    tt = q_ref.shape[0]
    hb = q_ref.shape[1] // GDN_DK
    C = GDN_CHUNK

    @pl.when(t_idx == 0)
    def _():
        s_ref[...] = jnp.zeros_like(s_ref)
        tq_ref[...] = jnp.zeros_like(tq_ref)
        tk_ref[...] = jnp.zeros_like(tk_ref)
        tv_ref[...] = jnp.zeros_like(tv_ref)

    def conv(x_ref, c_ref, tail_ref):
        x = x_ref[...]
        hist = tail_ref[...]
        row = lax.broadcasted_iota(jnp.int32, x.shape, 0)
        c = c_ref[...]
        out = x * c[GDN_CONV - 1:GDN_CONV]
        for j in range(GDN_CONV - 1):
            out = out + _shift_rows(x, hist, GDN_CONV - 1 - j, row) * c[j:j + 1]
        tail_ref[...] = x[tt - SUBLANES:tt]
        return _silu(out)

    qa = conv(q_ref, cq_ref, tq_ref)
    ka = conv(k_ref, ck_ref, tk_ref)
    vs_ref[...] = conv(v_ref, cv_ref, tv_ref)
    for h in range(hb):
        sl = slice(h * GDN_DK, (h + 1) * GDN_DK)
        qs_ref[:, sl] = _l2norm(qa[:, sl]) * GDN_DK ** -0.5
        ks_ref[:, sl] = _l2norm(ka[:, sl])
    ba = ba_ref[...]
    bs_ref[...] = _sigmoid(ba)
    n_groups = hb // GDN_HB
    par_row = lambda ref: jnp.concatenate([ref[g, 0:1, :] for g in range(n_groups)], axis=1)
    gs_ref[...] = -jnp.exp(par_row(alog_ref)) * _softplus(ba + par_row(dtb_ref))

    tri = (lax.broadcasted_iota(jnp.int32, (C, C), 0) >= lax.broadcasted_iota(jnp.int32, (C, C), 1)).astype(F32)
    sel = (lax.broadcasted_iota(jnp.int32, (2 * SUBLANES, LANES), 0)
           == lax.broadcasted_iota(jnp.int32, (2 * SUBLANES, LANES), 1)).astype(F32)
    nw = nw_ref[...]

    chunk_rows = [slice(c * C, (c + 1) * C) for c in range(tt // C)]
    gc_alls = [_dot(tri, gs_ref[rows, :], HI) for rows in chunk_rows]
    groups, group_heads = [], []
    for g in range(n_groups):
        heads = [slice((g * GDN_HB + h) * GDN_DK, (g * GDN_HB + h + 1) * GDN_DK) for h in range(GDN_HB)]
        lanes = slice(g * LANES, (g + 1) * LANES)
        chunks = []
        for rows, gc_full in zip(chunk_rows, gc_alls):
            gc_all = gc_full[:, lanes]
            gc_t = _dot_nt(sel, gc_all, HI)
            beta_all = bs_ref[rows, lanes]
            chunks.append((
                [qs_ref[rows, sl] for sl in heads], [ks_ref[rows, sl] for sl in heads], [vs_ref[rows, sl] for sl in heads],
                [beta_all[:, GDN_COL_B + h:GDN_COL_B + h + 1] for h in range(GDN_HB)],
                [gc_all[:, GDN_COL_A + h:GDN_COL_A + h + 1] for h in range(GDN_HB)],
                [gc_t[GDN_COL_A + h:GDN_COL_A + h + 1, :] for h in range(GDN_HB)]))
        groups.append(chunks)
        group_heads.append(heads)
    gdk = GDN_HB * GDN_DK
    outs, states = _gdn_chunks_groups(groups, [s_ref[0, g * gdk:(g + 1) * gdk] for g in range(n_groups)])
    for g in range(n_groups):
        s_ref[0, g * gdk:(g + 1) * gdk] = states[g]
        for rows, o in zip(chunk_rows, outs[g]):
            on = o * lax.rsqrt(jnp.mean(o * o, axis=-1, keepdims=True) + EPS) * nw
            for h, sl in enumerate(group_heads[g]):
                o_ref[rows, sl] = (on[h * C:(h + 1) * C] * _silu(z_ref[rows, sl])).astype(o_ref.dtype)


def gdn_seq(u_qkv, u_z, u_small, conv_w, alog_rows, dtb_rows, norm_w, n_seq, tt):
    M = u_qkv.shape[0]
    T = M // n_seq
    nt = T // tt
    hb = GDN_HB * GDN_STEP_GROUPS
    nhb = GDN_HEADS // hb
    wb = hb * GDN_DK
    gl = GDN_STEP_GROUPS * LANES
    row_blk = lambda off: pl.BlockSpec((tt, wb), lambda b, h, t: (b * nt + t, h + off))
    cw_blk = lambda off: pl.BlockSpec((GDN_CONV, wb), lambda b, h, t: (0, h + off))
    par_blk = pl.BlockSpec((GDN_STEP_GROUPS, SUBLANES, LANES), lambda b, h, t: (h, 0, 0))
    return pl.pallas_call(
        _gdn_seq_body,
        grid=(n_seq, nhb, nt),
        in_specs=[row_blk(0), row_blk(nhb), row_blk(2 * nhb), row_blk(0),
                  pl.BlockSpec((tt, gl), lambda b, h, t: (b * nt + t, h)),
                  cw_blk(0), cw_blk(nhb), cw_blk(2 * nhb), par_blk, par_blk,
                  pl.BlockSpec((1, GDN_DV), lambda b, h, t: (0, 0))],
        out_specs=[row_blk(0), pl.BlockSpec((1, hb * GDN_DK, GDN_DV), lambda b, h, t: (b, h, 0))],
        out_shape=[jax.ShapeDtypeStruct((M, GDN_HEADS * GDN_DV), BF16),
                   jax.ShapeDtypeStruct((n_seq, GDN_HEADS * GDN_DK, GDN_DV), F32)],
        scratch_shapes=[pltpu.VMEM((SUBLANES, wb), F32)] * 3 + [pltpu.VMEM((tt, wb), F32)] * 3
                       + [pltpu.VMEM((tt, gl), F32)] * 2,
        compiler_params=_cparams("parallel", "parallel", "arbitrary"),
        name="gdn_seq",
    )(u_qkv, u_qkv, u_qkv, u_z, u_small, conv_w, conv_w, conv_w, alog_rows, dtb_rows, norm_w.reshape(1, GDN_DV))


def gdn_param_rows(gdn_a_log, gdn_dt_bias):
    nhb = GDN_HEADS // GDN_HB

    def rows(p):
        r = _place_lanes(((GDN_COL_A, p.reshape(nhb, GDN_HB)),), (nhb,))
        return jnp.broadcast_to(r[:, None, :], (nhb, SUBLANES, LANES))

    return rows(gdn_a_log), rows(gdn_dt_bias)


CMP_RATIO = CMP_LEN // CMP_STRIDE
CMP_FEAT = CMP_STRIDE * HEAD_DIM
MASKED = -1e30


def _compress_mlp(part, pe_ref, w1_ref, w2_ref, chunk_step=1):
    n = part.shape[0]
    pe_part = _dot(pe_ref[...], w1_ref[...])
    hid0 = pe_part[0:1, :CMP_HIDDEN] + pe_part[1:2, CMP_HIDDEN:]
    hid = hid0 + part[:, :CMP_HIDDEN]
    hid = hid + pltpu.roll(part[:, CMP_HIDDEN:], n - chunk_step, axis=0)
    return _dot(_silu(hid).astype(BF16), w2_ref[...])


def _compress_seq_body(x_ref, pe_ref, w1_ref, w2_ref, o_ref, xc_ref):
    n = o_ref.shape[3]
    for s in range(CMP_STRIDE):
        xc_ref[:, s * HEAD_DIM:(s + 1) * HEAD_DIM] = x_ref[pl.ds(s, n, stride=CMP_STRIDE), :].astype(BF16)
    part = _dot(xc_ref[...], w1_ref[0])
    o_ref[0, 0, 0] = _compress_mlp(part, pe_ref.at[0], w1_ref.at[0], w2_ref.at[0])


def compress_seq(kv_rows, pe2, w1c, w2c, n_seq):
    M = kv_rows.shape[0]
    T = M // n_seq
    n = T // CMP_STRIDE
    G = NSA_KV_HEADS
    return pl.pallas_call(
        _compress_seq_body,
        grid=(n_seq, 2, G),
        in_specs=[pl.BlockSpec((T, HEAD_DIM), lambda b, c, g: (b, c * G + g)),
                  pl.BlockSpec((1, SUBLANES, CMP_FEAT), lambda b, c, g: (c, 0, 0)),
                  pl.BlockSpec((1, CMP_FEAT, 2 * CMP_HIDDEN), lambda b, c, g: (c, 0, 0)),
                  pl.BlockSpec((1, CMP_HIDDEN, HEAD_DIM), lambda b, c, g: (c, 0, 0))],
        out_specs=pl.BlockSpec((1, 1, 1, n, HEAD_DIM), lambda b, c, g: (b, c, g, 0, 0)),
        out_shape=jax.ShapeDtypeStruct((n_seq, 2, G, n, HEAD_DIM), F32),
        scratch_shapes=[pltpu.VMEM((n, CMP_FEAT), BF16)],
        compiler_params=_cparams("parallel", "parallel", "parallel"),
        name="compress_seq",
    )(kv_rows, pe2, w1c, w2c)


def compress_params(cmp_pe, cmp_w1, cmp_w2):
    w1 = cmp_w1.reshape(2, CMP_RATIO, CMP_FEAT, CMP_HIDDEN)
    w1c = jnp.concatenate([w1[:, r] for r in range(CMP_RATIO)], axis=-1).astype(BF16)
    pe = cmp_pe.reshape(2, CMP_RATIO, CMP_FEAT)
    pe2 = jnp.zeros((2, SUBLANES, CMP_FEAT), F32).at[:, :CMP_RATIO].set(pe).astype(BF16)
    return pe2, w1c, cmp_w2.astype(BF16)


def _masked_softmax(s, mask, axis=-1):
    s = jnp.where(mask, s, -jnp.inf)
    m = jnp.max(s, axis=axis, keepdims=True)
    m = jnp.where(m > -jnp.inf, m, 0.0)
    e = jnp.exp(s - m)
    return e * (1.0 / jnp.maximum(jnp.sum(e, axis=axis, keepdims=True), 1e-30))


def _masked_softmax_heads(s, mask, n_heads):
    t = mask.shape[0]
    return jnp.concatenate([_masked_softmax(s[h * t:(h + 1) * t], mask) for h in range(n_heads)], axis=0)


def _split_bf16(x):
    hi = x.astype(BF16)
    return hi, (x - hi.astype(F32)).astype(BF16)


def _select_blocks_t(imp_t, pos_row, n_blocks):
    J = imp_t.shape[0]
    j = lax.broadcasted_iota(jnp.int32, imp_t.shape, 0)
    cur = pos_row // SLC_LEN
    forced = (j == 0) | ((j <= cur) & (j > cur - SLC_LOCAL))
    score = jnp.where(j * SLC_LEN > pos_row, -jnp.inf, jnp.where(forced, jnp.inf, imp_t))
    score = jnp.where(j < n_blocks, score, -jnp.inf)
    tiles = [score[r:r + SUBLANES] for r in range(0, J, SUBLANES)]
    ranks = [jnp.zeros(t.shape, F32) for t in tiles]
    for jp in range(n_blocks):
        row = score[jp:jp + 1, :]
        for i, t in enumerate(tiles):
            lo_j = i * SUBLANES
            ge, gt = jnp.where(row >= t, 1.0, 0.0), jnp.where(row > t, 1.0, 0.0)
            if lo_j > jp:
                ahead = ge
            elif lo_j + SUBLANES - 1 <= jp:
                ahead = gt
            else:
                ahead = jnp.where(lax.broadcasted_iota(jnp.int32, t.shape, 0) + lo_j > jp, ge, gt)
            ranks[i] = ranks[i] + ahead
    rank = jnp.concatenate(ranks, axis=0)
    keep = (rank < float(min(SLC_TOP_N, n_blocks))) & (j < n_blocks)
    return jnp.where(keep, 1.0, 0.0)


def _nsa_seq_body(q_ref, kc_ref, vc_ref, ks_ref, vs_ref, kw_ref, vw_ref, gl_ref, ovl_ref, o_ref, *, kblk):
    i = pl.program_id(2)
    tq = q_ref.shape[0]
    T = ks_ref.shape[0]
    H = NSA_GROUP
    D = HEAD_DIM
    n_slc = T // SLC_LEN
    qb = q_ref[...]
    q4 = jnp.concatenate([qb[:, h * D:(h + 1) * D] for h in range(H)], axis=0)
    t0 = i * tq
    pos1 = t0 + lax.broadcasted_iota(jnp.int32, (tq, 1), 0)

    kc = kc_ref[0, 0, 0].astype(BF16)
    vc = vc_ref[0, 0, 0].astype(BF16)
    n_c = kc.shape[0]
    s = _dot_nt(q4, kc)
    cmp_end = lax.broadcasted_iota(jnp.int32, (1, n_c), 1) * CMP_STRIDE + (CMP_LEN - 1)
    p = _masked_softmax_heads(s, cmp_end <= pos1, H)
    o_c = _dot(p.astype(BF16), vc)

    psum = p[0:tq]
    for h in range(1, H):
        psum = psum + p[h * tq:(h + 1) * tq]
    p_hi, p_lo = _split_bf16(psum)
    ovl = ovl_ref[...]
    imp_t = _dot_nt(ovl, p_hi) + _dot_nt(ovl, p_lo)
    pos_row = t0 + lax.broadcasted_iota(jnp.int32, (1, tq), 1)
    sel_t = _select_blocks_t(imp_t, pos_row, n_slc)
    eye = (lax.broadcasted_iota(jnp.int32, (tq, tq), 0) == lax.broadcasted_iota(jnp.int32, (tq, tq), 1))
    sel = _dot_nt(jnp.where(eye, 1.0, 0.0).astype(BF16), sel_t.astype(BF16)).astype(BF16)

    n_kb = (t0 + tq + kblk - 1) // kblk

    def kv_step(kb, carry):
        ms, ls, accs = carry
        k0 = pl.multiple_of(kb * kblk, kblk)
        kk = ks_ref[pl.ds(k0, kblk), :].astype(BF16)
        vv = vs_ref[pl.ds(k0, kblk), :].astype(BF16)
        s2 = _dot_nt(q4, kk)
        kpos = k0 + lax.broadcasted_iota(jnp.int32, (1, kblk), 1)
        expand = (lax.broadcasted_iota(jnp.int32, (n_slc, kblk), 0) == kpos // SLC_LEN)
        chosen = _dot(sel, jnp.where(expand, 1.0, 0.0).astype(BF16))
        bias = jnp.where((chosen > 0.5) & (kpos <= pos1), 0.0, MASKED)
        m_out, l_out, alphas, es = [], [], [], []
        for h in range(H):
            sh = s2[h * tq:(h + 1) * tq] + bias
            m_new = jnp.maximum(ms[h], jnp.max(sh, axis=-1, keepdims=True))
            alpha = jnp.exp(ms[h] - m_new)
            e = jnp.exp(sh - m_new)
            m_out.append(m_new)
            l_out.append(alpha * ls[h] + jnp.sum(e, axis=-1, keepdims=True))
            alphas.append(alpha)
            es.append(e.astype(BF16))
        pv = _dot(jnp.concatenate(es, axis=0), vv)
        acc_out = [alphas[h] * accs[h] + pv[h * tq:(h + 1) * tq] for h in range(H)]
        return tuple(m_out), tuple(l_out), tuple(acc_out)

    init = (tuple(jnp.full((tq, 1), MASKED, F32) for _ in range(H)), tuple(jnp.zeros((tq, 1), F32) for _ in range(H)),
            tuple(jnp.zeros((tq, D), F32) for _ in range(H)))
    ms, ls, accs = lax.fori_loop(0, n_kb, kv_step, init)
    o_s = jnp.concatenate([accs[h] / jnp.maximum(ls[h], 1e-30) for h in range(H)], axis=0)

    band = WINDOW + tq
    w0 = pl.multiple_of(jnp.maximum(t0 - WINDOW, 0), tq)
    kw = kw_ref[pl.ds(w0, band), :].astype(BF16)
    vw = vw_ref[pl.ds(w0, band), :].astype(BF16)
    sw = _dot_nt(q4, kw)
    diff = pos1 - (w0 + lax.broadcasted_iota(jnp.int32, (1, band), 1))
    pw = _masked_softmax_heads(sw, (diff >= 0) & (diff < WINDOW), H)
    o_w = _dot(pw.astype(BF16), vw)

    gates = _sigmoid(gl_ref[...])
    for h in range(H):
        rows = slice(h * tq, (h + 1) * tq)
        g = lambda br: gates[:, NSA_COL_G + H * br + h:NSA_COL_G + H * br + h + 1]
        o_ref[:, h * D:(h + 1) * D] = (g(0) * o_c[rows] + g(1) * o_s[rows] + g(2) * o_w[rows]).astype(o_ref.dtype)


def _overlap_t(n_cmp_rows, n_slc):
    start = np.arange(n_cmp_rows)[None, :] * CMP_STRIDE
    blk = np.arange(n_slc)[:, None] * SLC_LEN
    return ((start < blk + SLC_LEN) & (start + CMP_LEN > blk)).astype(np.float32)


def nsa_seq(q, kv_rows, win_rows, cmp_kv, u_small, n_seq, tq=128, kblk=512):
    M = q.shape[0]
    T = M // n_seq
    nt = T // tq
    G = NSA_KV_HEADS
    gw = NSA_GROUP * HEAD_DIM
    n_c = cmp_kv.shape[3]
    ovl = jnp.asarray(_overlap_t(n_c, T // SLC_LEN), BF16)
    body = functools.partial(_nsa_seq_body, kblk=kblk)
    seq_blk = lambda off: pl.BlockSpec((T, HEAD_DIM), lambda b, g, i: (b, g + off))
    cmp_blk = lambda c: pl.BlockSpec((1, 1, 1, n_c, HEAD_DIM), lambda b, g, i: (b, c, g, 0, 0))
    return pl.pallas_call(
        body,
        grid=(n_seq, G, nt),
        in_specs=[pl.BlockSpec((tq, gw), lambda b, g, i: (b * nt + i, g)),
                  cmp_blk(0), cmp_blk(1), seq_blk(2 * G), seq_blk(3 * G), seq_blk(0), seq_blk(G),
                  pl.BlockSpec((tq, LANES), lambda b, g, i: (b * nt + i, g)),
                  pl.BlockSpec(ovl.shape, lambda b, g, i: (0, 0))],
        out_specs=pl.BlockSpec((tq, gw), lambda b, g, i: (b * nt + i, g)),
        out_shape=jax.ShapeDtypeStruct((M, NSA_HEADS * HEAD_DIM), BF16),
        compiler_params=_cparams("parallel", "parallel", "arbitrary"),
        name="nsa_seq",
    )(q, cmp_kv, cmp_kv, kv_rows, kv_rows, win_rows, win_rows, u_small, ovl)


STEP_COL_G = 0
STEP_COL_B = 3 * NSA_HEADS
STEP_COL_A = STEP_COL_B + GDN_HEADS


def _gdn_step_prep_body(x_ref, h_ref, c_ref, us_ref, alog_ref, dtb_ref, q_ref, k_ref, v_ref, b_ref, e_ref):
    W = x_ref.shape[1]
    c = c_ref[...]
    x = x_ref[...] * c[GDN_CONV - 1:GDN_CONV]
    for j in range(GDN_CONV - 1):
        x = x + h_ref[:, j * W:(j + 1) * W] * c[j:j + 1]
    x = _silu(x)
    us = us_ref[...]
    beta = _sigmoid(us)
    eg = jnp.exp(-jnp.exp(alog_ref[...]) * _softplus(us + dtb_ref[...]))
    nq = GDN_HEADS * GDN_DK
    for h in range(GDN_HEADS):
        sl = slice(h * GDN_DK, (h + 1) * GDN_DK)
        q_ref[:, sl] = _l2norm(x[:, sl]) * GDN_DK ** -0.5
        k_ref[:, sl] = _l2norm(x[:, nq + h * GDN_DK:nq + (h + 1) * GDN_DK])
        b_ref[:, sl] = jnp.broadcast_to(beta[:, STEP_COL_B + h:STEP_COL_B + h + 1], (x.shape[0], GDN_DK))
        e_ref[:, sl] = jnp.broadcast_to(eg[:, STEP_COL_A + h:STEP_COL_A + h + 1], (x.shape[0], GDN_DK))
    v_ref[...] = x[:, 2 * nq:]


def _gdn_step_body(q_ref, k_ref, v_ref, b_ref, e_ref, z_ref, nw_ref, s_ref, o_ref, so_ref):
    bb = q_ref.shape[0]
    eye = jnp.where(lax.broadcasted_iota(jnp.int32, (GDN_DK, GDN_DK), 0)
                    == lax.broadcasted_iota(jnp.int32, (GDN_DK, GDN_DK), 1), 1.0, 0.0)
    nw = nw_ref[...]

    def per_seq(bi, carry):
        q, k, v, beta, eg, z = q_ref[bi], k_ref[bi], v_ref[bi], b_ref[bi], e_ref[bi], z_ref[bi]
        k_t = _dot_nt(eye, k, HI)
        q_t = _dot_nt(eye, q, HI)
        outs = []
        for h in range(GDN_HEADS):
            r = slice(h, h + 1)
            s = s_ref[bi, h]
            kcol = k_t[:, h:h + 1]
            k_s = jnp.sum(kcol * s, axis=0, keepdims=True)
            q_s = jnp.sum(q_t[:, h:h + 1] * s, axis=0, keepdims=True)
            v_new = v[r] * beta[r] - (beta[r] * eg[r]) * k_s
            qk = jnp.sum(q[r] * k[r], axis=-1, keepdims=True)
            o = eg[r] * q_s + qk * v_new
            so_ref[bi, h] = s * eg[r] + kcol * v_new
            on = o * lax.rsqrt(jnp.mean(o * o, axis=-1, keepdims=True) + EPS) * nw
            outs.append(on * _silu(z[r]))
        o_ref[bi] = jnp.concatenate(outs, axis=0).astype(o_ref.dtype)
        return carry

    lax.fori_loop(0, bb, per_seq, 0)


def gdn_step(u_qkv, u_z, u_small, hist, state, conv_w, a_log, dt_bias, norm_w, bb=4):
    Bd, W = u_qkv.shape
    H = GDN_HEADS
    nq = H * GDN_DK
    row = lambda p: _place_lanes(((STEP_COL_A, p.reshape(1, H)),), (1,))
    full = lambda shape: pl.BlockSpec(shape, lambda i: (0,) * len(shape))
    outs = pl.pallas_call(
        _gdn_step_prep_body,
        grid=(1,),
        in_specs=[full((Bd, W)), full((Bd, (GDN_CONV - 1) * W)), full((GDN_CONV, W)), full((Bd, LANES)),
                  full((1, LANES)), full((1, LANES))],
        out_specs=[full((Bd, nq))] * 5,
        out_shape=[jax.ShapeDtypeStruct((Bd, nq), F32)] * 5,
        compiler_params=_cparams("arbitrary"),
        name="gdn_step_prep",
    )(u_qkv, hist.reshape(Bd, (GDN_CONV - 1) * W), conv_w, u_small, row(a_log), row(dt_bias))
    heads = lambda a: a.reshape(Bd, H, GDN_DK)
    vec_blk = pl.BlockSpec((bb, H, GDN_DK), lambda i: (i, 0, 0))
    st_blk = pl.BlockSpec((bb, H, GDN_DK, GDN_DV), lambda i: (i, 0, 0, 0))
    o, s_new = pl.pallas_call(
        _gdn_step_body,
        grid=(Bd // bb,),
        in_specs=[vec_blk] * 6 + [pl.BlockSpec((1, GDN_DV), lambda i: (0, 0)), st_blk],
        out_specs=[vec_blk, st_blk],
        out_shape=[jax.ShapeDtypeStruct((Bd, H, GDN_DV), BF16), jax.ShapeDtypeStruct(state.shape, F32)],
        compiler_params=_cparams("parallel"),
        name="gdn_step",
    )(*[heads(a) for a in outs], heads(u_z), norm_w.reshape(1, GDN_DV), state)
    return o.reshape(Bd, H * GDN_DV), s_new


def _softmax_with_new(s_past, valid, s_new):
    s_past = jnp.where(valid, s_past, -jnp.inf)
    m = jnp.maximum(jnp.max(s_past, axis=-1, keepdims=True), s_new)
    e = jnp.exp(s_past - m)
    e_new = jnp.exp(s_new - m)
    return e, e_new, 1.0 / (jnp.sum(e, axis=-1, keepdims=True) + e_new)


def _nsa_step_body(pt_ref, *refs, n_pages, page, n_past):
    lo = refs[:n_pages]
    hi = refs[n_pages:2 * n_pages]
    (q_ref, new_ref, win_ref, wnew_ref, gl_ref, pe_ref, w1_ref, w2_ref, ovl_ref, o_ref, wout_ref, xk_ref, xv_ref) = refs[2 * n_pages:]
    G, H, D = NSA_KV_HEADS, NSA_HEADS, HEAD_DIM
    HS = 2 * G
    pos = n_past
    cpp = page // CMP_STRIDE
    n_chunk = n_pages * cpp
    L = n_pages * page
    q = q_ref[0]
    qf = q.astype(F32)
    new = new_ref[0]
    row_g = lax.broadcasted_iota(jnp.int32, (H, 1), 0) // NSA_GROUP

    def per_head_rows(rows):
        return jnp.concatenate([jnp.broadcast_to(rows[g:g + 1], (NSA_GROUP, rows.shape[1])) for g in range(G)], axis=0)

    first_half = lax.broadcasted_iota(jnp.int32, (SUBLANES, D), 0) < G
    for j in range(n_pages):
        for s in range(CMP_STRIDE):
            for m in range(cpp // 4):
                ks, vs = [], []
                for pair in range(2):
                    ta = lo[j][0, (4 * m + 2 * pair) * CMP_STRIDE + s]
                    tb = lo[j][0, (4 * m + 2 * pair + 1) * CMP_STRIDE + s]
                    ks.append(jnp.where(first_half, ta, pltpu.roll(tb, G, axis=0)))
                    vs.append(jnp.where(first_half, pltpu.roll(ta, G, axis=0), tb))
                r0 = (j * cpp + 4 * m) * G
                xk_ref[r0:r0 + 4 * G, s * D:(s + 1) * D] = jnp.concatenate(ks, axis=0).astype(BF16)
                xv_ref[r0:r0 + 4 * G, s * D:(s + 1) * D] = jnp.concatenate(vs, axis=0).astype(BF16)
    ckv = []
    for c, x_ref in enumerate((xk_ref, xv_ref)):
        part = _dot(x_ref[...], w1_ref[c])
        ckv.append(_compress_mlp(part, pe_ref.at[c], w1_ref.at[c], w2_ref.at[c], G).astype(BF16))

    n_all = G * n_chunk
    s = _dot_nt(q, ckv[0])
    col = lax.broadcasted_iota(jnp.int32, (1, n_all), 1)
    blk = col // G
    ok = (col % G == row_g) & (blk * CMP_STRIDE + (CMP_LEN - 1) <= pos) & (blk < n_chunk - CMP_RATIO + 1)
    p = _masked_softmax(s, ok)
    o_c = _dot(p.astype(BF16), ckv[1])

    gsum = jnp.where(lax.broadcasted_iota(jnp.int32, (LANES, H), 0) == lax.broadcasted_iota(jnp.int32, (LANES, H), 1) // NSA_GROUP, 1.0, 0.0)
    psum = _dot(gsum, p, HI)
    p_hi, p_lo = _split_bf16(psum)
    ovl = ovl_ref[...]
    imp_t = _dot_nt(ovl, p_hi) + _dot_nt(ovl, p_lo)
    n_slc = -(-(L + 1) // SLC_LEN)
    sel_t = _select_blocks_t(imp_t, jnp.full((1, LANES), pos, jnp.int32), n_slc)
    J = sel_t.shape[0]
    eye8 = jnp.where(lax.broadcasted_iota(jnp.int32, (SUBLANES, LANES), 0) == lax.broadcasted_iota(jnp.int32, (SUBLANES, LANES), 1), 1.0, 0.0)
    sel = _dot_nt(eye8.astype(BF16), sel_t.astype(BF16)).astype(BF16)

    def slot_softmax(sc, tok_ok, s_new):
        slot = lax.broadcasted_iota(jnp.int32, (1, sc.shape[1]), 1) % HS
        return _softmax_with_new(sc, (slot == row_g) & tok_ok, s_new)

    tok = lax.broadcasted_iota(jnp.int32, (1, L * HS), 1) // HS
    expand = lax.broadcasted_iota(jnp.int32, (J, L * HS), 0) == tok // SLC_LEN
    chosen = per_head_rows(_dot(sel, jnp.where(expand, 1.0, 0.0).astype(BF16)))
    kv_hi = [hi[j][0].reshape(page * HS, D).astype(BF16) for j in range(n_pages)]
    sc = jnp.concatenate([_dot_nt(q, kv) for kv in kv_hi], axis=1)
    s_new = jnp.sum(qf * per_head_rows(new[2 * G:3 * G]), axis=-1, keepdims=True)
    e, e_new, inv = slot_softmax(sc, (chosen > 0.5) & (tok <= pos), s_new)
    e = pltpu.roll(e, G, axis=1).astype(BF16)
    acc = jnp.zeros((H, D), F32)
    for j, kv in enumerate(kv_hi):
        acc = acc + _dot(e[:, j * page * HS:(j + 1) * page * HS], kv)
    o_s = (acc + e_new * per_head_rows(new[3 * G:4 * G])) * inv

    n_win = win_ref.shape[1] // HS
    wnew = wnew_ref[0]
    win = win_ref[0].astype(BF16)
    wtok = pos - n_win + lax.broadcasted_iota(jnp.int32, (1, n_win * HS), 1) // HS
    sw_new = jnp.sum(qf * per_head_rows(wnew[0:G]), axis=-1, keepdims=True)
    ew, ew_new, winv = slot_softmax(_dot_nt(q, win), (pos - wtok < WINDOW) & (wtok >= 0), sw_new)
    o_w = (_dot(pltpu.roll(ew, G, axis=1).astype(BF16), win) + ew_new * per_head_rows(wnew[G:2 * G])) * winv

    gl = jnp.broadcast_to(_sigmoid(gl_ref[0]), (SUBLANES, LANES))
    hh = lax.broadcasted_iota(jnp.int32, (H, LANES), 0)
    cc = lax.broadcasted_iota(jnp.int32, (H, LANES), 1)
    gate = lambda br: _dot_nt(jnp.where(cc == STEP_COL_G + br * H + hh, 1.0, 0.0), gl, HI)[:, 0:1]
    o_ref[0] = (gate(0) * o_c + gate(1) * o_s + gate(2) * o_w).astype(o_ref.dtype)

    wout_ref[0, :(n_win - 1) * HS] = win_ref[0, HS:]
    wout_ref[0, (n_win - 1) * HS:] = wnew


def nsa_step(q, new_rows, win_new, cache, page_table, win_state, u_small, pe2, w1c, w2c):
    Bd = q.shape[0]
    n_pool, page = cache.shape[:2]
    n_pages = page_table.shape[1]
    G, H, D = NSA_KV_HEADS, NSA_HEADS, HEAD_DIM
    n_past = n_pages * page
    n_chunk = n_past // CMP_STRIDE
    n_win = win_state.shape[1]
    assert n_past % SLC_LEN == 0 and n_win == WINDOW <= n_past and page % (4 * CMP_STRIDE) == 0 and 2 * G == SUBLANES
    n_slc = -(-(n_past + 1) // SLC_LEN)
    J = -(-n_slc // (2 * SUBLANES)) * (2 * SUBLANES)
    ovl = np.zeros((J, n_chunk * G), np.float32)
    ovl[:n_slc] = np.repeat(_overlap_t(n_chunk, n_slc), G, axis=1)
    ovl = jnp.asarray(ovl, BF16)
    body = functools.partial(_nsa_step_body, n_pages=n_pages, page=page, n_past=n_past)
    half_spec = lambda j, half: pl.BlockSpec((1, page, None, 2 * G, D), lambda b, pt: (pt[b, j], 0, half, 0, 0))
    const = lambda a: pl.BlockSpec(a.shape, lambda b, pt: (0,) * a.ndim, pipeline_mode=pl.Buffered(1))
    per_b = lambda shape: pl.BlockSpec((1,) + shape, lambda b, pt: (b,) + (0,) * len(shape))
    cache5 = cache.reshape(n_pool, page, 2, 2 * G, D)
    out, win_out = pl.pallas_call(
        body,
        grid_spec=pltpu.PrefetchScalarGridSpec(
            num_scalar_prefetch=1,
            grid=(Bd,),
            in_specs=[half_spec(j, 0) for j in range(n_pages)] + [half_spec(j, 1) for j in range(n_pages)]
                     + [per_b((H, D)), per_b((4 * G, D)), per_b((n_win * 2 * G, D)), per_b((2 * G, D)), per_b((1, LANES)),
                        const(pe2), const(w1c), const(w2c), const(ovl)],
            out_specs=[per_b((H, D)), per_b((n_win * 2 * G, D))],
            scratch_shapes=[pltpu.VMEM((G * n_chunk, CMP_FEAT), BF16)] * 2,
        ),
        out_shape=[jax.ShapeDtypeStruct((Bd, H, D), BF16), jax.ShapeDtypeStruct((Bd, n_win * 2 * G, D), F32)],
        compiler_params=_cparams("arbitrary"),
        name="nsa_step",
    )(page_table, *([cache5] * (2 * n_pages)), q.reshape(Bd, H, D), new_rows.reshape(Bd, 4 * G, D),
      win_state.reshape(Bd, n_win * 2 * G, D), win_new.reshape(Bd, 2 * G, D), u_small.reshape(Bd, 1, LANES),
      pe2, w1c, w2c, ovl)
    return out.reshape(Bd, H * D), win_out.reshape(win_state.shape)


def rope_tables(pos):
    half = HEAD_DIM // 2
    inv = ROPE_THETA ** (-jnp.arange(half, dtype=F32) / half)
    ang = pos.astype(F32)[:, None] * inv[None, :]
    cos, sin = jnp.cos(ang), jnp.sin(ang)
    return jnp.concatenate([cos, cos], axis=-1), jnp.concatenate([-sin, sin], axis=-1)


def _split_w_in(w_in):
    H, G = NSA_HEADS, NSA_KV_HEADS
    sizes = (H * HEAD_DIM, 6 * G * HEAD_DIM, 3 * H, GDN_HEADS * (2 * GDN_DK + GDN_DV), GDN_HEADS * GDN_DV, 2 * GDN_HEADS)
    o = np.cumsum((0,) + sizes).tolist()
    K = w_in.shape[0]
    w_qkv = w_in[:, o[0]:o[2]].astype(BF16)
    w_gqkv = w_in[:, o[3]:o[4]].astype(BF16)
    w_z = w_in[:, o[4]:o[5]].astype(BF16)
    w_mg = w_in[:, o[6]:].astype(BF16)
    w_ng = w_in[:, o[2]:o[3]]
    w_ba = w_in[:, o[5]:o[6]]
    nhb = GDN_HEADS // GDN_HB
    ng = w_ng.reshape(K, 3, G, NSA_GROUP).transpose(0, 2, 1, 3).reshape(K, G, 3 * NSA_GROUP)
    pieces = ((GDN_COL_B, w_ba[:, :GDN_HEADS].reshape(K, nhb, GDN_HB)),
              (GDN_COL_A, w_ba[:, GDN_HEADS:].reshape(K, nhb, GDN_HB)), (NSA_COL_G, ng))
    w_small_seq = _place_lanes(pieces, (K, nhb)).reshape(K, nhb * LANES).astype(BF16)
    w_small_step = _place_lanes(((STEP_COL_G, w_ng), (STEP_COL_B, w_ba)), (K,)).astype(BF16)
    return w_qkv, w_gqkv, w_z, w_mg, w_small_seq, w_small_step


def _place_lanes(pieces, lead):
    out, at = [], 0
    for lane, vals in pieces:
        out += [jnp.zeros(lead + (lane - at,), F32), vals.astype(F32)]
        at = lane + vals.shape[-1]
    return jnp.concatenate(out + [jnp.zeros(lead + (LANES - at,), F32)], axis=-1)


def kernel(x_prompt, x_sample, cache_nsa_kv, page_table, state_win_kv, state_gdn, state_gdn_conv, state_ffn_conv, norm_mix, w_in, cmp_pe, cmp_w1, cmp_w2, gdn_conv_w, gdn_a_log, gdn_dt_bias, gdn_norm, w_nsa_out, w_gdn_out, w_o, norm_ffn, w_up, ffn_conv_w, w_down, norm_final):
    B, T, D = x_prompt.shape
    Bd = x_sample.shape[0]
    G = NSA_KV_HEADS
    n_past = page_table.shape[1] * cache_nsa_kv.shape[1]
    F = w_down.shape[0]

    w_qkv, w_gqkv, w_z, w_mg, w_small_seq, w_small_step = _split_w_in(w_in)
    w_nsa_b, w_gdn_b, w_o_b = w_nsa_out.astype(BF16), w_gdn_out.astype(BF16), w_o.astype(BF16)
    w_up_b, w_down_b = w_up.astype(BF16), w_down.astype(BF16)
    pe2, w1c, w2c = compress_params(cmp_pe, cmp_w1, cmp_w2)

    def tail(x, mixed_nsa, mixed_gdn, u_mg, ffn_up, tm):
        mixin = matmul_mix(mixed_nsa, mixed_gdn, w_nsa_b, w_gdn_b, u_mg, tm, 512)
        h = matmul_residual(mixin, w_o_b, x, tm, 512)
        hn = rmsnorm(h, norm_ffn, BF16, min(tm, 256))
        act, extra = ffn_up(hn)
        h2 = matmul_residual(act, w_down_b, h, min(tm, 512), 256)
        return rmsnorm(h2, norm_final, F32, min(tm, 256)), extra

    xp = x_prompt.reshape(B * T, D)
    xn = rmsnorm(xp, norm_mix, BF16, 256)
    tm = 1024
    u_qkv = matmul(xn, w_qkv, tm, 512)
    u_gqkv = matmul(xn, w_gqkv, tm, 512)
    u_z = matmul(xn, w_z, tm, 512)
    u_mg = matmul(xn, w_mg, tm, 512)
    u_small = matmul(xn, w_small_seq, tm, w_small_seq.shape[1])
    cos, sin = rope_tables(jnp.arange(T, dtype=jnp.int32))
    q, kv_rows, win_rows = rope_split(u_qkv, cos, sin, 256)
    cmp_kv = compress_seq(kv_rows, pe2, w1c, w2c, B)
    o_nsa = nsa_seq(q, kv_rows, win_rows, cmp_kv, u_small, B)
    a_rows, d_rows = gdn_param_rows(gdn_a_log, gdn_dt_bias)
    o_gdn, s_p = gdn_seq(u_gqkv, u_z, u_small, gdn_conv_w, a_rows, d_rows, gdn_norm, B, 256)
    s_p = s_p.reshape(B, GDN_HEADS, GDN_DK, GDN_DV)

    def ffn_up_p(hn):
        act, tg, tv = ffn_up_seq(hn, w_up_b, ffn_conv_w, B, tm, 256)
        last = lambda a: a[T // tm - 1::T // tm, SUBLANES - (FFN_CONV - 1):]
        return act, jnp.concatenate([last(tg), last(tv)], axis=-1)

    y_p, ffn_p = tail(xp, o_nsa, o_gdn, u_mg, ffn_up_p, tm)
    n_win_p = min(WINDOW, T)
    kv_p = kv_rows.reshape(B, T, 4, G, HEAD_DIM)
    win_p = win_rows.reshape(B, T, 2, G, HEAD_DIM)[:, T - n_win_p:]
    conv_p = u_gqkv.reshape(B, T, -1)[:, T - (GDN_CONV - 1):]

    xs = x_sample.reshape(Bd, D)
    xn = rmsnorm(xs, norm_mix, BF16, Bd)
    u_qkv = matmul(xn, w_qkv, Bd, 512)
    u_gqkv = matmul(xn, w_gqkv, Bd, 512)
    u_z = matmul(xn, w_z, Bd, 512)
    u_mg = matmul(xn, w_mg, Bd, 512)
    u_small = matmul(xn, w_small_step, Bd, LANES)
    cos, sin = rope_tables(jnp.full((Bd,), n_past, jnp.int32))
    q, new_rows, win_new = rope_split(u_qkv, cos, sin, Bd)
    o_nsa, win_s = nsa_step(q, new_rows, win_new, cache_nsa_kv, page_table, state_win_kv, u_small, pe2, w1c, w2c)
    o_gdn, s_s = gdn_step(u_gqkv, u_z, u_small, state_gdn_conv, state_gdn, gdn_conv_w, gdn_a_log, gdn_dt_bias, gdn_norm)

    def ffn_up_s(hn):
        act, ug, uv = ffn_up_step(hn, w_up_b, ffn_conv_w, state_ffn_conv, 256)
        return act, jnp.concatenate([ug, uv], axis=-1)

    y_s, up_new = tail(xs, o_nsa, o_gdn, u_mg, ffn_up_s, Bd)
    kv_s = new_rows.reshape(Bd, 1, 4, G, HEAD_DIM)
    conv_s = jnp.concatenate([state_gdn_conv[:, 1:], u_gqkv[:, None]], axis=1)
    ffn_s = jnp.concatenate([state_ffn_conv[:, 1:], up_new[:, None]], axis=1)

    return (y_p.reshape(B, T, D), y_s.reshape(Bd, 1, D), kv_p, win_p, s_p, conv_p, ffn_p,
            kv_s, win_s, s_s, conv_s, ffn_s)
```

```python
import functools
import math

import jax
import jax.numpy as jnp
import numpy as np
from jax import lax
from jax.experimental import pallas as pl
from jax.experimental.pallas import tpu as pltpu

F32 = jnp.float32
BF16 = jnp.bfloat16
HI = lax.Precision.HIGHEST

LANES = 128
SUBLANES = 8
VMEM_LIMIT_BYTES = 56 * 1024 * 1024

HEAD_DIM = 128
NSA_HEADS = 16
NSA_KV_HEADS = 4
NSA_GROUP = NSA_HEADS // NSA_KV_HEADS
CMP_LEN = 32
CMP_STRIDE = 16
CMP_HIDDEN = 256
SLC_LEN = 64
SLC_TOP_N = 16
SLC_LOCAL = 2
WINDOW = 512
GDN_HEADS = 16
GDN_DK = 128
GDN_DV = 128
GDN_CONV = 4
GDN_CHUNK = 64
FFN_CONV = 3
ROPE_THETA = 10000.0
EPS = 1e-6


def _cparams(*sem):
    return pltpu.CompilerParams(dimension_semantics=sem, vmem_limit_bytes=VMEM_LIMIT_BYTES)


def _dot(a, b, precision=None):
    return jnp.dot(a, b, preferred_element_type=F32, precision=precision)


def _dot_nt(a, b, precision=None):
    return lax.dot_general(a, b, (((1,), (1,)), ((), ())), preferred_element_type=F32, precision=precision)


def _dot_tn(a, b, precision=None):
    return lax.dot_general(a, b, (((0,), (0,)), ((), ())), preferred_element_type=F32, precision=precision)


def _sigmoid(x):
    return 1.0 / (1.0 + jnp.exp(-x))


def _silu(x):
    return x * _sigmoid(x)


def _rmsnorm_body(x_ref, w_ref, o_ref):
    x = x_ref[...]
    y = x * lax.rsqrt(jnp.mean(x * x, axis=-1, keepdims=True) + EPS)
    o_ref[...] = (y * w_ref[...]).astype(o_ref.dtype)


def rmsnorm(x, w, out_dtype, tm):
    M, D = x.shape
    return pl.pallas_call(
        _rmsnorm_body,
        grid=(M // tm,),
        in_specs=[pl.BlockSpec((tm, D), lambda i: (i, 0)), pl.BlockSpec((1, D), lambda i: (0, 0))],
        out_specs=pl.BlockSpec((tm, D), lambda i: (i, 0)),
        out_shape=jax.ShapeDtypeStruct((M, D), out_dtype),
        compiler_params=_cparams("parallel"),
        name="rmsnorm",
    )(x, w.reshape(1, D))


def _mm_body(x_ref, w_ref, o_ref):
    o_ref[...] = _dot(x_ref[...], w_ref[...]).astype(o_ref.dtype)


def matmul(x, w, tm, tn, out_dtype=F32):
    M, K = x.shape
    N = w.shape[1]
    return pl.pallas_call(
        _mm_body,
        grid=(M // tm, N // tn),
        in_specs=[pl.BlockSpec((tm, K), lambda i, j: (i, 0)), pl.BlockSpec((K, tn), lambda i, j: (0, j))],
        out_specs=pl.BlockSpec((tm, tn), lambda i, j: (i, j)),
        out_shape=jax.ShapeDtypeStruct((M, N), out_dtype),
        compiler_params=_cparams("parallel", "arbitrary"),
        name="matmul",
    )(x, w)


def _mm_res_body(x_ref, w_ref, r_ref, o_ref):
    o_ref[...] = r_ref[...] + _dot(x_ref[...], w_ref[...])


def matmul_residual(x, w, res, tm, tn):
    M, K = x.shape
    N = w.shape[1]
    return pl.pallas_call(
        _mm_res_body,
        grid=(M // tm, N // tn),
        in_specs=[pl.BlockSpec((tm, K), lambda i, j: (i, 0)), pl.BlockSpec((K, tn), lambda i, j: (0, j)),
                  pl.BlockSpec((tm, tn), lambda i, j: (i, j))],
        out_specs=pl.BlockSpec((tm, tn), lambda i, j: (i, j)),
        out_shape=jax.ShapeDtypeStruct((M, N), F32),
        compiler_params=_cparams("parallel", "arbitrary"),
        name="matmul_residual",
    )(x, w, res)


def _mm_mix_body(a_ref, b_ref, wa_ref, wb_ref, ga_ref, gb_ref, o_ref):
    ua = _dot(a_ref[...], wa_ref[...])
    ub = _dot(b_ref[...], wb_ref[...])
    o_ref[...] = (_sigmoid(ga_ref[...]) * ua + _sigmoid(gb_ref[...]) * ub).astype(o_ref.dtype)


def matmul_mix(a, b, wa, wb, gates, tm, tn):
    M, K = a.shape
    N = wa.shape[1]
    nj = N // tn
    return pl.pallas_call(
        _mm_mix_body,
        grid=(M // tm, nj),
        in_specs=[pl.BlockSpec((tm, K), lambda i, j: (i, 0)), pl.BlockSpec((tm, K), lambda i, j: (i, 0)),
                  pl.BlockSpec((K, tn), lambda i, j: (0, j)), pl.BlockSpec((K, tn), lambda i, j: (0, j)),
                  pl.BlockSpec((tm, tn), lambda i, j: (i, j)), pl.BlockSpec((tm, tn), lambda i, j: (i, j + nj))],
        out_specs=pl.BlockSpec((tm, tn), lambda i, j: (i, j)),
        out_shape=jax.ShapeDtypeStruct((M, N), BF16),
        compiler_params=_cparams("parallel", "arbitrary"),
        name="matmul_mix",
    )(a, b, wa, wb, gates, gates)


def _shift_rows(x, hist, sh):
    s = pltpu.roll(x, sh, axis=0)
    head = s[:SUBLANES]
    row8 = lax.broadcasted_iota(jnp.int32, head.shape, 0)
    for r in range(sh):
        head = jnp.where(row8 == r, hist[SUBLANES - sh + r:SUBLANES - sh + r + 1], head)
    return jnp.concatenate([head, s[SUBLANES:]], axis=0)


def _ffn_up_seq_body(x_ref, wg_ref, wv_ref, cg_ref, cv_ref, act_ref, tg_ref, tv_ref, hg_ref, hv_ref, *, tiles_per_seq):
    i = pl.program_id(0)
    j = pl.program_id(1)
    first = (i % tiles_per_seq) == 0
    x = x_ref[...]

    def branch(w_ref, c_ref, hist_ref, tail_ref):
        up = _dot(x, w_ref[...])
        tm = up.shape[0]
        hist = jnp.where(first, 0.0, hist_ref[j])
        c = c_ref[...]
        out = up * c[2:3] + _shift_rows(up, hist, 2) * c[0:1] + _shift_rows(up, hist, 1) * c[1:2]
        tail = up[tm - SUBLANES:tm]
        hist_ref[j] = tail
        tail_ref[0] = tail
        return out

    gate = branch(wg_ref, cg_ref, hg_ref, tg_ref)
    val = branch(wv_ref, cv_ref, hv_ref, tv_ref)
    act_ref[...] = (_silu(gate) * val).astype(act_ref.dtype)


def ffn_up_seq(x, w_up, conv_w, n_seq, tm, tn):
    M, K = x.shape
    F = w_up.shape[1] // 2
    nj = F // tn
    tiles_per_seq = M // n_seq // tm
    body = functools.partial(_ffn_up_seq_body, tiles_per_seq=tiles_per_seq)
    return pl.pallas_call(
        body,
        grid=(M // tm, nj),
        in_specs=[pl.BlockSpec((tm, K), lambda i, j: (i, 0)),
                  pl.BlockSpec((K, tn), lambda i, j: (0, j)), pl.BlockSpec((K, tn), lambda i, j: (0, j + nj)),
                  pl.BlockSpec((FFN_CONV, tn), lambda i, j: (0, j)), pl.BlockSpec((FFN_CONV, tn), lambda i, j: (0, j + nj))],
        out_specs=[pl.BlockSpec((tm, tn), lambda i, j: (i, j)),
                   pl.BlockSpec((1, SUBLANES, tn), lambda i, j: (i, 0, j)),
                   pl.BlockSpec((1, SUBLANES, tn), lambda i, j: (i, 0, j))],
        out_shape=[jax.ShapeDtypeStruct((M, F), BF16),
                   jax.ShapeDtypeStruct((M // tm, SUBLANES, F), F32), jax.ShapeDtypeStruct((M // tm, SUBLANES, F), F32)],
        scratch_shapes=[pltpu.VMEM((nj, SUBLANES, tn), F32), pltpu.VMEM((nj, SUBLANES, tn), F32)],
        compiler_params=_cparams("arbitrary", "arbitrary"),
        name="ffn_up_seq",
    )(x, w_up, w_up, conv_w, conv_w)


def _ffn_up_step_body(x_ref, wg_ref, wv_ref, cg_ref, cv_ref, g0_ref, g1_ref, v0_ref, v1_ref, act_ref, ug_ref, uv_ref):
    x = x_ref[...]

    def branch(w_ref, c_ref, h0_ref, h1_ref, up_ref):
        up = _dot(x, w_ref[...])
        up_ref[...] = up
        c = c_ref[...]
        return up * c[2:3] + h0_ref[...] * c[0:1] + h1_ref[...] * c[1:2]

    gate = branch(wg_ref, cg_ref, g0_ref, g1_ref, ug_ref)
    val = branch(wv_ref, cv_ref, v0_ref, v1_ref, uv_ref)
    act_ref[...] = (_silu(gate) * val).astype(act_ref.dtype)


def ffn_up_step(x, w_up, conv_w, hist, tn):
    M, K = x.shape
    F2 = w_up.shape[1]
    F = F2 // 2
    nj = F // tn
    h2 = hist.reshape(M, 2 * F2)
    return pl.pallas_call(
        _ffn_up_step_body,
        grid=(nj,),
        in_specs=[pl.BlockSpec((M, K), lambda j: (0, 0)),
                  pl.BlockSpec((K, tn), lambda j: (0, j)), pl.BlockSpec((K, tn), lambda j: (0, j + nj)),
                  pl.BlockSpec((FFN_CONV, tn), lambda j: (0, j)), pl.BlockSpec((FFN_CONV, tn), lambda j: (0, j + nj)),
                  pl.BlockSpec((M, tn), lambda j: (0, j)), pl.BlockSpec((M, tn), lambda j: (0, j + 2 * nj)),
                  pl.BlockSpec((M, tn), lambda j: (0, j + nj)), pl.BlockSpec((M, tn), lambda j: (0, j + 3 * nj))],
        out_specs=[pl.BlockSpec((M, tn), lambda j: (0, j))] * 3,
        out_shape=[jax.ShapeDtypeStruct((M, F), BF16), jax.ShapeDtypeStruct((M, F), F32), jax.ShapeDtypeStruct((M, F), F32)],
        compiler_params=_cparams("arbitrary"),
        name="ffn_up_step",
    )(x, w_up, w_up, conv_w, conv_w, h2, h2, h2, h2)


def _rope_body(u_ref, cos_ref, sin_ref, q_ref, kv_ref, win_ref):
    cos = cos_ref[...]
    sin = sin_ref[...]
    scale = HEAD_DIM ** -0.5

    def rot(x):
        return x * cos + pltpu.roll(x, HEAD_DIM // 2, axis=1) * sin

    for h in range(NSA_HEADS):
        sl = slice(h * HEAD_DIM, (h + 1) * HEAD_DIM)
        q_ref[:, sl] = (rot(u_ref[:, sl]) * scale).astype(q_ref.dtype)
    base = NSA_HEADS * HEAD_DIM
    n_glob = 4 * NSA_KV_HEADS
    for slot in range(6 * NSA_KV_HEADS):
        x = u_ref[:, base + slot * HEAD_DIM:base + (slot + 1) * HEAD_DIM]
        if (slot // NSA_KV_HEADS) % 2 == 0:
            x = rot(x)
        if slot < n_glob:
            kv_ref[:, slot * HEAD_DIM:(slot + 1) * HEAD_DIM] = x
        else:
            win_ref[:, (slot - n_glob) * HEAD_DIM:(slot - n_glob + 1) * HEAD_DIM] = x


def rope_split(u, cos, sin, tm):
    M, W = u.shape
    nt = cos.shape[0] // tm
    nq = NSA_HEADS * HEAD_DIM
    ng = 4 * NSA_KV_HEADS * HEAD_DIM
    nw = 2 * NSA_KV_HEADS * HEAD_DIM
    return pl.pallas_call(
        _rope_body,
        grid=(M // tm,),
        in_specs=[pl.BlockSpec((tm, W), lambda i: (i, 0)),
                  pl.BlockSpec((tm, HEAD_DIM), lambda i: (i % nt, 0)), pl.BlockSpec((tm, HEAD_DIM), lambda i: (i % nt, 0))],
        out_specs=[pl.BlockSpec((tm, nq), lambda i: (i, 0)), pl.BlockSpec((tm, ng), lambda i: (i, 0)),
                   pl.BlockSpec((tm, nw), lambda i: (i, 0))],
        out_shape=[jax.ShapeDtypeStruct((M, nq), BF16), jax.ShapeDtypeStruct((M, ng), F32), jax.ShapeDtypeStruct((M, nw), F32)],
        compiler_params=_cparams("parallel"),
        name="rope_split",
    )(u, cos, sin)


GDN_HB = 4
GDN_COL_B = 0
GDN_COL_A = 8
NSA_COL_G = 16
GDN_STEP_GROUPS = 2


def _softplus(x):
    return jnp.maximum(x, 0.0) + jnp.log1p(jnp.exp(-jnp.abs(x)))


def _l2norm(x):
    return x * lax.rsqrt(jnp.sum(x * x, axis=-1, keepdims=True) + EPS)


def _solve_unit_lower(a_list, r_list, order):
    steps = int(math.log2(order))
    a_list, r_list = list(a_list), list(r_list)
    m, n = a_list[0].shape[1], r_list[0].shape[1]
    for i in range(steps):
        last = i == steps - 1
        for c in range(len(a_list)):
            a_b = a_list[c].astype(BF16)
            r_hi, r_lo = _split_bf16(r_list[c])
            prod = _dot(a_b, jnp.concatenate(([] if last else [a_b]) + [r_hi, r_lo], axis=1))
            if not last:
                a_list[c], prod = prod[:, :m], prod[:, m:]
            r_list[c] = r_list[c] + (prod[:, :n] + prod[:, n:])
    return r_list


def _block_diag(blocks):
    z = jnp.zeros_like(blocks[0])
    nb = len(blocks)
    return jnp.concatenate([jnp.concatenate([blocks[h] if j == h else z for j in range(nb)], axis=1)
                            for h in range(nb)], axis=0)


def _gdn_chunks_groups(groups, states):
    nh = len(groups[0][0][0])
    C, d = groups[0][0][2][0].shape
    W = nh * C
    stack = lambda xs: jnp.concatenate(xs, axis=0)
    bd = lambda xs: _block_diag([x.astype(BF16) for x in xs])
    ri = lax.broadcasted_iota(jnp.int32, (W, W), 0)
    ci = lax.broadcasted_iota(jnp.int32, (W, W), 1)
    same = (ri // C) == (ci // C)

    a_list, r_list, pre = [], [], []
    for chunks in groups:
        for q, k, v, beta, gc, gcr in chunks:
            dmat = jnp.where(same & (ri >= ci), jnp.exp(jnp.minimum(stack(gc) - jnp.concatenate(gcr, axis=1), 0.0)), 0.0)
            eg = [jnp.exp(g) for g in gc]
            kb = [k[h] * beta[h] for h in range(nh)]
            xk = _dot_nt(jnp.concatenate([bd(kb), bd(q)], axis=0), bd(k))
            a_list.append(jnp.where(same & (ri > ci), -(xk[:W] * dmat), 0.0))
            r_list.append(jnp.concatenate([stack([v[h] * beta[h] for h in range(nh)]),
                                           stack([kb[h] * eg[h] for h in range(nh)])], axis=1))
            g_last = [g[C - 1:C, :] for g in gc]
            pre.append(((xk[W:] * dmat).astype(BF16), bd([q[h] * eg[h] for h in range(nh)]),
                        bd([k[h] * jnp.exp(g_last[h] - gc[h]) for h in range(nh)]),
                        stack([jnp.broadcast_to(jnp.exp(g_last[h]), (d, 1)) for h in range(nh)])))
    w_list = _solve_unit_lower(a_list, r_list, C)

    n_chunks = len(groups[0])
    states = list(states)
    outs = [[] for _ in groups]
    for c in range(n_chunks):
        for g in range(len(groups)):
            w, (aqk, q_eg, kd, keep) = w_list[g * n_chunks + c], pre[g * n_chunks + c]
            wk = bd([w[h * C:(h + 1) * C, d:] for h in range(nh)])
            xs = _dot(jnp.concatenate([wk, q_eg], axis=0), states[g].astype(BF16))
            v_new = (w[:, :d] - xs[:W]).astype(BF16)
            outs[g].append(xs[W:] + _dot(aqk, v_new))
            states[g] = states[g] * keep + _dot_tn(kd, v_new)
    return outs, states


def _gdn_seq_body(q_ref, k_ref, v_ref, z_ref, ba_ref, cq_ref, ck_ref, cv_ref, alog_ref, dtb_ref, nw_ref,
                  o_ref, s_ref, tq_ref, tk_ref, tv_ref, qs_ref, ks_ref, vs_ref, gs_ref, bs_ref):
    t_idx = pl.program_id(2)
    tt = q_ref.shape[0]
    hb = q_ref.shape[1] // GDN_DK
    C = GDN_CHUNK

    @pl.when(t_idx == 0)
    def _():
        s_ref[...] = jnp.zeros_like(s_ref)
        tq_ref[...] = jnp.zeros_like(tq_ref)
        tk_ref[...] = jnp.zeros_like(tk_ref)
        tv_ref[...] = jnp.zeros_like(tv_ref)

    def conv(x_ref, c_ref, tail_ref):
        x = x_ref[...]
        hist = tail_ref[...]
        row = lax.broadcasted_iota(jnp.int32, x.shape, 0)
        c = c_ref[...]
        out = x * c[GDN_CONV - 1:GDN_CONV]
        for j in range(GDN_CONV - 1):
            out = out + _shift_rows(x, hist, GDN_CONV - 1 - j) * c[j:j + 1]
        tail_ref[...] = x[tt - SUBLANES:tt]
        return _silu(out)

    qa = conv(q_ref, cq_ref, tq_ref)
    ka = conv(k_ref, ck_ref, tk_ref)
    vs_ref[...] = conv(v_ref, cv_ref, tv_ref)
    for h in range(hb):
        sl = slice(h * GDN_DK, (h + 1) * GDN_DK)
        qs_ref[:, sl] = _l2norm(qa[:, sl]) * GDN_DK ** -0.5
        ks_ref[:, sl] = _l2norm(ka[:, sl])
    ba = ba_ref[...]
    bs_ref[...] = _sigmoid(ba)
    n_groups = hb // GDN_HB
    par_row = lambda ref: jnp.concatenate([ref[g, 0:1, :] for g in range(n_groups)], axis=1)
    gs_ref[...] = -jnp.exp(par_row(alog_ref)) * _softplus(ba + par_row(dtb_ref))

    tri = (lax.broadcasted_iota(jnp.int32, (C, C), 0) >= lax.broadcasted_iota(jnp.int32, (C, C), 1)).astype(F32)
    sel = (lax.broadcasted_iota(jnp.int32, (2 * SUBLANES, LANES), 0)
           == lax.broadcasted_iota(jnp.int32, (2 * SUBLANES, LANES), 1)).astype(F32)
    nw = nw_ref[...]

    chunk_rows = [slice(c * C, (c + 1) * C) for c in range(tt // C)]
    gc_alls = [_dot(tri, gs_ref[rows, :], HI) for rows in chunk_rows]
    groups, group_heads = [], []
    for g in range(n_groups):
        heads = [slice((g * GDN_HB + h) * GDN_DK, (g * GDN_HB + h + 1) * GDN_DK) for h in range(GDN_HB)]
        lanes = slice(g * LANES, (g + 1) * LANES)
        chunks = []
        for rows, gc_full in zip(chunk_rows, gc_alls):
            gc_all = gc_full[:, lanes]
            gc_t = _dot_nt(sel, gc_all, HI)
            beta_all = bs_ref[rows, lanes]
            chunks.append((
                [qs_ref[rows, sl] for sl in heads], [ks_ref[rows, sl] for sl in heads], [vs_ref[rows, sl] for sl in heads],
                [beta_all[:, GDN_COL_B + h:GDN_COL_B + h + 1] for h in range(GDN_HB)],
                [gc_all[:, GDN_COL_A + h:GDN_COL_A + h + 1] for h in range(GDN_HB)],
                [gc_t[GDN_COL_A + h:GDN_COL_A + h + 1, :] for h in range(GDN_HB)]))
        groups.append(chunks)
        group_heads.append(heads)
    gdk = GDN_HB * GDN_DK
    outs, states = _gdn_chunks_groups(groups, [s_ref[0, g * gdk:(g + 1) * gdk] for g in range(n_groups)])
    for g in range(n_groups):
        s_ref[0, g * gdk:(g + 1) * gdk] = states[g]
        for rows, o in zip(chunk_rows, outs[g]):
            on = o * lax.rsqrt(jnp.mean(o * o, axis=-1, keepdims=True) + EPS) * nw
            for h, sl in enumerate(group_heads[g]):
                o_ref[rows, sl] = (on[h * C:(h + 1) * C] * _silu(z_ref[rows, sl])).astype(o_ref.dtype)


def gdn_seq(u_qkv, u_z, u_small, conv_w, alog_rows, dtb_rows, norm_w, n_seq, tt):
    M = u_qkv.shape[0]
    T = M // n_seq
    nt = T // tt
    hb = GDN_HB * GDN_STEP_GROUPS
    nhb = GDN_HEADS // hb
    wb = hb * GDN_DK
    gl = GDN_STEP_GROUPS * LANES
    row_blk = lambda off: pl.BlockSpec((tt, wb), lambda b, h, t: (b * nt + t, h + off))
    cw_blk = lambda off: pl.BlockSpec((GDN_CONV, wb), lambda b, h, t: (0, h + off))
    par_blk = pl.BlockSpec((GDN_STEP_GROUPS, SUBLANES, LANES), lambda b, h, t: (h, 0, 0))
    return pl.pallas_call(
        _gdn_seq_body,
        grid=(n_seq, nhb, nt),
        in_specs=[row_blk(0), row_blk(nhb), row_blk(2 * nhb), row_blk(0),
                  pl.BlockSpec((tt, gl), lambda b, h, t: (b * nt + t, h)),
                  cw_blk(0), cw_blk(nhb), cw_blk(2 * nhb), par_blk, par_blk,
                  pl.BlockSpec((1, GDN_DV), lambda b, h, t: (0, 0))],
        out_specs=[row_blk(0), pl.BlockSpec((1, hb * GDN_DK, GDN_DV), lambda b, h, t: (b, h, 0))],
        out_shape=[jax.ShapeDtypeStruct((M, GDN_HEADS * GDN_DV), BF16),
                   jax.ShapeDtypeStruct((n_seq, GDN_HEADS * GDN_DK, GDN_DV), F32)],
        scratch_shapes=[pltpu.VMEM((SUBLANES, wb), F32)] * 3 + [pltpu.VMEM((tt, wb), F32)] * 3
                       + [pltpu.VMEM((tt, gl), F32)] * 2,
        compiler_params=_cparams("parallel", "parallel", "arbitrary"),
        name="gdn_seq",
    )(u_qkv, u_qkv, u_qkv, u_z, u_small, conv_w, conv_w, conv_w, alog_rows, dtb_rows, norm_w.reshape(1, GDN_DV))


def gdn_param_rows(gdn_a_log, gdn_dt_bias):
    nhb = GDN_HEADS // GDN_HB

    def rows(p):
        r = _place_lanes(((GDN_COL_A, p.reshape(nhb, GDN_HB)),), (nhb,))
        return jnp.broadcast_to(r[:, None, :], (nhb, SUBLANES, LANES))

    return rows(gdn_a_log), rows(gdn_dt_bias)


CMP_RATIO = CMP_LEN // CMP_STRIDE
CMP_FEAT = CMP_STRIDE * HEAD_DIM
MASKED = -1e30


def _compress_mlp(part, pe_ref, w1_ref, w2_ref, chunk_step=1):
    n = part.shape[0]
    pe_part = _dot(pe_ref[...], w1_ref[...])
    hid0 = pe_part[0:1, :CMP_HIDDEN] + pe_part[1:2, CMP_HIDDEN:]
    hid = hid0 + part[:, :CMP_HIDDEN]
    hid = hid + pltpu.roll(part[:, CMP_HIDDEN:], n - chunk_step, axis=0)
    return _dot(_silu(hid).astype(BF16), w2_ref[...])


def _compress_seq_body(x_ref, pe_ref, w1_ref, w2_ref, o_ref, xc_ref):
    n = o_ref.shape[3]
    for s in range(CMP_STRIDE):
        xc_ref[:, s * HEAD_DIM:(s + 1) * HEAD_DIM] = x_ref[pl.ds(s, n, stride=CMP_STRIDE), :].astype(BF16)
    part = _dot(xc_ref[...], w1_ref[0])
    o_ref[0, 0, 0] = _compress_mlp(part, pe_ref.at[0], w1_ref.at[0], w2_ref.at[0])


def compress_seq(kv_rows, pe2, w1c, w2c, n_seq):
    M = kv_rows.shape[0]
    T = M // n_seq
    n = T // CMP_STRIDE
    G = NSA_KV_HEADS
    return pl.pallas_call(
        _compress_seq_body,
        grid=(n_seq, 2, G),
        in_specs=[pl.BlockSpec((T, HEAD_DIM), lambda b, c, g: (b, c * G + g)),
                  pl.BlockSpec((1, SUBLANES, CMP_FEAT), lambda b, c, g: (c, 0, 0)),
                  pl.BlockSpec((1, CMP_FEAT, 2 * CMP_HIDDEN), lambda b, c, g: (c, 0, 0)),
                  pl.BlockSpec((1, CMP_HIDDEN, HEAD_DIM), lambda b, c, g: (c, 0, 0))],
        out_specs=pl.BlockSpec((1, 1, 1, n, HEAD_DIM), lambda b, c, g: (b, c, g, 0, 0)),
        out_shape=jax.ShapeDtypeStruct((n_seq, 2, G, n, HEAD_DIM), F32),
        scratch_shapes=[pltpu.VMEM((n, CMP_FEAT), BF16)],
        compiler_params=_cparams("parallel", "parallel", "parallel"),
        name="compress_seq",
    )(kv_rows, pe2, w1c, w2c)


def compress_params(cmp_pe, cmp_w1, cmp_w2):
    w1 = cmp_w1.reshape(2, CMP_RATIO, CMP_FEAT, CMP_HIDDEN)
    w1c = jnp.concatenate([w1[:, r] for r in range(CMP_RATIO)], axis=-1).astype(BF16)
    pe = cmp_pe.reshape(2, CMP_RATIO, CMP_FEAT)
    pe2 = jnp.zeros((2, SUBLANES, CMP_FEAT), F32).at[:, :CMP_RATIO].set(pe).astype(BF16)
    return pe2, w1c, cmp_w2.astype(BF16)


def _masked_softmax(s, mask, axis=-1):
    s = jnp.where(mask, s, -jnp.inf)
    m = jnp.max(s, axis=axis, keepdims=True)
    m = jnp.where(m > -jnp.inf, m, 0.0)
    e = jnp.exp(s - m)
    return e * (1.0 / jnp.maximum(jnp.sum(e, axis=axis, keepdims=True), 1e-30))


def _masked_softmax_heads(s, mask, n_heads):
    t = mask.shape[0]
    return jnp.concatenate([_masked_softmax(s[h * t:(h + 1) * t], mask) for h in range(n_heads)], axis=0)


def _split_bf16(x):
    hi = x.astype(BF16)
    return hi, (x - hi.astype(F32)).astype(BF16)


def _select_blocks_t(imp_t, pos_row, n_blocks):
    J = imp_t.shape[0]
    j = lax.broadcasted_iota(jnp.int32, imp_t.shape, 0)
    cur = pos_row // SLC_LEN
    forced = (j == 0) | ((j <= cur) & (j > cur - SLC_LOCAL))
    score = jnp.where(j * SLC_LEN > pos_row, -jnp.inf, jnp.where(forced, jnp.inf, imp_t))
    score = jnp.where(j < n_blocks, score, -jnp.inf)
    tiles = [score[r:r + SUBLANES] for r in range(0, J, SUBLANES)]
    ranks = [jnp.zeros(t.shape, F32) for t in tiles]
    for jp in range(n_blocks):
        row = score[jp:jp + 1, :]
        for i, t in enumerate(tiles):
            lo_j = i * SUBLANES
            ge, gt = jnp.where(row >= t, 1.0, 0.0), jnp.where(row > t, 1.0, 0.0)
            if lo_j > jp:
                ahead = ge
            elif lo_j + SUBLANES - 1 <= jp:
                ahead = gt
            else:
                ahead = jnp.where(lax.broadcasted_iota(jnp.int32, t.shape, 0) + lo_j > jp, ge, gt)
            ranks[i] = ranks[i] + ahead
    rank = jnp.concatenate(ranks, axis=0)
    keep = (rank < float(min(SLC_TOP_N, n_blocks))) & (j < n_blocks)
    return jnp.where(keep, 1.0, 0.0)


def _nsa_seq_body(q_ref, kc_ref, vc_ref, ks_ref, vs_ref, kw_ref, vw_ref, gl_ref, ovl_ref, o_ref, m_ref, l_ref, acc_ref, *, kblk):
    i = pl.program_id(2)
    tq = q_ref.shape[0]
    T = ks_ref.shape[0]
    H = NSA_GROUP
    D = HEAD_DIM
    n_slc = T // SLC_LEN
    qb = q_ref[...]
    q4 = jnp.concatenate([qb[:, h * D:(h + 1) * D] for h in range(H)], axis=0)
    t0 = i * tq
    pos1 = t0 + lax.broadcasted_iota(jnp.int32, (tq, 1), 0)

    kc = kc_ref[0, 0, 0].astype(BF16)
    vc = vc_ref[0, 0, 0].astype(BF16)
    n_c = kc.shape[0]
    s = _dot_nt(q4, kc)
    cmp_end = lax.broadcasted_iota(jnp.int32, (1, n_c), 1) * CMP_STRIDE + (CMP_LEN - 1)
    p = _masked_softmax_heads(s, cmp_end <= pos1, H)
    o_c = _dot(p.astype(BF16), vc)

    psum = p[0:tq]
    for h in range(1, H):
        psum = psum + p[h * tq:(h + 1) * tq]
    p_hi, p_lo = _split_bf16(psum)
    ovl = ovl_ref[...]
    imp_t = _dot_nt(ovl, p_hi) + _dot_nt(ovl, p_lo)
    pos_row = t0 + lax.broadcasted_iota(jnp.int32, (1, tq), 1)
    sel_t = _select_blocks_t(imp_t, pos_row, n_slc)
    eye = (lax.broadcasted_iota(jnp.int32, (tq, tq), 0) == lax.broadcasted_iota(jnp.int32, (tq, tq), 1))
    sel = _dot_nt(jnp.where(eye, 1.0, 0.0).astype(BF16), sel_t.astype(BF16)).astype(BF16)

    n_kb = (t0 + tq + kblk - 1) // kblk

    m_ref[...] = jnp.full(m_ref.shape, MASKED, F32)
    l_ref[...] = jnp.zeros(l_ref.shape, F32)
    acc_ref[...] = jnp.zeros(acc_ref.shape, F32)

    def kv_step(kb, carry):
        k0 = pl.multiple_of(kb * kblk, kblk)
        kk = ks_ref[pl.ds(k0, kblk), :].astype(BF16)
        vv = vs_ref[pl.ds(k0, kblk), :].astype(BF16)
        s2 = _dot_nt(q4, kk)
        kpos = k0 + lax.broadcasted_iota(jnp.int32, (1, kblk), 1)
        expand = (lax.broadcasted_iota(jnp.int32, (n_slc, kblk), 0) == kpos // SLC_LEN)
        chosen = _dot(sel, jnp.where(expand, 1.0, 0.0).astype(BF16))
        bias = jnp.where((chosen > 0.5) & (kpos <= pos1), 0.0, MASKED)
        es = []
        for h in range(H):
            rows = slice(h * tq, (h + 1) * tq)
            sh = s2[rows] + bias
            m_old = m_ref[rows]
            m_new = jnp.maximum(m_old, jnp.max(sh, axis=-1, keepdims=True))
            alpha = jnp.exp(m_old - m_new)
            e = jnp.exp(sh - m_new)
            m_ref[rows] = m_new
            l_ref[rows] = alpha * l_ref[rows] + jnp.sum(e, axis=-1, keepdims=True)
            acc_ref[rows] = alpha * acc_ref[rows]
            es.append(e.astype(BF16))
        acc_ref[...] += _dot(jnp.concatenate(es, axis=0), vv)
        return carry

    lax.fori_loop(0, n_kb, kv_step, 0)
    o_s = acc_ref[...] * (1.0 / jnp.maximum(l_ref[...], 1e-30))

    band = WINDOW + tq
    w0 = pl.multiple_of(jnp.maximum(t0 - WINDOW, 0), tq)
    kw = kw_ref[pl.ds(w0, band), :].astype(BF16)
    vw = vw_ref[pl.ds(w0, band), :].astype(BF16)
    sw = _dot_nt(q4, kw)
    diff = pos1 - (w0 + lax.broadcasted_iota(jnp.int32, (1, band), 1))
    pw = _masked_softmax_heads(sw, (diff >= 0) & (diff < WINDOW), H)
    o_w = _dot(pw.astype(BF16), vw)

    gates = _sigmoid(gl_ref[...])
    for h in range(H):
        rows = slice(h * tq, (h + 1) * tq)
        g = lambda br: gates[:, NSA_COL_G + H * br + h:NSA_COL_G + H * br + h + 1]
        o_ref[:, h * D:(h + 1) * D] = (g(0) * o_c[rows] + g(1) * o_s[rows] + g(2) * o_w[rows]).astype(o_ref.dtype)


def _overlap_t(n_cmp_rows, n_slc):
    start = np.arange(n_cmp_rows)[None, :] * CMP_STRIDE
    blk = np.arange(n_slc)[:, None] * SLC_LEN
    return ((start < blk + SLC_LEN) & (start + CMP_LEN > blk)).astype(np.float32)


def nsa_seq(q, kv_rows, win_rows, cmp_kv, u_small, n_seq, tq=256, kblk=512):
    M = q.shape[0]
    T = M // n_seq
    nt = T // tq
    G = NSA_KV_HEADS
    gw = NSA_GROUP * HEAD_DIM
    n_c = cmp_kv.shape[3]
    ovl = jnp.asarray(_overlap_t(n_c, T // SLC_LEN), BF16)
    body = functools.partial(_nsa_seq_body, kblk=kblk)
    seq_blk = lambda off: pl.BlockSpec((T, HEAD_DIM), lambda b, g, i: (b, g + off))
    cmp_blk = lambda c: pl.BlockSpec((1, 1, 1, n_c, HEAD_DIM), lambda b, g, i: (b, c, g, 0, 0))
    return pl.pallas_call(
        body,
        grid=(n_seq, G, nt),
        in_specs=[pl.BlockSpec((tq, gw), lambda b, g, i: (b * nt + i, g)),
                  cmp_blk(0), cmp_blk(1), seq_blk(2 * G), seq_blk(3 * G), seq_blk(0), seq_blk(G),
                  pl.BlockSpec((tq, LANES), lambda b, g, i: (b * nt + i, g)),
                  pl.BlockSpec(ovl.shape, lambda b, g, i: (0, 0))],
        out_specs=pl.BlockSpec((tq, gw), lambda b, g, i: (b * nt + i, g)),
        out_shape=jax.ShapeDtypeStruct((M, NSA_HEADS * HEAD_DIM), BF16),
        scratch_shapes=[pltpu.VMEM((NSA_GROUP * tq, 1), F32)] * 2 + [pltpu.VMEM((NSA_GROUP * tq, HEAD_DIM), F32)],
        compiler_params=_cparams("parallel", "parallel", "arbitrary"),
        name="nsa_seq",
    )(q, cmp_kv, cmp_kv, kv_rows, kv_rows, win_rows, win_rows, u_small, ovl)


STEP_COL_G = 0
STEP_COL_B = 3 * NSA_HEADS
STEP_COL_A = STEP_COL_B + GDN_HEADS


def _gdn_step_prep_body(x_ref, h_ref, c_ref, us_ref, alog_ref, dtb_ref, q_ref, k_ref, v_ref, b_ref, e_ref):
    W = x_ref.shape[1]
    c = c_ref[...]
    x = x_ref[...] * c[GDN_CONV - 1:GDN_CONV]
    for j in range(GDN_CONV - 1):
        x = x + h_ref[:, j * W:(j + 1) * W] * c[j:j + 1]
    x = _silu(x)
    us = us_ref[...]
    beta = _sigmoid(us)
    eg = jnp.exp(-jnp.exp(alog_ref[...]) * _softplus(us + dtb_ref[...]))
    nq = GDN_HEADS * GDN_DK
    for h in range(GDN_HEADS):
        sl = slice(h * GDN_DK, (h + 1) * GDN_DK)
        q_ref[:, sl] = _l2norm(x[:, sl]) * GDN_DK ** -0.5
        k_ref[:, sl] = _l2norm(x[:, nq + h * GDN_DK:nq + (h + 1) * GDN_DK])
        b_ref[:, sl] = jnp.broadcast_to(beta[:, STEP_COL_B + h:STEP_COL_B + h + 1], (x.shape[0], GDN_DK))
        e_ref[:, sl] = jnp.broadcast_to(eg[:, STEP_COL_A + h:STEP_COL_A + h + 1], (x.shape[0], GDN_DK))
    v_ref[...] = x[:, 2 * nq:]


def _gdn_step_body(q_ref, k_ref, v_ref, b_ref, e_ref, z_ref, nw_ref, s_ref, o_ref, so_ref):
    bb = q_ref.shape[0]
    eye = jnp.where(lax.broadcasted_iota(jnp.int32, (GDN_DK, GDN_DK), 0)
                    == lax.broadcasted_iota(jnp.int32, (GDN_DK, GDN_DK), 1), 1.0, 0.0)
    nw = nw_ref[...]

    def per_seq(bi, carry):
        q, k, v, beta, eg, z = q_ref[bi], k_ref[bi], v_ref[bi], b_ref[bi], e_ref[bi], z_ref[bi]
        k_t = _dot_nt(eye, k, HI)
        q_t = _dot_nt(eye, q, HI)
        outs = []
        for h in range(GDN_HEADS):
            r = slice(h, h + 1)
            s = s_ref[bi, h]
            kcol = k_t[:, h:h + 1]
            k_s = jnp.sum(kcol * s, axis=0, keepdims=True)
            q_s = jnp.sum(q_t[:, h:h + 1] * s, axis=0, keepdims=True)
            v_new = v[r] * beta[r] - (beta[r] * eg[r]) * k_s
            qk = jnp.sum(q[r] * k[r], axis=-1, keepdims=True)
            o = eg[r] * q_s + qk * v_new
            so_ref[bi, h] = s * eg[r] + kcol * v_new
            on = o * lax.rsqrt(jnp.mean(o * o, axis=-1, keepdims=True) + EPS) * nw
            outs.append(on * _silu(z[r]))
        o_ref[bi] = jnp.concatenate(outs, axis=0).astype(o_ref.dtype)
        return carry

    lax.fori_loop(0, bb, per_seq, 0)


def gdn_step(u_qkv, u_z, u_small, hist, state, conv_w, a_log, dt_bias, norm_w, bb=4):
    Bd, W = u_qkv.shape
    H = GDN_HEADS
    nq = H * GDN_DK
    row = lambda p: _place_lanes(((STEP_COL_A, p.reshape(1, H)),), (1,))
    full = lambda shape: pl.BlockSpec(shape, lambda i: (0,) * len(shape))
    outs = pl.pallas_call(
        _gdn_step_prep_body,
        grid=(1,),
        in_specs=[full((Bd, W)), full((Bd, (GDN_CONV - 1) * W)), full((GDN_CONV, W)), full((Bd, LANES)),
                  full((1, LANES)), full((1, LANES))],
        out_specs=[full((Bd, nq))] * 5,
        out_shape=[jax.ShapeDtypeStruct((Bd, nq), F32)] * 5,
        compiler_params=_cparams("arbitrary"),
        name="gdn_step_prep",
    )(u_qkv, hist.reshape(Bd, (GDN_CONV - 1) * W), conv_w, u_small, row(a_log), row(dt_bias))
    heads = lambda a: a.reshape(Bd, H, GDN_DK)
    vec_blk = pl.BlockSpec((bb, H, GDN_DK), lambda i: (i, 0, 0))
    st_blk = pl.BlockSpec((bb, H, GDN_DK, GDN_DV), lambda i: (i, 0, 0, 0))
    o, s_new = pl.pallas_call(
        _gdn_step_body,
        grid=(Bd // bb,),
        in_specs=[vec_blk] * 6 + [pl.BlockSpec((1, GDN_DV), lambda i: (0, 0)), st_blk],
        out_specs=[vec_blk, st_blk],
        out_shape=[jax.ShapeDtypeStruct((Bd, H, GDN_DV), BF16), jax.ShapeDtypeStruct(state.shape, F32)],
        compiler_params=_cparams("parallel"),
        name="gdn_step",
    )(*[heads(a) for a in outs], heads(u_z), norm_w.reshape(1, GDN_DV), state)
    return o.reshape(Bd, H * GDN_DV), s_new


def _softmax_with_new(s_past, valid, s_new):
    s_past = jnp.where(valid, s_past, -jnp.inf)
    m = jnp.maximum(jnp.max(s_past, axis=-1, keepdims=True), s_new)
    e = jnp.exp(s_past - m)
    e_new = jnp.exp(s_new - m)
    return e, e_new, 1.0 / (jnp.sum(e, axis=-1, keepdims=True) + e_new)


def _nsa_step_body(pt_ref, *refs, n_pages, page, n_past):
    lo = refs[:n_pages]
    hi = refs[n_pages:2 * n_pages]
    (q_ref, new_ref, win_ref, wnew_ref, gl_ref, pe_ref, w1_ref, w2_ref, ovl_ref, o_ref, wout_ref, xk_ref, xv_ref) = refs[2 * n_pages:]
    G, H, D = NSA_KV_HEADS, NSA_HEADS, HEAD_DIM
    HS = 2 * G
    pos = n_past
    cpp = page // CMP_STRIDE
    n_chunk = n_pages * cpp
    L = n_pages * page
    q = q_ref[0]
    qf = q.astype(F32)
    new = new_ref[0]
    row_g = lax.broadcasted_iota(jnp.int32, (H, 1), 0) // NSA_GROUP

    def per_head_rows(rows):
        return jnp.concatenate([jnp.broadcast_to(rows[g:g + 1], (NSA_GROUP, rows.shape[1])) for g in range(G)], axis=0)

    first_half = lax.broadcasted_iota(jnp.int32, (SUBLANES, D), 0) < G
    for j in range(n_pages):
        for s in range(CMP_STRIDE):
            for m in range(cpp // 4):
                ks, vs = [], []
                for pair in range(2):
                    ta = lo[j][0, (4 * m + 2 * pair) * CMP_STRIDE + s]
                    tb = lo[j][0, (4 * m + 2 * pair + 1) * CMP_STRIDE + s]
                    ks.append(jnp.where(first_half, ta, pltpu.roll(tb, G, axis=0)))
                    vs.append(jnp.where(first_half, pltpu.roll(ta, G, axis=0), tb))
                r0 = (j * cpp + 4 * m) * G
                xk_ref[r0:r0 + 4 * G, s * D:(s + 1) * D] = jnp.concatenate(ks, axis=0).astype(BF16)
                xv_ref[r0:r0 + 4 * G, s * D:(s + 1) * D] = jnp.concatenate(vs, axis=0).astype(BF16)
    ckv = []
    for c, x_ref in enumerate((xk_ref, xv_ref)):
        part = _dot(x_ref[...], w1_ref[c])
        ckv.append(_compress_mlp(part, pe_ref.at[c], w1_ref.at[c], w2_ref.at[c], G).astype(BF16))

    n_all = G * n_chunk
    s = _dot_nt(q, ckv[0])
    col = lax.broadcasted_iota(jnp.int32, (1, n_all), 1)
    blk = col // G
    ok = (col % G == row_g) & (blk * CMP_STRIDE + (CMP_LEN - 1) <= pos) & (blk < n_chunk - CMP_RATIO + 1)
    p = _masked_softmax(s, ok)
    o_c = _dot(p.astype(BF16), ckv[1])

    gsum = jnp.where(lax.broadcasted_iota(jnp.int32, (LANES, H), 0) == lax.broadcasted_iota(jnp.int32, (LANES, H), 1) // NSA_GROUP, 1.0, 0.0)
    psum = _dot(gsum, p, HI)
    p_hi, p_lo = _split_bf16(psum)
    ovl = ovl_ref[...]
    imp_t = _dot_nt(ovl, p_hi) + _dot_nt(ovl, p_lo)
    n_slc = -(-(L + 1) // SLC_LEN)
    sel_t = _select_blocks_t(imp_t, jnp.full((1, LANES), pos, jnp.int32), n_slc)
    J = sel_t.shape[0]
    eye8 = jnp.where(lax.broadcasted_iota(jnp.int32, (SUBLANES, LANES), 0) == lax.broadcasted_iota(jnp.int32, (SUBLANES, LANES), 1), 1.0, 0.0)
    sel = _dot_nt(eye8.astype(BF16), sel_t.astype(BF16))

    def slot_softmax(sc, tok_ok, s_new):
        slot = lax.broadcasted_iota(jnp.int32, (1, sc.shape[1]), 1) % HS
        return _softmax_with_new(sc, (slot == row_g) & tok_ok, s_new)

    tok = lax.broadcasted_iota(jnp.int32, (1, L * HS), 1) // HS
    chosen = per_head_rows(jnp.concatenate(
        [jnp.broadcast_to(sel[:, b:b + 1], (SUBLANES, SLC_LEN * HS)) for b in range(L // SLC_LEN)], axis=1))
    kv_hi = [hi[j][0].reshape(page * HS, D).astype(BF16) for j in range(n_pages)]
    sc = jnp.concatenate([_dot_nt(q, kv) for kv in kv_hi], axis=1)
    s_new = jnp.sum(qf * per_head_rows(new[2 * G:3 * G]), axis=-1, keepdims=True)
    e, e_new, inv = slot_softmax(sc, (chosen > 0.5) & (tok <= pos), s_new)
    e = pltpu.roll(e, G, axis=1).astype(BF16)
    acc = jnp.zeros((H, D), F32)
    for j, kv in enumerate(kv_hi):
        acc = acc + _dot(e[:, j * page * HS:(j + 1) * page * HS], kv)
    o_s = (acc + e_new * per_head_rows(new[3 * G:4 * G])) * inv

    n_win = win_ref.shape[1] // HS
    wnew = wnew_ref[0]
    win = win_ref[0].astype(BF16)
    wtok = pos - n_win + lax.broadcasted_iota(jnp.int32, (1, n_win * HS), 1) // HS
    sw_new = jnp.sum(qf * per_head_rows(wnew[0:G]), axis=-1, keepdims=True)
    ew, ew_new, winv = slot_softmax(_dot_nt(q, win), (pos - wtok < WINDOW) & (wtok >= 0), sw_new)
    o_w = (_dot(pltpu.roll(ew, G, axis=1).astype(BF16), win) + ew_new * per_head_rows(wnew[G:2 * G])) * winv

    gl = jnp.broadcast_to(_sigmoid(gl_ref[0]), (SUBLANES, LANES))
    hh = lax.broadcasted_iota(jnp.int32, (H, LANES), 0)
    cc = lax.broadcasted_iota(jnp.int32, (H, LANES), 1)
    gate = lambda br: _dot_nt(jnp.where(cc == STEP_COL_G + br * H + hh, 1.0, 0.0), gl, HI)[:, 0:1]
    o_ref[0] = (gate(0) * o_c + gate(1) * o_s + gate(2) * o_w).astype(o_ref.dtype)

    wout_ref[0, :(n_win - 1) * HS] = win_ref[0, HS:]
    wout_ref[0, (n_win - 1) * HS:] = wnew


def nsa_step(q, new_rows, win_new, cache, page_table, win_state, u_small, pe2, w1c, w2c):
    Bd = q.shape[0]
    n_pool, page = cache.shape[:2]
    n_pages = page_table.shape[1]
    G, H, D = NSA_KV_HEADS, NSA_HEADS, HEAD_DIM
    n_past = n_pages * page
    n_chunk = n_past // CMP_STRIDE
    n_win = win_state.shape[1]
    assert n_past % SLC_LEN == 0 and n_win == WINDOW <= n_past and page % (4 * CMP_STRIDE) == 0 and 2 * G == SUBLANES
    n_slc = -(-(n_past + 1) // SLC_LEN)
    J = -(-n_slc // (2 * SUBLANES)) * (2 * SUBLANES)
    ovl = np.zeros((J, n_chunk * G), np.float32)
    ovl[:n_slc] = np.repeat(_overlap_t(n_chunk, n_slc), G, axis=1)
    ovl = jnp.asarray(ovl, BF16)
    body = functools.partial(_nsa_step_body, n_pages=n_pages, page=page, n_past=n_past)
    half_spec = lambda j, half: pl.BlockSpec((1, page, None, 2 * G, D), lambda b, pt: (pt[b, j], 0, half, 0, 0))
    const = lambda a: pl.BlockSpec(a.shape, lambda b, pt: (0,) * a.ndim, pipeline_mode=pl.Buffered(1))
    per_b = lambda shape: pl.BlockSpec((1,) + shape, lambda b, pt: (b,) + (0,) * len(shape))
    cache5 = cache.reshape(n_pool, page, 2, 2 * G, D)
    out, win_out = pl.pallas_call(
        body,
        grid_spec=pltpu.PrefetchScalarGridSpec(
            num_scalar_prefetch=1,
            grid=(Bd,),
            in_specs=[half_spec(j, 0) for j in range(n_pages)] + [half_spec(j, 1) for j in range(n_pages)]
                     + [per_b((H, D)), per_b((4 * G, D)), per_b((n_win * 2 * G, D)), per_b((2 * G, D)), per_b((1, LANES)),
                        const(pe2), const(w1c), const(w2c), const(ovl)],
            out_specs=[per_b((H, D)), per_b((n_win * 2 * G, D))],
            scratch_shapes=[pltpu.VMEM((G * n_chunk, CMP_FEAT), BF16)] * 2,
        ),
        out_shape=[jax.ShapeDtypeStruct((Bd, H, D), BF16), jax.ShapeDtypeStruct((Bd, n_win * 2 * G, D), F32)],
        compiler_params=_cparams("arbitrary"),
        name="nsa_step",
    )(page_table, *([cache5] * (2 * n_pages)), q.reshape(Bd, H, D), new_rows.reshape(Bd, 4 * G, D),
      win_state.reshape(Bd, n_win * 2 * G, D), win_new.reshape(Bd, 2 * G, D), u_small.reshape(Bd, 1, LANES),
      pe2, w1c, w2c, ovl)
    return out.reshape(Bd, H * D), win_out.reshape(win_state.shape)


def rope_tables(pos):
    half = HEAD_DIM // 2
    inv = ROPE_THETA ** (-jnp.arange(half, dtype=F32) / half)
    ang = pos.astype(F32)[:, None] * inv[None, :]
    cos, sin = jnp.cos(ang), jnp.sin(ang)
    return jnp.concatenate([cos, cos], axis=-1), jnp.concatenate([-sin, sin], axis=-1)


def _split_w_in(w_in):
    H, G = NSA_HEADS, NSA_KV_HEADS
    sizes = (H * HEAD_DIM, 6 * G * HEAD_DIM, 3 * H, GDN_HEADS * (2 * GDN_DK + GDN_DV), GDN_HEADS * GDN_DV, 2 * GDN_HEADS)
    o = np.cumsum((0,) + sizes).tolist()
    K = w_in.shape[0]
    w_qkv = w_in[:, o[0]:o[2]].astype(BF16)
    w_gqkv = w_in[:, o[3]:o[4]].astype(BF16)
    w_z = w_in[:, o[4]:o[5]].astype(BF16)
    w_mg = w_in[:, o[6]:].astype(BF16)
    w_ng = w_in[:, o[2]:o[3]]
    w_ba = w_in[:, o[5]:o[6]]
    nhb = GDN_HEADS // GDN_HB
    ng = w_ng.reshape(K, 3, G, NSA_GROUP).transpose(0, 2, 1, 3).reshape(K, G, 3 * NSA_GROUP)
    pieces = ((GDN_COL_B, w_ba[:, :GDN_HEADS].reshape(K, nhb, GDN_HB)),
              (GDN_COL_A, w_ba[:, GDN_HEADS:].reshape(K, nhb, GDN_HB)), (NSA_COL_G, ng))
    w_small_seq = _place_lanes(pieces, (K, nhb)).reshape(K, nhb * LANES).astype(BF16)
    w_small_step = _place_lanes(((STEP_COL_G, w_ng), (STEP_COL_B, w_ba)), (K,)).astype(BF16)
    return w_qkv, w_gqkv, w_z, w_mg, w_small_seq, w_small_step


def _place_lanes(pieces, lead):
    out, at = [], 0
    for lane, vals in pieces:
        out += [jnp.zeros(lead + (lane - at,), F32), vals.astype(F32)]
        at = lane + vals.shape[-1]
    return jnp.concatenate(out + [jnp.zeros(lead + (LANES - at,), F32)], axis=-1)


def kernel(x_prompt, x_sample, cache_nsa_kv, page_table, state_win_kv, state_gdn, state_gdn_conv, state_ffn_conv, norm_mix, w_in, cmp_pe, cmp_w1, cmp_w2, gdn_conv_w, gdn_a_log, gdn_dt_bias, gdn_norm, w_nsa_out, w_gdn_out, w_o, norm_ffn, w_up, ffn_conv_w, w_down, norm_final):
    B, T, D = x_prompt.shape
    Bd = x_sample.shape[0]
    G = NSA_KV_HEADS
    n_past = page_table.shape[1] * cache_nsa_kv.shape[1]
    F = w_down.shape[0]

    w_qkv, w_gqkv, w_z, w_mg, w_small_seq, w_small_step = _split_w_in(w_in)
    w_nsa_b, w_gdn_b, w_o_b = w_nsa_out.astype(BF16), w_gdn_out.astype(BF16), w_o.astype(BF16)
    w_up_b, w_down_b = w_up.astype(BF16), w_down.astype(BF16)
    pe2, w1c, w2c = compress_params(cmp_pe, cmp_w1, cmp_w2)

    def tail(x, mixed_nsa, mixed_gdn, u_mg, ffn_up, tm):
        mixin = matmul_mix(mixed_nsa, mixed_gdn, w_nsa_b, w_gdn_b, u_mg, tm, 512)
        h = matmul_residual(mixin, w_o_b, x, tm, 512)
        hn = rmsnorm(h, norm_ffn, BF16, min(tm, 256))
        act, extra = ffn_up(hn)
        h2 = matmul_residual(act, w_down_b, h, min(tm, 512), 256)
        return rmsnorm(h2, norm_final, F32, min(tm, 256)), extra

    xp = x_prompt.reshape(B * T, D)
    xn = rmsnorm(xp, norm_mix, BF16, 256)
    tm = 1024
    u_qkv = matmul(xn, w_qkv, tm, 512)
    u_gqkv = matmul(xn, w_gqkv, tm, 512)
    u_z = matmul(xn, w_z, tm, 512)
    u_mg = matmul(xn, w_mg, tm, 512)
    u_small = matmul(xn, w_small_seq, tm, w_small_seq.shape[1])
    cos, sin = rope_tables(jnp.arange(T, dtype=jnp.int32))
    q, kv_rows, win_rows = rope_split(u_qkv, cos, sin, 256)
    cmp_kv = compress_seq(kv_rows, pe2, w1c, w2c, B)
    o_nsa = nsa_seq(q, kv_rows, win_rows, cmp_kv, u_small, B)
    a_rows, d_rows = gdn_param_rows(gdn_a_log, gdn_dt_bias)
    o_gdn, s_p = gdn_seq(u_gqkv, u_z, u_small, gdn_conv_w, a_rows, d_rows, gdn_norm, B, 256)
    s_p = s_p.reshape(B, GDN_HEADS, GDN_DK, GDN_DV)

    def ffn_up_p(hn):
        act, tg, tv = ffn_up_seq(hn, w_up_b, ffn_conv_w, B, tm, 256)
        last = lambda a: a[T // tm - 1::T // tm, SUBLANES - (FFN_CONV - 1):]
        return act, jnp.concatenate([last(tg), last(tv)], axis=-1)

    y_p, ffn_p = tail(xp, o_nsa, o_gdn, u_mg, ffn_up_p, tm)
    n_win_p = min(WINDOW, T)
    kv_p = kv_rows.reshape(B, T, 4, G, HEAD_DIM)
    win_p = win_rows.reshape(B, T, 2, G, HEAD_DIM)[:, T - n_win_p:]
    conv_p = u_gqkv.reshape(B, T, -1)[:, T - (GDN_CONV - 1):]

    xs = x_sample.reshape(Bd, D)
    xn = rmsnorm(xs, norm_mix, BF16, Bd)
    u_qkv = matmul(xn, w_qkv, Bd, 512)
    u_gqkv = matmul(xn, w_gqkv, Bd, 512)
    u_z = matmul(xn, w_z, Bd, 512)
    u_mg = matmul(xn, w_mg, Bd, 512)
    u_small = matmul(xn, w_small_step, Bd, LANES)
    cos, sin = rope_tables(jnp.full((Bd,), n_past, jnp.int32))
    q, new_rows, win_new = rope_split(u_qkv, cos, sin, Bd)
    o_nsa, win_s = nsa_step(q, new_rows, win_new, cache_nsa_kv, page_table, state_win_kv, u_small, pe2, w1c, w2c)
    o_gdn, s_s = gdn_step(u_gqkv, u_z, u_small, state_gdn_conv, state_gdn, gdn_conv_w, gdn_a_log, gdn_dt_bias, gdn_norm)

    def ffn_up_s(hn):
        act, ug, uv = ffn_up_step(hn, w_up_b, ffn_conv_w, state_ffn_conv, 256)
        return act, jnp.concatenate([ug, uv], axis=-1)

    y_s, up_new = tail(xs, o_nsa, o_gdn, u_mg, ffn_up_s, Bd)
    kv_s = new_rows.reshape(Bd, 1, 4, G, HEAD_DIM)
    conv_s = jnp.concatenate([state_gdn_conv[:, 1:], u_gqkv[:, None]], axis=1)
    ffn_s = jnp.concatenate([state_ffn_conv[:, 1:], up_new[:, None]], axis=1)

    return (y_p.reshape(B, T, D), y_s.reshape(Bd, 1, D), kv_p, win_p, s_p, conv_p, ffn_p,
            kv_s, win_s, s_s, conv_s, ffn_s)
```

```python
import functools
import math

import jax
import jax.numpy as jnp
import numpy as np
from jax import lax
from jax.experimental import pallas as pl
from jax.experimental.pallas import tpu as pltpu

F32 = jnp.float32
BF16 = jnp.bfloat16
HI = lax.Precision.HIGHEST

LANES = 128
SUBLANES = 8
VMEM_LIMIT_BYTES = 56 * 1024 * 1024

HEAD_DIM = 128
NSA_HEADS = 16
NSA_KV_HEADS = 4
NSA_GROUP = NSA_HEADS // NSA_KV_HEADS
CMP_LEN = 32
CMP_STRIDE = 16
CMP_HIDDEN = 256
SLC_LEN = 64
SLC_TOP_N = 16
SLC_LOCAL = 2
WINDOW = 512
GDN_HEADS = 16
GDN_DK = 128
GDN_DV = 128
GDN_CONV = 4
GDN_CHUNK = 64
FFN_CONV = 3
ROPE_THETA = 10000.0
EPS = 1e-6


def _cparams(*sem):
    return pltpu.CompilerParams(dimension_semantics=sem, vmem_limit_bytes=VMEM_LIMIT_BYTES)


def _dot(a, b, precision=None):
    return jnp.dot(a, b, preferred_element_type=F32, precision=precision)


def _dot_nt(a, b, precision=None):
    return lax.dot_general(a, b, (((1,), (1,)), ((), ())), preferred_element_type=F32, precision=precision)


def _dot_tn(a, b, precision=None):
    return lax.dot_general(a, b, (((0,), (0,)), ((), ())), preferred_element_type=F32, precision=precision)


def _sigmoid(x):
    return 1.0 / (1.0 + jnp.exp(-x))


def _silu(x):
    return x * _sigmoid(x)


def _rmsnorm_body(x_ref, w_ref, o_ref):
    x = x_ref[...]
    y = x * lax.rsqrt(jnp.mean(x * x, axis=-1, keepdims=True) + EPS)
    o_ref[...] = (y * w_ref[...]).astype(o_ref.dtype)


def rmsnorm(x, w, out_dtype, tm):
    M, D = x.shape
    return pl.pallas_call(
        _rmsnorm_body,
        grid=(M // tm,),
        in_specs=[pl.BlockSpec((tm, D), lambda i: (i, 0)), pl.BlockSpec((1, D), lambda i: (0, 0))],
        out_specs=pl.BlockSpec((tm, D), lambda i: (i, 0)),
        out_shape=jax.ShapeDtypeStruct((M, D), out_dtype),
        compiler_params=_cparams("parallel"),
        name="rmsnorm",
    )(x, w.reshape(1, D))


def _mm_body(x_ref, w_ref, o_ref):
    o_ref[...] = _dot(x_ref[...], w_ref[...]).astype(o_ref.dtype)


def matmul(x, w, tm, tn, out_dtype=F32):
    M, K = x.shape
    N = w.shape[1]
    return pl.pallas_call(
        _mm_body,
        grid=(M // tm, N // tn),
        in_specs=[pl.BlockSpec((tm, K), lambda i, j: (i, 0)), pl.BlockSpec((K, tn), lambda i, j: (0, j))],
        out_specs=pl.BlockSpec((tm, tn), lambda i, j: (i, j)),
        out_shape=jax.ShapeDtypeStruct((M, N), out_dtype),
        compiler_params=_cparams("parallel", "arbitrary"),
        name="matmul",
    )(x, w)


def _mm_nt_body(x_ref, w_ref, o_ref):
    o_ref[...] = _dot_nt(x_ref[...], w_ref[...]).astype(o_ref.dtype)


def matmul_nt(x, w_t, tm, tn, rows, out_dtype=F32):
    M, K = x.shape
    r0, N = rows
    assert r0 % (2 * SUBLANES) == 0 and N % tn == 0
    return pl.pallas_call(
        _mm_nt_body,
        grid=(M // tm, N // tn),
        in_specs=[pl.BlockSpec((tm, K), lambda i, j: (i, 0)),
                  pl.BlockSpec((pl.Element(tn), pl.Element(K)), lambda i, j: (pl.multiple_of(r0 + j * tn, 2 * SUBLANES), 0))],
        out_specs=pl.BlockSpec((tm, tn), lambda i, j: (i, j)),
        out_shape=jax.ShapeDtypeStruct((M, N), out_dtype),
        compiler_params=_cparams("parallel", "arbitrary"),
        name="matmul_nt",
    )(x, w_t)


def _mm_res_body(x_ref, w_ref, r_ref, o_ref):
    o_ref[...] = r_ref[...] + _dot(x_ref[...], w_ref[...])


def matmul_residual(x, w, res, tm, tn, x_buffers=2):
    M, K = x.shape
    N = w.shape[1]
    return pl.pallas_call(
        _mm_res_body,
        grid=(M // tm, N // tn),
        in_specs=[pl.BlockSpec((tm, K), lambda i, j: (i, 0), pipeline_mode=pl.Buffered(x_buffers)),
                  pl.BlockSpec((K, tn), lambda i, j: (0, j)),
                  pl.BlockSpec((tm, tn), lambda i, j: (i, j))],
        out_specs=pl.BlockSpec((tm, tn), lambda i, j: (i, j)),
        out_shape=jax.ShapeDtypeStruct((M, N), F32),
        compiler_params=_cparams("parallel", "arbitrary"),
        name="matmul_residual",
    )(x, w, res)


def _mm_mix_body(a_ref, b_ref, wa_ref, wb_ref, ga_ref, gb_ref, o_ref):
    ua = _dot(a_ref[...], wa_ref[...])
    ub = _dot(b_ref[...], wb_ref[...])
    o_ref[...] = (_sigmoid(ga_ref[...]) * ua + _sigmoid(gb_ref[...]) * ub).astype(o_ref.dtype)


def matmul_mix(a, b, wa, wb, gates, tm, tn):
    M, K = a.shape
    N = wa.shape[1]
    nj = N // tn
    return pl.pallas_call(
        _mm_mix_body,
        grid=(M // tm, nj),
        in_specs=[pl.BlockSpec((tm, K), lambda i, j: (i, 0)), pl.BlockSpec((tm, K), lambda i, j: (i, 0)),
                  pl.BlockSpec((K, tn), lambda i, j: (0, j)), pl.BlockSpec((K, tn), lambda i, j: (0, j)),
                  pl.BlockSpec((tm, tn), lambda i, j: (i, j)), pl.BlockSpec((tm, tn), lambda i, j: (i, j + nj))],
        out_specs=pl.BlockSpec((tm, tn), lambda i, j: (i, j)),
        out_shape=jax.ShapeDtypeStruct((M, N), BF16),
        compiler_params=_cparams("parallel", "arbitrary"),
        name="matmul_mix",
    )(a, b, wa, wb, gates, gates)


def _shift_rows(x, hist, sh):
    s = pltpu.roll(x, sh, axis=0)
    head = s[:SUBLANES]
    row8 = lax.broadcasted_iota(jnp.int32, head.shape, 0)
    for r in range(sh):
        head = jnp.where(row8 == r, hist[SUBLANES - sh + r:SUBLANES - sh + r + 1], head)
    return jnp.concatenate([head, s[SUBLANES:]], axis=0)


def _ffn_up_seq_body(x_ref, wg_ref, wv_ref, cg_ref, cv_ref, act_ref, tg_ref, tv_ref, hg_ref, hv_ref, *, tiles_per_seq):
    i = pl.program_id(0)
    j = pl.program_id(1)
    first = (i % tiles_per_seq) == 0
    x = x_ref[...]

    def branch(w_ref, c_ref, hist_ref, tail_ref):
        up = _dot(x, w_ref[...])
        tm = up.shape[0]
        hist = jnp.where(first, 0.0, hist_ref[j])
        c = c_ref[...]
        out = up * c[2:3] + _shift_rows(up, hist, 2) * c[0:1] + _shift_rows(up, hist, 1) * c[1:2]
        tail = up[tm - SUBLANES:tm]
        hist_ref[j] = tail
        tail_ref[0] = tail
        return out

    gate = branch(wg_ref, cg_ref, hg_ref, tg_ref)
    val = branch(wv_ref, cv_ref, hv_ref, tv_ref)
    act_ref[...] = (_silu(gate) * val).astype(act_ref.dtype)


def ffn_up_seq(x, w_up, conv_w, n_seq, tm, tn):
    M, K = x.shape
    F = w_up.shape[1] // 2
    nj = F // tn
    tiles_per_seq = M // n_seq // tm
    body = functools.partial(_ffn_up_seq_body, tiles_per_seq=tiles_per_seq)
    return pl.pallas_call(
        body,
        grid=(M // tm, nj),
        in_specs=[pl.BlockSpec((tm, K), lambda i, j: (i, 0)),
                  pl.BlockSpec((K, tn), lambda i, j: (0, j)), pl.BlockSpec((K, tn), lambda i, j: (0, j + nj)),
                  pl.BlockSpec((FFN_CONV, tn), lambda i, j: (0, j)), pl.BlockSpec((FFN_CONV, tn), lambda i, j: (0, j + nj))],
        out_specs=[pl.BlockSpec((tm, tn), lambda i, j: (i, j)),
                   pl.BlockSpec((1, SUBLANES, tn), lambda i, j: (i, 0, j)),
                   pl.BlockSpec((1, SUBLANES, tn), lambda i, j: (i, 0, j))],
        out_shape=[jax.ShapeDtypeStruct((M, F), BF16),
                   jax.ShapeDtypeStruct((M // tm, SUBLANES, F), F32), jax.ShapeDtypeStruct((M // tm, SUBLANES, F), F32)],
        scratch_shapes=[pltpu.VMEM((nj, SUBLANES, tn), F32), pltpu.VMEM((nj, SUBLANES, tn), F32)],
        compiler_params=_cparams("arbitrary", "arbitrary"),
        name="ffn_up_seq",
    )(x, w_up, w_up, conv_w, conv_w)


def _ffn_up_step_body(x_ref, wg_ref, wv_ref, cg_ref, cv_ref, g0_ref, g1_ref, v0_ref, v1_ref, act_ref, ug_ref, uv_ref):
    x = x_ref[...]

    def branch(w_ref, c_ref, h0_ref, h1_ref, up_ref):
        up = _dot(x, w_ref[...])
        up_ref[...] = up
        c = c_ref[...]
        return up * c[2:3] + h0_ref[...] * c[0:1] + h1_ref[...] * c[1:2]

    gate = branch(wg_ref, cg_ref, g0_ref, g1_ref, ug_ref)
    val = branch(wv_ref, cv_ref, v0_ref, v1_ref, uv_ref)
    act_ref[...] = (_silu(gate) * val).astype(act_ref.dtype)


def ffn_up_step(x, w_up, conv_w, hist, tn):
    M, K = x.shape
    F2 = w_up.shape[1]
    F = F2 // 2
    nj = F // tn
    h2 = hist.reshape(M, 2 * F2)
    return pl.pallas_call(
        _ffn_up_step_body,
        grid=(nj,),
        in_specs=[pl.BlockSpec((M, K), lambda j: (0, 0)),
                  pl.BlockSpec((K, tn), lambda j: (0, j)), pl.BlockSpec((K, tn), lambda j: (0, j + nj)),
                  pl.BlockSpec((FFN_CONV, tn), lambda j: (0, j)), pl.BlockSpec((FFN_CONV, tn), lambda j: (0, j + nj)),
                  pl.BlockSpec((M, tn), lambda j: (0, j)), pl.BlockSpec((M, tn), lambda j: (0, j + 2 * nj)),
                  pl.BlockSpec((M, tn), lambda j: (0, j + nj)), pl.BlockSpec((M, tn), lambda j: (0, j + 3 * nj))],
        out_specs=[pl.BlockSpec((M, tn), lambda j: (0, j))] * 3,
        out_shape=[jax.ShapeDtypeStruct((M, F), BF16), jax.ShapeDtypeStruct((M, F), F32), jax.ShapeDtypeStruct((M, F), F32)],
        compiler_params=_cparams("arbitrary"),
        name="ffn_up_step",
    )(x, w_up, w_up, conv_w, conv_w, h2, h2, h2, h2)


def _rope_body(u_ref, cos_ref, sin_ref, q_ref, kv_ref, win_ref):
    cos = cos_ref[...]
    sin = sin_ref[...]
    scale = HEAD_DIM ** -0.5

    def rot(x):
        return x * cos + pltpu.roll(x, HEAD_DIM // 2, axis=1) * sin

    for h in range(NSA_HEADS):
        sl = slice(h * HEAD_DIM, (h + 1) * HEAD_DIM)
        q_ref[:, sl] = (rot(u_ref[:, sl]) * scale).astype(q_ref.dtype)
    base = NSA_HEADS * HEAD_DIM
    n_glob = 4 * NSA_KV_HEADS
    for slot in range(6 * NSA_KV_HEADS):
        x = u_ref[:, base + slot * HEAD_DIM:base + (slot + 1) * HEAD_DIM]
        if (slot // NSA_KV_HEADS) % 2 == 0:
            x = rot(x)
        if slot < n_glob:
            kv_ref[:, slot * HEAD_DIM:(slot + 1) * HEAD_DIM] = x
        else:
            win_ref[:, (slot - n_glob) * HEAD_DIM:(slot - n_glob + 1) * HEAD_DIM] = x


def rope_split(u, cos, sin, tm):
    M, W = u.shape
    nt = cos.shape[0] // tm
    nq = NSA_HEADS * HEAD_DIM
    ng = 4 * NSA_KV_HEADS * HEAD_DIM
    nw = 2 * NSA_KV_HEADS * HEAD_DIM
    return pl.pallas_call(
        _rope_body,
        grid=(M // tm,),
        in_specs=[pl.BlockSpec((tm, W), lambda i: (i, 0)),
                  pl.BlockSpec((tm, HEAD_DIM), lambda i: (i % nt, 0)), pl.BlockSpec((tm, HEAD_DIM), lambda i: (i % nt, 0))],
        out_specs=[pl.BlockSpec((tm, nq), lambda i: (i, 0)), pl.BlockSpec((tm, ng), lambda i: (i, 0)),
                   pl.BlockSpec((tm, nw), lambda i: (i, 0))],
        out_shape=[jax.ShapeDtypeStruct((M, nq), BF16), jax.ShapeDtypeStruct((M, ng), F32), jax.ShapeDtypeStruct((M, nw), F32)],
        compiler_params=_cparams("parallel"),
        name="rope_split",
    )(u, cos, sin)


GDN_HB = 4
GDN_COL_B = 0
GDN_COL_A = 8
NSA_COL_G = 16
GDN_STEP_GROUPS = 2


def _softplus(x):
    return jnp.maximum(x, 0.0) + jnp.log1p(jnp.exp(-jnp.abs(x)))


def _l2norm(x):
    return x * lax.rsqrt(jnp.sum(x * x, axis=-1, keepdims=True) + EPS)


def _solve_unit_lower(a_list, r_list, order):
    steps = int(math.log2(order))
    a_list, r_list = list(a_list), list(r_list)
    m, n = a_list[0].shape[1], r_list[0].shape[1]
    for i in range(steps):
        last = i == steps - 1
        for c in range(len(a_list)):
            a_b = a_list[c].astype(BF16)
            r_hi, r_lo = _split_bf16(r_list[c])
            prod = _dot(a_b, jnp.concatenate(([] if last else [a_b]) + [r_hi, r_lo], axis=1))
            if not last:
                a_list[c], prod = prod[:, :m], prod[:, m:]
            r_list[c] = r_list[c] + (prod[:, :n] + prod[:, n:])
    return r_list


def _block_diag(blocks):
    z = jnp.zeros_like(blocks[0])
    nb = len(blocks)
    return jnp.concatenate([jnp.concatenate([blocks[h] if j == h else z for j in range(nb)], axis=1)
                            for h in range(nb)], axis=0)


def _gdn_chunks_groups(groups, states):
    nh = len(groups[0][0][0])
    C, d = groups[0][0][2][0].shape
    W = nh * C
    stack = lambda xs: jnp.concatenate(xs, axis=0)
    bd = lambda xs: _block_diag([x.astype(BF16) for x in xs])
    ri = lax.broadcasted_iota(jnp.int32, (W, W), 0)
    ci = lax.broadcasted_iota(jnp.int32, (W, W), 1)
    same = (ri // C) == (ci // C)

    a_list, r_list, pre = [], [], []
    for chunks in groups:
        for q, k, v, beta, gc, gcr in chunks:
            dmat = jnp.where(same & (ri >= ci), jnp.exp(jnp.minimum(stack(gc) - jnp.concatenate(gcr, axis=1), 0.0)), 0.0)
            eg = [jnp.exp(g) for g in gc]
            kb = [k[h] * beta[h] for h in range(nh)]
            xk = _dot_nt(jnp.concatenate([bd(kb), bd(q)], axis=0), bd(k))
            a_list.append(jnp.where(same & (ri > ci), -(xk[:W] * dmat), 0.0))
            r_list.append(jnp.concatenate([stack([v[h] * beta[h] for h in range(nh)]),
                                           stack([kb[h] * eg[h] for h in range(nh)])], axis=1))
            g_last = [g[C - 1:C, :] for g in gc]
            pre.append(((xk[W:] * dmat).astype(BF16), bd([q[h] * eg[h] for h in range(nh)]),
                        bd([k[h] * jnp.exp(g_last[h] - gc[h]) for h in range(nh)]),
                        stack([jnp.broadcast_to(jnp.exp(g_last[h]), (d, 1)) for h in range(nh)])))
    w_list = _solve_unit_lower(a_list, r_list, C)

    n_chunks = len(groups[0])
    states = list(states)
    outs = [[] for _ in groups]
    for c in range(n_chunks):
        for g in range(len(groups)):
            w, (aqk, q_eg, kd, keep) = w_list[g * n_chunks + c], pre[g * n_chunks + c]
            wk = bd([w[h * C:(h + 1) * C, d:] for h in range(nh)])
            xs = _dot(jnp.concatenate([wk, q_eg], axis=0), states[g].astype(BF16))
            v_new = (w[:, :d] - xs[:W]).astype(BF16)
            outs[g].append(xs[W:] + _dot(aqk, v_new))
            states[g] = states[g] * keep + _dot_tn(kd, v_new)
    return outs, states


def _gdn_seq_body(q_ref, k_ref, v_ref, z_ref, ba_ref, cq_ref, ck_ref, cv_ref, alog_ref, dtb_ref, nw_ref,
                  o_ref, s_ref, tq_ref, tk_ref, tv_ref, qs_ref, ks_ref, vs_ref, gs_ref, bs_ref):
    t_idx = pl.program_id(2)
    tt = q_ref.shape[0]
    hb = q_ref.shape[1] // GDN_DK
    C = GDN_CHUNK

    @pl.when(t_idx == 0)
    def _():
        s_ref[...] = jnp.zeros_like(s_ref)
        tq_ref[...] = jnp.zeros_like(tq_ref)
        tk_ref[...] = jnp.zeros_like(tk_ref)
        tv_ref[...] = jnp.zeros_like(tv_ref)

    def conv(x_ref, c_ref, tail_ref):
        x = x_ref[...]
        hist = tail_ref[...]
        row = lax.broadcasted_iota(jnp.int32, x.shape, 0)
        c = c_ref[...]
        out = x * c[GDN_CONV - 1:GDN_CONV]
        for j in range(GDN_CONV - 1):
            out = out + _shift_rows(x, hist, GDN_CONV - 1 - j) * c[j:j + 1]
        tail_ref[...] = x[tt - SUBLANES:tt]
        return _silu(out)

    qa = conv(q_ref, cq_ref, tq_ref)
    ka = conv(k_ref, ck_ref, tk_ref)
    vs_ref[...] = conv(v_ref, cv_ref, tv_ref)
    for h in range(hb):
        sl = slice(h * GDN_DK, (h + 1) * GDN_DK)
        qs_ref[:, sl] = _l2norm(qa[:, sl]) * GDN_DK ** -0.5
        ks_ref[:, sl] = _l2norm(ka[:, sl])
    ba = ba_ref[...]
    bs_ref[...] = _sigmoid(ba)
    n_groups = hb // GDN_HB
    par_row = lambda ref: jnp.concatenate([ref[g, 0:1, :] for g in range(n_groups)], axis=1)
    gs_ref[...] = -jnp.exp(par_row(alog_ref)) * _softplus(ba + par_row(dtb_ref))

    tri = (lax.broadcasted_iota(jnp.int32, (C, C), 0) >= lax.broadcasted_iota(jnp.int32, (C, C), 1)).astype(F32)
    sel = (lax.broadcasted_iota(jnp.int32, (2 * SUBLANES, LANES), 0)
           == lax.broadcasted_iota(jnp.int32, (2 * SUBLANES, LANES), 1)).astype(F32)
    nw = nw_ref[...]

    chunk_rows = [slice(c * C, (c + 1) * C) for c in range(tt // C)]
    gc_alls = [_dot(tri, gs_ref[rows, :], HI) for rows in chunk_rows]
    groups, group_heads = [], []
    for g in range(n_groups):
        heads = [slice((g * GDN_HB + h) * GDN_DK, (g * GDN_HB + h + 1) * GDN_DK) for h in range(GDN_HB)]
        lanes = slice(g * LANES, (g + 1) * LANES)
        chunks = []
        for rows, gc_full in zip(chunk_rows, gc_alls):
            gc_all = gc_full[:, lanes]
            gc_t = _dot_nt(sel, gc_all, HI)
            beta_all = bs_ref[rows, lanes]
            chunks.append((
                [qs_ref[rows, sl] for sl in heads], [ks_ref[rows, sl] for sl in heads], [vs_ref[rows, sl] for sl in heads],
                [beta_all[:, GDN_COL_B + h:GDN_COL_B + h + 1] for h in range(GDN_HB)],
                [gc_all[:, GDN_COL_A + h:GDN_COL_A + h + 1] for h in range(GDN_HB)],
                [gc_t[GDN_COL_A + h:GDN_COL_A + h + 1, :] for h in range(GDN_HB)]))
        groups.append(chunks)
        group_heads.append(heads)
    gdk = GDN_HB * GDN_DK
    outs, states = _gdn_chunks_groups(groups, [s_ref[0, g * gdk:(g + 1) * gdk] for g in range(n_groups)])
    for g in range(n_groups):
        s_ref[0, g * gdk:(g + 1) * gdk] = states[g]
        for rows, o in zip(chunk_rows, outs[g]):
            on = o * lax.rsqrt(jnp.mean(o * o, axis=-1, keepdims=True) + EPS) * nw
            for h, sl in enumerate(group_heads[g]):
                o_ref[rows, sl] = (on[h * C:(h + 1) * C] * _silu(z_ref[rows, sl])).astype(o_ref.dtype)


def gdn_seq(u_qkv, u_z, u_small, conv_w, alog_rows, dtb_rows, norm_w, n_seq, tt):
    M = u_qkv.shape[0]
    T = M // n_seq
    nt = T // tt
    hb = GDN_HB * GDN_STEP_GROUPS
    nhb = GDN_HEADS // hb
    wb = hb * GDN_DK
    gl = GDN_STEP_GROUPS * LANES
    row_blk = lambda off: pl.BlockSpec((tt, wb), lambda b, h, t: (b * nt + t, h + off))
    cw_blk = lambda off: pl.BlockSpec((GDN_CONV, wb), lambda b, h, t: (0, h + off))
    par_blk = pl.BlockSpec((GDN_STEP_GROUPS, SUBLANES, LANES), lambda b, h, t: (h, 0, 0))
    return pl.pallas_call(
        _gdn_seq_body,
        grid=(n_seq, nhb, nt),
        in_specs=[row_blk(0), row_blk(nhb), row_blk(2 * nhb), row_blk(0),
                  pl.BlockSpec((tt, gl), lambda b, h, t: (b * nt + t, h)),
                  cw_blk(0), cw_blk(nhb), cw_blk(2 * nhb), par_blk, par_blk,
                  pl.BlockSpec((1, GDN_DV), lambda b, h, t: (0, 0))],
        out_specs=[row_blk(0), pl.BlockSpec((1, hb * GDN_DK, GDN_DV), lambda b, h, t: (b, h, 0))],
        out_shape=[jax.ShapeDtypeStruct((M, GDN_HEADS * GDN_DV), BF16),
                   jax.ShapeDtypeStruct((n_seq, GDN_HEADS * GDN_DK, GDN_DV), F32)],
        scratch_shapes=[pltpu.VMEM((SUBLANES, wb), F32)] * 3 + [pltpu.VMEM((tt, wb), F32)] * 3
                       + [pltpu.VMEM((tt, gl), F32)] * 2,
        compiler_params=_cparams("parallel", "parallel", "arbitrary"),
        name="gdn_seq",
    )(u_qkv, u_qkv, u_qkv, u_z, u_small, conv_w, conv_w, conv_w, alog_rows, dtb_rows, norm_w.reshape(1, GDN_DV))


def gdn_param_rows(gdn_a_log, gdn_dt_bias):
    nhb = GDN_HEADS // GDN_HB

    def rows(p):
        r = _place_lanes(((GDN_COL_A, p.reshape(nhb, GDN_HB)),), (nhb,))
        return jnp.broadcast_to(r[:, None, :], (nhb, SUBLANES, LANES))

    return rows(gdn_a_log), rows(gdn_dt_bias)


CMP_RATIO = CMP_LEN // CMP_STRIDE
CMP_FEAT = CMP_STRIDE * HEAD_DIM
MASKED = -1e30


def _compress_mlp(part, pe_ref, w1_ref, w2_ref, chunk_step=1):
    n = part.shape[0]
    pe_part = _dot(pe_ref[...], w1_ref[...])
    hid0 = pe_part[0:1, :CMP_HIDDEN] + pe_part[1:2, CMP_HIDDEN:]
    hid = hid0 + part[:, :CMP_HIDDEN]
    hid = hid + pltpu.roll(part[:, CMP_HIDDEN:], n - chunk_step, axis=0)
    return _dot(_silu(hid).astype(BF16), w2_ref[...])


def _compress_seq_body(x_ref, pe_ref, w1_ref, w2_ref, o_ref, xc_ref):
    n = o_ref.shape[3]
    for s in range(CMP_STRIDE):
        xc_ref[:, s * HEAD_DIM:(s + 1) * HEAD_DIM] = x_ref[pl.ds(s, n, stride=CMP_STRIDE), :].astype(BF16)
    part = _dot(xc_ref[...], w1_ref[0])
    o_ref[0, 0, 0] = _compress_mlp(part, pe_ref.at[0], w1_ref.at[0], w2_ref.at[0])


def compress_seq(kv_rows, pe2, w1c, w2c, n_seq):
    M = kv_rows.shape[0]
    T = M // n_seq
    n = T // CMP_STRIDE
    G = NSA_KV_HEADS
    return pl.pallas_call(
        _compress_seq_body,
        grid=(n_seq, 2, G),
        in_specs=[pl.BlockSpec((T, HEAD_DIM), lambda b, c, g: (b, c * G + g)),
                  pl.BlockSpec((1, SUBLANES, CMP_FEAT), lambda b, c, g: (c, 0, 0)),
                  pl.BlockSpec((1, CMP_FEAT, 2 * CMP_HIDDEN), lambda b, c, g: (c, 0, 0)),
                  pl.BlockSpec((1, CMP_HIDDEN, HEAD_DIM), lambda b, c, g: (c, 0, 0))],
        out_specs=pl.BlockSpec((1, 1, 1, n, HEAD_DIM), lambda b, c, g: (b, c, g, 0, 0)),
        out_shape=jax.ShapeDtypeStruct((n_seq, 2, G, n, HEAD_DIM), F32),
        scratch_shapes=[pltpu.VMEM((n, CMP_FEAT), BF16)],
        compiler_params=_cparams("parallel", "parallel", "parallel"),
        name="compress_seq",
    )(kv_rows, pe2, w1c, w2c)


def compress_params(cmp_pe, cmp_w1, cmp_w2):
    w1 = cmp_w1.reshape(2, CMP_RATIO, CMP_FEAT, CMP_HIDDEN)
    w1c = jnp.concatenate([w1[:, r] for r in range(CMP_RATIO)], axis=-1).astype(BF16)
    pe = cmp_pe.reshape(2, CMP_RATIO, CMP_FEAT)
    pe2 = jnp.zeros((2, SUBLANES, CMP_FEAT), F32).at[:, :CMP_RATIO].set(pe).astype(BF16)
    return pe2, w1c, cmp_w2.astype(BF16)


def _masked_softmax(s, mask, axis=-1):
    s = jnp.where(mask, s, -jnp.inf)
    m = jnp.max(s, axis=axis, keepdims=True)
    m = jnp.where(m > -jnp.inf, m, 0.0)
    e = jnp.exp(s - m)
    return e * (1.0 / jnp.maximum(jnp.sum(e, axis=axis, keepdims=True), 1e-30))


def _masked_softmax_heads(s, mask, n_heads):
    t = mask.shape[0]
    return jnp.concatenate([_masked_softmax(s[h * t:(h + 1) * t], mask) for h in range(n_heads)], axis=0)


def _split_bf16(x):
    hi = x.astype(BF16)
    return hi, (x - hi.astype(F32)).astype(BF16)


def _select_blocks_t(imp_t, pos_row, n_blocks):
    J = imp_t.shape[0]
    j = lax.broadcasted_iota(jnp.int32, imp_t.shape, 0)
    cur = pos_row // SLC_LEN
    forced = (j == 0) | ((j <= cur) & (j > cur - SLC_LOCAL))
    score = jnp.where(j * SLC_LEN > pos_row, -jnp.inf, jnp.where(forced, jnp.inf, imp_t))
    score = jnp.where(j < n_blocks, score, -jnp.inf)
    tiles = [score[r:r + SUBLANES] for r in range(0, J, SUBLANES)]
    ranks = [jnp.zeros(t.shape, F32) for t in tiles]
    for jp in range(n_blocks):
        row = score[jp:jp + 1, :]
        for i, t in enumerate(tiles):
            lo_j = i * SUBLANES
            ge, gt = jnp.where(row >= t, 1.0, 0.0), jnp.where(row > t, 1.0, 0.0)
            if lo_j > jp:
                ahead = ge
            elif lo_j + SUBLANES - 1 <= jp:
                ahead = gt
            else:
                ahead = jnp.where(lax.broadcasted_iota(jnp.int32, t.shape, 0) + lo_j > jp, ge, gt)
            ranks[i] = ranks[i] + ahead
    rank = jnp.concatenate(ranks, axis=0)
    keep = (rank < float(min(SLC_TOP_N, n_blocks))) & (j < n_blocks)
    return jnp.where(keep, 1.0, 0.0)


def _nsa_seq_body(q_ref, kc_ref, vc_ref, ks_ref, vs_ref, kw_ref, vw_ref, gl_ref, ovl_ref, o_ref, m_ref, l_ref, acc_ref, *, kblk):
    i = pl.program_id(2)
    tq = q_ref.shape[0]
    T = ks_ref.shape[0]
    H = NSA_GROUP
    D = HEAD_DIM
    n_slc = T // SLC_LEN
    qb = q_ref[...]
    q4 = jnp.concatenate([qb[:, h * D:(h + 1) * D] for h in range(H)], axis=0)
    t0 = i * tq
    pos1 = t0 + lax.broadcasted_iota(jnp.int32, (tq, 1), 0)

    kc = kc_ref[0, 0, 0].astype(BF16)
    vc = vc_ref[0, 0, 0].astype(BF16)
    n_c = kc.shape[0]
    s = _dot_nt(q4, kc)
    cmp_end = lax.broadcasted_iota(jnp.int32, (1, n_c), 1) * CMP_STRIDE + (CMP_LEN - 1)
    p = _masked_softmax_heads(s, cmp_end <= pos1, H)
    o_c = _dot(p.astype(BF16), vc)

    psum = p[0:tq]
    for h in range(1, H):
        psum = psum + p[h * tq:(h + 1) * tq]
    p_hi, p_lo = _split_bf16(psum)
    ovl = ovl_ref[...]
    imp_t = _dot_nt(ovl, p_hi) + _dot_nt(ovl, p_lo)
    pos_row = t0 + lax.broadcasted_iota(jnp.int32, (1, tq), 1)
    sel_t = _select_blocks_t(imp_t, pos_row, n_slc)
    eye = (lax.broadcasted_iota(jnp.int32, (tq, tq), 0) == lax.broadcasted_iota(jnp.int32, (tq, tq), 1))
    sel = _dot_nt(jnp.where(eye, 1.0, 0.0).astype(BF16), sel_t.astype(BF16)).astype(BF16)

    n_kb = (t0 + tq + kblk - 1) // kblk

    m_ref[...] = jnp.full(m_ref.shape, MASKED, F32)
    l_ref[...] = jnp.zeros(l_ref.shape, F32)
    acc_ref[...] = jnp.zeros(acc_ref.shape, F32)

    def kv_step(kb, carry):
        k0 = pl.multiple_of(kb * kblk, kblk)
        kk = ks_ref[pl.ds(k0, kblk), :].astype(BF16)
        vv = vs_ref[pl.ds(k0, kblk), :].astype(BF16)
        s2 = _dot_nt(q4, kk)
        kpos = k0 + lax.broadcasted_iota(jnp.int32, (1, kblk), 1)
        expand = (lax.broadcasted_iota(jnp.int32, (n_slc, kblk), 0) == kpos // SLC_LEN)
        chosen = _dot(sel, jnp.where(expand, 1.0, 0.0).astype(BF16))
        bias = jnp.where((chosen > 0.5) & (kpos <= pos1), 0.0, MASKED)
        es = []
        for h in range(H):
            rows = slice(h * tq, (h + 1) * tq)
            sh = s2[rows] + bias
            m_old = m_ref[rows]
            m_new = jnp.maximum(m_old, jnp.max(sh, axis=-1, keepdims=True))
            alpha = jnp.exp(m_old - m_new)
            e = jnp.exp(sh - m_new)
            m_ref[rows] = m_new
            l_ref[rows] = alpha * l_ref[rows] + jnp.sum(e, axis=-1, keepdims=True)
            acc_ref[rows] = alpha * acc_ref[rows]
            es.append(e.astype(BF16))
        acc_ref[...] += _dot(jnp.concatenate(es, axis=0), vv)
        return carry

    lax.fori_loop(0, n_kb, kv_step, 0)
    o_s = acc_ref[...] * (1.0 / jnp.maximum(l_ref[...], 1e-30))

    band = WINDOW + tq
    w0 = pl.multiple_of(jnp.maximum(t0 - WINDOW, 0), tq)
    kw = kw_ref[pl.ds(w0, band), :].astype(BF16)
    vw = vw_ref[pl.ds(w0, band), :].astype(BF16)
    sw = _dot_nt(q4, kw)
    diff = pos1 - (w0 + lax.broadcasted_iota(jnp.int32, (1, band), 1))
    pw = _masked_softmax_heads(sw, (diff >= 0) & (diff < WINDOW), H)
    o_w = _dot(pw.astype(BF16), vw)

    gates = _sigmoid(gl_ref[...])
    for h in range(H):
        rows = slice(h * tq, (h + 1) * tq)
        g = lambda br: gates[:, NSA_COL_G + H * br + h:NSA_COL_G + H * br + h + 1]
        o_ref[:, h * D:(h + 1) * D] = (g(0) * o_c[rows] + g(1) * o_s[rows] + g(2) * o_w[rows]).astype(o_ref.dtype)


def _overlap_t(n_cmp_rows, n_slc):
    start = np.arange(n_cmp_rows)[None, :] * CMP_STRIDE
    blk = np.arange(n_slc)[:, None] * SLC_LEN
    return ((start < blk + SLC_LEN) & (start + CMP_LEN > blk)).astype(np.float32)


def nsa_seq(q, kv_rows, win_rows, cmp_kv, u_small, n_seq, tq=256, kblk=512):
    M = q.shape[0]
    T = M // n_seq
    nt = T // tq
    G = NSA_KV_HEADS
    gw = NSA_GROUP * HEAD_DIM
    n_c = cmp_kv.shape[3]
    ovl = jnp.asarray(_overlap_t(n_c, T // SLC_LEN), BF16)
    body = functools.partial(_nsa_seq_body, kblk=kblk)
    seq_blk = lambda off: pl.BlockSpec((T, HEAD_DIM), lambda b, g, i: (b, g + off))
    cmp_blk = lambda c: pl.BlockSpec((1, 1, 1, n_c, HEAD_DIM), lambda b, g, i: (b, c, g, 0, 0))
    return pl.pallas_call(
        body,
        grid=(n_seq, G, nt),
        in_specs=[pl.BlockSpec((tq, gw), lambda b, g, i: (b * nt + i, g)),
                  cmp_blk(0), cmp_blk(1), seq_blk(2 * G), seq_blk(3 * G), seq_blk(0), seq_blk(G),
                  pl.BlockSpec((tq, LANES), lambda b, g, i: (b * nt + i, g)),
                  pl.BlockSpec(ovl.shape, lambda b, g, i: (0, 0))],
        out_specs=pl.BlockSpec((tq, gw), lambda b, g, i: (b * nt + i, g)),
        out_shape=jax.ShapeDtypeStruct((M, NSA_HEADS * HEAD_DIM), BF16),
        scratch_shapes=[pltpu.VMEM((NSA_GROUP * tq, 1), F32)] * 2 + [pltpu.VMEM((NSA_GROUP * tq, HEAD_DIM), F32)],
        compiler_params=_cparams("parallel", "parallel", "arbitrary"),
        name="nsa_seq",
    )(q, cmp_kv, cmp_kv, kv_rows, kv_rows, win_rows, win_rows, u_small, ovl)


STEP_COL_G = 0
STEP_COL_B = 3 * NSA_HEADS
STEP_COL_A = STEP_COL_B + GDN_HEADS


def _gdn_step_prep_body(x_ref, h_ref, c_ref, us_ref, alog_ref, dtb_ref, q_ref, k_ref, v_ref, b_ref, e_ref):
    W = x_ref.shape[1]
    c = c_ref[...]
    x = x_ref[...] * c[GDN_CONV - 1:GDN_CONV]
    for j in range(GDN_CONV - 1):
        x = x + h_ref[:, j * W:(j + 1) * W] * c[j:j + 1]
    x = _silu(x)
    us = us_ref[...]
    beta = _sigmoid(us)
    eg = jnp.exp(-jnp.exp(alog_ref[...]) * _softplus(us + dtb_ref[...]))
    nq = GDN_HEADS * GDN_DK
    for h in range(GDN_HEADS):
        sl = slice(h * GDN_DK, (h + 1) * GDN_DK)
        q_ref[:, sl] = _l2norm(x[:, sl]) * GDN_DK ** -0.5
        k_ref[:, sl] = _l2norm(x[:, nq + h * GDN_DK:nq + (h + 1) * GDN_DK])
        b_ref[:, sl] = jnp.broadcast_to(beta[:, STEP_COL_B + h:STEP_COL_B + h + 1], (x.shape[0], GDN_DK))
        e_ref[:, sl] = jnp.broadcast_to(eg[:, STEP_COL_A + h:STEP_COL_A + h + 1], (x.shape[0], GDN_DK))
    v_ref[...] = x[:, 2 * nq:]


def _gdn_step_body(q_ref, k_ref, v_ref, b_ref, e_ref, z_ref, nw_ref, s_ref, o_ref, so_ref):
    bb = q_ref.shape[0]
    eye = jnp.where(lax.broadcasted_iota(jnp.int32, (GDN_DK, GDN_DK), 0)
                    == lax.broadcasted_iota(jnp.int32, (GDN_DK, GDN_DK), 1), 1.0, 0.0)
    nw = nw_ref[...]

    def per_seq(bi, carry):
        q, k, v, beta, eg, z = q_ref[bi], k_ref[bi], v_ref[bi], b_ref[bi], e_ref[bi], z_ref[bi]
        k_t = _dot_nt(eye, k, HI)
        q_t = _dot_nt(eye, q, HI)
        outs = []
        for h in range(GDN_HEADS):
            r = slice(h, h + 1)
            s = s_ref[bi, h]
            kcol = k_t[:, h:h + 1]
            k_s = jnp.sum(kcol * s, axis=0, keepdims=True)
            q_s = jnp.sum(q_t[:, h:h + 1] * s, axis=0, keepdims=True)
            v_new = v[r] * beta[r] - (beta[r] * eg[r]) * k_s
            qk = jnp.sum(q[r] * k[r], axis=-1, keepdims=True)
            o = eg[r] * q_s + qk * v_new
            so_ref[bi, h] = s * eg[r] + kcol * v_new
            on = o * lax.rsqrt(jnp.mean(o * o, axis=-1, keepdims=True) + EPS) * nw
            outs.append(on * _silu(z[r]))
        o_ref[bi] = jnp.concatenate(outs, axis=0).astype(o_ref.dtype)
        return carry

    lax.fori_loop(0, bb, per_seq, 0)


def gdn_step(u_qkv, u_z, u_small, hist, state, conv_w, a_log, dt_bias, norm_w, bb=4):
    Bd, W = u_qkv.shape
    H = GDN_HEADS
    nq = H * GDN_DK
    row = lambda p: _place_lanes(((STEP_COL_A, p.reshape(1, H)),), (1,))
    full = lambda shape: pl.BlockSpec(shape, lambda i: (0,) * len(shape))
    outs = pl.pallas_call(
        _gdn_step_prep_body,
        grid=(1,),
        in_specs=[full((Bd, W)), full((Bd, (GDN_CONV - 1) * W)), full((GDN_CONV, W)), full((Bd, LANES)),
                  full((1, LANES)), full((1, LANES))],
        out_specs=[full((Bd, nq))] * 5,
        out_shape=[jax.ShapeDtypeStruct((Bd, nq), F32)] * 5,
        compiler_params=_cparams("arbitrary"),
        name="gdn_step_prep",
    )(u_qkv, hist.reshape(Bd, (GDN_CONV - 1) * W), conv_w, u_small, row(a_log), row(dt_bias))
    heads = lambda a: a.reshape(Bd, H, GDN_DK)
    vec_blk = pl.BlockSpec((bb, H, GDN_DK), lambda i: (i, 0, 0))
    st_blk = pl.BlockSpec((bb, H, GDN_DK, GDN_DV), lambda i: (i, 0, 0, 0))
    o, s_new = pl.pallas_call(
        _gdn_step_body,
        grid=(Bd // bb,),
        in_specs=[vec_blk] * 6 + [pl.BlockSpec((1, GDN_DV), lambda i: (0, 0)), st_blk],
        out_specs=[vec_blk, st_blk],
        out_shape=[jax.ShapeDtypeStruct((Bd, H, GDN_DV), BF16), jax.ShapeDtypeStruct(state.shape, F32)],
        compiler_params=_cparams("parallel"),
        name="gdn_step",
    )(*[heads(a) for a in outs], heads(u_z), norm_w.reshape(1, GDN_DV), state)
    return o.reshape(Bd, H * GDN_DV), s_new


def _softmax_with_new(s_past, valid, s_new):
    s_past = jnp.where(valid, s_past, -jnp.inf)
    m = jnp.maximum(jnp.max(s_past, axis=-1, keepdims=True), s_new)
    e = jnp.exp(s_past - m)
    e_new = jnp.exp(s_new - m)
    return e, e_new, 1.0 / (jnp.sum(e, axis=-1, keepdims=True) + e_new)


def _nsa_step_body(pt_ref, *refs, n_pages, page, n_past):
    lo = refs[:n_pages]
    hi = refs[n_pages:2 * n_pages]
    (q_ref, new_ref, win_ref, wnew_ref, gl_ref, pe_ref, w1_ref, w2_ref, ovl_ref, o_ref, wout_ref, xk_ref, xv_ref) = refs[2 * n_pages:]
    G, H, D = NSA_KV_HEADS, NSA_HEADS, HEAD_DIM
    HS = 2 * G
    pos = n_past
    cpp = page // CMP_STRIDE
    n_chunk = n_pages * cpp
    L = n_pages * page
    q = q_ref[0]
    qf = q.astype(F32)
    new = new_ref[0]
    row_g = lax.broadcasted_iota(jnp.int32, (H, 1), 0) // NSA_GROUP

    def per_head_rows(rows):
        return jnp.concatenate([jnp.broadcast_to(rows[g:g + 1], (NSA_GROUP, rows.shape[1])) for g in range(G)], axis=0)

    first_half = lax.broadcasted_iota(jnp.int32, (SUBLANES, D), 0) < G
    for j in range(n_pages):
        for s in range(CMP_STRIDE):
            for m in range(cpp // 4):
                ks, vs = [], []
                for pair in range(2):
                    ta = lo[j][0, (4 * m + 2 * pair) * CMP_STRIDE + s]
                    tb = lo[j][0, (4 * m + 2 * pair + 1) * CMP_STRIDE + s]
                    ks.append(jnp.where(first_half, ta, pltpu.roll(tb, G, axis=0)))
                    vs.append(jnp.where(first_half, pltpu.roll(ta, G, axis=0), tb))
                r0 = (j * cpp + 4 * m) * G
                xk_ref[r0:r0 + 4 * G, s * D:(s + 1) * D] = jnp.concatenate(ks, axis=0).astype(BF16)
                xv_ref[r0:r0 + 4 * G, s * D:(s + 1) * D] = jnp.concatenate(vs, axis=0).astype(BF16)
    ckv = []
    for c, x_ref in enumerate((xk_ref, xv_ref)):
        part = _dot(x_ref[...], w1_ref[c])
        ckv.append(_compress_mlp(part, pe_ref.at[c], w1_ref.at[c], w2_ref.at[c], G).astype(BF16))

    n_all = G * n_chunk
    s = _dot_nt(q, ckv[0])
    col = lax.broadcasted_iota(jnp.int32, (1, n_all), 1)
    blk = col // G
    ok = (col % G == row_g) & (blk * CMP_STRIDE + (CMP_LEN - 1) <= pos) & (blk < n_chunk - CMP_RATIO + 1)
    p = _masked_softmax(s, ok)
    o_c = _dot(p.astype(BF16), ckv[1])

    gsum = jnp.where(lax.broadcasted_iota(jnp.int32, (LANES, H), 0) == lax.broadcasted_iota(jnp.int32, (LANES, H), 1) // NSA_GROUP, 1.0, 0.0)
    psum = _dot(gsum, p, HI)
    p_hi, p_lo = _split_bf16(psum)
    ovl = ovl_ref[...]
    imp_t = _dot_nt(ovl, p_hi) + _dot_nt(ovl, p_lo)
    n_slc = -(-(L + 1) // SLC_LEN)
    sel_t = _select_blocks_t(imp_t, jnp.full((1, LANES), pos, jnp.int32), n_slc)
    J = sel_t.shape[0]
    eye8 = jnp.where(lax.broadcasted_iota(jnp.int32, (SUBLANES, LANES), 0) == lax.broadcasted_iota(jnp.int32, (SUBLANES, LANES), 1), 1.0, 0.0)
    sel = _dot_nt(eye8.astype(BF16), sel_t.astype(BF16))

    def slot_softmax(sc, tok_ok, s_new):
        slot = lax.broadcasted_iota(jnp.int32, (1, sc.shape[1]), 1) % HS
        return _softmax_with_new(sc, (slot == row_g) & tok_ok, s_new)

    tok = lax.broadcasted_iota(jnp.int32, (1, L * HS), 1) // HS
    chosen = per_head_rows(jnp.concatenate(
        [jnp.broadcast_to(sel[:, b:b + 1], (SUBLANES, SLC_LEN * HS)) for b in range(L // SLC_LEN)], axis=1))
    kv_hi = [hi[j][0].reshape(page * HS, D).astype(BF16) for j in range(n_pages)]
    sc = jnp.concatenate([_dot_nt(q, kv) for kv in kv_hi], axis=1)
    s_new = jnp.sum(qf * per_head_rows(new[2 * G:3 * G]), axis=-1, keepdims=True)
    e, e_new, inv = slot_softmax(sc, (chosen > 0.5) & (tok <= pos), s_new)
    e = pltpu.roll(e, G, axis=1).astype(BF16)
    acc = jnp.zeros((H, D), F32)
    for j, kv in enumerate(kv_hi):
        acc = acc + _dot(e[:, j * page * HS:(j + 1) * page * HS], kv)
    o_s = (acc + e_new * per_head_rows(new[3 * G:4 * G])) * inv

    n_win = win_ref.shape[1] // HS
    wnew = wnew_ref[0]
    win = win_ref[0].astype(BF16)
    wtok = pos - n_win + lax.broadcasted_iota(jnp.int32, (1, n_win * HS), 1) // HS
    sw_new = jnp.sum(qf * per_head_rows(wnew[0:G]), axis=-1, keepdims=True)
    ew, ew_new, winv = slot_softmax(_dot_nt(q, win), (pos - wtok < WINDOW) & (wtok >= 0), sw_new)
    o_w = (_dot(pltpu.roll(ew, G, axis=1).astype(BF16), win) + ew_new * per_head_rows(wnew[G:2 * G])) * winv

    gl = jnp.broadcast_to(_sigmoid(gl_ref[0]), (SUBLANES, LANES))
    hh = lax.broadcasted_iota(jnp.int32, (H, LANES), 0)
    cc = lax.broadcasted_iota(jnp.int32, (H, LANES), 1)
    gate = lambda br: _dot_nt(jnp.where(cc == STEP_COL_G + br * H + hh, 1.0, 0.0), gl, HI)[:, 0:1]
    o_ref[0] = (gate(0) * o_c + gate(1) * o_s + gate(2) * o_w).astype(o_ref.dtype)

    wout_ref[0, :(n_win - 1) * HS] = win_ref[0, HS:]
    wout_ref[0, (n_win - 1) * HS:] = wnew


def nsa_step(q, new_rows, win_new, cache, page_table, win_state, u_small, pe2, w1c, w2c):
    Bd = q.shape[0]
    n_pool, page = cache.shape[:2]
    n_pages = page_table.shape[1]
    G, H, D = NSA_KV_HEADS, NSA_HEADS, HEAD_DIM
    n_past = n_pages * page
    n_chunk = n_past // CMP_STRIDE
    n_win = win_state.shape[1]
    assert n_past % SLC_LEN == 0 and n_win == WINDOW <= n_past and page % (4 * CMP_STRIDE) == 0 and 2 * G == SUBLANES
    n_slc = -(-(n_past + 1) // SLC_LEN)
    J = -(-n_slc // (2 * SUBLANES)) * (2 * SUBLANES)
    ovl = np.zeros((J, n_chunk * G), np.float32)
    ovl[:n_slc] = np.repeat(_overlap_t(n_chunk, n_slc), G, axis=1)
    ovl = jnp.asarray(ovl, BF16)
    body = functools.partial(_nsa_step_body, n_pages=n_pages, page=page, n_past=n_past)
    half_spec = lambda j, half: pl.BlockSpec((1, page, None, 2 * G, D), lambda b, pt: (pt[b, j], 0, half, 0, 0))
    const = lambda a: pl.BlockSpec(a.shape, lambda b, pt: (0,) * a.ndim, pipeline_mode=pl.Buffered(1))
    per_b = lambda shape: pl.BlockSpec((1,) + shape, lambda b, pt: (b,) + (0,) * len(shape))
    cache5 = cache.reshape(n_pool, page, 2, 2 * G, D)
    out, win_out = pl.pallas_call(
        body,
        grid_spec=pltpu.PrefetchScalarGridSpec(
            num_scalar_prefetch=1,
            grid=(Bd,),
            in_specs=[half_spec(j, 0) for j in range(n_pages)] + [half_spec(j, 1) for j in range(n_pages)]
                     + [per_b((H, D)), per_b((4 * G, D)), per_b((n_win * 2 * G, D)), per_b((2 * G, D)), per_b((1, LANES)),
                        const(pe2), const(w1c), const(w2c), const(ovl)],
            out_specs=[per_b((H, D)), per_b((n_win * 2 * G, D))],
            scratch_shapes=[pltpu.VMEM((G * n_chunk, CMP_FEAT), BF16)] * 2,
        ),
        out_shape=[jax.ShapeDtypeStruct((Bd, H, D), BF16), jax.ShapeDtypeStruct((Bd, n_win * 2 * G, D), F32)],
        compiler_params=_cparams("arbitrary"),
        name="nsa_step",
    )(page_table, *([cache5] * (2 * n_pages)), q.reshape(Bd, H, D), new_rows.reshape(Bd, 4 * G, D),
      win_state.reshape(Bd, n_win * 2 * G, D), win_new.reshape(Bd, 2 * G, D), u_small.reshape(Bd, 1, LANES),
      pe2, w1c, w2c, ovl)
    return out.reshape(Bd, H * D), win_out.reshape(win_state.shape)


def rope_tables(pos):
    half = HEAD_DIM // 2
    inv = ROPE_THETA ** (-jnp.arange(half, dtype=F32) / half)
    ang = pos.astype(F32)[:, None] * inv[None, :]
    cos, sin = jnp.cos(ang), jnp.sin(ang)
    return jnp.concatenate([cos, cos], axis=-1), jnp.concatenate([-sin, sin], axis=-1)


def _arrange_w_in(w_in):
    H, G = NSA_HEADS, NSA_KV_HEADS
    sizes = (H * HEAD_DIM, 6 * G * HEAD_DIM, 3 * H, GDN_HEADS * (2 * GDN_DK + GDN_DV), GDN_HEADS * GDN_DV, 2 * GDN_HEADS)
    o = np.cumsum((0,) + sizes).tolist()
    w_t = w_in.T
    K = w_in.shape[0]
    wide = ((o[0], o[2] - o[0]), (o[3], o[4] - o[3]), (o[4], o[5] - o[4]), (o[6], w_t.shape[0] - o[6]))
    w_ng = w_t[o[2]:o[3]].T
    w_ba = w_t[o[5]:o[6]].T
    nhb = GDN_HEADS // GDN_HB
    ng = w_ng.reshape(K, 3, G, NSA_GROUP).transpose(0, 2, 1, 3).reshape(K, G, 3 * NSA_GROUP)
    pieces = ((GDN_COL_B, w_ba[:, :GDN_HEADS].reshape(K, nhb, GDN_HB)),
              (GDN_COL_A, w_ba[:, GDN_HEADS:].reshape(K, nhb, GDN_HB)), (NSA_COL_G, ng))
    w_small_seq = _place_lanes(pieces, (K, nhb)).reshape(K, nhb * LANES).astype(BF16)
    w_small_step = _place_lanes(((STEP_COL_G, w_ng), (STEP_COL_B, w_ba)), (K,)).astype(BF16)
    return w_t.astype(BF16), wide, w_small_seq, w_small_step


def _place_lanes(pieces, lead):
    out, at = [], 0
    for lane, vals in pieces:
        out += [jnp.zeros(lead + (lane - at,), F32), vals.astype(F32)]
        at = lane + vals.shape[-1]
    return jnp.concatenate(out + [jnp.zeros(lead + (LANES - at,), F32)], axis=-1)


def kernel(x_prompt, x_sample, cache_nsa_kv, page_table, state_win_kv, state_gdn, state_gdn_conv, state_ffn_conv, norm_mix, w_in, cmp_pe, cmp_w1, cmp_w2, gdn_conv_w, gdn_a_log, gdn_dt_bias, gdn_norm, w_nsa_out, w_gdn_out, w_o, norm_ffn, w_up, ffn_conv_w, w_down, norm_final):
    B, T, D = x_prompt.shape
    Bd = x_sample.shape[0]
    G = NSA_KV_HEADS
    n_past = page_table.shape[1] * cache_nsa_kv.shape[1]
    F = w_down.shape[0]

    w_t, (r_qkv, r_gqkv, r_z, r_mg), w_small_seq, w_small_step = _arrange_w_in(w_in)
    w_nsa_b, w_gdn_b, w_o_b = w_nsa_out.astype(BF16), w_gdn_out.astype(BF16), w_o.astype(BF16)
    w_up_b, w_down_b = w_up.astype(BF16), w_down.astype(BF16)
    pe2, w1c, w2c = compress_params(cmp_pe, cmp_w1, cmp_w2)

    def tail(x, mixed_nsa, mixed_gdn, u_mg, ffn_up, tm):
        mixin = matmul_mix(mixed_nsa, mixed_gdn, w_nsa_b, w_gdn_b, u_mg, tm, 512)
        h = matmul_residual(mixin, w_o_b, x, tm, 512)
        hn = rmsnorm(h, norm_ffn, BF16, min(tm, 256))
        act, extra = ffn_up(hn)
        h2 = matmul_residual(act, w_down_b, h, tm, 256, x_buffers=1)
        return rmsnorm(h2, norm_final, F32, min(tm, 256)), extra

    xp = x_prompt.reshape(B * T, D)
    xn = rmsnorm(xp, norm_mix, BF16, 256)
    tm = 1024
    u_qkv = matmul_nt(xn, w_t, tm, 512, r_qkv)
    u_gqkv = matmul_nt(xn, w_t, tm, 512, r_gqkv)
    u_z = matmul_nt(xn, w_t, tm, 512, r_z)
    u_mg = matmul_nt(xn, w_t, tm, 512, r_mg)
    u_small = matmul(xn, w_small_seq, tm, w_small_seq.shape[1])
    cos, sin = rope_tables(jnp.arange(T, dtype=jnp.int32))
    q, kv_rows, win_rows = rope_split(u_qkv, cos, sin, 256)
    cmp_kv = compress_seq(kv_rows, pe2, w1c, w2c, B)
    o_nsa = nsa_seq(q, kv_rows, win_rows, cmp_kv, u_small, B)
    a_rows, d_rows = gdn_param_rows(gdn_a_log, gdn_dt_bias)
    o_gdn, s_p = gdn_seq(u_gqkv, u_z, u_small, gdn_conv_w, a_rows, d_rows, gdn_norm, B, 256)
    s_p = s_p.reshape(B, GDN_HEADS, GDN_DK, GDN_DV)

    def ffn_up_p(hn):
        act, tg, tv = ffn_up_seq(hn, w_up_b, ffn_conv_w, B, tm, 256)
        last = lambda a: a[T // tm - 1::T // tm, SUBLANES - (FFN_CONV - 1):]
        return act, jnp.concatenate([last(tg), last(tv)], axis=-1)

    y_p, ffn_p = tail(xp, o_nsa, o_gdn, u_mg, ffn_up_p, tm)
    n_win_p = min(WINDOW, T)
    kv_p = kv_rows.reshape(B, T, 4, G, HEAD_DIM)
    win_p = win_rows.reshape(B, T, 2, G, HEAD_DIM)[:, T - n_win_p:]
    conv_p = u_gqkv.reshape(B, T, -1)[:, T - (GDN_CONV - 1):]

    xs = x_sample.reshape(Bd, D)
    xn = rmsnorm(xs, norm_mix, BF16, Bd)
    u_qkv = matmul_nt(xn, w_t, Bd, 512, r_qkv)
    u_gqkv = matmul_nt(xn, w_t, Bd, 512, r_gqkv)
    u_z = matmul_nt(xn, w_t, Bd, 512, r_z)
    u_mg = matmul_nt(xn, w_t, Bd, 512, r_mg)
    u_small = matmul(xn, w_small_step, Bd, LANES)
    cos, sin = rope_tables(jnp.full((Bd,), n_past, jnp.int32))
    q, new_rows, win_new = rope_split(u_qkv, cos, sin, Bd)
    o_nsa, win_s = nsa_step(q, new_rows, win_new, cache_nsa_kv, page_table, state_win_kv, u_small, pe2, w1c, w2c)
    o_gdn, s_s = gdn_step(u_gqkv, u_z, u_small, state_gdn_conv, state_gdn, gdn_conv_w, gdn_a_log, gdn_dt_bias, gdn_norm)

    def ffn_up_s(hn):
        act, ug, uv = ffn_up_step(hn, w_up_b, ffn_conv_w, state_ffn_conv, 256)
        return act, jnp.concatenate([ug, uv], axis=-1)

    y_s, up_new = tail(xs, o_nsa, o_gdn, u_mg, ffn_up_s, Bd)
    kv_s = new_rows.reshape(Bd, 1, 4, G, HEAD_DIM)
    conv_s = jnp.concatenate([state_gdn_conv[:, 1:], u_gqkv[:, None]], axis=1)
    ffn_s = jnp.concatenate([state_ffn_conv[:, 1:], up_new[:, None]], axis=1)

    return (y_p.reshape(B, T, D), y_s.reshape(Bd, 1, D), kv_p, win_p, s_p, conv_p, ffn_p,
            kv_s, win_s, s_s, conv_s, ffn_s)
```

```python
import functools
import math
from typing import NamedTuple

import jax
import jax.numpy as jnp
import numpy as np
from jax import lax
from jax.experimental import pallas as pl
from jax.experimental.pallas import tpu as pltpu

F32 = jnp.float32
BF16 = jnp.bfloat16
HI = lax.Precision.HIGHEST

LANES = 128
SUBLANES = 8
VMEM_LIMIT_BYTES = 56 * 1024 * 1024

HEAD_DIM = 128
NSA_HEADS = 16
NSA_KV_HEADS = 4
NSA_GROUP = NSA_HEADS // NSA_KV_HEADS
CMP_LEN = 32
CMP_STRIDE = 16
CMP_HIDDEN = 256
SLC_LEN = 64
SLC_TOP_N = 16
SLC_LOCAL = 2
WINDOW = 512
GDN_HEADS = 16
GDN_DK = 128
GDN_DV = 128
GDN_CONV = 4
GDN_CHUNK = 64
FFN_CONV = 3
ROPE_THETA = 10000.0
EPS = 1e-6


def _cparams(*sem):
    return pltpu.CompilerParams(dimension_semantics=sem, vmem_limit_bytes=VMEM_LIMIT_BYTES)


def _dot(a, b, precision=None):
    return jnp.dot(a, b, preferred_element_type=F32, precision=precision)


def _dot_nt(a, b, precision=None):
    return lax.dot_general(a, b, (((1,), (1,)), ((), ())), preferred_element_type=F32, precision=precision)


def _dot_tn(a, b, precision=None):
    return lax.dot_general(a, b, (((0,), (0,)), ((), ())), preferred_element_type=F32, precision=precision)


def _sigmoid(x):
    return 1.0 / (1.0 + jnp.exp(-x))


def _silu(x):
    return x * _sigmoid(x)


def _rmsnorm_body(x_ref, w_ref, o_ref):
    x = x_ref[...]
    y = x * lax.rsqrt(jnp.mean(x * x, axis=-1, keepdims=True) + EPS)
    o_ref[...] = (y * w_ref[...]).astype(o_ref.dtype)


def rmsnorm(x, w, out_dtype, tm):
    M, D = x.shape
    return pl.pallas_call(
        _rmsnorm_body,
        grid=(M // tm,),
        in_specs=[pl.BlockSpec((tm, D), lambda i: (i, 0)), pl.BlockSpec((1, D), lambda i: (0, 0))],
        out_specs=pl.BlockSpec((tm, D), lambda i: (i, 0)),
        out_shape=jax.ShapeDtypeStruct((M, D), out_dtype),
        compiler_params=_cparams("parallel"),
        name="rmsnorm",
    )(x, w.reshape(1, D))


def _mm_body(x_ref, w_ref, o_ref):
    o_ref[...] = _dot(x_ref[...], w_ref[...]).astype(o_ref.dtype)


def matmul(x, w, tm, tn, out_dtype=F32):
    M, K = x.shape
    N = w.shape[1]
    return pl.pallas_call(
        _mm_body,
        grid=(M // tm, N // tn),
        in_specs=[pl.BlockSpec((tm, K), lambda i, j: (i, 0)), pl.BlockSpec((K, tn), lambda i, j: (0, j))],
        out_specs=pl.BlockSpec((tm, tn), lambda i, j: (i, j)),
        out_shape=jax.ShapeDtypeStruct((M, N), out_dtype),
        compiler_params=_cparams("parallel", "arbitrary"),
        name="matmul",
    )(x, w)


def _mm_nt_body(x_ref, w_ref, o_ref):
    o_ref[...] = _dot_nt(x_ref[...], w_ref[...]).astype(o_ref.dtype)


def matmul_nt(x, w_t, tm, tn, rows, out_dtype=F32):
    M, K = x.shape
    r0, N = rows
    assert r0 % (2 * SUBLANES) == 0 and N % tn == 0
    return pl.pallas_call(
        _mm_nt_body,
        grid=(M // tm, N // tn),
        in_specs=[pl.BlockSpec((tm, K), lambda i, j: (i, 0)),
                  pl.BlockSpec((pl.Element(tn), pl.Element(K)), lambda i, j: (pl.multiple_of(r0 + j * tn, 2 * SUBLANES), 0))],
        out_specs=pl.BlockSpec((tm, tn), lambda i, j: (i, j)),
        out_shape=jax.ShapeDtypeStruct((M, N), out_dtype),
        compiler_params=_cparams("parallel", "arbitrary"),
        name="matmul_nt",
    )(x, w_t)


def _mm_res_body(x_ref, w_ref, r_ref, o_ref):
    o_ref[...] = r_ref[...] + _dot(x_ref[...], w_ref[...])


def matmul_residual(x, w, res, tm, tn, x_buffers=2):
    M, K = x.shape
    N = w.shape[1]
    return pl.pallas_call(
        _mm_res_body,
        grid=(M // tm, N // tn),
        in_specs=[pl.BlockSpec((tm, K), lambda i, j: (i, 0), pipeline_mode=pl.Buffered(x_buffers)),
                  pl.BlockSpec((K, tn), lambda i, j: (0, j)),
                  pl.BlockSpec((tm, tn), lambda i, j: (i, j))],
        out_specs=pl.BlockSpec((tm, tn), lambda i, j: (i, j)),
        out_shape=jax.ShapeDtypeStruct((M, N), F32),
        compiler_params=_cparams("parallel", "arbitrary"),
        name="matmul_residual",
    )(x, w, res)


def _mm_mix_body(a_ref, b_ref, wa_ref, wb_ref, ga_ref, gb_ref, o_ref):
    ua = _dot(a_ref[...], wa_ref[...])
    ub = _dot(b_ref[...], wb_ref[...])
    o_ref[...] = (_sigmoid(ga_ref[...]) * ua + _sigmoid(gb_ref[...]) * ub).astype(o_ref.dtype)


def matmul_mix(a, b, wa, wb, gates, tm, tn):
    M, K = a.shape
    N = wa.shape[1]
    nj = N // tn
    return pl.pallas_call(
        _mm_mix_body,
        grid=(M // tm, nj),
        in_specs=[pl.BlockSpec((tm, K), lambda i, j: (i, 0)), pl.BlockSpec((tm, K), lambda i, j: (i, 0)),
                  pl.BlockSpec((K, tn), lambda i, j: (0, j)), pl.BlockSpec((K, tn), lambda i, j: (0, j)),
                  pl.BlockSpec((tm, tn), lambda i, j: (i, j)), pl.BlockSpec((tm, tn), lambda i, j: (i, j + nj))],
        out_specs=pl.BlockSpec((tm, tn), lambda i, j: (i, j)),
        out_shape=jax.ShapeDtypeStruct((M, N), BF16),
        compiler_params=_cparams("parallel", "arbitrary"),
        name="matmul_mix",
    )(a, b, wa, wb, gates, gates)


def _shift_rows(x, hist, sh):
    s = pltpu.roll(x, sh, axis=0)
    head = s[:SUBLANES]
    row8 = lax.broadcasted_iota(jnp.int32, head.shape, 0)
    for r in range(sh):
        head = jnp.where(row8 == r, hist[SUBLANES - sh + r:SUBLANES - sh + r + 1], head)
    return jnp.concatenate([head, s[SUBLANES:]], axis=0)


def _ffn_up_seq_body(x_ref, wg_ref, wv_ref, cg_ref, cv_ref, act_ref, tg_ref, tv_ref, hg_ref, hv_ref, *, tiles_per_seq):
    i = pl.program_id(0)
    j = pl.program_id(1)
    first = (i % tiles_per_seq) == 0
    x = x_ref[...]

    def branch(w_ref, c_ref, hist_ref, tail_ref):
        up = _dot(x, w_ref[...])
        tm = up.shape[0]
        hist = jnp.where(first, 0.0, hist_ref[j])
        c = c_ref[...]
        out = up * c[2:3] + _shift_rows(up, hist, 2) * c[0:1] + _shift_rows(up, hist, 1) * c[1:2]
        tail = up[tm - SUBLANES:tm]
        hist_ref[j] = tail
        tail_ref[0] = tail
        return out

    gate = branch(wg_ref, cg_ref, hg_ref, tg_ref)
    val = branch(wv_ref, cv_ref, hv_ref, tv_ref)
    act_ref[...] = (_silu(gate) * val).astype(act_ref.dtype)


def ffn_up_seq(x, w_up, conv_w, n_seq, tm, tn):
    M, K = x.shape
    F = w_up.shape[1] // 2
    nj = F // tn
    tiles_per_seq = M // n_seq // tm
    body = functools.partial(_ffn_up_seq_body, tiles_per_seq=tiles_per_seq)
    return pl.pallas_call(
        body,
        grid=(M // tm, nj),
        in_specs=[pl.BlockSpec((tm, K), lambda i, j: (i, 0)),
                  pl.BlockSpec((K, tn), lambda i, j: (0, j)), pl.BlockSpec((K, tn), lambda i, j: (0, j + nj)),
                  pl.BlockSpec((FFN_CONV, tn), lambda i, j: (0, j)), pl.BlockSpec((FFN_CONV, tn), lambda i, j: (0, j + nj))],
        out_specs=[pl.BlockSpec((tm, tn), lambda i, j: (i, j)),
                   pl.BlockSpec((1, SUBLANES, tn), lambda i, j: (i, 0, j)),
                   pl.BlockSpec((1, SUBLANES, tn), lambda i, j: (i, 0, j))],
        out_shape=[jax.ShapeDtypeStruct((M, F), BF16),
                   jax.ShapeDtypeStruct((M // tm, SUBLANES, F), F32), jax.ShapeDtypeStruct((M // tm, SUBLANES, F), F32)],
        scratch_shapes=[pltpu.VMEM((nj, SUBLANES, tn), F32), pltpu.VMEM((nj, SUBLANES, tn), F32)],
        compiler_params=_cparams("arbitrary", "arbitrary"),
        name="ffn_up_seq",
    )(x, w_up, w_up, conv_w, conv_w)


def _ffn_up_step_body(x_ref, wg_ref, wv_ref, cg_ref, cv_ref, g0_ref, g1_ref, v0_ref, v1_ref, act_ref, ug_ref, uv_ref):
    x = x_ref[...]

    def branch(w_ref, c_ref, h0_ref, h1_ref, up_ref):
        up = _dot(x, w_ref[...])
        up_ref[...] = up
        c = c_ref[...]
        return up * c[2:3] + h0_ref[...] * c[0:1] + h1_ref[...] * c[1:2]

    gate = branch(wg_ref, cg_ref, g0_ref, g1_ref, ug_ref)
    val = branch(wv_ref, cv_ref, v0_ref, v1_ref, uv_ref)
    act_ref[...] = (_silu(gate) * val).astype(act_ref.dtype)


def ffn_up_step(x, w_up, conv_w, hist, tn):
    M, K = x.shape
    F2 = w_up.shape[1]
    F = F2 // 2
    nj = F // tn
    h2 = hist.reshape(M, 2 * F2)
    return pl.pallas_call(
        _ffn_up_step_body,
        grid=(nj,),
        in_specs=[pl.BlockSpec((M, K), lambda j: (0, 0)),
                  pl.BlockSpec((K, tn), lambda j: (0, j)), pl.BlockSpec((K, tn), lambda j: (0, j + nj)),
                  pl.BlockSpec((FFN_CONV, tn), lambda j: (0, j)), pl.BlockSpec((FFN_CONV, tn), lambda j: (0, j + nj)),
                  pl.BlockSpec((M, tn), lambda j: (0, j)), pl.BlockSpec((M, tn), lambda j: (0, j + 2 * nj)),
                  pl.BlockSpec((M, tn), lambda j: (0, j + nj)), pl.BlockSpec((M, tn), lambda j: (0, j + 3 * nj))],
        out_specs=[pl.BlockSpec((M, tn), lambda j: (0, j))] * 3,
        out_shape=[jax.ShapeDtypeStruct((M, F), BF16), jax.ShapeDtypeStruct((M, F), F32), jax.ShapeDtypeStruct((M, F), F32)],
        compiler_params=_cparams("arbitrary"),
        name="ffn_up_step",
    )(x, w_up, w_up, conv_w, conv_w, h2, h2, h2, h2)


def _rope_body(u_ref, cos_ref, sin_ref, q_ref, kv_ref, win_ref):
    cos = cos_ref[...]
    sin = sin_ref[...]
    scale = HEAD_DIM ** -0.5

    def rot(x):
        return x * cos + pltpu.roll(x, HEAD_DIM // 2, axis=1) * sin

    for h in range(NSA_HEADS):
        sl = slice(h * HEAD_DIM, (h + 1) * HEAD_DIM)
        q_ref[:, sl] = (rot(u_ref[:, sl]) * scale).astype(q_ref.dtype)
    base = NSA_HEADS * HEAD_DIM
    n_glob = 4 * NSA_KV_HEADS
    for slot in range(6 * NSA_KV_HEADS):
        x = u_ref[:, base + slot * HEAD_DIM:base + (slot + 1) * HEAD_DIM]
        if (slot // NSA_KV_HEADS) % 2 == 0:
            x = rot(x)
        if slot < n_glob:
            kv_ref[:, slot * HEAD_DIM:(slot + 1) * HEAD_DIM] = x
        else:
            win_ref[:, (slot - n_glob) * HEAD_DIM:(slot - n_glob + 1) * HEAD_DIM] = x


def rope_split(u, cos, sin, tm):
    M, W = u.shape
    nt = cos.shape[0] // tm
    nq = NSA_HEADS * HEAD_DIM
    ng = 4 * NSA_KV_HEADS * HEAD_DIM
    nw = 2 * NSA_KV_HEADS * HEAD_DIM
    return pl.pallas_call(
        _rope_body,
        grid=(M // tm,),
        in_specs=[pl.BlockSpec((tm, W), lambda i: (i, 0)),
                  pl.BlockSpec((tm, HEAD_DIM), lambda i: (i % nt, 0)), pl.BlockSpec((tm, HEAD_DIM), lambda i: (i % nt, 0))],
        out_specs=[pl.BlockSpec((tm, nq), lambda i: (i, 0)), pl.BlockSpec((tm, ng), lambda i: (i, 0)),
                   pl.BlockSpec((tm, nw), lambda i: (i, 0))],
        out_shape=[jax.ShapeDtypeStruct((M, nq), BF16), jax.ShapeDtypeStruct((M, ng), F32), jax.ShapeDtypeStruct((M, nw), F32)],
        compiler_params=_cparams("parallel"),
        name="rope_split",
    )(u, cos, sin)


GDN_HB = 4
GDN_COL_B = 0
GDN_COL_A = 8
NSA_COL_G = 16
GDN_STEP_GROUPS = 2


def _softplus(x):
    return jnp.maximum(x, 0.0) + jnp.log1p(jnp.exp(-jnp.abs(x)))


def _l2norm(x):
    return x * lax.rsqrt(jnp.sum(x * x, axis=-1, keepdims=True) + EPS)


def _solve_unit_lower(a_list, r_list, order):
    steps = int(math.log2(order))
    a_list, r_list = list(a_list), list(r_list)
    m, n = a_list[0].shape[1], r_list[0].shape[1]
    for i in range(steps):
        last = i == steps - 1
        for c in range(len(a_list)):
            a_b = a_list[c].astype(BF16)
            r_hi, r_lo = _split_bf16(r_list[c])
            prod = _dot(a_b, jnp.concatenate(([] if last else [a_b]) + [r_hi, r_lo], axis=1))
            if not last:
                a_list[c], prod = prod[:, :m], prod[:, m:]
            r_list[c] = r_list[c] + (prod[:, :n] + prod[:, n:])
    return r_list


def _block_diag(blocks):
    z = jnp.zeros_like(blocks[0])
    nb = len(blocks)
    return jnp.concatenate([jnp.concatenate([blocks[h] if j == h else z for j in range(nb)], axis=1)
                            for h in range(nb)], axis=0)


def _gdn_chunks_groups(groups, states):
    nh = len(groups[0][0][0])
    C, d = groups[0][0][2][0].shape
    W = nh * C
    stack = lambda xs: jnp.concatenate(xs, axis=0)
    bd = lambda xs: _block_diag([x.astype(BF16) for x in xs])
    ri = lax.broadcasted_iota(jnp.int32, (W, W), 0)
    ci = lax.broadcasted_iota(jnp.int32, (W, W), 1)
    same = (ri // C) == (ci // C)

    a_list, r_list, pre = [], [], []
    for chunks in groups:
        for q, k, v, beta, gc, gcr in chunks:
            dmat = jnp.where(same & (ri >= ci), jnp.exp(jnp.minimum(stack(gc) - jnp.concatenate(gcr, axis=1), 0.0)), 0.0)
            eg = [jnp.exp(g) for g in gc]
            kb = [k[h] * beta[h] for h in range(nh)]
            xk = _dot_nt(jnp.concatenate([bd(kb), bd(q)], axis=0), bd(k))
            a_list.append(jnp.where(same & (ri > ci), -(xk[:W] * dmat), 0.0))
            r_list.append(jnp.concatenate([stack([v[h] * beta[h] for h in range(nh)]),
                                           stack([kb[h] * eg[h] for h in range(nh)])], axis=1))
            g_last = [g[C - 1:C, :] for g in gc]
            pre.append(((xk[W:] * dmat).astype(BF16), bd([q[h] * eg[h] for h in range(nh)]),
                        bd([k[h] * jnp.exp(g_last[h] - gc[h]) for h in range(nh)]),
                        stack([jnp.broadcast_to(jnp.exp(g_last[h]), (d, 1)) for h in range(nh)])))
    w_list = _solve_unit_lower(a_list, r_list, C)

    n_chunks = len(groups[0])
    states = list(states)
    outs = [[] for _ in groups]
    for c in range(n_chunks):
        for g in range(len(groups)):
            w, (aqk, q_eg, kd, keep) = w_list[g * n_chunks + c], pre[g * n_chunks + c]
            wk = bd([w[h * C:(h + 1) * C, d:] for h in range(nh)])
            xs = _dot(jnp.concatenate([wk, q_eg], axis=0), states[g].astype(BF16))
            v_new = (w[:, :d] - xs[:W]).astype(BF16)
            outs[g].append(xs[W:] + _dot(aqk, v_new))
            states[g] = states[g] * keep + _dot_tn(kd, v_new)
    return outs, states


def _gdn_seq_body(q_ref, k_ref, v_ref, z_ref, ba_ref, cq_ref, ck_ref, cv_ref, alog_ref, dtb_ref, nw_ref,
                  o_ref, s_ref, tq_ref, tk_ref, tv_ref, qs_ref, ks_ref, vs_ref, gs_ref, bs_ref):
    t_idx = pl.program_id(2)
    tt = q_ref.shape[0]
    hb = q_ref.shape[1] // GDN_DK
    C = GDN_CHUNK

    @pl.when(t_idx == 0)
    def _():
        s_ref[...] = jnp.zeros_like(s_ref)
        tq_ref[...] = jnp.zeros_like(tq_ref)
        tk_ref[...] = jnp.zeros_like(tk_ref)
        tv_ref[...] = jnp.zeros_like(tv_ref)

    def conv(x_ref, c_ref, tail_ref):
        x = x_ref[...]
        hist = tail_ref[...]
        row = lax.broadcasted_iota(jnp.int32, x.shape, 0)
        c = c_ref[...]
        out = x * c[GDN_CONV - 1:GDN_CONV]
        for j in range(GDN_CONV - 1):
            out = out + _shift_rows(x, hist, GDN_CONV - 1 - j) * c[j:j + 1]
        tail_ref[...] = x[tt - SUBLANES:tt]
        return _silu(out)

    qa = conv(q_ref, cq_ref, tq_ref)
    ka = conv(k_ref, ck_ref, tk_ref)
    vs_ref[...] = conv(v_ref, cv_ref, tv_ref)
    for h in range(hb):
        sl = slice(h * GDN_DK, (h + 1) * GDN_DK)
        qs_ref[:, sl] = _l2norm(qa[:, sl]) * GDN_DK ** -0.5
        ks_ref[:, sl] = _l2norm(ka[:, sl])
    ba = ba_ref[...]
    bs_ref[...] = _sigmoid(ba)
    n_groups = hb // GDN_HB
    par_row = lambda ref: jnp.concatenate([ref[g, 0:1, :] for g in range(n_groups)], axis=1)
    gs_ref[...] = -jnp.exp(par_row(alog_ref)) * _softplus(ba + par_row(dtb_ref))

    tri = (lax.broadcasted_iota(jnp.int32, (C, C), 0) >= lax.broadcasted_iota(jnp.int32, (C, C), 1)).astype(F32)
    sel = (lax.broadcasted_iota(jnp.int32, (2 * SUBLANES, LANES), 0)
           == lax.broadcasted_iota(jnp.int32, (2 * SUBLANES, LANES), 1)).astype(F32)
    nw = nw_ref[...]

    chunk_rows = [slice(c * C, (c + 1) * C) for c in range(tt // C)]
    gc_alls = [_dot(tri, gs_ref[rows, :], HI) for rows in chunk_rows]
    groups, group_heads = [], []
    for g in range(n_groups):
        heads = [slice((g * GDN_HB + h) * GDN_DK, (g * GDN_HB + h + 1) * GDN_DK) for h in range(GDN_HB)]
        lanes = slice(g * LANES, (g + 1) * LANES)
        chunks = []
        for rows, gc_full in zip(chunk_rows, gc_alls):
            gc_all = gc_full[:, lanes]
            gc_t = _dot_nt(sel, gc_all, HI)
            beta_all = bs_ref[rows, lanes]
            chunks.append((
                [qs_ref[rows, sl] for sl in heads], [ks_ref[rows, sl] for sl in heads], [vs_ref[rows, sl] for sl in heads],
                [beta_all[:, GDN_COL_B + h:GDN_COL_B + h + 1] for h in range(GDN_HB)],
                [gc_all[:, GDN_COL_A + h:GDN_COL_A + h + 1] for h in range(GDN_HB)],
                [gc_t[GDN_COL_A + h:GDN_COL_A + h + 1, :] for h in range(GDN_HB)]))
        groups.append(chunks)
        group_heads.append(heads)
    gdk = GDN_HB * GDN_DK
    outs, states = _gdn_chunks_groups(groups, [s_ref[0, g * gdk:(g + 1) * gdk] for g in range(n_groups)])
    for g in range(n_groups):
        s_ref[0, g * gdk:(g + 1) * gdk] = states[g]
        for rows, o in zip(chunk_rows, outs[g]):
            on = o * lax.rsqrt(jnp.mean(o * o, axis=-1, keepdims=True) + EPS) * nw
            for h, sl in enumerate(group_heads[g]):
                o_ref[rows, sl] = (on[h * C:(h + 1) * C] * _silu(z_ref[rows, sl])).astype(o_ref.dtype)


def gdn_seq(u_qkv, u_z, u_small, conv_w, alog_rows, dtb_rows, norm_w, n_seq, tt):
    M = u_qkv.shape[0]
    T = M // n_seq
    nt = T // tt
    hb = GDN_HB * GDN_STEP_GROUPS
    nhb = GDN_HEADS // hb
    wb = hb * GDN_DK
    gl = GDN_STEP_GROUPS * LANES
    row_blk = lambda off: pl.BlockSpec((tt, wb), lambda b, h, t: (b * nt + t, h + off))
    cw_blk = lambda off: pl.BlockSpec((GDN_CONV, wb), lambda b, h, t: (0, h + off))
    par_blk = pl.BlockSpec((GDN_STEP_GROUPS, SUBLANES, LANES), lambda b, h, t: (h, 0, 0))
    return pl.pallas_call(
        _gdn_seq_body,
        grid=(n_seq, nhb, nt),
        in_specs=[row_blk(0), row_blk(nhb), row_blk(2 * nhb), row_blk(0),
                  pl.BlockSpec((tt, gl), lambda b, h, t: (b * nt + t, h)),
                  cw_blk(0), cw_blk(nhb), cw_blk(2 * nhb), par_blk, par_blk,
                  pl.BlockSpec((1, GDN_DV), lambda b, h, t: (0, 0))],
        out_specs=[row_blk(0), pl.BlockSpec((1, hb * GDN_DK, GDN_DV), lambda b, h, t: (b, h, 0))],
        out_shape=[jax.ShapeDtypeStruct((M, GDN_HEADS * GDN_DV), BF16),
                   jax.ShapeDtypeStruct((n_seq, GDN_HEADS * GDN_DK, GDN_DV), F32)],
        scratch_shapes=[pltpu.VMEM((SUBLANES, wb), F32)] * 3 + [pltpu.VMEM((tt, wb), F32)] * 3
                       + [pltpu.VMEM((tt, gl), F32)] * 2,
        compiler_params=_cparams("parallel", "parallel", "arbitrary"),
        name="gdn_seq",
    )(u_qkv, u_qkv, u_qkv, u_z, u_small, conv_w, conv_w, conv_w, alog_rows, dtb_rows, norm_w.reshape(1, GDN_DV))


def gdn_param_rows(gdn_a_log, gdn_dt_bias):
    nhb = GDN_HEADS // GDN_HB

    def rows(p):
        r = _place_lanes(((GDN_COL_A, p.reshape(nhb, GDN_HB)),), (nhb,))
        return jnp.broadcast_to(r[:, None, :], (nhb, SUBLANES, LANES))

    return rows(gdn_a_log), rows(gdn_dt_bias)


CMP_RATIO = CMP_LEN // CMP_STRIDE
CMP_FEAT = CMP_STRIDE * HEAD_DIM
MASKED = -1e30


def _compress_mlp(part, pe_ref, w1_ref, w2_ref, chunk_step=1):
    n = part.shape[0]
    pe_part = _dot(pe_ref[...], w1_ref[...])
    hid0 = pe_part[0:1, :CMP_HIDDEN] + pe_part[1:2, CMP_HIDDEN:]
    hid = hid0 + part[:, :CMP_HIDDEN]
    hid = hid + pltpu.roll(part[:, CMP_HIDDEN:], n - chunk_step, axis=0)
    return _dot(_silu(hid).astype(BF16), w2_ref[...])


def _compress_seq_body(x_ref, pe_ref, w1_ref, w2_ref, o_ref, xc_ref):
    n = o_ref.shape[3]
    for s in range(CMP_STRIDE):
        xc_ref[:, s * HEAD_DIM:(s + 1) * HEAD_DIM] = x_ref[pl.ds(s, n, stride=CMP_STRIDE), :].astype(BF16)
    part = _dot(xc_ref[...], w1_ref[0])
    o_ref[0, 0, 0] = _compress_mlp(part, pe_ref.at[0], w1_ref.at[0], w2_ref.at[0])


def compress_seq(kv_rows, pe2, w1c, w2c, n_seq):
    M = kv_rows.shape[0]
    T = M // n_seq
    n = T // CMP_STRIDE
    G = NSA_KV_HEADS
    return pl.pallas_call(
        _compress_seq_body,
        grid=(n_seq, 2, G),
        in_specs=[pl.BlockSpec((T, HEAD_DIM), lambda b, c, g: (b, c * G + g)),
                  pl.BlockSpec((1, SUBLANES, CMP_FEAT), lambda b, c, g: (c, 0, 0)),
                  pl.BlockSpec((1, CMP_FEAT, 2 * CMP_HIDDEN), lambda b, c, g: (c, 0, 0)),
                  pl.BlockSpec((1, CMP_HIDDEN, HEAD_DIM), lambda b, c, g: (c, 0, 0))],
        out_specs=pl.BlockSpec((1, 1, 1, n, HEAD_DIM), lambda b, c, g: (b, c, g, 0, 0)),
        out_shape=jax.ShapeDtypeStruct((n_seq, 2, G, n, HEAD_DIM), F32),
        scratch_shapes=[pltpu.VMEM((n, CMP_FEAT), BF16)],
        compiler_params=_cparams("parallel", "parallel", "parallel"),
        name="compress_seq",
    )(kv_rows, pe2, w1c, w2c)


def compress_params(cmp_pe, cmp_w1, cmp_w2):
    w1 = cmp_w1.reshape(2, CMP_RATIO, CMP_FEAT, CMP_HIDDEN)
    w1c = jnp.concatenate([w1[:, r] for r in range(CMP_RATIO)], axis=-1).astype(BF16)
    pe = cmp_pe.reshape(2, CMP_RATIO, CMP_FEAT)
    pe2 = jnp.zeros((2, SUBLANES, CMP_FEAT), F32).at[:, :CMP_RATIO].set(pe).astype(BF16)
    return pe2, w1c, cmp_w2.astype(BF16)


def _masked_softmax(s, mask, axis=-1):
    s = jnp.where(mask, s, -jnp.inf)
    m = jnp.max(s, axis=axis, keepdims=True)
    m = jnp.where(m > -jnp.inf, m, 0.0)
    e = jnp.exp(s - m)
    return e * (1.0 / jnp.maximum(jnp.sum(e, axis=axis, keepdims=True), 1e-30))


def _masked_softmax_heads(s, mask, n_heads):
    t = mask.shape[0]
    return jnp.concatenate([_masked_softmax(s[h * t:(h + 1) * t], mask) for h in range(n_heads)], axis=0)


def _split_bf16(x):
    hi = x.astype(BF16)
    return hi, (x - hi.astype(F32)).astype(BF16)


def _select_blocks_t(imp_t, pos_row, n_blocks):
    J = imp_t.shape[0]
    j = lax.broadcasted_iota(jnp.int32, imp_t.shape, 0)
    cur = pos_row // SLC_LEN
    forced = (j == 0) | ((j <= cur) & (j > cur - SLC_LOCAL))
    score = jnp.where(j * SLC_LEN > pos_row, -jnp.inf, jnp.where(forced, jnp.inf, imp_t))
    score = jnp.where(j < n_blocks, score, -jnp.inf)
    tiles = [score[r:r + SUBLANES] for r in range(0, J, SUBLANES)]
    ranks = [jnp.zeros(t.shape, F32) for t in tiles]
    for jp in range(n_blocks):
        row = score[jp:jp + 1, :]
        for i, t in enumerate(tiles):
            lo_j = i * SUBLANES
            ge, gt = jnp.where(row >= t, 1.0, 0.0), jnp.where(row > t, 1.0, 0.0)
            if lo_j > jp:
                ahead = ge
            elif lo_j + SUBLANES - 1 <= jp:
                ahead = gt
            else:
                ahead = jnp.where(lax.broadcasted_iota(jnp.int32, t.shape, 0) + lo_j > jp, ge, gt)
            ranks[i] = ranks[i] + ahead
    rank = jnp.concatenate(ranks, axis=0)
    keep = (rank < float(min(SLC_TOP_N, n_blocks))) & (j < n_blocks)
    return jnp.where(keep, 1.0, 0.0)


def _nsa_seq_body(q_ref, kc_ref, vc_ref, ks_ref, vs_ref, kw_ref, vw_ref, gl_ref, ovl_ref, o_ref, m_ref, l_ref, acc_ref, *, kblk):
    i = pl.program_id(2)
    tq = q_ref.shape[0]
    T = ks_ref.shape[0]
    H = NSA_GROUP
    D = HEAD_DIM
    n_slc = T // SLC_LEN
    qb = q_ref[...]
    q4 = jnp.concatenate([qb[:, h * D:(h + 1) * D] for h in range(H)], axis=0)
    t0 = i * tq
    pos1 = t0 + lax.broadcasted_iota(jnp.int32, (tq, 1), 0)

    kc = kc_ref[0, 0, 0].astype(BF16)
    vc = vc_ref[0, 0, 0].astype(BF16)
    n_c = kc.shape[0]
    s = _dot_nt(q4, kc)
    cmp_end = lax.broadcasted_iota(jnp.int32, (1, n_c), 1) * CMP_STRIDE + (CMP_LEN - 1)
    p = _masked_softmax_heads(s, cmp_end <= pos1, H)
    o_c = _dot(p.astype(BF16), vc)

    psum = p[0:tq]
    for h in range(1, H):
        psum = psum + p[h * tq:(h + 1) * tq]
    p_hi, p_lo = _split_bf16(psum)
    ovl = ovl_ref[...]
    imp_t = _dot_nt(ovl, p_hi) + _dot_nt(ovl, p_lo)
    pos_row = t0 + lax.broadcasted_iota(jnp.int32, (1, tq), 1)
    sel_t = _select_blocks_t(imp_t, pos_row, n_slc)
    eye = (lax.broadcasted_iota(jnp.int32, (tq, tq), 0) == lax.broadcasted_iota(jnp.int32, (tq, tq), 1))
    sel = _dot_nt(jnp.where(eye, 1.0, 0.0).astype(BF16), sel_t.astype(BF16)).astype(BF16)

    n_kb = (t0 + tq + kblk - 1) // kblk

    m_ref[...] = jnp.full(m_ref.shape, MASKED, F32)
    l_ref[...] = jnp.zeros(l_ref.shape, F32)
    acc_ref[...] = jnp.zeros(acc_ref.shape, F32)

    def kv_step(kb, carry):
        k0 = pl.multiple_of(kb * kblk, kblk)
        kk = ks_ref[pl.ds(k0, kblk), :].astype(BF16)
        vv = vs_ref[pl.ds(k0, kblk), :].astype(BF16)
        s2 = _dot_nt(q4, kk)
        kpos = k0 + lax.broadcasted_iota(jnp.int32, (1, kblk), 1)
        expand = (lax.broadcasted_iota(jnp.int32, (n_slc, kblk), 0) == kpos // SLC_LEN)
        chosen = _dot(sel, jnp.where(expand, 1.0, 0.0).astype(BF16))
        bias = jnp.where((chosen > 0.5) & (kpos <= pos1), 0.0, MASKED)
        es = []
        for h in range(H):
            rows = slice(h * tq, (h + 1) * tq)
            sh = s2[rows] + bias
            m_old = m_ref[rows]
            m_new = jnp.maximum(m_old, jnp.max(sh, axis=-1, keepdims=True))
            alpha = jnp.exp(m_old - m_new)
            e = jnp.exp(sh - m_new)
            m_ref[rows] = m_new
            l_ref[rows] = alpha * l_ref[rows] + jnp.sum(e, axis=-1, keepdims=True)
            acc_ref[rows] = alpha * acc_ref[rows]
            es.append(e.astype(BF16))
        acc_ref[...] += _dot(jnp.concatenate(es, axis=0), vv)
        return carry

    lax.fori_loop(0, n_kb, kv_step, 0)
    o_s = acc_ref[...] * (1.0 / jnp.maximum(l_ref[...], 1e-30))

    band = WINDOW + tq
    w0 = pl.multiple_of(jnp.maximum(t0 - WINDOW, 0), tq)
    kw = kw_ref[pl.ds(w0, band), :].astype(BF16)
    vw = vw_ref[pl.ds(w0, band), :].astype(BF16)
    sw = _dot_nt(q4, kw)
    diff = pos1 - (w0 + lax.broadcasted_iota(jnp.int32, (1, band), 1))
    pw = _masked_softmax_heads(sw, (diff >= 0) & (diff < WINDOW), H)
    o_w = _dot(pw.astype(BF16), vw)

    gates = _sigmoid(gl_ref[...])
    for h in range(H):
        rows = slice(h * tq, (h + 1) * tq)
        g = lambda br: gates[:, NSA_COL_G + H * br + h:NSA_COL_G + H * br + h + 1]
        o_ref[:, h * D:(h + 1) * D] = (g(0) * o_c[rows] + g(1) * o_s[rows] + g(2) * o_w[rows]).astype(o_ref.dtype)


def _overlap_t(n_cmp_rows, n_slc):
    start = np.arange(n_cmp_rows)[None, :] * CMP_STRIDE
    blk = np.arange(n_slc)[:, None] * SLC_LEN
    return ((start < blk + SLC_LEN) & (start + CMP_LEN > blk)).astype(np.float32)


def nsa_seq(q, kv_rows, win_rows, cmp_kv, u_small, n_seq, tq=256, kblk=512):
    M = q.shape[0]
    T = M // n_seq
    nt = T // tq
    G = NSA_KV_HEADS
    gw = NSA_GROUP * HEAD_DIM
    n_c = cmp_kv.shape[3]
    ovl = jnp.asarray(_overlap_t(n_c, T // SLC_LEN), BF16)
    body = functools.partial(_nsa_seq_body, kblk=kblk)
    seq_blk = lambda off: pl.BlockSpec((T, HEAD_DIM), lambda b, g, i: (b, g + off))
    cmp_blk = lambda c: pl.BlockSpec((1, 1, 1, n_c, HEAD_DIM), lambda b, g, i: (b, c, g, 0, 0))
    return pl.pallas_call(
        body,
        grid=(n_seq, G, nt),
        in_specs=[pl.BlockSpec((tq, gw), lambda b, g, i: (b * nt + i, g)),
                  cmp_blk(0), cmp_blk(1), seq_blk(2 * G), seq_blk(3 * G), seq_blk(0), seq_blk(G),
                  pl.BlockSpec((tq, LANES), lambda b, g, i: (b * nt + i, g)),
                  pl.BlockSpec(ovl.shape, lambda b, g, i: (0, 0))],
        out_specs=pl.BlockSpec((tq, gw), lambda b, g, i: (b * nt + i, g)),
        out_shape=jax.ShapeDtypeStruct((M, NSA_HEADS * HEAD_DIM), BF16),
        scratch_shapes=[pltpu.VMEM((NSA_GROUP * tq, 1), F32)] * 2 + [pltpu.VMEM((NSA_GROUP * tq, HEAD_DIM), F32)],
        compiler_params=_cparams("parallel", "parallel", "arbitrary"),
        name="nsa_seq",
    )(q, cmp_kv, cmp_kv, kv_rows, kv_rows, win_rows, win_rows, u_small, ovl)


STEP_COL_G = 0
STEP_COL_B = 3 * NSA_HEADS
STEP_COL_A = STEP_COL_B + GDN_HEADS


def _gdn_step_prep_body(x_ref, h_ref, c_ref, us_ref, alog_ref, dtb_ref, q_ref, k_ref, v_ref, b_ref, e_ref):
    W = x_ref.shape[1]
    c = c_ref[...]
    x = x_ref[...] * c[GDN_CONV - 1:GDN_CONV]
    for j in range(GDN_CONV - 1):
        x = x + h_ref[:, j * W:(j + 1) * W] * c[j:j + 1]
    x = _silu(x)
    us = us_ref[...]
    beta = _sigmoid(us)
    eg = jnp.exp(-jnp.exp(alog_ref[...]) * _softplus(us + dtb_ref[...]))
    nq = GDN_HEADS * GDN_DK
    for h in range(GDN_HEADS):
        sl = slice(h * GDN_DK, (h + 1) * GDN_DK)
        q_ref[:, sl] = _l2norm(x[:, sl]) * GDN_DK ** -0.5
        k_ref[:, sl] = _l2norm(x[:, nq + h * GDN_DK:nq + (h + 1) * GDN_DK])
        b_ref[:, sl] = jnp.broadcast_to(beta[:, STEP_COL_B + h:STEP_COL_B + h + 1], (x.shape[0], GDN_DK))
        e_ref[:, sl] = jnp.broadcast_to(eg[:, STEP_COL_A + h:STEP_COL_A + h + 1], (x.shape[0], GDN_DK))
    v_ref[...] = x[:, 2 * nq:]


def _gdn_step_body(q_ref, k_ref, v_ref, b_ref, e_ref, z_ref, nw_ref, s_ref, o_ref, so_ref):
    bb = q_ref.shape[0]
    eye = jnp.where(lax.broadcasted_iota(jnp.int32, (GDN_DK, GDN_DK), 0)
                    == lax.broadcasted_iota(jnp.int32, (GDN_DK, GDN_DK), 1), 1.0, 0.0)
    nw = nw_ref[...]

    def per_seq(bi, carry):
        q, k, v, beta, eg, z = q_ref[bi], k_ref[bi], v_ref[bi], b_ref[bi], e_ref[bi], z_ref[bi]
        k_t = _dot_nt(eye, k, HI)
        q_t = _dot_nt(eye, q, HI)
        outs = []
        for h in range(GDN_HEADS):
            r = slice(h, h + 1)
            s = s_ref[bi, h]
            kcol = k_t[:, h:h + 1]
            k_s = jnp.sum(kcol * s, axis=0, keepdims=True)
            q_s = jnp.sum(q_t[:, h:h + 1] * s, axis=0, keepdims=True)
            v_new = v[r] * beta[r] - (beta[r] * eg[r]) * k_s
            qk = jnp.sum(q[r] * k[r], axis=-1, keepdims=True)
            o = eg[r] * q_s + qk * v_new
            so_ref[bi, h] = s * eg[r] + kcol * v_new
            on = o * lax.rsqrt(jnp.mean(o * o, axis=-1, keepdims=True) + EPS) * nw
            outs.append(on * _silu(z[r]))
        o_ref[bi] = jnp.concatenate(outs, axis=0).astype(o_ref.dtype)
        return carry

    lax.fori_loop(0, bb, per_seq, 0)


def gdn_step(u_qkv, u_z, u_small, hist, state, conv_w, a_log, dt_bias, norm_w, bb=4):
    Bd, W = u_qkv.shape
    H = GDN_HEADS
    nq = H * GDN_DK
    row = lambda p: _place_lanes(((STEP_COL_A, p.reshape(1, H)),), (1,))
    full = lambda shape: pl.BlockSpec(shape, lambda i: (0,) * len(shape))
    outs = pl.pallas_call(
        _gdn_step_prep_body,
        grid=(1,),
        in_specs=[full((Bd, W)), full((Bd, (GDN_CONV - 1) * W)), full((GDN_CONV, W)), full((Bd, LANES)),
                  full((1, LANES)), full((1, LANES))],
        out_specs=[full((Bd, nq))] * 5,
        out_shape=[jax.ShapeDtypeStruct((Bd, nq), F32)] * 5,
        compiler_params=_cparams("arbitrary"),
        name="gdn_step_prep",
    )(u_qkv, hist.reshape(Bd, (GDN_CONV - 1) * W), conv_w, u_small, row(a_log), row(dt_bias))
    heads = lambda a: a.reshape(Bd, H, GDN_DK)
    vec_blk = pl.BlockSpec((bb, H, GDN_DK), lambda i: (i, 0, 0))
    st_blk = pl.BlockSpec((bb, H, GDN_DK, GDN_DV), lambda i: (i, 0, 0, 0))
    o, s_new = pl.pallas_call(
        _gdn_step_body,
        grid=(Bd // bb,),
        in_specs=[vec_blk] * 6 + [pl.BlockSpec((1, GDN_DV), lambda i: (0, 0)), st_blk],
        out_specs=[vec_blk, st_blk],
        out_shape=[jax.ShapeDtypeStruct((Bd, H, GDN_DV), BF16), jax.ShapeDtypeStruct(state.shape, F32)],
        compiler_params=_cparams("parallel"),
        name="gdn_step",
    )(*[heads(a) for a in outs], heads(u_z), norm_w.reshape(1, GDN_DV), state)
    return o.reshape(Bd, H * GDN_DV), s_new


def _softmax_with_new(s_past, valid, s_new):
    s_past = jnp.where(valid, s_past, -jnp.inf)
    m = jnp.maximum(jnp.max(s_past, axis=-1, keepdims=True), s_new)
    e = jnp.exp(s_past - m)
    e_new = jnp.exp(s_new - m)
    return e, e_new, 1.0 / (jnp.sum(e, axis=-1, keepdims=True) + e_new)


def _nsa_step_body(pt_ref, *refs, n_pages, page, n_past):
    lo = refs[:n_pages]
    hi = refs[n_pages:2 * n_pages]
    (q_ref, new_ref, win_ref, wnew_ref, gl_ref, pe_ref, w1_ref, w2_ref, ovl_ref, o_ref, wout_ref, xk_ref, xv_ref) = refs[2 * n_pages:]
    G, H, D = NSA_KV_HEADS, NSA_HEADS, HEAD_DIM
    HS = 2 * G
    pos = n_past
    cpp = page // CMP_STRIDE
    n_chunk = n_pages * cpp
    L = n_pages * page
    q = q_ref[0]
    qf = q.astype(F32)
    new = new_ref[0]
    row_g = lax.broadcasted_iota(jnp.int32, (H, 1), 0) // NSA_GROUP

    def per_head_rows(rows):
        return jnp.concatenate([jnp.broadcast_to(rows[g:g + 1], (NSA_GROUP, rows.shape[1])) for g in range(G)], axis=0)

    first_half = lax.broadcasted_iota(jnp.int32, (SUBLANES, D), 0) < G
    for j in range(n_pages):
        for s in range(CMP_STRIDE):
            for m in range(cpp // 4):
                ks, vs = [], []
                for pair in range(2):
                    ta = lo[j][0, (4 * m + 2 * pair) * CMP_STRIDE + s]
                    tb = lo[j][0, (4 * m + 2 * pair + 1) * CMP_STRIDE + s]
                    ks.append(jnp.where(first_half, ta, pltpu.roll(tb, G, axis=0)))
                    vs.append(jnp.where(first_half, pltpu.roll(ta, G, axis=0), tb))
                r0 = (j * cpp + 4 * m) * G
                xk_ref[r0:r0 + 4 * G, s * D:(s + 1) * D] = jnp.concatenate(ks, axis=0).astype(BF16)
                xv_ref[r0:r0 + 4 * G, s * D:(s + 1) * D] = jnp.concatenate(vs, axis=0).astype(BF16)
    ckv = []
    for c, x_ref in enumerate((xk_ref, xv_ref)):
        part = _dot(x_ref[...], w1_ref[c])
        ckv.append(_compress_mlp(part, pe_ref.at[c], w1_ref.at[c], w2_ref.at[c], G).astype(BF16))

    n_all = G * n_chunk
    s = _dot_nt(q, ckv[0])
    col = lax.broadcasted_iota(jnp.int32, (1, n_all), 1)
    blk = col // G
    ok = (col % G == row_g) & (blk * CMP_STRIDE + (CMP_LEN - 1) <= pos) & (blk < n_chunk - CMP_RATIO + 1)
    p = _masked_softmax(s, ok)
    o_c = _dot(p.astype(BF16), ckv[1])

    gsum = jnp.where(lax.broadcasted_iota(jnp.int32, (LANES, H), 0) == lax.broadcasted_iota(jnp.int32, (LANES, H), 1) // NSA_GROUP, 1.0, 0.0)
    psum = _dot(gsum, p, HI)
    p_hi, p_lo = _split_bf16(psum)
    ovl = ovl_ref[...]
    imp_t = _dot_nt(ovl, p_hi) + _dot_nt(ovl, p_lo)
    n_slc = -(-(L + 1) // SLC_LEN)
    sel_t = _select_blocks_t(imp_t, jnp.full((1, LANES), pos, jnp.int32), n_slc)
    J = sel_t.shape[0]
    eye8 = jnp.where(lax.broadcasted_iota(jnp.int32, (SUBLANES, LANES), 0) == lax.broadcasted_iota(jnp.int32, (SUBLANES, LANES), 1), 1.0, 0.0)
    sel = _dot_nt(eye8.astype(BF16), sel_t.astype(BF16))

    def slot_softmax(sc, tok_ok, s_new):
        slot = lax.broadcasted_iota(jnp.int32, (1, sc.shape[1]), 1) % HS
        return _softmax_with_new(sc, (slot == row_g) & tok_ok, s_new)

    tok = lax.broadcasted_iota(jnp.int32, (1, L * HS), 1) // HS
    chosen = per_head_rows(jnp.concatenate(
        [jnp.broadcast_to(sel[:, b:b + 1], (SUBLANES, SLC_LEN * HS)) for b in range(L // SLC_LEN)], axis=1))
    kv_hi = [hi[j][0].reshape(page * HS, D).astype(BF16) for j in range(n_pages)]
    sc = jnp.concatenate([_dot_nt(q, kv) for kv in kv_hi], axis=1)
    s_new = jnp.sum(qf * per_head_rows(new[2 * G:3 * G]), axis=-1, keepdims=True)
    e, e_new, inv = slot_softmax(sc, (chosen > 0.5) & (tok <= pos), s_new)
    e = pltpu.roll(e, G, axis=1).astype(BF16)
    acc = jnp.zeros((H, D), F32)
    for j, kv in enumerate(kv_hi):
        acc = acc + _dot(e[:, j * page * HS:(j + 1) * page * HS], kv)
    o_s = (acc + e_new * per_head_rows(new[3 * G:4 * G])) * inv

    n_win = win_ref.shape[1] // HS
    wnew = wnew_ref[0]
    win = win_ref[0].astype(BF16)
    wtok = pos - n_win + lax.broadcasted_iota(jnp.int32, (1, n_win * HS), 1) // HS
    sw_new = jnp.sum(qf * per_head_rows(wnew[0:G]), axis=-1, keepdims=True)
    ew, ew_new, winv = slot_softmax(_dot_nt(q, win), (pos - wtok < WINDOW) & (wtok >= 0), sw_new)
    o_w = (_dot(pltpu.roll(ew, G, axis=1).astype(BF16), win) + ew_new * per_head_rows(wnew[G:2 * G])) * winv

    gl = jnp.broadcast_to(_sigmoid(gl_ref[0]), (SUBLANES, LANES))
    hh = lax.broadcasted_iota(jnp.int32, (H, LANES), 0)
    cc = lax.broadcasted_iota(jnp.int32, (H, LANES), 1)
    gate = lambda br: _dot_nt(jnp.where(cc == STEP_COL_G + br * H + hh, 1.0, 0.0), gl, HI)[:, 0:1]
    o_ref[0] = (gate(0) * o_c + gate(1) * o_s + gate(2) * o_w).astype(o_ref.dtype)

    wout_ref[0, :(n_win - 1) * HS] = win_ref[0, HS:]
    wout_ref[0, (n_win - 1) * HS:] = wnew


def nsa_step(q, new_rows, win_new, cache, page_table, win_state, u_small, pe2, w1c, w2c):
    Bd = q.shape[0]
    n_pool, page = cache.shape[:2]
    n_pages = page_table.shape[1]
    G, H, D = NSA_KV_HEADS, NSA_HEADS, HEAD_DIM
    n_past = n_pages * page
    n_chunk = n_past // CMP_STRIDE
    n_win = win_state.shape[1]
    assert n_past % SLC_LEN == 0 and n_win == WINDOW <= n_past and page % (4 * CMP_STRIDE) == 0 and 2 * G == SUBLANES
    n_slc = -(-(n_past + 1) // SLC_LEN)
    J = -(-n_slc // (2 * SUBLANES)) * (2 * SUBLANES)
    ovl = np.zeros((J, n_chunk * G), np.float32)
    ovl[:n_slc] = np.repeat(_overlap_t(n_chunk, n_slc), G, axis=1)
    ovl = jnp.asarray(ovl, BF16)
    body = functools.partial(_nsa_step_body, n_pages=n_pages, page=page, n_past=n_past)
    half_spec = lambda j, half: pl.BlockSpec((1, page, None, 2 * G, D), lambda b, pt: (pt[b, j], 0, half, 0, 0))
    const = lambda a: pl.BlockSpec(a.shape, lambda b, pt: (0,) * a.ndim, pipeline_mode=pl.Buffered(1))
    per_b = lambda shape: pl.BlockSpec((1,) + shape, lambda b, pt: (b,) + (0,) * len(shape))
    cache5 = cache.reshape(n_pool, page, 2, 2 * G, D)
    out, win_out = pl.pallas_call(
        body,
        grid_spec=pltpu.PrefetchScalarGridSpec(
            num_scalar_prefetch=1,
            grid=(Bd,),
            in_specs=[half_spec(j, 0) for j in range(n_pages)] + [half_spec(j, 1) for j in range(n_pages)]
                     + [per_b((H, D)), per_b((4 * G, D)), per_b((n_win * 2 * G, D)), per_b((2 * G, D)), per_b((1, LANES)),
                        const(pe2), const(w1c), const(w2c), const(ovl)],
            out_specs=[per_b((H, D)), per_b((n_win * 2 * G, D))],
            scratch_shapes=[pltpu.VMEM((G * n_chunk, CMP_FEAT), BF16)] * 2,
        ),
        out_shape=[jax.ShapeDtypeStruct((Bd, H, D), BF16), jax.ShapeDtypeStruct((Bd, n_win * 2 * G, D), F32)],
        compiler_params=_cparams("arbitrary"),
        name="nsa_step",
    )(page_table, *([cache5] * (2 * n_pages)), q.reshape(Bd, H, D), new_rows.reshape(Bd, 4 * G, D),
      win_state.reshape(Bd, n_win * 2 * G, D), win_new.reshape(Bd, 2 * G, D), u_small.reshape(Bd, 1, LANES),
      pe2, w1c, w2c, ovl)
    return out.reshape(Bd, H * D), win_out.reshape(win_state.shape)


def rope_tables(pos):
    half = HEAD_DIM // 2
    inv = ROPE_THETA ** (-jnp.arange(half, dtype=F32) / half)
    ang = pos.astype(F32)[:, None] * inv[None, :]
    cos, sin = jnp.cos(ang), jnp.sin(ang)
    return jnp.concatenate([cos, cos], axis=-1), jnp.concatenate([-sin, sin], axis=-1)


def _arrange_w_in(w_in):
    H, G = NSA_HEADS, NSA_KV_HEADS
    sizes = (H * HEAD_DIM, 6 * G * HEAD_DIM, 3 * H, GDN_HEADS * (2 * GDN_DK + GDN_DV), GDN_HEADS * GDN_DV, 2 * GDN_HEADS)
    o = np.cumsum((0,) + sizes).tolist()
    w_t = w_in.T
    K = w_in.shape[0]
    wide = ((o[0], o[2] - o[0]), (o[3], o[4] - o[3]), (o[4], o[5] - o[4]), (o[6], w_t.shape[0] - o[6]))
    w_ng = w_t[o[2]:o[3]].T
    w_ba = w_t[o[5]:o[6]].T
    nhb = GDN_HEADS // GDN_HB
    ng = w_ng.reshape(K, 3, G, NSA_GROUP).transpose(0, 2, 1, 3).reshape(K, G, 3 * NSA_GROUP)
    pieces = ((GDN_COL_B, w_ba[:, :GDN_HEADS].reshape(K, nhb, GDN_HB)),
              (GDN_COL_A, w_ba[:, GDN_HEADS:].reshape(K, nhb, GDN_HB)), (NSA_COL_G, ng))
    w_small_seq = _place_lanes(pieces, (K, nhb)).reshape(K, nhb * LANES).astype(BF16)
    w_small_step = _place_lanes(((STEP_COL_G, w_ng), (STEP_COL_B, w_ba)), (K,)).astype(BF16)
    return w_t.astype(BF16), wide, w_small_seq, w_small_step


def _place_lanes(pieces, lead):
    out, at = [], 0
    for lane, vals in pieces:
        out += [jnp.zeros(lead + (lane - at,), F32), vals.astype(F32)]
        at = lane + vals.shape[-1]
    return jnp.concatenate(out + [jnp.zeros(lead + (LANES - at,), F32)], axis=-1)


class _Tiles(NamedTuple):
    rows: int
    cols: int
    ffn_cols: int
    norm_rows: int


def _tiles(n_rows):
    if n_rows >= 1024:
        return _Tiles(rows=1024, cols=512, ffn_cols=256, norm_rows=256)
    return _Tiles(rows=n_rows, cols=1024, ffn_cols=256, norm_rows=n_rows)


GDN_SEQ_ROWS = 256


def kernel(x_prompt, x_sample, cache_nsa_kv, page_table, state_win_kv, state_gdn, state_gdn_conv, state_ffn_conv, norm_mix, w_in, cmp_pe, cmp_w1, cmp_w2, gdn_conv_w, gdn_a_log, gdn_dt_bias, gdn_norm, w_nsa_out, w_gdn_out, w_o, norm_ffn, w_up, ffn_conv_w, w_down, norm_final):
    B, T, D = x_prompt.shape
    Bd = x_sample.shape[0]
    G = NSA_KV_HEADS
    n_past = page_table.shape[1] * cache_nsa_kv.shape[1]

    w_t, wide, w_small_seq, w_small_step = _arrange_w_in(w_in)
    w_nsa_b, w_gdn_b, w_o_b = w_nsa_out.astype(BF16), w_gdn_out.astype(BF16), w_o.astype(BF16)
    w_up_b, w_down_b = w_up.astype(BF16), w_down.astype(BF16)
    pe2, w1c, w2c = compress_params(cmp_pe, cmp_w1, cmp_w2)

    def project(x2d, t):
        xn = rmsnorm(x2d, norm_mix, BF16, t.norm_rows)
        return xn, [matmul_nt(xn, w_t, t.rows, t.cols, r) for r in wide]

    def tail(x, mixed_nsa, mixed_gdn, u_mg, ffn_up, t):
        mixin = matmul_mix(mixed_nsa, mixed_gdn, w_nsa_b, w_gdn_b, u_mg, t.rows, t.cols)
        h = matmul_residual(mixin, w_o_b, x, t.rows, t.cols)
        hn = rmsnorm(h, norm_ffn, BF16, t.norm_rows)
        act, extra = ffn_up(hn)
        h2 = matmul_residual(act, w_down_b, h, t.rows, t.ffn_cols, x_buffers=1)
        return rmsnorm(h2, norm_final, F32, t.norm_rows), extra

    t = _tiles(B * T)
    xp = x_prompt.reshape(B * T, D)
    xn, (u_qkv, u_gqkv, u_z, u_mg) = project(xp, t)
    u_small = matmul(xn, w_small_seq, t.rows, w_small_seq.shape[1])
    cos, sin = rope_tables(jnp.arange(T, dtype=jnp.int32))
    q, kv_rows, win_rows = rope_split(u_qkv, cos, sin, t.norm_rows)
    cmp_kv = compress_seq(kv_rows, pe2, w1c, w2c, B)
    o_nsa = nsa_seq(q, kv_rows, win_rows, cmp_kv, u_small, B)
    a_rows, d_rows = gdn_param_rows(gdn_a_log, gdn_dt_bias)
    o_gdn, s_p = gdn_seq(u_gqkv, u_z, u_small, gdn_conv_w, a_rows, d_rows, gdn_norm, B, GDN_SEQ_ROWS)
    s_p = s_p.reshape(B, GDN_HEADS, GDN_DK, GDN_DV)

    def ffn_up_p(hn):
        act, tg, tv = ffn_up_seq(hn, w_up_b, ffn_conv_w, B, t.rows, t.ffn_cols)
        last = lambda a: a[T // t.rows - 1::T // t.rows, SUBLANES - (FFN_CONV - 1):]
        return act, jnp.concatenate([last(tg), last(tv)], axis=-1)

    y_p, ffn_p = tail(xp, o_nsa, o_gdn, u_mg, ffn_up_p, t)
    n_win_p = min(WINDOW, T)
    kv_p = kv_rows.reshape(B, T, 4, G, HEAD_DIM)
    win_p = win_rows.reshape(B, T, 2, G, HEAD_DIM)[:, T - n_win_p:]
    conv_p = u_gqkv.reshape(B, T, -1)[:, T - (GDN_CONV - 1):]

    t = _tiles(Bd)
    xs = x_sample.reshape(Bd, D)
    xn, (u_qkv, u_gqkv, u_z, u_mg) = project(xs, t)
    u_small = matmul(xn, w_small_step, t.rows, LANES)
    cos, sin = rope_tables(jnp.full((Bd,), n_past, jnp.int32))
    q, new_rows, win_new = rope_split(u_qkv, cos, sin, t.norm_rows)
    o_nsa, win_s = nsa_step(q, new_rows, win_new, cache_nsa_kv, page_table, state_win_kv, u_small, pe2, w1c, w2c)
    o_gdn, s_s = gdn_step(u_gqkv, u_z, u_small, state_gdn_conv, state_gdn, gdn_conv_w, gdn_a_log, gdn_dt_bias, gdn_norm)

    def ffn_up_s(hn):
        act, ug, uv = ffn_up_step(hn, w_up_b, ffn_conv_w, state_ffn_conv, t.ffn_cols)
        return act, jnp.concatenate([ug, uv], axis=-1)

    y_s, up_new = tail(xs, o_nsa, o_gdn, u_mg, ffn_up_s, t)
    kv_s = new_rows.reshape(Bd, 1, 4, G, HEAD_DIM)
    conv_s = jnp.concatenate([state_gdn_conv[:, 1:], u_gqkv[:, None]], axis=1)
    ffn_s = jnp.concatenate([state_ffn_conv[:, 1:], up_new[:, None]], axis=1)

    return (y_p.reshape(B, T, D), y_s.reshape(Bd, 1, D), kv_p, win_p, s_p, conv_p, ffn_p,
            kv_s, win_s, s_s, conv_s, ffn_s)
```

```python
import functools
import math
from typing import NamedTuple

import jax
import jax.numpy as jnp
import numpy as np
from jax import lax
from jax.experimental import pallas as pl
from jax.experimental.pallas import tpu as pltpu

F32 = jnp.float32
BF16 = jnp.bfloat16
HI = lax.Precision.HIGHEST

LANES = 128
SUBLANES = 8
VMEM_LIMIT_BYTES = 56 * 1024 * 1024

HEAD_DIM = 128
NSA_HEADS = 16
NSA_KV_HEADS = 4
NSA_GROUP = NSA_HEADS // NSA_KV_HEADS
CMP_LEN = 32
CMP_STRIDE = 16
CMP_HIDDEN = 256
SLC_LEN = 64
SLC_TOP_N = 16
SLC_LOCAL = 2
WINDOW = 512
GDN_HEADS = 16
GDN_DK = 128
GDN_DV = 128
GDN_CONV = 4
GDN_CHUNK = 64
FFN_CONV = 3
ROPE_THETA = 10000.0
EPS = 1e-6


def _cparams(*sem):
    return pltpu.CompilerParams(dimension_semantics=sem, vmem_limit_bytes=VMEM_LIMIT_BYTES)


def _dot(a, b, precision=None):
    return jnp.dot(a, b, preferred_element_type=F32, precision=precision)


def _dot_nt(a, b, precision=None):
    return lax.dot_general(a, b, (((1,), (1,)), ((), ())), preferred_element_type=F32, precision=precision)


def _dot_tn(a, b, precision=None):
    return lax.dot_general(a, b, (((0,), (0,)), ((), ())), preferred_element_type=F32, precision=precision)


def _sigmoid(x):
    return 1.0 / (1.0 + jnp.exp(-x))


def _silu(x):
    return x * _sigmoid(x)


def _rmsnorm_body(x_ref, w_ref, o_ref):
    x = x_ref[...]
    y = x * lax.rsqrt(jnp.mean(x * x, axis=-1, keepdims=True) + EPS)
    o_ref[...] = (y * w_ref[...]).astype(o_ref.dtype)


def rmsnorm(x, w, out_dtype, tm):
    M, D = x.shape
    return pl.pallas_call(
        _rmsnorm_body,
        grid=(M // tm,),
        in_specs=[pl.BlockSpec((tm, D), lambda i: (i, 0)), pl.BlockSpec((1, D), lambda i: (0, 0))],
        out_specs=pl.BlockSpec((tm, D), lambda i: (i, 0)),
        out_shape=jax.ShapeDtypeStruct((M, D), out_dtype),
        compiler_params=_cparams("parallel"),
        name="rmsnorm",
    )(x, w.reshape(1, D))


def _mm_body(x_ref, w_ref, o_ref):
    o_ref[...] = _dot(x_ref[...], w_ref[...]).astype(o_ref.dtype)


def matmul(x, w, tm, tn, out_dtype=F32):
    M, K = x.shape
    N = w.shape[1]
    return pl.pallas_call(
        _mm_body,
        grid=(M // tm, N // tn),
        in_specs=[pl.BlockSpec((tm, K), lambda i, j: (i, 0)), pl.BlockSpec((K, tn), lambda i, j: (0, j))],
        out_specs=pl.BlockSpec((tm, tn), lambda i, j: (i, j)),
        out_shape=jax.ShapeDtypeStruct((M, N), out_dtype),
        compiler_params=_cparams("parallel", "arbitrary"),
        name="matmul",
    )(x, w)


def _mm_nt_body(x_ref, w_ref, o_ref):
    o_ref[...] = _dot_nt(x_ref[...], w_ref[...]).astype(o_ref.dtype)


def matmul_nt(x, w_t, tm, tn, rows, out_dtype=F32):
    M, K = x.shape
    r0, N = rows
    assert r0 % (2 * SUBLANES) == 0 and N % tn == 0
    return pl.pallas_call(
        _mm_nt_body,
        grid=(M // tm, N // tn),
        in_specs=[pl.BlockSpec((tm, K), lambda i, j: (i, 0)),
                  pl.BlockSpec((pl.Element(tn), pl.Element(K)), lambda i, j: (pl.multiple_of(r0 + j * tn, 2 * SUBLANES), 0))],
        out_specs=pl.BlockSpec((tm, tn), lambda i, j: (i, j)),
        out_shape=jax.ShapeDtypeStruct((M, N), out_dtype),
        compiler_params=_cparams("parallel", "arbitrary"),
        name="matmul_nt",
    )(x, w_t)


def _mm_res_body(x_ref, w_ref, r_ref, o_ref):
    o_ref[...] = r_ref[...] + _dot(x_ref[...], w_ref[...])


def matmul_residual(x, w, res, tm, tn, x_buffers=2):
    M, K = x.shape
    N = w.shape[1]
    return pl.pallas_call(
        _mm_res_body,
        grid=(M // tm, N // tn),
        in_specs=[pl.BlockSpec((tm, K), lambda i, j: (i, 0), pipeline_mode=pl.Buffered(x_buffers)),
                  pl.BlockSpec((K, tn), lambda i, j: (0, j)),
                  pl.BlockSpec((tm, tn), lambda i, j: (i, j))],
        out_specs=pl.BlockSpec((tm, tn), lambda i, j: (i, j)),
        out_shape=jax.ShapeDtypeStruct((M, N), F32),
        compiler_params=_cparams("parallel", "arbitrary"),
        name="matmul_residual",
    )(x, w, res)


def _mm_mix_body(a_ref, b_ref, wa_ref, wb_ref, ga_ref, gb_ref, o_ref):
    ua = _dot(a_ref[...], wa_ref[...])
    ub = _dot(b_ref[...], wb_ref[...])
    o_ref[...] = (_sigmoid(ga_ref[...]) * ua + _sigmoid(gb_ref[...]) * ub).astype(o_ref.dtype)


def matmul_mix(a, b, wa, wb, gates, tm, tn):
    M, K = a.shape
    N = wa.shape[1]
    nj = N // tn
    return pl.pallas_call(
        _mm_mix_body,
        grid=(M // tm, nj),
        in_specs=[pl.BlockSpec((tm, K), lambda i, j: (i, 0)), pl.BlockSpec((tm, K), lambda i, j: (i, 0)),
                  pl.BlockSpec((K, tn), lambda i, j: (0, j)), pl.BlockSpec((K, tn), lambda i, j: (0, j)),
                  pl.BlockSpec((tm, tn), lambda i, j: (i, j)), pl.BlockSpec((tm, tn), lambda i, j: (i, j + nj))],
        out_specs=pl.BlockSpec((tm, tn), lambda i, j: (i, j)),
        out_shape=jax.ShapeDtypeStruct((M, N), BF16),
        compiler_params=_cparams("parallel", "arbitrary"),
        name="matmul_mix",
    )(a, b, wa, wb, gates, gates)


def _shift_rows(x, hist, sh):
    s = pltpu.roll(x, sh, axis=0)
    head = s[:SUBLANES]
    row8 = lax.broadcasted_iota(jnp.int32, head.shape, 0)
    for r in range(sh):
        head = jnp.where(row8 == r, hist[SUBLANES - sh + r:SUBLANES - sh + r + 1], head)
    return jnp.concatenate([head, s[SUBLANES:]], axis=0)


def _ffn_up_seq_body(x_ref, wg_ref, wv_ref, cg_ref, cv_ref, act_ref, tg_ref, tv_ref, hg_ref, hv_ref, *, tiles_per_seq):
    i = pl.program_id(0)
    j = pl.program_id(1)
    first = (i % tiles_per_seq) == 0
    x = x_ref[...]

    def branch(w_ref, c_ref, hist_ref, tail_ref):
        up = _dot(x, w_ref[...])
        tm = up.shape[0]
        hist = jnp.where(first, 0.0, hist_ref[j])
        c = c_ref[...]
        out = up * c[2:3] + _shift_rows(up, hist, 2) * c[0:1] + _shift_rows(up, hist, 1) * c[1:2]
        tail = up[tm - SUBLANES:tm]
        hist_ref[j] = tail
        tail_ref[0] = tail
        return out

    gate = branch(wg_ref, cg_ref, hg_ref, tg_ref)
    val = branch(wv_ref, cv_ref, hv_ref, tv_ref)
    act_ref[...] = (_silu(gate) * val).astype(act_ref.dtype)


def ffn_up_seq(x, w_up, conv_w, n_seq, tm, tn):
    M, K = x.shape
    F = w_up.shape[1] // 2
    nj = F // tn
    tiles_per_seq = M // n_seq // tm
    body = functools.partial(_ffn_up_seq_body, tiles_per_seq=tiles_per_seq)
    return pl.pallas_call(
        body,
        grid=(M // tm, nj),
        in_specs=[pl.BlockSpec((tm, K), lambda i, j: (i, 0)),
                  pl.BlockSpec((K, tn), lambda i, j: (0, j)), pl.BlockSpec((K, tn), lambda i, j: (0, j + nj)),
                  pl.BlockSpec((FFN_CONV, tn), lambda i, j: (0, j)), pl.BlockSpec((FFN_CONV, tn), lambda i, j: (0, j + nj))],
        out_specs=[pl.BlockSpec((tm, tn), lambda i, j: (i, j)),
                   pl.BlockSpec((1, SUBLANES, tn), lambda i, j: (i, 0, j)),
                   pl.BlockSpec((1, SUBLANES, tn), lambda i, j: (i, 0, j))],
        out_shape=[jax.ShapeDtypeStruct((M, F), BF16),
                   jax.ShapeDtypeStruct((M // tm, SUBLANES, F), F32), jax.ShapeDtypeStruct((M // tm, SUBLANES, F), F32)],
        scratch_shapes=[pltpu.VMEM((nj, SUBLANES, tn), F32), pltpu.VMEM((nj, SUBLANES, tn), F32)],
        compiler_params=_cparams("arbitrary", "arbitrary"),
        name="ffn_up_seq",
    )(x, w_up, w_up, conv_w, conv_w)


def _ffn_up_step_body(x_ref, wg_ref, wv_ref, cg_ref, cv_ref, g0_ref, g1_ref, v0_ref, v1_ref, act_ref, ug_ref, uv_ref):
    x = x_ref[...]

    def branch(w_ref, c_ref, h0_ref, h1_ref, up_ref):
        up = _dot(x, w_ref[...])
        up_ref[...] = up
        c = c_ref[...]
        return up * c[2:3] + h0_ref[...] * c[0:1] + h1_ref[...] * c[1:2]

    gate = branch(wg_ref, cg_ref, g0_ref, g1_ref, ug_ref)
    val = branch(wv_ref, cv_ref, v0_ref, v1_ref, uv_ref)
    act_ref[...] = (_silu(gate) * val).astype(act_ref.dtype)


def ffn_up_step(x, w_up, conv_w, hist, tn):
    M, K = x.shape
    F2 = w_up.shape[1]
    F = F2 // 2
    nj = F // tn
    h2 = hist.reshape(M, 2 * F2)
    return pl.pallas_call(
        _ffn_up_step_body,
        grid=(nj,),
        in_specs=[pl.BlockSpec((M, K), lambda j: (0, 0)),
                  pl.BlockSpec((K, tn), lambda j: (0, j)), pl.BlockSpec((K, tn), lambda j: (0, j + nj)),
                  pl.BlockSpec((FFN_CONV, tn), lambda j: (0, j)), pl.BlockSpec((FFN_CONV, tn), lambda j: (0, j + nj)),
                  pl.BlockSpec((M, tn), lambda j: (0, j)), pl.BlockSpec((M, tn), lambda j: (0, j + 2 * nj)),
                  pl.BlockSpec((M, tn), lambda j: (0, j + nj)), pl.BlockSpec((M, tn), lambda j: (0, j + 3 * nj))],
        out_specs=[pl.BlockSpec((M, tn), lambda j: (0, j))] * 3,
        out_shape=[jax.ShapeDtypeStruct((M, F), BF16), jax.ShapeDtypeStruct((M, F), F32), jax.ShapeDtypeStruct((M, F), F32)],
        compiler_params=_cparams("arbitrary"),
        name="ffn_up_step",
    )(x, w_up, w_up, conv_w, conv_w, h2, h2, h2, h2)


def _rope_body(u_ref, cos_ref, sin_ref, q_ref, kv_ref, win_ref):
    cos = cos_ref[...]
    sin = sin_ref[...]
    scale = HEAD_DIM ** -0.5

    def rot(x):
        return x * cos + pltpu.roll(x, HEAD_DIM // 2, axis=1) * sin

    for h in range(NSA_HEADS):
        sl = slice(h * HEAD_DIM, (h + 1) * HEAD_DIM)
        q_ref[:, sl] = (rot(u_ref[:, sl]) * scale).astype(q_ref.dtype)
    base = NSA_HEADS * HEAD_DIM
    n_glob = 4 * NSA_KV_HEADS
    for slot in range(6 * NSA_KV_HEADS):
        x = u_ref[:, base + slot * HEAD_DIM:base + (slot + 1) * HEAD_DIM]
        if (slot // NSA_KV_HEADS) % 2 == 0:
            x = rot(x)
        if slot < n_glob:
            kv_ref[:, slot * HEAD_DIM:(slot + 1) * HEAD_DIM] = x
        else:
            win_ref[:, (slot - n_glob) * HEAD_DIM:(slot - n_glob + 1) * HEAD_DIM] = x


def rope_split(u, cos, sin, tm):
    M, W = u.shape
    nt = cos.shape[0] // tm
    nq = NSA_HEADS * HEAD_DIM
    ng = 4 * NSA_KV_HEADS * HEAD_DIM
    nw = 2 * NSA_KV_HEADS * HEAD_DIM
    return pl.pallas_call(
        _rope_body,
        grid=(M // tm,),
        in_specs=[pl.BlockSpec((tm, W), lambda i: (i, 0)),
                  pl.BlockSpec((tm, HEAD_DIM), lambda i: (i % nt, 0)), pl.BlockSpec((tm, HEAD_DIM), lambda i: (i % nt, 0))],
        out_specs=[pl.BlockSpec((tm, nq), lambda i: (i, 0)), pl.BlockSpec((tm, ng), lambda i: (i, 0)),
                   pl.BlockSpec((tm, nw), lambda i: (i, 0))],
        out_shape=[jax.ShapeDtypeStruct((M, nq), BF16), jax.ShapeDtypeStruct((M, ng), F32), jax.ShapeDtypeStruct((M, nw), F32)],
        compiler_params=_cparams("parallel"),
        name="rope_split",
    )(u, cos, sin)


GDN_HB = 4
GDN_COL_B = 0
GDN_COL_A = 8
NSA_COL_G = 16
GDN_STEP_GROUPS = 2


def _softplus(x):
    return jnp.maximum(x, 0.0) + jnp.log1p(jnp.exp(-jnp.abs(x)))


def _l2norm(x):
    return x * lax.rsqrt(jnp.sum(x * x, axis=-1, keepdims=True) + EPS)


def _solve_unit_lower(a_list, r_list, order):
    steps = int(math.log2(order))
    a_list, r_list = list(a_list), list(r_list)
    m, n = a_list[0].shape[1], r_list[0].shape[1]
    for i in range(steps):
        last = i == steps - 1
        for c in range(len(a_list)):
            a_b = a_list[c].astype(BF16)
            r_hi, r_lo = _split_bf16(r_list[c])
            prod = _dot(a_b, jnp.concatenate(([] if last else [a_b]) + [r_hi, r_lo], axis=1))
            if not last:
                a_list[c], prod = prod[:, :m], prod[:, m:]
            r_list[c] = r_list[c] + (prod[:, :n] + prod[:, n:])
    return r_list


def _block_diag(blocks):
    z = jnp.zeros_like(blocks[0])
    nb = len(blocks)
    return jnp.concatenate([jnp.concatenate([blocks[h] if j == h else z for j in range(nb)], axis=1)
                            for h in range(nb)], axis=0)


def _gdn_chunks_groups(groups, states):
    nh = len(groups[0][0][0])
    C, d = groups[0][0][2][0].shape
    W = nh * C
    stack = lambda xs: jnp.concatenate(xs, axis=0)
    ri = lax.broadcasted_iota(jnp.int32, (W, W), 0)
    ci = lax.broadcasted_iota(jnp.int32, (W, W), 1)
    same = (ri // C) == (ci // C)

    a_list, r_list, pre = [], [], []
    for chunks in groups:
        for q, k, v, beta, gc, gcr in chunks:
            dmat = jnp.where(same & (ri >= ci), jnp.exp(jnp.minimum(stack(gc) - jnp.concatenate(gcr, axis=1), 0.0)), 0.0)
            eg = [jnp.exp(g) for g in gc]
            kb = [k[h] * beta[h] for h in range(nh)]
            xk = [_dot_nt(jnp.concatenate([kb[h], q[h]], axis=0).astype(BF16), k[h].astype(BF16)) for h in range(nh)]
            kk = _block_diag([x[:C] for x in xk])
            qk = _block_diag([x[C:] for x in xk])
            a_list.append(jnp.where(ri > ci, -(kk * dmat), 0.0))
            r_list.append(jnp.concatenate([stack([v[h] * beta[h] for h in range(nh)]),
                                           stack([kb[h] * eg[h] for h in range(nh)])], axis=1))
            g_last = [g[C - 1:C, :] for g in gc]
            pre.append(((qk * dmat).astype(BF16), [(q[h] * eg[h]).astype(BF16) for h in range(nh)],
                        [(k[h] * jnp.exp(g_last[h] - gc[h])).astype(BF16) for h in range(nh)],
                        stack([jnp.broadcast_to(jnp.exp(g_last[h]), (d, 1)) for h in range(nh)])))
    w_list = _solve_unit_lower(a_list, r_list, C)

    n_chunks = len(groups[0])
    states = list(states)
    outs = [[] for _ in groups]
    for c in range(n_chunks):
        for g in range(len(groups)):
            w, (aqk, q_eg, kd, keep) = w_list[g * n_chunks + c], pre[g * n_chunks + c]
            s_b = states[g].astype(BF16)
            xs = [_dot(jnp.concatenate([w[h * C:(h + 1) * C, d:].astype(BF16), q_eg[h]], axis=0), s_b[h * d:(h + 1) * d])
                  for h in range(nh)]
            v_new = (w[:, :d] - stack([x[:C] for x in xs])).astype(BF16)
            outs[g].append(stack([x[C:] for x in xs]) + _dot(aqk, v_new))
            upd = stack([_dot_tn(kd[h], v_new[h * C:(h + 1) * C]) for h in range(nh)])
            states[g] = states[g] * keep + upd
    return outs, states


def _gdn_seq_body(q_ref, k_ref, v_ref, z_ref, ba_ref, cq_ref, ck_ref, cv_ref, alog_ref, dtb_ref, nw_ref,
                  o_ref, s_ref, tq_ref, tk_ref, tv_ref, qs_ref, ks_ref, vs_ref, gs_ref, bs_ref):
    t_idx = pl.program_id(2)
    tt = q_ref.shape[0]
    hb = q_ref.shape[1] // GDN_DK
    C = GDN_CHUNK

    @pl.when(t_idx == 0)
    def _():
        s_ref[...] = jnp.zeros_like(s_ref)
        tq_ref[...] = jnp.zeros_like(tq_ref)
        tk_ref[...] = jnp.zeros_like(tk_ref)
        tv_ref[...] = jnp.zeros_like(tv_ref)

    def conv(x_ref, c_ref, tail_ref):
        x = x_ref[...]
        hist = tail_ref[...]
        row = lax.broadcasted_iota(jnp.int32, x.shape, 0)
        c = c_ref[...]
        out = x * c[GDN_CONV - 1:GDN_CONV]
        for j in range(GDN_CONV - 1):
            out = out + _shift_rows(x, hist, GDN_CONV - 1 - j) * c[j:j + 1]
        tail_ref[...] = x[tt - SUBLANES:tt]
        return _silu(out)

    qa = conv(q_ref, cq_ref, tq_ref)
    ka = conv(k_ref, ck_ref, tk_ref)
    vs_ref[...] = conv(v_ref, cv_ref, tv_ref)
    for h in range(hb):
        sl = slice(h * GDN_DK, (h + 1) * GDN_DK)
        qs_ref[:, sl] = _l2norm(qa[:, sl]) * GDN_DK ** -0.5
        ks_ref[:, sl] = _l2norm(ka[:, sl])
    ba = ba_ref[...]
    bs_ref[...] = _sigmoid(ba)
    n_groups = hb // GDN_HB
    par_row = lambda ref: jnp.concatenate([ref[g, 0:1, :] for g in range(n_groups)], axis=1)
    gs_ref[...] = -jnp.exp(par_row(alog_ref)) * _softplus(ba + par_row(dtb_ref))

    tri = (lax.broadcasted_iota(jnp.int32, (C, C), 0) >= lax.broadcasted_iota(jnp.int32, (C, C), 1)).astype(F32)
    sel = (lax.broadcasted_iota(jnp.int32, (2 * SUBLANES, LANES), 0)
           == lax.broadcasted_iota(jnp.int32, (2 * SUBLANES, LANES), 1)).astype(F32)
    nw = nw_ref[...]

    chunk_rows = [slice(c * C, (c + 1) * C) for c in range(tt // C)]
    gc_alls = [_dot(tri, gs_ref[rows, :], HI) for rows in chunk_rows]
    groups, group_heads = [], []
    for g in range(n_groups):
        heads = [slice((g * GDN_HB + h) * GDN_DK, (g * GDN_HB + h + 1) * GDN_DK) for h in range(GDN_HB)]
        lanes = slice(g * LANES, (g + 1) * LANES)
        chunks = []
        for rows, gc_full in zip(chunk_rows, gc_alls):
            gc_all = gc_full[:, lanes]
            gc_t = _dot_nt(sel, gc_all, HI)
            beta_all = bs_ref[rows, lanes]
            chunks.append((
                [qs_ref[rows, sl] for sl in heads], [ks_ref[rows, sl] for sl in heads], [vs_ref[rows, sl] for sl in heads],
                [beta_all[:, GDN_COL_B + h:GDN_COL_B + h + 1] for h in range(GDN_HB)],
                [gc_all[:, GDN_COL_A + h:GDN_COL_A + h + 1] for h in range(GDN_HB)],
                [gc_t[GDN_COL_A + h:GDN_COL_A + h + 1, :] for h in range(GDN_HB)]))
        groups.append(chunks)
        group_heads.append(heads)
    gdk = GDN_HB * GDN_DK
    outs, states = _gdn_chunks_groups(groups, [s_ref[0, g * gdk:(g + 1) * gdk] for g in range(n_groups)])
    for g in range(n_groups):
        s_ref[0, g * gdk:(g + 1) * gdk] = states[g]
        for rows, o in zip(chunk_rows, outs[g]):
            on = o * lax.rsqrt(jnp.mean(o * o, axis=-1, keepdims=True) + EPS) * nw
            for h, sl in enumerate(group_heads[g]):
                o_ref[rows, sl] = (on[h * C:(h + 1) * C] * _silu(z_ref[rows, sl])).astype(o_ref.dtype)


def gdn_seq(u_qkv, u_z, u_small, conv_w, alog_rows, dtb_rows, norm_w, n_seq, tt):
    M = u_qkv.shape[0]
    T = M // n_seq
    nt = T // tt
    hb = GDN_HB * GDN_STEP_GROUPS
    nhb = GDN_HEADS // hb
    wb = hb * GDN_DK
    gl = GDN_STEP_GROUPS * LANES
    row_blk = lambda off: pl.BlockSpec((tt, wb), lambda b, h, t: (b * nt + t, h + off))
    cw_blk = lambda off: pl.BlockSpec((GDN_CONV, wb), lambda b, h, t: (0, h + off))
    par_blk = pl.BlockSpec((GDN_STEP_GROUPS, SUBLANES, LANES), lambda b, h, t: (h, 0, 0))
    return pl.pallas_call(
        _gdn_seq_body,
        grid=(n_seq, nhb, nt),
        in_specs=[row_blk(0), row_blk(nhb), row_blk(2 * nhb), row_blk(0),
                  pl.BlockSpec((tt, gl), lambda b, h, t: (b * nt + t, h)),
                  cw_blk(0), cw_blk(nhb), cw_blk(2 * nhb), par_blk, par_blk,
                  pl.BlockSpec((1, GDN_DV), lambda b, h, t: (0, 0))],
        out_specs=[row_blk(0), pl.BlockSpec((1, hb * GDN_DK, GDN_DV), lambda b, h, t: (b, h, 0))],
        out_shape=[jax.ShapeDtypeStruct((M, GDN_HEADS * GDN_DV), BF16),
                   jax.ShapeDtypeStruct((n_seq, GDN_HEADS * GDN_DK, GDN_DV), F32)],
        scratch_shapes=[pltpu.VMEM((SUBLANES, wb), F32)] * 3 + [pltpu.VMEM((tt, wb), F32)] * 3
                       + [pltpu.VMEM((tt, gl), F32)] * 2,
        compiler_params=_cparams("parallel", "parallel", "arbitrary"),
        name="gdn_seq",
    )(u_qkv, u_qkv, u_qkv, u_z, u_small, conv_w, conv_w, conv_w, alog_rows, dtb_rows, norm_w.reshape(1, GDN_DV))


def gdn_param_rows(gdn_a_log, gdn_dt_bias):
    nhb = GDN_HEADS // GDN_HB

    def rows(p):
        r = _place_lanes(((GDN_COL_A, p.reshape(nhb, GDN_HB)),), (nhb,))
        return jnp.broadcast_to(r[:, None, :], (nhb, SUBLANES, LANES))

    return rows(gdn_a_log), rows(gdn_dt_bias)


CMP_RATIO = CMP_LEN // CMP_STRIDE
CMP_FEAT = CMP_STRIDE * HEAD_DIM
MASKED = -1e30


def _compress_mlp(part, pe_ref, w1_ref, w2_ref, chunk_step=1):
    n = part.shape[0]
    pe_part = _dot(pe_ref[...], w1_ref[...])
    hid0 = pe_part[0:1, :CMP_HIDDEN] + pe_part[1:2, CMP_HIDDEN:]
    hid = hid0 + part[:, :CMP_HIDDEN]
    hid = hid + pltpu.roll(part[:, CMP_HIDDEN:], n - chunk_step, axis=0)
    return _dot(_silu(hid).astype(BF16), w2_ref[...])


def _compress_seq_body(x_ref, pe_ref, w1_ref, w2_ref, o_ref, xc_ref):
    n = o_ref.shape[3]
    for s in range(CMP_STRIDE):
        xc_ref[:, s * HEAD_DIM:(s + 1) * HEAD_DIM] = x_ref[pl.ds(s, n, stride=CMP_STRIDE), :].astype(BF16)
    part = _dot(xc_ref[...], w1_ref[0])
    o_ref[0, 0, 0] = _compress_mlp(part, pe_ref.at[0], w1_ref.at[0], w2_ref.at[0])


def compress_seq(kv_rows, pe2, w1c, w2c, n_seq):
    M = kv_rows.shape[0]
    T = M // n_seq
    n = T // CMP_STRIDE
    G = NSA_KV_HEADS
    return pl.pallas_call(
        _compress_seq_body,
        grid=(n_seq, 2, G),
        in_specs=[pl.BlockSpec((T, HEAD_DIM), lambda b, c, g: (b, c * G + g)),
                  pl.BlockSpec((1, SUBLANES, CMP_FEAT), lambda b, c, g: (c, 0, 0)),
                  pl.BlockSpec((1, CMP_FEAT, 2 * CMP_HIDDEN), lambda b, c, g: (c, 0, 0)),
                  pl.BlockSpec((1, CMP_HIDDEN, HEAD_DIM), lambda b, c, g: (c, 0, 0))],
        out_specs=pl.BlockSpec((1, 1, 1, n, HEAD_DIM), lambda b, c, g: (b, c, g, 0, 0)),
        out_shape=jax.ShapeDtypeStruct((n_seq, 2, G, n, HEAD_DIM), F32),
        scratch_shapes=[pltpu.VMEM((n, CMP_FEAT), BF16)],
        compiler_params=_cparams("parallel", "parallel", "parallel"),
        name="compress_seq",
    )(kv_rows, pe2, w1c, w2c)


def compress_params(cmp_pe, cmp_w1, cmp_w2):
    w1 = cmp_w1.reshape(2, CMP_RATIO, CMP_FEAT, CMP_HIDDEN)
    w1c = jnp.concatenate([w1[:, r] for r in range(CMP_RATIO)], axis=-1).astype(BF16)
    pe = cmp_pe.reshape(2, CMP_RATIO, CMP_FEAT)
    pe2 = jnp.zeros((2, SUBLANES, CMP_FEAT), F32).at[:, :CMP_RATIO].set(pe).astype(BF16)
    return pe2, w1c, cmp_w2.astype(BF16)


def _masked_softmax(s, mask, axis=-1):
    s = jnp.where(mask, s, -jnp.inf)
    m = jnp.max(s, axis=axis, keepdims=True)
    m = jnp.where(m > -jnp.inf, m, 0.0)
    e = jnp.exp(s - m)
    return e * (1.0 / jnp.maximum(jnp.sum(e, axis=axis, keepdims=True), 1e-30))


def _masked_softmax_heads(s, mask, n_heads):
    t = mask.shape[0]
    return jnp.concatenate([_masked_softmax(s[h * t:(h + 1) * t], mask) for h in range(n_heads)], axis=0)


def _split_bf16(x):
    hi = x.astype(BF16)
    return hi, (x - hi.astype(F32)).astype(BF16)


def _select_blocks_t(imp_t, pos_row, n_blocks):
    J = imp_t.shape[0]
    j = lax.broadcasted_iota(jnp.int32, imp_t.shape, 0)
    cur = pos_row // SLC_LEN
    forced = (j == 0) | ((j <= cur) & (j > cur - SLC_LOCAL))
    score = jnp.where(j * SLC_LEN > pos_row, -jnp.inf, jnp.where(forced, jnp.inf, imp_t))
    score = jnp.where(j < n_blocks, score, -jnp.inf)
    tiles = [score[r:r + SUBLANES] for r in range(0, J, SUBLANES)]
    ranks = [jnp.zeros(t.shape, F32) for t in tiles]
    for jp in range(n_blocks):
        row = score[jp:jp + 1, :]
        for i, t in enumerate(tiles):
            lo_j = i * SUBLANES
            ge, gt = jnp.where(row >= t, 1.0, 0.0), jnp.where(row > t, 1.0, 0.0)
            if lo_j > jp:
                ahead = ge
            elif lo_j + SUBLANES - 1 <= jp:
                ahead = gt
            else:
                ahead = jnp.where(lax.broadcasted_iota(jnp.int32, t.shape, 0) + lo_j > jp, ge, gt)
            ranks[i] = ranks[i] + ahead
    rank = jnp.concatenate(ranks, axis=0)
    keep = (rank < float(min(SLC_TOP_N, n_blocks))) & (j < n_blocks)
    return jnp.where(keep, 1.0, 0.0)


def _nsa_seq_body(q_ref, kc_ref, vc_ref, ks_ref, vs_ref, kw_ref, vw_ref, gl_ref, ovl_ref, o_ref, m_ref, l_ref, acc_ref, *, kblk):
    i = pl.program_id(2)
    tq = q_ref.shape[0]
    T = ks_ref.shape[0]
    H = NSA_GROUP
    D = HEAD_DIM
    n_slc = T // SLC_LEN
    qb = q_ref[...]
    q4 = jnp.concatenate([qb[:, h * D:(h + 1) * D] for h in range(H)], axis=0)
    t0 = i * tq
    pos1 = t0 + lax.broadcasted_iota(jnp.int32, (tq, 1), 0)

    kc = kc_ref[0, 0, 0].astype(BF16)
    vc = vc_ref[0, 0, 0].astype(BF16)
    n_c = kc.shape[0]
    s = _dot_nt(q4, kc)
    cmp_end = lax.broadcasted_iota(jnp.int32, (1, n_c), 1) * CMP_STRIDE + (CMP_LEN - 1)
    p = _masked_softmax_heads(s, cmp_end <= pos1, H)
    o_c = _dot(p.astype(BF16), vc)

    psum = p[0:tq]
    for h in range(1, H):
        psum = psum + p[h * tq:(h + 1) * tq]
    p_hi, p_lo = _split_bf16(psum)
    ovl = ovl_ref[...]
    imp_t = _dot_nt(ovl, p_hi) + _dot_nt(ovl, p_lo)
    pos_row = t0 + lax.broadcasted_iota(jnp.int32, (1, tq), 1)
    sel_t = _select_blocks_t(imp_t, pos_row, n_slc)
    eye = (lax.broadcasted_iota(jnp.int32, (tq, tq), 0) == lax.broadcasted_iota(jnp.int32, (tq, tq), 1))
    sel = _dot_nt(jnp.where(eye, 1.0, 0.0).astype(BF16), sel_t.astype(BF16)).astype(BF16)

    n_kb = (t0 + tq + kblk - 1) // kblk

    m_ref[...] = jnp.full(m_ref.shape, MASKED, F32)
    l_ref[...] = jnp.zeros(l_ref.shape, F32)
    acc_ref[...] = jnp.zeros(acc_ref.shape, F32)

    def kv_step(kb, carry):
        k0 = pl.multiple_of(kb * kblk, kblk)
        kk = ks_ref[pl.ds(k0, kblk), :].astype(BF16)
        vv = vs_ref[pl.ds(k0, kblk), :].astype(BF16)
        s2 = _dot_nt(q4, kk)
        kpos = k0 + lax.broadcasted_iota(jnp.int32, (1, kblk), 1)
        expand = (lax.broadcasted_iota(jnp.int32, (n_slc, kblk), 0) == kpos // SLC_LEN)
        chosen = _dot(sel, jnp.where(expand, 1.0, 0.0).astype(BF16))
        bias = jnp.where((chosen > 0.5) & (kpos <= pos1), 0.0, MASKED)
        es = []
        for h in range(H):
            rows = slice(h * tq, (h + 1) * tq)
            sh = s2[rows] + bias
            m_old = m_ref[rows]
            m_new = jnp.maximum(m_old, jnp.max(sh, axis=-1, keepdims=True))
            alpha = jnp.exp(m_old - m_new)
            e = jnp.exp(sh - m_new)
            m_ref[rows] = m_new
            l_ref[rows] = alpha * l_ref[rows] + jnp.sum(e, axis=-1, keepdims=True)
            acc_ref[rows] = alpha * acc_ref[rows]
            es.append(e.astype(BF16))
        acc_ref[...] += _dot(jnp.concatenate(es, axis=0), vv)
        return carry

    lax.fori_loop(0, n_kb, kv_step, 0)
    o_s = acc_ref[...] * (1.0 / jnp.maximum(l_ref[...], 1e-30))

    band = WINDOW + tq
    w0 = pl.multiple_of(jnp.maximum(t0 - WINDOW, 0), tq)
    kw = kw_ref[pl.ds(w0, band), :].astype(BF16)
    vw = vw_ref[pl.ds(w0, band), :].astype(BF16)
    sw = _dot_nt(q4, kw)
    diff = pos1 - (w0 + lax.broadcasted_iota(jnp.int32, (1, band), 1))
    pw = _masked_softmax_heads(sw, (diff >= 0) & (diff < WINDOW), H)
    o_w = _dot(pw.astype(BF16), vw)

    gates = _sigmoid(gl_ref[...])
    for h in range(H):
        rows = slice(h * tq, (h + 1) * tq)
        g = lambda br: gates[:, NSA_COL_G + H * br + h:NSA_COL_G + H * br + h + 1]
        o_ref[:, h * D:(h + 1) * D] = (g(0) * o_c[rows] + g(1) * o_s[rows] + g(2) * o_w[rows]).astype(o_ref.dtype)


def _overlap_t(n_cmp_rows, n_slc):
    start = np.arange(n_cmp_rows)[None, :] * CMP_STRIDE
    blk = np.arange(n_slc)[:, None] * SLC_LEN
    return ((start < blk + SLC_LEN) & (start + CMP_LEN > blk)).astype(np.float32)


def nsa_seq(q, kv_rows, win_rows, cmp_kv, u_small, n_seq, tq=256, kblk=512):
    M = q.shape[0]
    T = M // n_seq
    nt = T // tq
    G = NSA_KV_HEADS
    gw = NSA_GROUP * HEAD_DIM
    n_c = cmp_kv.shape[3]
    ovl = jnp.asarray(_overlap_t(n_c, T // SLC_LEN), BF16)
    body = functools.partial(_nsa_seq_body, kblk=kblk)
    seq_blk = lambda off: pl.BlockSpec((T, HEAD_DIM), lambda b, g, i: (b, g + off))
    cmp_blk = lambda c: pl.BlockSpec((1, 1, 1, n_c, HEAD_DIM), lambda b, g, i: (b, c, g, 0, 0))
    return pl.pallas_call(
        body,
        grid=(n_seq, G, nt),
        in_specs=[pl.BlockSpec((tq, gw), lambda b, g, i: (b * nt + i, g)),
                  cmp_blk(0), cmp_blk(1), seq_blk(2 * G), seq_blk(3 * G), seq_blk(0), seq_blk(G),
                  pl.BlockSpec((tq, LANES), lambda b, g, i: (b * nt + i, g)),
                  pl.BlockSpec(ovl.shape, lambda b, g, i: (0, 0))],
        out_specs=pl.BlockSpec((tq, gw), lambda b, g, i: (b * nt + i, g)),
        out_shape=jax.ShapeDtypeStruct((M, NSA_HEADS * HEAD_DIM), BF16),
        scratch_shapes=[pltpu.VMEM((NSA_GROUP * tq, 1), F32)] * 2 + [pltpu.VMEM((NSA_GROUP * tq, HEAD_DIM), F32)],
        compiler_params=_cparams("parallel", "parallel", "arbitrary"),
        name="nsa_seq",
    )(q, cmp_kv, cmp_kv, kv_rows, kv_rows, win_rows, win_rows, u_small, ovl)


STEP_COL_G = 0
STEP_COL_B = 3 * NSA_HEADS
STEP_COL_A = STEP_COL_B + GDN_HEADS


def _gdn_step_prep_body(x_ref, h_ref, c_ref, us_ref, alog_ref, dtb_ref, q_ref, k_ref, v_ref, b_ref, e_ref):
    W = x_ref.shape[1]
    c = c_ref[...]
    x = x_ref[...] * c[GDN_CONV - 1:GDN_CONV]
    for j in range(GDN_CONV - 1):
        x = x + h_ref[:, j * W:(j + 1) * W] * c[j:j + 1]
    x = _silu(x)
    us = us_ref[...]
    beta = _sigmoid(us)
    eg = jnp.exp(-jnp.exp(alog_ref[...]) * _softplus(us + dtb_ref[...]))
    nq = GDN_HEADS * GDN_DK
    for h in range(GDN_HEADS):
        sl = slice(h * GDN_DK, (h + 1) * GDN_DK)
        q_ref[:, sl] = _l2norm(x[:, sl]) * GDN_DK ** -0.5
        k_ref[:, sl] = _l2norm(x[:, nq + h * GDN_DK:nq + (h + 1) * GDN_DK])
        b_ref[:, sl] = jnp.broadcast_to(beta[:, STEP_COL_B + h:STEP_COL_B + h + 1], (x.shape[0], GDN_DK))
        e_ref[:, sl] = jnp.broadcast_to(eg[:, STEP_COL_A + h:STEP_COL_A + h + 1], (x.shape[0], GDN_DK))
    v_ref[...] = x[:, 2 * nq:]


def _gdn_step_body(q_ref, k_ref, v_ref, b_ref, e_ref, z_ref, nw_ref, s_ref, o_ref, so_ref):
    bb = q_ref.shape[0]
    eye = jnp.where(lax.broadcasted_iota(jnp.int32, (GDN_DK, GDN_DK), 0)
                    == lax.broadcasted_iota(jnp.int32, (GDN_DK, GDN_DK), 1), 1.0, 0.0)
    nw = nw_ref[...]

    def per_seq(bi, carry):
        q, k, v, beta, eg, z = q_ref[bi], k_ref[bi], v_ref[bi], b_ref[bi], e_ref[bi], z_ref[bi]
        k_t = _dot_nt(eye, k, HI)
        q_t = _dot_nt(eye, q, HI)
        outs = []
        for h in range(GDN_HEADS):
            r = slice(h, h + 1)
            s = s_ref[bi, h]
            kcol = k_t[:, h:h + 1]
            k_s = jnp.sum(kcol * s, axis=0, keepdims=True)
            q_s = jnp.sum(q_t[:, h:h + 1] * s, axis=0, keepdims=True)
            v_new = v[r] * beta[r] - (beta[r] * eg[r]) * k_s
            qk = jnp.sum(q[r] * k[r], axis=-1, keepdims=True)
            o = eg[r] * q_s + qk * v_new
            so_ref[bi, h] = s * eg[r] + kcol * v_new
            on = o * lax.rsqrt(jnp.mean(o * o, axis=-1, keepdims=True) + EPS) * nw
            outs.append(on * _silu(z[r]))
        o_ref[bi] = jnp.concatenate(outs, axis=0).astype(o_ref.dtype)
        return carry

    lax.fori_loop(0, bb, per_seq, 0)


def gdn_step(u_qkv, u_z, u_small, hist, state, conv_w, a_log, dt_bias, norm_w, bb=4):
    Bd, W = u_qkv.shape
    H = GDN_HEADS
    nq = H * GDN_DK
    row = lambda p: _place_lanes(((STEP_COL_A, p.reshape(1, H)),), (1,))
    full = lambda shape: pl.BlockSpec(shape, lambda i: (0,) * len(shape))
    outs = pl.pallas_call(
        _gdn_step_prep_body,
        grid=(1,),
        in_specs=[full((Bd, W)), full((Bd, (GDN_CONV - 1) * W)), full((GDN_CONV, W)), full((Bd, LANES)),
                  full((1, LANES)), full((1, LANES))],
        out_specs=[full((Bd, nq))] * 5,
        out_shape=[jax.ShapeDtypeStruct((Bd, nq), F32)] * 5,
        compiler_params=_cparams("arbitrary"),
        name="gdn_step_prep",
    )(u_qkv, hist.reshape(Bd, (GDN_CONV - 1) * W), conv_w, u_small, row(a_log), row(dt_bias))
    heads = lambda a: a.reshape(Bd, H, GDN_DK)
    vec_blk = pl.BlockSpec((bb, H, GDN_DK), lambda i: (i, 0, 0))
    st_blk = pl.BlockSpec((bb, H, GDN_DK, GDN_DV), lambda i: (i, 0, 0, 0))
    o, s_new = pl.pallas_call(
        _gdn_step_body,
        grid=(Bd // bb,),
        in_specs=[vec_blk] * 6 + [pl.BlockSpec((1, GDN_DV), lambda i: (0, 0)), st_blk],
        out_specs=[vec_blk, st_blk],
        out_shape=[jax.ShapeDtypeStruct((Bd, H, GDN_DV), BF16), jax.ShapeDtypeStruct(state.shape, F32)],
        compiler_params=_cparams("parallel"),
        name="gdn_step",
    )(*[heads(a) for a in outs], heads(u_z), norm_w.reshape(1, GDN_DV), state)
    return o.reshape(Bd, H * GDN_DV), s_new


def _softmax_with_new(s_past, valid, s_new):
    s_past = jnp.where(valid, s_past, -jnp.inf)
    m = jnp.maximum(jnp.max(s_past, axis=-1, keepdims=True), s_new)
    e = jnp.exp(s_past - m)
    e_new = jnp.exp(s_new - m)
    return e, e_new, 1.0 / (jnp.sum(e, axis=-1, keepdims=True) + e_new)


def _nsa_step_body(pt_ref, *refs, n_pages, page, n_past):
    lo = refs[:n_pages]
    hi = refs[n_pages:2 * n_pages]
    (q_ref, new_ref, win_ref, wnew_ref, gl_ref, pe_ref, w1_ref, w2_ref, ovl_ref, o_ref, wout_ref, xk_ref, xv_ref) = refs[2 * n_pages:]
    G, H, D = NSA_KV_HEADS, NSA_HEADS, HEAD_DIM
    HS = 2 * G
    pos = n_past
    cpp = page // CMP_STRIDE
    n_chunk = n_pages * cpp
    L = n_pages * page
    q = q_ref[0]
    qf = q.astype(F32)
    new = new_ref[0]
    row_g = lax.broadcasted_iota(jnp.int32, (H, 1), 0) // NSA_GROUP

    def per_head_rows(rows):
        return jnp.concatenate([jnp.broadcast_to(rows[g:g + 1], (NSA_GROUP, rows.shape[1])) for g in range(G)], axis=0)

    first_half = lax.broadcasted_iota(jnp.int32, (SUBLANES, D), 0) < G
    for j in range(n_pages):
        for s in range(CMP_STRIDE):
            for m in range(cpp // 4):
                ks, vs = [], []
                for pair in range(2):
                    ta = lo[j][0, (4 * m + 2 * pair) * CMP_STRIDE + s]
                    tb = lo[j][0, (4 * m + 2 * pair + 1) * CMP_STRIDE + s]
                    ks.append(jnp.where(first_half, ta, pltpu.roll(tb, G, axis=0)))
                    vs.append(jnp.where(first_half, pltpu.roll(ta, G, axis=0), tb))
                r0 = (j * cpp + 4 * m) * G
                xk_ref[r0:r0 + 4 * G, s * D:(s + 1) * D] = jnp.concatenate(ks, axis=0).astype(BF16)
                xv_ref[r0:r0 + 4 * G, s * D:(s + 1) * D] = jnp.concatenate(vs, axis=0).astype(BF16)
    ckv = []
    for c, x_ref in enumerate((xk_ref, xv_ref)):
        part = _dot(x_ref[...], w1_ref[c])
        ckv.append(_compress_mlp(part, pe_ref.at[c], w1_ref.at[c], w2_ref.at[c], G).astype(BF16))

    n_all = G * n_chunk
    s = _dot_nt(q, ckv[0])
    col = lax.broadcasted_iota(jnp.int32, (1, n_all), 1)
    blk = col // G
    ok = (col % G == row_g) & (blk * CMP_STRIDE + (CMP_LEN - 1) <= pos) & (blk < n_chunk - CMP_RATIO + 1)
    p = _masked_softmax(s, ok)
    o_c = _dot(p.astype(BF16), ckv[1])

    gsum = jnp.where(lax.broadcasted_iota(jnp.int32, (LANES, H), 0) == lax.broadcasted_iota(jnp.int32, (LANES, H), 1) // NSA_GROUP, 1.0, 0.0)
    psum = _dot(gsum, p, HI)
    p_hi, p_lo = _split_bf16(psum)
    ovl = ovl_ref[...]
    imp_t = _dot_nt(ovl, p_hi) + _dot_nt(ovl, p_lo)
    n_slc = -(-(L + 1) // SLC_LEN)
    sel_t = _select_blocks_t(imp_t, jnp.full((1, LANES), pos, jnp.int32), n_slc)
    J = sel_t.shape[0]
    eye8 = jnp.where(lax.broadcasted_iota(jnp.int32, (SUBLANES, LANES), 0) == lax.broadcasted_iota(jnp.int32, (SUBLANES, LANES), 1), 1.0, 0.0)
    sel = _dot_nt(eye8.astype(BF16), sel_t.astype(BF16))

    def slot_softmax(sc, tok_ok, s_new):
        slot = lax.broadcasted_iota(jnp.int32, (1, sc.shape[1]), 1) % HS
        return _softmax_with_new(sc, (slot == row_g) & tok_ok, s_new)

    tok = lax.broadcasted_iota(jnp.int32, (1, L * HS), 1) // HS
    chosen = per_head_rows(jnp.concatenate(
        [jnp.broadcast_to(sel[:, b:b + 1], (SUBLANES, SLC_LEN * HS)) for b in range(L // SLC_LEN)], axis=1))
    kv_hi = [hi[j][0].reshape(page * HS, D).astype(BF16) for j in range(n_pages)]
    sc = jnp.concatenate([_dot_nt(q, kv) for kv in kv_hi], axis=1)
    s_new = jnp.sum(qf * per_head_rows(new[2 * G:3 * G]), axis=-1, keepdims=True)
    e, e_new, inv = slot_softmax(sc, (chosen > 0.5) & (tok <= pos), s_new)
    e = pltpu.roll(e, G, axis=1).astype(BF16)
    acc = jnp.zeros((H, D), F32)
    for j, kv in enumerate(kv_hi):
        acc = acc + _dot(e[:, j * page * HS:(j + 1) * page * HS], kv)
    o_s = (acc + e_new * per_head_rows(new[3 * G:4 * G])) * inv

    n_win = win_ref.shape[1] // HS
    wnew = wnew_ref[0]
    win = win_ref[0].astype(BF16)
    wtok = pos - n_win + lax.broadcasted_iota(jnp.int32, (1, n_win * HS), 1) // HS
    sw_new = jnp.sum(qf * per_head_rows(wnew[0:G]), axis=-1, keepdims=True)
    ew, ew_new, winv = slot_softmax(_dot_nt(q, win), (pos - wtok < WINDOW) & (wtok >= 0), sw_new)
    o_w = (_dot(pltpu.roll(ew, G, axis=1).astype(BF16), win) + ew_new * per_head_rows(wnew[G:2 * G])) * winv

    gl = jnp.broadcast_to(_sigmoid(gl_ref[0]), (SUBLANES, LANES))
    hh = lax.broadcasted_iota(jnp.int32, (H, LANES), 0)
    cc = lax.broadcasted_iota(jnp.int32, (H, LANES), 1)
    gate = lambda br: _dot_nt(jnp.where(cc == STEP_COL_G + br * H + hh, 1.0, 0.0), gl, HI)[:, 0:1]
    o_ref[0] = (gate(0) * o_c + gate(1) * o_s + gate(2) * o_w).astype(o_ref.dtype)

    wout_ref[0, :(n_win - 1) * HS] = win_ref[0, HS:]
    wout_ref[0, (n_win - 1) * HS:] = wnew


def nsa_step(q, new_rows, win_new, cache, page_table, win_state, u_small, pe2, w1c, w2c):
    Bd = q.shape[0]
    n_pool, page = cache.shape[:2]
    n_pages = page_table.shape[1]
    G, H, D = NSA_KV_HEADS, NSA_HEADS, HEAD_DIM
    n_past = n_pages * page
    n_chunk = n_past // CMP_STRIDE
    n_win = win_state.shape[1]
    assert n_past % SLC_LEN == 0 and n_win == WINDOW <= n_past and page % (4 * CMP_STRIDE) == 0 and 2 * G == SUBLANES
    n_slc = -(-(n_past + 1) // SLC_LEN)
    J = -(-n_slc // (2 * SUBLANES)) * (2 * SUBLANES)
    ovl = np.zeros((J, n_chunk * G), np.float32)
    ovl[:n_slc] = np.repeat(_overlap_t(n_chunk, n_slc), G, axis=1)
    ovl = jnp.asarray(ovl, BF16)
    body = functools.partial(_nsa_step_body, n_pages=n_pages, page=page, n_past=n_past)
    half_spec = lambda j, half: pl.BlockSpec((1, page, None, 2 * G, D), lambda b, pt: (pt[b, j], 0, half, 0, 0))
    const = lambda a: pl.BlockSpec(a.shape, lambda b, pt: (0,) * a.ndim, pipeline_mode=pl.Buffered(1))
    per_b = lambda shape: pl.BlockSpec((1,) + shape, lambda b, pt: (b,) + (0,) * len(shape))
    cache5 = cache.reshape(n_pool, page, 2, 2 * G, D)
    out, win_out = pl.pallas_call(
        body,
        grid_spec=pltpu.PrefetchScalarGridSpec(
            num_scalar_prefetch=1,
            grid=(Bd,),
            in_specs=[half_spec(j, 0) for j in range(n_pages)] + [half_spec(j, 1) for j in range(n_pages)]
                     + [per_b((H, D)), per_b((4 * G, D)), per_b((n_win * 2 * G, D)), per_b((2 * G, D)), per_b((1, LANES)),
                        const(pe2), const(w1c), const(w2c), const(ovl)],
            out_specs=[per_b((H, D)), per_b((n_win * 2 * G, D))],
            scratch_shapes=[pltpu.VMEM((G * n_chunk, CMP_FEAT), BF16)] * 2,
        ),
        out_shape=[jax.ShapeDtypeStruct((Bd, H, D), BF16), jax.ShapeDtypeStruct((Bd, n_win * 2 * G, D), F32)],
        compiler_params=_cparams("arbitrary"),
        name="nsa_step",
    )(page_table, *([cache5] * (2 * n_pages)), q.reshape(Bd, H, D), new_rows.reshape(Bd, 4 * G, D),
      win_state.reshape(Bd, n_win * 2 * G, D), win_new.reshape(Bd, 2 * G, D), u_small.reshape(Bd, 1, LANES),
      pe2, w1c, w2c, ovl)
    return out.reshape(Bd, H * D), win_out.reshape(win_state.shape)


def rope_tables(pos):
    half = HEAD_DIM // 2
    inv = ROPE_THETA ** (-jnp.arange(half, dtype=F32) / half)
    ang = pos.astype(F32)[:, None] * inv[None, :]
    cos, sin = jnp.cos(ang), jnp.sin(ang)
    return jnp.concatenate([cos, cos], axis=-1), jnp.concatenate([-sin, sin], axis=-1)


def _arrange_w_in(w_in):
    H, G = NSA_HEADS, NSA_KV_HEADS
    sizes = (H * HEAD_DIM, 6 * G * HEAD_DIM, 3 * H, GDN_HEADS * (2 * GDN_DK + GDN_DV), GDN_HEADS * GDN_DV, 2 * GDN_HEADS)
    o = np.cumsum((0,) + sizes).tolist()
    w_t = w_in.T
    K = w_in.shape[0]
    wide = ((o[0], o[2] - o[0]), (o[3], o[4] - o[3]), (o[4], o[5] - o[4]), (o[6], w_t.shape[0] - o[6]))
    w_ng = w_t[o[2]:o[3]].T
    w_ba = w_t[o[5]:o[6]].T
    nhb = GDN_HEADS // GDN_HB
    ng = w_ng.reshape(K, 3, G, NSA_GROUP).transpose(0, 2, 1, 3).reshape(K, G, 3 * NSA_GROUP)
    pieces = ((GDN_COL_B, w_ba[:, :GDN_HEADS].reshape(K, nhb, GDN_HB)),
              (GDN_COL_A, w_ba[:, GDN_HEADS:].reshape(K, nhb, GDN_HB)), (NSA_COL_G, ng))
    w_small_seq = _place_lanes(pieces, (K, nhb)).reshape(K, nhb * LANES).astype(BF16)
    w_small_step = _place_lanes(((STEP_COL_G, w_ng), (STEP_COL_B, w_ba)), (K,)).astype(BF16)
    return w_t.astype(BF16), wide, w_small_seq, w_small_step


def _place_lanes(pieces, lead):
    out, at = [], 0
    for lane, vals in pieces:
        out += [jnp.zeros(lead + (lane - at,), F32), vals.astype(F32)]
        at = lane + vals.shape[-1]
    return jnp.concatenate(out + [jnp.zeros(lead + (LANES - at,), F32)], axis=-1)


class _Tiles(NamedTuple):
    rows: int
    cols: int
    ffn_cols: int
    norm_rows: int


def _tiles(n_rows):
    if n_rows >= 1024:
        return _Tiles(rows=1024, cols=512, ffn_cols=256, norm_rows=256)
    return _Tiles(rows=n_rows, cols=1024, ffn_cols=256, norm_rows=n_rows)


GDN_SEQ_ROWS = 256


def kernel(x_prompt, x_sample, cache_nsa_kv, page_table, state_win_kv, state_gdn, state_gdn_conv, state_ffn_conv, norm_mix, w_in, cmp_pe, cmp_w1, cmp_w2, gdn_conv_w, gdn_a_log, gdn_dt_bias, gdn_norm, w_nsa_out, w_gdn_out, w_o, norm_ffn, w_up, ffn_conv_w, w_down, norm_final):
    B, T, D = x_prompt.shape
    Bd = x_sample.shape[0]
    G = NSA_KV_HEADS
    n_past = page_table.shape[1] * cache_nsa_kv.shape[1]

    w_t, wide, w_small_seq, w_small_step = _arrange_w_in(w_in)
    w_nsa_b, w_gdn_b, w_o_b = w_nsa_out.astype(BF16), w_gdn_out.astype(BF16), w_o.astype(BF16)
    w_up_b, w_down_b = w_up.astype(BF16), w_down.astype(BF16)
    pe2, w1c, w2c = compress_params(cmp_pe, cmp_w1, cmp_w2)

    def project(x2d, t):
        xn = rmsnorm(x2d, norm_mix, BF16, t.norm_rows)
        return xn, [matmul_nt(xn, w_t, t.rows, t.cols, r) for r in wide]

    def tail(x, mixed_nsa, mixed_gdn, u_mg, ffn_up, t):
        mixin = matmul_mix(mixed_nsa, mixed_gdn, w_nsa_b, w_gdn_b, u_mg, t.rows, t.cols)
        h = matmul_residual(mixin, w_o_b, x, t.rows, t.cols)
        hn = rmsnorm(h, norm_ffn, BF16, t.norm_rows)
        act, extra = ffn_up(hn)
        h2 = matmul_residual(act, w_down_b, h, t.rows, t.ffn_cols, x_buffers=1)
        return rmsnorm(h2, norm_final, F32, t.norm_rows), extra

    t = _tiles(B * T)
    xp = x_prompt.reshape(B * T, D)
    xn, (u_qkv, u_gqkv, u_z, u_mg) = project(xp, t)
    u_small = matmul(xn, w_small_seq, t.rows, w_small_seq.shape[1])
    cos, sin = rope_tables(jnp.arange(T, dtype=jnp.int32))
    q, kv_rows, win_rows = rope_split(u_qkv, cos, sin, t.norm_rows)
    cmp_kv = compress_seq(kv_rows, pe2, w1c, w2c, B)
    o_nsa = nsa_seq(q, kv_rows, win_rows, cmp_kv, u_small, B)
    a_rows, d_rows = gdn_param_rows(gdn_a_log, gdn_dt_bias)
    o_gdn, s_p = gdn_seq(u_gqkv, u_z, u_small, gdn_conv_w, a_rows, d_rows, gdn_norm, B, GDN_SEQ_ROWS)
    s_p = s_p.reshape(B, GDN_HEADS, GDN_DK, GDN_DV)

    def ffn_up_p(hn):
        act, tg, tv = ffn_up_seq(hn, w_up_b, ffn_conv_w, B, t.rows, t.ffn_cols)
        last = lambda a: a[T // t.rows - 1::T // t.rows, SUBLANES - (FFN_CONV - 1):]
        return act, jnp.concatenate([last(tg), last(tv)], axis=-1)

    y_p, ffn_p = tail(xp, o_nsa, o_gdn, u_mg, ffn_up_p, t)
    n_win_p = min(WINDOW, T)
    kv_p = kv_rows.reshape(B, T, 4, G, HEAD_DIM)
    win_p = win_rows.reshape(B, T, 2, G, HEAD_DIM)[:, T - n_win_p:]
    conv_p = u_gqkv.reshape(B, T, -1)[:, T - (GDN_CONV - 1):]

    t = _tiles(Bd)
    xs = x_sample.reshape(Bd, D)
    xn, (u_qkv, u_gqkv, u_z, u_mg) = project(xs, t)
    u_small = matmul(xn, w_small_step, t.rows, LANES)
    cos, sin = rope_tables(jnp.full((Bd,), n_past, jnp.int32))
    q, new_rows, win_new = rope_split(u_qkv, cos, sin, t.norm_rows)
    o_nsa, win_s = nsa_step(q, new_rows, win_new, cache_nsa_kv, page_table, state_win_kv, u_small, pe2, w1c, w2c)
    o_gdn, s_s = gdn_step(u_gqkv, u_z, u_small, state_gdn_conv, state_gdn, gdn_conv_w, gdn_a_log, gdn_dt_bias, gdn_norm)

    def ffn_up_s(hn):
        act, ug, uv = ffn_up_step(hn, w_up_b, ffn_conv_w, state_ffn_conv, t.ffn_cols)
        return act, jnp.concatenate([ug, uv], axis=-1)

    y_s, up_new = tail(xs, o_nsa, o_gdn, u_mg, ffn_up_s, t)
    kv_s = new_rows.reshape(Bd, 1, 4, G, HEAD_DIM)
    conv_s = jnp.concatenate([state_gdn_conv[:, 1:], u_gqkv[:, None]], axis=1)
    ffn_s = jnp.concatenate([state_ffn_conv[:, 1:], up_new[:, None]], axis=1)

    return (y_p.reshape(B, T, D), y_s.reshape(Bd, 1, D), kv_p, win_p, s_p, conv_p, ffn_p,
            kv_s, win_s, s_s, conv_s, ffn_s)
```

```python
import functools
import math
from typing import NamedTuple

import jax
import jax.numpy as jnp
import numpy as np
from jax import lax
from jax.experimental import pallas as pl
from jax.experimental.pallas import tpu as pltpu

F32 = jnp.float32
BF16 = jnp.bfloat16
HI = lax.Precision.HIGHEST

LANES = 128
SUBLANES = 8
VMEM_LIMIT_BYTES = 56 * 1024 * 1024

HEAD_DIM = 128
NSA_HEADS = 16
NSA_KV_HEADS = 4
NSA_GROUP = NSA_HEADS // NSA_KV_HEADS
CMP_LEN = 32
CMP_STRIDE = 16
CMP_HIDDEN = 256
SLC_LEN = 64
SLC_TOP_N = 16
SLC_LOCAL = 2
WINDOW = 512
GDN_HEADS = 16
GDN_DK = 128
GDN_DV = 128
GDN_CONV = 4
GDN_CHUNK = 64
FFN_CONV = 3
ROPE_THETA = 10000.0
EPS = 1e-6


def _cparams(*sem):
    return pltpu.CompilerParams(dimension_semantics=sem, vmem_limit_bytes=VMEM_LIMIT_BYTES)


def _dot(a, b, precision=None):
    return jnp.dot(a, b, preferred_element_type=F32, precision=precision)


def _dot_nt(a, b, precision=None):
    return lax.dot_general(a, b, (((1,), (1,)), ((), ())), preferred_element_type=F32, precision=precision)


def _dot_tn(a, b, precision=None):
    return lax.dot_general(a, b, (((0,), (0,)), ((), ())), preferred_element_type=F32, precision=precision)


def _sigmoid(x):
    return 1.0 / (1.0 + jnp.exp(-x))


def _silu(x):
    return x * _sigmoid(x)


def _rmsnorm_body(x_ref, w_ref, o_ref):
    x = x_ref[...]
    y = x * lax.rsqrt(jnp.mean(x * x, axis=-1, keepdims=True) + EPS)
    o_ref[...] = (y * w_ref[...]).astype(o_ref.dtype)


def rmsnorm(x, w, out_dtype, tm):
    M, D = x.shape
    return pl.pallas_call(
        _rmsnorm_body,
        grid=(M // tm,),
        in_specs=[pl.BlockSpec((tm, D), lambda i: (i, 0)), pl.BlockSpec((1, D), lambda i: (0, 0))],
        out_specs=pl.BlockSpec((tm, D), lambda i: (i, 0)),
        out_shape=jax.ShapeDtypeStruct((M, D), out_dtype),
        compiler_params=_cparams("parallel"),
        name="rmsnorm",
    )(x, w.reshape(1, D))


def _mm_body(x_ref, w_ref, o_ref):
    o_ref[...] = _dot(x_ref[...], w_ref[...]).astype(o_ref.dtype)


def matmul(x, w, tm, tn, out_dtype=F32):
    M, K = x.shape
    N = w.shape[1]
    return pl.pallas_call(
        _mm_body,
        grid=(M // tm, N // tn),
        in_specs=[pl.BlockSpec((tm, K), lambda i, j: (i, 0)), pl.BlockSpec((K, tn), lambda i, j: (0, j))],
        out_specs=pl.BlockSpec((tm, tn), lambda i, j: (i, j)),
        out_shape=jax.ShapeDtypeStruct((M, N), out_dtype),
        compiler_params=_cparams("parallel", "arbitrary"),
        name="matmul",
    )(x, w)


def _mm_nt_body(x_ref, w_ref, o_ref):
    o_ref[...] = _dot_nt(x_ref[...], w_ref[...]).astype(o_ref.dtype)


def matmul_nt(x, w_t, tm, tn, rows, out_dtype=F32):
    M, K = x.shape
    r0, N = rows
    assert r0 % (2 * SUBLANES) == 0 and N % tn == 0
    return pl.pallas_call(
        _mm_nt_body,
        grid=(M // tm, N // tn),
        in_specs=[pl.BlockSpec((tm, K), lambda i, j: (i, 0)),
                  pl.BlockSpec((pl.Element(tn), pl.Element(K)), lambda i, j: (pl.multiple_of(r0 + j * tn, 2 * SUBLANES), 0))],
        out_specs=pl.BlockSpec((tm, tn), lambda i, j: (i, j)),
        out_shape=jax.ShapeDtypeStruct((M, N), out_dtype),
        compiler_params=_cparams("parallel", "arbitrary"),
        name="matmul_nt",
    )(x, w_t)


def _mm_res_body(x_ref, w_ref, r_ref, o_ref):
    o_ref[...] = r_ref[...] + _dot(x_ref[...], w_ref[...])


def matmul_residual(x, w, res, tm, tn, x_buffers=2):
    M, K = x.shape
    N = w.shape[1]
    return pl.pallas_call(
        _mm_res_body,
        grid=(M // tm, N // tn),
        in_specs=[pl.BlockSpec((tm, K), lambda i, j: (i, 0), pipeline_mode=pl.Buffered(x_buffers)),
                  pl.BlockSpec((K, tn), lambda i, j: (0, j)),
                  pl.BlockSpec((tm, tn), lambda i, j: (i, j))],
        out_specs=pl.BlockSpec((tm, tn), lambda i, j: (i, j)),
        out_shape=jax.ShapeDtypeStruct((M, N), F32),
        compiler_params=_cparams("parallel", "arbitrary"),
        name="matmul_residual",
    )(x, w, res)


def _mm_mix_body(a_ref, b_ref, wa_ref, wb_ref, ga_ref, gb_ref, o_ref):
    ua = _dot(a_ref[...], wa_ref[...])
    ub = _dot(b_ref[...], wb_ref[...])
    o_ref[...] = (_sigmoid(ga_ref[...]) * ua + _sigmoid(gb_ref[...]) * ub).astype(o_ref.dtype)


def matmul_mix(a, b, wa, wb, gates, tm, tn):
    M, K = a.shape
    N = wa.shape[1]
    nj = N // tn
    return pl.pallas_call(
        _mm_mix_body,
        grid=(M // tm, nj),
        in_specs=[pl.BlockSpec((tm, K), lambda i, j: (i, 0)), pl.BlockSpec((tm, K), lambda i, j: (i, 0)),
                  pl.BlockSpec((K, tn), lambda i, j: (0, j)), pl.BlockSpec((K, tn), lambda i, j: (0, j)),
                  pl.BlockSpec((tm, tn), lambda i, j: (i, j)), pl.BlockSpec((tm, tn), lambda i, j: (i, j + nj))],
        out_specs=pl.BlockSpec((tm, tn), lambda i, j: (i, j)),
        out_shape=jax.ShapeDtypeStruct((M, N), BF16),
        compiler_params=_cparams("parallel", "arbitrary"),
        name="matmul_mix",
    )(a, b, wa, wb, gates, gates)


def _shift_rows(x, hist, sh):
    s = pltpu.roll(x, sh, axis=0)
    head = s[:SUBLANES]
    row8 = lax.broadcasted_iota(jnp.int32, head.shape, 0)
    for r in range(sh):
        head = jnp.where(row8 == r, hist[SUBLANES - sh + r:SUBLANES - sh + r + 1], head)
    return jnp.concatenate([head, s[SUBLANES:]], axis=0)


def _ffn_up_seq_body(x_ref, wg_ref, wv_ref, cg_ref, cv_ref, act_ref, tg_ref, tv_ref, hg_ref, hv_ref, *, tiles_per_seq):
    i = pl.program_id(0)
    j = pl.program_id(1)
    first = (i % tiles_per_seq) == 0
    x = x_ref[...]

    def branch(w_ref, c_ref, hist_ref, tail_ref):
        up = _dot(x, w_ref[...])
        tm = up.shape[0]
        hist = jnp.where(first, 0.0, hist_ref[j])
        c = c_ref[...]
        out = up * c[2:3] + _shift_rows(up, hist, 2) * c[0:1] + _shift_rows(up, hist, 1) * c[1:2]
        tail = up[tm - SUBLANES:tm]
        hist_ref[j] = tail
        tail_ref[0] = tail
        return out

    gate = branch(wg_ref, cg_ref, hg_ref, tg_ref)
    val = branch(wv_ref, cv_ref, hv_ref, tv_ref)
    act_ref[...] = (_silu(gate) * val).astype(act_ref.dtype)


def ffn_up_seq(x, w_up, conv_w, n_seq, tm, tn):
    M, K = x.shape
    F = w_up.shape[1] // 2
    nj = F // tn
    tiles_per_seq = M // n_seq // tm
    body = functools.partial(_ffn_up_seq_body, tiles_per_seq=tiles_per_seq)
    return pl.pallas_call(
        body,
        grid=(M // tm, nj),
        in_specs=[pl.BlockSpec((tm, K), lambda i, j: (i, 0)),
                  pl.BlockSpec((K, tn), lambda i, j: (0, j)), pl.BlockSpec((K, tn), lambda i, j: (0, j + nj)),
                  pl.BlockSpec((FFN_CONV, tn), lambda i, j: (0, j)), pl.BlockSpec((FFN_CONV, tn), lambda i, j: (0, j + nj))],
        out_specs=[pl.BlockSpec((tm, tn), lambda i, j: (i, j)),
                   pl.BlockSpec((1, SUBLANES, tn), lambda i, j: (i, 0, j)),
                   pl.BlockSpec((1, SUBLANES, tn), lambda i, j: (i, 0, j))],
        out_shape=[jax.ShapeDtypeStruct((M, F), BF16),
                   jax.ShapeDtypeStruct((M // tm, SUBLANES, F), F32), jax.ShapeDtypeStruct((M // tm, SUBLANES, F), F32)],
        scratch_shapes=[pltpu.VMEM((nj, SUBLANES, tn), F32), pltpu.VMEM((nj, SUBLANES, tn), F32)],
        compiler_params=_cparams("arbitrary", "arbitrary"),
        name="ffn_up_seq",
    )(x, w_up, w_up, conv_w, conv_w)


def _ffn_up_step_body(x_ref, wg_ref, wv_ref, cg_ref, cv_ref, g0_ref, g1_ref, v0_ref, v1_ref, act_ref, ug_ref, uv_ref):
    x = x_ref[...]

    def branch(w_ref, c_ref, h0_ref, h1_ref, up_ref):
        up = _dot(x, w_ref[...])
        up_ref[...] = up
        c = c_ref[...]
        return up * c[2:3] + h0_ref[...] * c[0:1] + h1_ref[...] * c[1:2]

    gate = branch(wg_ref, cg_ref, g0_ref, g1_ref, ug_ref)
    val = branch(wv_ref, cv_ref, v0_ref, v1_ref, uv_ref)
    act_ref[...] = (_silu(gate) * val).astype(act_ref.dtype)


def ffn_up_step(x, w_up, conv_w, hist, tn):
    M, K = x.shape
    F2 = w_up.shape[1]
    F = F2 // 2
    nj = F // tn
    h2 = hist.reshape(M, 2 * F2)
    return pl.pallas_call(
        _ffn_up_step_body,
        grid=(nj,),
        in_specs=[pl.BlockSpec((M, K), lambda j: (0, 0)),
                  pl.BlockSpec((K, tn), lambda j: (0, j)), pl.BlockSpec((K, tn), lambda j: (0, j + nj)),
                  pl.BlockSpec((FFN_CONV, tn), lambda j: (0, j)), pl.BlockSpec((FFN_CONV, tn), lambda j: (0, j + nj)),
                  pl.BlockSpec((M, tn), lambda j: (0, j)), pl.BlockSpec((M, tn), lambda j: (0, j + 2 * nj)),
                  pl.BlockSpec((M, tn), lambda j: (0, j + nj)), pl.BlockSpec((M, tn), lambda j: (0, j + 3 * nj))],
        out_specs=[pl.BlockSpec((M, tn), lambda j: (0, j))] * 3,
        out_shape=[jax.ShapeDtypeStruct((M, F), BF16), jax.ShapeDtypeStruct((M, F), F32), jax.ShapeDtypeStruct((M, F), F32)],
        compiler_params=_cparams("arbitrary"),
        name="ffn_up_step",
    )(x, w_up, w_up, conv_w, conv_w, h2, h2, h2, h2)


def _rope_body(u_ref, cos_ref, sin_ref, q_ref, kv_ref, win_ref):
    cos = cos_ref[...]
    sin = sin_ref[...]
    scale = HEAD_DIM ** -0.5

    def rot(x):
        return x * cos + pltpu.roll(x, HEAD_DIM // 2, axis=1) * sin

    for h in range(NSA_HEADS):
        sl = slice(h * HEAD_DIM, (h + 1) * HEAD_DIM)
        q_ref[:, sl] = (rot(u_ref[:, sl]) * scale).astype(q_ref.dtype)
    base = NSA_HEADS * HEAD_DIM
    n_glob = 4 * NSA_KV_HEADS
    for slot in range(6 * NSA_KV_HEADS):
        x = u_ref[:, base + slot * HEAD_DIM:base + (slot + 1) * HEAD_DIM]
        if (slot // NSA_KV_HEADS) % 2 == 0:
            x = rot(x)
        if slot < n_glob:
            kv_ref[:, slot * HEAD_DIM:(slot + 1) * HEAD_DIM] = x
        else:
            win_ref[:, (slot - n_glob) * HEAD_DIM:(slot - n_glob + 1) * HEAD_DIM] = x


def rope_split(u, cos, sin, tm):
    M, W = u.shape
    nt = cos.shape[0] // tm
    nq = NSA_HEADS * HEAD_DIM
    ng = 4 * NSA_KV_HEADS * HEAD_DIM
    nw = 2 * NSA_KV_HEADS * HEAD_DIM
    return pl.pallas_call(
        _rope_body,
        grid=(M // tm,),
        in_specs=[pl.BlockSpec((tm, W), lambda i: (i, 0)),
                  pl.BlockSpec((tm, HEAD_DIM), lambda i: (i % nt, 0)), pl.BlockSpec((tm, HEAD_DIM), lambda i: (i % nt, 0))],
        out_specs=[pl.BlockSpec((tm, nq), lambda i: (i, 0)), pl.BlockSpec((tm, ng), lambda i: (i, 0)),
                   pl.BlockSpec((tm, nw), lambda i: (i, 0))],
        out_shape=[jax.ShapeDtypeStruct((M, nq), BF16), jax.ShapeDtypeStruct((M, ng), F32), jax.ShapeDtypeStruct((M, nw), F32)],
        compiler_params=_cparams("parallel"),
        name="rope_split",
    )(u, cos, sin)


GDN_HB = 4
GDN_COL_B = 0
GDN_COL_A = 8
NSA_COL_G = 16
GDN_STEP_GROUPS = 2


def _softplus(x):
    return jnp.maximum(x, 0.0) + jnp.log1p(jnp.exp(-jnp.abs(x)))


def _l2norm(x):
    return x * lax.rsqrt(jnp.sum(x * x, axis=-1, keepdims=True) + EPS)


def _solve_unit_lower(a_list, r_list, order):
    steps = int(math.log2(order))
    a_list, r_list = list(a_list), list(r_list)
    m, n = a_list[0].shape[1], r_list[0].shape[1]
    for i in range(steps):
        last = i == steps - 1
        for c in range(len(a_list)):
            a_b = a_list[c].astype(BF16)
            r_hi, r_lo = _split_bf16(r_list[c])
            prod = _dot(a_b, jnp.concatenate(([] if last else [a_b]) + [r_hi, r_lo], axis=1))
            if not last:
                a_list[c], prod = prod[:, :m], prod[:, m:]
            r_list[c] = r_list[c] + (prod[:, :n] + prod[:, n:])
    return r_list


def _block_diag(blocks):
    z = jnp.zeros_like(blocks[0])
    nb = len(blocks)
    return jnp.concatenate([jnp.concatenate([blocks[h] if j == h else z for j in range(nb)], axis=1)
                            for h in range(nb)], axis=0)


def _gdn_chunks_groups(groups, states):
    nh = len(groups[0][0][0])
    C, d = groups[0][0][2][0].shape
    W = nh * C
    stack = lambda xs: jnp.concatenate(xs, axis=0)
    ri = lax.broadcasted_iota(jnp.int32, (W, W), 0)
    ci = lax.broadcasted_iota(jnp.int32, (W, W), 1)
    same = (ri // C) == (ci // C)

    a_list, r_list, pre = [], [], []
    for chunks in groups:
        for q, k, v, beta, gc, gcr in chunks:
            dmat = jnp.where(same & (ri >= ci), jnp.exp(jnp.minimum(stack(gc) - jnp.concatenate(gcr, axis=1), 0.0)), 0.0)
            eg = [jnp.exp(g) for g in gc]
            kb = [k[h] * beta[h] for h in range(nh)]
            xk = [_dot_nt(jnp.concatenate([kb[h], q[h]], axis=0).astype(BF16), k[h].astype(BF16)) for h in range(nh)]
            kk = _block_diag([x[:C] for x in xk])
            qk = _block_diag([x[C:] for x in xk])
            a_list.append(jnp.where(ri > ci, -(kk * dmat), 0.0))
            r_list.append(jnp.concatenate([stack([v[h] * beta[h] for h in range(nh)]),
                                           stack([kb[h] * eg[h] for h in range(nh)])], axis=1))
            g_last = [g[C - 1:C, :] for g in gc]
            pre.append(((qk * dmat).astype(BF16), [(q[h] * eg[h]).astype(BF16) for h in range(nh)],
                        [(k[h] * jnp.exp(g_last[h] - gc[h])).astype(BF16) for h in range(nh)],
                        stack([jnp.broadcast_to(jnp.exp(g_last[h]), (d, 1)) for h in range(nh)])))
    w_list = _solve_unit_lower(a_list, r_list, C)

    n_chunks = len(groups[0])
    states = list(states)
    outs = [[] for _ in groups]
    for c in range(n_chunks):
        for g in range(len(groups)):
            w, (aqk, q_eg, kd, keep) = w_list[g * n_chunks + c], pre[g * n_chunks + c]
            s_b = states[g].astype(BF16)
            xs = [_dot(jnp.concatenate([w[h * C:(h + 1) * C, d:].astype(BF16), q_eg[h]], axis=0), s_b[h * d:(h + 1) * d])
                  for h in range(nh)]
            v_new = (w[:, :d] - stack([x[:C] for x in xs])).astype(BF16)
            outs[g].append(stack([x[C:] for x in xs]) + _dot(aqk, v_new))
            upd = stack([_dot_tn(kd[h], v_new[h * C:(h + 1) * C]) for h in range(nh)])
            states[g] = states[g] * keep + upd
    return outs, states


def _gdn_seq_body(q_ref, k_ref, v_ref, z_ref, ba_ref, cq_ref, ck_ref, cv_ref, alog_ref, dtb_ref, nw_ref,
                  o_ref, s_ref, tq_ref, tk_ref, tv_ref, qs_ref, ks_ref, vs_ref, gs_ref, bs_ref):
    t_idx = pl.program_id(2)
    tt = q_ref.shape[0]
    hb = q_ref.shape[1] // GDN_DK
    C = GDN_CHUNK

    @pl.when(t_idx == 0)
    def _():
        s_ref[...] = jnp.zeros_like(s_ref)
        tq_ref[...] = jnp.zeros_like(tq_ref)
        tk_ref[...] = jnp.zeros_like(tk_ref)
        tv_ref[...] = jnp.zeros_like(tv_ref)

    def conv(x_ref, c_ref, tail_ref):
        x = x_ref[...]
        hist = tail_ref[...]
        row = lax.broadcasted_iota(jnp.int32, x.shape, 0)
        c = c_ref[...]
        out = x * c[GDN_CONV - 1:GDN_CONV]
        for j in range(GDN_CONV - 1):
            out = out + _shift_rows(x, hist, GDN_CONV - 1 - j) * c[j:j + 1]
        tail_ref[...] = x[tt - SUBLANES:tt]
        return _silu(out)

    qa = conv(q_ref, cq_ref, tq_ref)
    ka = conv(k_ref, ck_ref, tk_ref)
    vs_ref[...] = conv(v_ref, cv_ref, tv_ref)
    for h in range(hb):
        sl = slice(h * GDN_DK, (h + 1) * GDN_DK)
        qs_ref[:, sl] = _l2norm(qa[:, sl]) * GDN_DK ** -0.5
        ks_ref[:, sl] = _l2norm(ka[:, sl])
    ba = ba_ref[...]
    bs_ref[...] = _sigmoid(ba)
    n_groups = hb // GDN_HB
    par_row = lambda ref: jnp.concatenate([ref[g, 0:1, :] for g in range(n_groups)], axis=1)
    gs_ref[...] = -jnp.exp(par_row(alog_ref)) * _softplus(ba + par_row(dtb_ref))

    tri = (lax.broadcasted_iota(jnp.int32, (C, C), 0) >= lax.broadcasted_iota(jnp.int32, (C, C), 1)).astype(F32)
    sel = (lax.broadcasted_iota(jnp.int32, (2 * SUBLANES, LANES), 0)
           == lax.broadcasted_iota(jnp.int32, (2 * SUBLANES, LANES), 1)).astype(F32)
    nw = nw_ref[...]

    chunk_rows = [slice(c * C, (c + 1) * C) for c in range(tt // C)]
    gc_alls = [_dot(tri, gs_ref[rows, :], HI) for rows in chunk_rows]
    groups, group_heads = [], []
    for g in range(n_groups):
        heads = [slice((g * GDN_HB + h) * GDN_DK, (g * GDN_HB + h + 1) * GDN_DK) for h in range(GDN_HB)]
        lanes = slice(g * LANES, (g + 1) * LANES)
        chunks = []
        for rows, gc_full in zip(chunk_rows, gc_alls):
            gc_all = gc_full[:, lanes]
            gc_t = _dot_nt(sel, gc_all, HI)
            beta_all = bs_ref[rows, lanes]
            chunks.append((
                [qs_ref[rows, sl] for sl in heads], [ks_ref[rows, sl] for sl in heads], [vs_ref[rows, sl] for sl in heads],
                [beta_all[:, GDN_COL_B + h:GDN_COL_B + h + 1] for h in range(GDN_HB)],
                [gc_all[:, GDN_COL_A + h:GDN_COL_A + h + 1] for h in range(GDN_HB)],
                [gc_t[GDN_COL_A + h:GDN_COL_A + h + 1, :] for h in range(GDN_HB)]))
        groups.append(chunks)
        group_heads.append(heads)
    gdk = GDN_HB * GDN_DK
    outs, states = _gdn_chunks_groups(groups, [s_ref[0, g * gdk:(g + 1) * gdk] for g in range(n_groups)])
    for g in range(n_groups):
        s_ref[0, g * gdk:(g + 1) * gdk] = states[g]
        for rows, o in zip(chunk_rows, outs[g]):
            on = o * lax.rsqrt(jnp.mean(o * o, axis=-1, keepdims=True) + EPS) * nw
            for h, sl in enumerate(group_heads[g]):
                o_ref[rows, sl] = (on[h * C:(h + 1) * C] * _silu(z_ref[rows, sl])).astype(o_ref.dtype)


def gdn_seq(u_qkv, u_z, u_small, conv_w, alog_rows, dtb_rows, norm_w, n_seq, tt):
    M = u_qkv.shape[0]
    T = M // n_seq
    nt = T // tt
    hb = GDN_HB * GDN_STEP_GROUPS
    nhb = GDN_HEADS // hb
    wb = hb * GDN_DK
    gl = GDN_STEP_GROUPS * LANES
    row_blk = lambda off: pl.BlockSpec((tt, wb), lambda b, h, t: (b * nt + t, h + off))
    cw_blk = lambda off: pl.BlockSpec((GDN_CONV, wb), lambda b, h, t: (0, h + off))
    par_blk = pl.BlockSpec((GDN_STEP_GROUPS, SUBLANES, LANES), lambda b, h, t: (h, 0, 0))
    return pl.pallas_call(
        _gdn_seq_body,
        grid=(n_seq, nhb, nt),
        in_specs=[row_blk(0), row_blk(nhb), row_blk(2 * nhb), row_blk(0),
                  pl.BlockSpec((tt, gl), lambda b, h, t: (b * nt + t, h)),
                  cw_blk(0), cw_blk(nhb), cw_blk(2 * nhb), par_blk, par_blk,
                  pl.BlockSpec((1, GDN_DV), lambda b, h, t: (0, 0))],
        out_specs=[row_blk(0), pl.BlockSpec((1, hb * GDN_DK, GDN_DV), lambda b, h, t: (b, h, 0))],
        out_shape=[jax.ShapeDtypeStruct((M, GDN_HEADS * GDN_DV), BF16),
                   jax.ShapeDtypeStruct((n_seq, GDN_HEADS * GDN_DK, GDN_DV), F32)],
        scratch_shapes=[pltpu.VMEM((SUBLANES, wb), F32)] * 3 + [pltpu.VMEM((tt, wb), F32)] * 3
                       + [pltpu.VMEM((tt, gl), F32)] * 2,
        compiler_params=_cparams("parallel", "parallel", "arbitrary"),
        name="gdn_seq",
    )(u_qkv, u_qkv, u_qkv, u_z, u_small, conv_w, conv_w, conv_w, alog_rows, dtb_rows, norm_w.reshape(1, GDN_DV))


def gdn_param_rows(gdn_a_log, gdn_dt_bias):
    nhb = GDN_HEADS // GDN_HB

    def rows(p):
        r = _place_lanes(((GDN_COL_A, p.reshape(nhb, GDN_HB)),), (nhb,))
        return jnp.broadcast_to(r[:, None, :], (nhb, SUBLANES, LANES))

    return rows(gdn_a_log), rows(gdn_dt_bias)


CMP_RATIO = CMP_LEN // CMP_STRIDE
CMP_FEAT = CMP_STRIDE * HEAD_DIM
MASKED = -1e30


def _compress_mlp(part, pe_ref, w1_ref, w2_ref, chunk_step=1):
    n = part.shape[0]
    pe_part = _dot(pe_ref[...], w1_ref[...])
    hid0 = pe_part[0:1, :CMP_HIDDEN] + pe_part[1:2, CMP_HIDDEN:]
    hid = hid0 + part[:, :CMP_HIDDEN]
    hid = hid + pltpu.roll(part[:, CMP_HIDDEN:], n - chunk_step, axis=0)
    return _dot(_silu(hid).astype(BF16), w2_ref[...])


def _compress_seq_body(x_ref, pe_ref, w1_ref, w2_ref, o_ref, xc_ref):
    n = o_ref.shape[3]
    for s in range(CMP_STRIDE):
        xc_ref[:, s * HEAD_DIM:(s + 1) * HEAD_DIM] = x_ref[pl.ds(s, n, stride=CMP_STRIDE), :].astype(BF16)
    part = _dot(xc_ref[...], w1_ref[0])
    o_ref[0, 0, 0] = _compress_mlp(part, pe_ref.at[0], w1_ref.at[0], w2_ref.at[0])


def compress_seq(kv_rows, pe2, w1c, w2c, n_seq):
    M = kv_rows.shape[0]
    T = M // n_seq
    n = T // CMP_STRIDE
    G = NSA_KV_HEADS
    return pl.pallas_call(
        _compress_seq_body,
        grid=(n_seq, 2, G),
        in_specs=[pl.BlockSpec((T, HEAD_DIM), lambda b, c, g: (b, c * G + g)),
                  pl.BlockSpec((1, SUBLANES, CMP_FEAT), lambda b, c, g: (c, 0, 0)),
                  pl.BlockSpec((1, CMP_FEAT, 2 * CMP_HIDDEN), lambda b, c, g: (c, 0, 0)),
                  pl.BlockSpec((1, CMP_HIDDEN, HEAD_DIM), lambda b, c, g: (c, 0, 0))],
        out_specs=pl.BlockSpec((1, 1, 1, n, HEAD_DIM), lambda b, c, g: (b, c, g, 0, 0)),
        out_shape=jax.ShapeDtypeStruct((n_seq, 2, G, n, HEAD_DIM), F32),
        scratch_shapes=[pltpu.VMEM((n, CMP_FEAT), BF16)],
        compiler_params=_cparams("parallel", "parallel", "parallel"),
        name="compress_seq",
    )(kv_rows, pe2, w1c, w2c)


def compress_params(cmp_pe, cmp_w1, cmp_w2):
    w1 = cmp_w1.reshape(2, CMP_RATIO, CMP_FEAT, CMP_HIDDEN)
    w1c = jnp.concatenate([w1[:, r] for r in range(CMP_RATIO)], axis=-1).astype(BF16)
    pe = cmp_pe.reshape(2, CMP_RATIO, CMP_FEAT)
    pe2 = jnp.zeros((2, SUBLANES, CMP_FEAT), F32).at[:, :CMP_RATIO].set(pe).astype(BF16)
    return pe2, w1c, cmp_w2.astype(BF16)


def _masked_softmax(s, mask, axis=-1):
    s = jnp.where(mask, s, -jnp.inf)
    m = jnp.max(s, axis=axis, keepdims=True)
    m = jnp.where(m > -jnp.inf, m, 0.0)
    e = jnp.exp(s - m)
    return e * (1.0 / jnp.maximum(jnp.sum(e, axis=axis, keepdims=True), 1e-30))


def _masked_softmax_heads(s, mask, n_heads):
    t = mask.shape[0]
    return jnp.concatenate([_masked_softmax(s[h * t:(h + 1) * t], mask) for h in range(n_heads)], axis=0)


def _split_bf16(x):
    hi = x.astype(BF16)
    return hi, (x - hi.astype(F32)).astype(BF16)


def _select_blocks_t(imp_t, pos_row, n_blocks):
    J = imp_t.shape[0]
    j = lax.broadcasted_iota(jnp.int32, imp_t.shape, 0)
    cur = pos_row // SLC_LEN
    forced = (j == 0) | ((j <= cur) & (j > cur - SLC_LOCAL))
    score = jnp.where(j * SLC_LEN > pos_row, -jnp.inf, jnp.where(forced, jnp.inf, imp_t))
    score = jnp.where(j < n_blocks, score, -jnp.inf)
    tiles = [score[r:r + SUBLANES] for r in range(0, J, SUBLANES)]
    ranks = [jnp.zeros(t.shape, F32) for t in tiles]
    for jp in range(n_blocks):
        row = score[jp:jp + 1, :]
        for i, t in enumerate(tiles):
            lo_j = i * SUBLANES
            ge, gt = jnp.where(row >= t, 1.0, 0.0), jnp.where(row > t, 1.0, 0.0)
            if lo_j > jp:
                ahead = ge
            elif lo_j + SUBLANES - 1 <= jp:
                ahead = gt
            else:
                ahead = jnp.where(lax.broadcasted_iota(jnp.int32, t.shape, 0) + lo_j > jp, ge, gt)
            ranks[i] = ranks[i] + ahead
    rank = jnp.concatenate(ranks, axis=0)
    keep = (rank < float(min(SLC_TOP_N, n_blocks))) & (j < n_blocks)
    return jnp.where(keep, 1.0, 0.0)


def _nsa_seq_body(q_ref, kc_ref, vc_ref, ks_ref, vs_ref, kw_ref, vw_ref, gl_ref, ovl_ref, o_ref, m_ref, l_ref, acc_ref, *, kblk):
    i = pl.program_id(2)
    tq = q_ref.shape[0]
    T = ks_ref.shape[0]
    H = NSA_GROUP
    D = HEAD_DIM
    n_slc = T // SLC_LEN
    qb = q_ref[...]
    q4 = jnp.concatenate([qb[:, h * D:(h + 1) * D] for h in range(H)], axis=0)
    t0 = i * tq
    pos1 = t0 + lax.broadcasted_iota(jnp.int32, (tq, 1), 0)

    kc = kc_ref[0, 0, 0].astype(BF16)
    vc = vc_ref[0, 0, 0].astype(BF16)
    n_c = kc.shape[0]
    s = _dot_nt(q4, kc)
    cmp_end = lax.broadcasted_iota(jnp.int32, (1, n_c), 1) * CMP_STRIDE + (CMP_LEN - 1)
    p = _masked_softmax_heads(s, cmp_end <= pos1, H)
    o_c = _dot(p.astype(BF16), vc)

    psum = p[0:tq]
    for h in range(1, H):
        psum = psum + p[h * tq:(h + 1) * tq]
    p_hi, p_lo = _split_bf16(psum)
    ovl = ovl_ref[...]
    imp_t = _dot_nt(ovl, p_hi) + _dot_nt(ovl, p_lo)
    pos_row = t0 + lax.broadcasted_iota(jnp.int32, (1, tq), 1)
    sel_t = _select_blocks_t(imp_t, pos_row, n_slc)
    eye = (lax.broadcasted_iota(jnp.int32, (tq, tq), 0) == lax.broadcasted_iota(jnp.int32, (tq, tq), 1))
    sel = _dot_nt(jnp.where(eye, 1.0, 0.0).astype(BF16), sel_t.astype(BF16)).astype(BF16)

    n_kb = (t0 + tq + kblk - 1) // kblk

    m_ref[...] = jnp.full(m_ref.shape, MASKED, F32)
    l_ref[...] = jnp.zeros(l_ref.shape, F32)
    acc_ref[...] = jnp.zeros(acc_ref.shape, F32)

    def kv_step(kb, carry):
        k0 = pl.multiple_of(kb * kblk, kblk)
        kk = ks_ref[pl.ds(k0, kblk), :].astype(BF16)
        vv = vs_ref[pl.ds(k0, kblk), :].astype(BF16)
        s2 = _dot_nt(q4, kk)
        kpos = k0 + lax.broadcasted_iota(jnp.int32, (1, kblk), 1)
        expand = (lax.broadcasted_iota(jnp.int32, (n_slc, kblk), 0) == kpos // SLC_LEN)
        chosen = _dot(sel, jnp.where(expand, 1.0, 0.0).astype(BF16))
        bias = jnp.where((chosen > 0.5) & (kpos <= pos1), 0.0, MASKED)
        es = []
        for h in range(H):
            rows = slice(h * tq, (h + 1) * tq)
            sh = s2[rows] + bias
            m_old = m_ref[rows]
            m_new = jnp.maximum(m_old, jnp.max(sh, axis=-1, keepdims=True))
            alpha = jnp.exp(m_old - m_new)
            e = jnp.exp(sh - m_new)
            m_ref[rows] = m_new
            l_ref[rows] = alpha * l_ref[rows] + jnp.sum(e, axis=-1, keepdims=True)
            acc_ref[rows] = alpha * acc_ref[rows]
            es.append(e.astype(BF16))
        acc_ref[...] += _dot(jnp.concatenate(es, axis=0), vv)
        return carry

    lax.fori_loop(0, n_kb, kv_step, 0)
    o_s = acc_ref[...] * (1.0 / jnp.maximum(l_ref[...], 1e-30))

    band = WINDOW + tq
    w0 = pl.multiple_of(jnp.maximum(t0 - WINDOW, 0), tq)
    kw = kw_ref[pl.ds(w0, band), :].astype(BF16)
    vw = vw_ref[pl.ds(w0, band), :].astype(BF16)
    sw = _dot_nt(q4, kw)
    diff = pos1 - (w0 + lax.broadcasted_iota(jnp.int32, (1, band), 1))
    pw = _masked_softmax_heads(sw, (diff >= 0) & (diff < WINDOW), H)
    o_w = _dot(pw.astype(BF16), vw)

    gates = _sigmoid(gl_ref[...])
    for h in range(H):
        rows = slice(h * tq, (h + 1) * tq)
        g = lambda br: gates[:, NSA_COL_G + H * br + h:NSA_COL_G + H * br + h + 1]
        o_ref[:, h * D:(h + 1) * D] = (g(0) * o_c[rows] + g(1) * o_s[rows] + g(2) * o_w[rows]).astype(o_ref.dtype)


def _overlap_t(n_cmp_rows, n_slc):
    start = np.arange(n_cmp_rows)[None, :] * CMP_STRIDE
    blk = np.arange(n_slc)[:, None] * SLC_LEN
    return ((start < blk + SLC_LEN) & (start + CMP_LEN > blk)).astype(np.float32)


def nsa_seq(q, kv_rows, win_rows, cmp_kv, u_small, n_seq, tq=256, kblk=1024):
    M = q.shape[0]
    T = M // n_seq
    nt = T // tq
    G = NSA_KV_HEADS
    gw = NSA_GROUP * HEAD_DIM
    n_c = cmp_kv.shape[3]
    ovl = jnp.asarray(_overlap_t(n_c, T // SLC_LEN), BF16)
    body = functools.partial(_nsa_seq_body, kblk=kblk)
    seq_blk = lambda off: pl.BlockSpec((T, HEAD_DIM), lambda b, g, i: (b, g + off))
    cmp_blk = lambda c: pl.BlockSpec((1, 1, 1, n_c, HEAD_DIM), lambda b, g, i: (b, c, g, 0, 0))
    return pl.pallas_call(
        body,
        grid=(n_seq, G, nt),
        in_specs=[pl.BlockSpec((tq, gw), lambda b, g, i: (b * nt + i, g)),
                  cmp_blk(0), cmp_blk(1), seq_blk(2 * G), seq_blk(3 * G), seq_blk(0), seq_blk(G),
                  pl.BlockSpec((tq, LANES), lambda b, g, i: (b * nt + i, g)),
                  pl.BlockSpec(ovl.shape, lambda b, g, i: (0, 0))],
        out_specs=pl.BlockSpec((tq, gw), lambda b, g, i: (b * nt + i, g)),
        out_shape=jax.ShapeDtypeStruct((M, NSA_HEADS * HEAD_DIM), BF16),
        scratch_shapes=[pltpu.VMEM((NSA_GROUP * tq, 1), F32)] * 2 + [pltpu.VMEM((NSA_GROUP * tq, HEAD_DIM), F32)],
        compiler_params=_cparams("parallel", "parallel", "arbitrary"),
        name="nsa_seq",
    )(q, cmp_kv, cmp_kv, kv_rows, kv_rows, win_rows, win_rows, u_small, ovl)


STEP_COL_G = 0
STEP_COL_B = 3 * NSA_HEADS
STEP_COL_A = STEP_COL_B + GDN_HEADS


def _gdn_step_prep_body(x_ref, h_ref, c_ref, us_ref, alog_ref, dtb_ref, q_ref, k_ref, v_ref, b_ref, e_ref):
    W = x_ref.shape[1]
    c = c_ref[...]
    x = x_ref[...] * c[GDN_CONV - 1:GDN_CONV]
    for j in range(GDN_CONV - 1):
        x = x + h_ref[:, j * W:(j + 1) * W] * c[j:j + 1]
    x = _silu(x)
    us = us_ref[...]
    beta = _sigmoid(us)
    eg = jnp.exp(-jnp.exp(alog_ref[...]) * _softplus(us + dtb_ref[...]))
    nq = GDN_HEADS * GDN_DK
    for h in range(GDN_HEADS):
        sl = slice(h * GDN_DK, (h + 1) * GDN_DK)
        q_ref[:, sl] = _l2norm(x[:, sl]) * GDN_DK ** -0.5
        k_ref[:, sl] = _l2norm(x[:, nq + h * GDN_DK:nq + (h + 1) * GDN_DK])
        b_ref[:, sl] = jnp.broadcast_to(beta[:, STEP_COL_B + h:STEP_COL_B + h + 1], (x.shape[0], GDN_DK))
        e_ref[:, sl] = jnp.broadcast_to(eg[:, STEP_COL_A + h:STEP_COL_A + h + 1], (x.shape[0], GDN_DK))
    v_ref[...] = x[:, 2 * nq:]


def _gdn_step_body(q_ref, k_ref, v_ref, b_ref, e_ref, z_ref, nw_ref, s_ref, o_ref, so_ref):
    bb = q_ref.shape[0]
    eye = jnp.where(lax.broadcasted_iota(jnp.int32, (GDN_DK, GDN_DK), 0)
                    == lax.broadcasted_iota(jnp.int32, (GDN_DK, GDN_DK), 1), 1.0, 0.0)
    nw = nw_ref[...]

    def per_seq(bi, carry):
        q, k, v, beta, eg, z = q_ref[bi], k_ref[bi], v_ref[bi], b_ref[bi], e_ref[bi], z_ref[bi]
        k_t = _dot_nt(eye, k, HI)
        q_t = _dot_nt(eye, q, HI)
        outs = []
        for h in range(GDN_HEADS):
            r = slice(h, h + 1)
            s = s_ref[bi, h]
            kcol = k_t[:, h:h + 1]
            k_s = jnp.sum(kcol * s, axis=0, keepdims=True)
            q_s = jnp.sum(q_t[:, h:h + 1] * s, axis=0, keepdims=True)
            v_new = v[r] * beta[r] - (beta[r] * eg[r]) * k_s
            qk = jnp.sum(q[r] * k[r], axis=-1, keepdims=True)
            o = eg[r] * q_s + qk * v_new
            so_ref[bi, h] = s * eg[r] + kcol * v_new
            on = o * lax.rsqrt(jnp.mean(o * o, axis=-1, keepdims=True) + EPS) * nw
            outs.append(on * _silu(z[r]))
        o_ref[bi] = jnp.concatenate(outs, axis=0).astype(o_ref.dtype)
        return carry

    lax.fori_loop(0, bb, per_seq, 0)


def gdn_step(u_qkv, u_z, u_small, hist, state, conv_w, a_log, dt_bias, norm_w, bb=4):
    Bd, W = u_qkv.shape
    H = GDN_HEADS
    nq = H * GDN_DK
    row = lambda p: _place_lanes(((STEP_COL_A, p.reshape(1, H)),), (1,))
    full = lambda shape: pl.BlockSpec(shape, lambda i: (0,) * len(shape))
    outs = pl.pallas_call(
        _gdn_step_prep_body,
        grid=(1,),
        in_specs=[full((Bd, W)), full((Bd, (GDN_CONV - 1) * W)), full((GDN_CONV, W)), full((Bd, LANES)),
                  full((1, LANES)), full((1, LANES))],
        out_specs=[full((Bd, nq))] * 5,
        out_shape=[jax.ShapeDtypeStruct((Bd, nq), F32)] * 5,
        compiler_params=_cparams("arbitrary"),
        name="gdn_step_prep",
    )(u_qkv, hist.reshape(Bd, (GDN_CONV - 1) * W), conv_w, u_small, row(a_log), row(dt_bias))
    heads = lambda a: a.reshape(Bd, H, GDN_DK)
    vec_blk = pl.BlockSpec((bb, H, GDN_DK), lambda i: (i, 0, 0))
    st_blk = pl.BlockSpec((bb, H, GDN_DK, GDN_DV), lambda i: (i, 0, 0, 0))
    o, s_new = pl.pallas_call(
        _gdn_step_body,
        grid=(Bd // bb,),
        in_specs=[vec_blk] * 6 + [pl.BlockSpec((1, GDN_DV), lambda i: (0, 0)), st_blk],
        out_specs=[vec_blk, st_blk],
        out_shape=[jax.ShapeDtypeStruct((Bd, H, GDN_DV), BF16), jax.ShapeDtypeStruct(state.shape, F32)],
        compiler_params=_cparams("parallel"),
        name="gdn_step",
    )(*[heads(a) for a in outs], heads(u_z), norm_w.reshape(1, GDN_DV), state)
    return o.reshape(Bd, H * GDN_DV), s_new


def _softmax_with_new(s_past, valid, s_new):
    s_past = jnp.where(valid, s_past, -jnp.inf)
    m = jnp.maximum(jnp.max(s_past, axis=-1, keepdims=True), s_new)
    e = jnp.exp(s_past - m)
    e_new = jnp.exp(s_new - m)
    return e, e_new, 1.0 / (jnp.sum(e, axis=-1, keepdims=True) + e_new)


def _nsa_step_body(pt_ref, *refs, n_pages, page, n_past):
    lo = refs[:n_pages]
    hi = refs[n_pages:2 * n_pages]
    (q_ref, new_ref, win_ref, wnew_ref, gl_ref, pe_ref, w1_ref, w2_ref, ovl_ref, o_ref, wout_ref, xk_ref, xv_ref) = refs[2 * n_pages:]
    G, H, D = NSA_KV_HEADS, NSA_HEADS, HEAD_DIM
    HS = 2 * G
    pos = n_past
    cpp = page // CMP_STRIDE
    n_chunk = n_pages * cpp
    L = n_pages * page
    q = q_ref[0]
    qf = q.astype(F32)
    new = new_ref[0]
    row_g = lax.broadcasted_iota(jnp.int32, (H, 1), 0) // NSA_GROUP

    def per_head_rows(rows):
        return jnp.concatenate([jnp.broadcast_to(rows[g:g + 1], (NSA_GROUP, rows.shape[1])) for g in range(G)], axis=0)

    first_half = lax.broadcasted_iota(jnp.int32, (SUBLANES, D), 0) < G
    for j in range(n_pages):
        for s in range(CMP_STRIDE):
            for m in range(cpp // 4):
                ks, vs = [], []
                for pair in range(2):
                    ta = lo[j][0, (4 * m + 2 * pair) * CMP_STRIDE + s]
                    tb = lo[j][0, (4 * m + 2 * pair + 1) * CMP_STRIDE + s]
                    ks.append(jnp.where(first_half, ta, pltpu.roll(tb, G, axis=0)))
                    vs.append(jnp.where(first_half, pltpu.roll(ta, G, axis=0), tb))
                r0 = (j * cpp + 4 * m) * G
                xk_ref[r0:r0 + 4 * G, s * D:(s + 1) * D] = jnp.concatenate(ks, axis=0).astype(BF16)
                xv_ref[r0:r0 + 4 * G, s * D:(s + 1) * D] = jnp.concatenate(vs, axis=0).astype(BF16)
    ckv = []
    for c, x_ref in enumerate((xk_ref, xv_ref)):
        part = _dot(x_ref[...], w1_ref[c])
        ckv.append(_compress_mlp(part, pe_ref.at[c], w1_ref.at[c], w2_ref.at[c], G).astype(BF16))

    n_all = G * n_chunk
    s = _dot_nt(q, ckv[0])
    col = lax.broadcasted_iota(jnp.int32, (1, n_all), 1)
    blk = col // G
    ok = (col % G == row_g) & (blk * CMP_STRIDE + (CMP_LEN - 1) <= pos) & (blk < n_chunk - CMP_RATIO + 1)
    p = _masked_softmax(s, ok)
    o_c = _dot(p.astype(BF16), ckv[1])

    gsum = jnp.where(lax.broadcasted_iota(jnp.int32, (LANES, H), 0) == lax.broadcasted_iota(jnp.int32, (LANES, H), 1) // NSA_GROUP, 1.0, 0.0)
    psum = _dot(gsum, p, HI)
    p_hi, p_lo = _split_bf16(psum)
    ovl = ovl_ref[...]
    imp_t = _dot_nt(ovl, p_hi) + _dot_nt(ovl, p_lo)
    n_slc = -(-(L + 1) // SLC_LEN)
    sel_t = _select_blocks_t(imp_t, jnp.full((1, LANES), pos, jnp.int32), n_slc)
    J = sel_t.shape[0]
    eye8 = jnp.where(lax.broadcasted_iota(jnp.int32, (SUBLANES, LANES), 0) == lax.broadcasted_iota(jnp.int32, (SUBLANES, LANES), 1), 1.0, 0.0)
    sel = _dot_nt(eye8.astype(BF16), sel_t.astype(BF16))

    def slot_softmax(sc, tok_ok, s_new):
        slot = lax.broadcasted_iota(jnp.int32, (1, sc.shape[1]), 1) % HS
        return _softmax_with_new(sc, (slot == row_g) & tok_ok, s_new)

    tok = lax.broadcasted_iota(jnp.int32, (1, L * HS), 1) // HS
    chosen = per_head_rows(jnp.concatenate(
        [jnp.broadcast_to(sel[:, b:b + 1], (SUBLANES, SLC_LEN * HS)) for b in range(L // SLC_LEN)], axis=1))
    kv_hi = [hi[j][0].reshape(page * HS, D).astype(BF16) for j in range(n_pages)]
    sc = jnp.concatenate([_dot_nt(q, kv) for kv in kv_hi], axis=1)
    s_new = jnp.sum(qf * per_head_rows(new[2 * G:3 * G]), axis=-1, keepdims=True)
    e, e_new, inv = slot_softmax(sc, (chosen > 0.5) & (tok <= pos), s_new)
    e = pltpu.roll(e, G, axis=1).astype(BF16)
    acc = jnp.zeros((H, D), F32)
    for j, kv in enumerate(kv_hi):
        acc = acc + _dot(e[:, j * page * HS:(j + 1) * page * HS], kv)
    o_s = (acc + e_new * per_head_rows(new[3 * G:4 * G])) * inv

    n_win = win_ref.shape[1] // HS
    wnew = wnew_ref[0]
    win = win_ref[0].astype(BF16)
    wtok = pos - n_win + lax.broadcasted_iota(jnp.int32, (1, n_win * HS), 1) // HS
    sw_new = jnp.sum(qf * per_head_rows(wnew[0:G]), axis=-1, keepdims=True)
    ew, ew_new, winv = slot_softmax(_dot_nt(q, win), (pos - wtok < WINDOW) & (wtok >= 0), sw_new)
    o_w = (_dot(pltpu.roll(ew, G, axis=1).astype(BF16), win) + ew_new * per_head_rows(wnew[G:2 * G])) * winv

    gl = jnp.broadcast_to(_sigmoid(gl_ref[0]), (SUBLANES, LANES))
    hh = lax.broadcasted_iota(jnp.int32, (H, LANES), 0)
    cc = lax.broadcasted_iota(jnp.int32, (H, LANES), 1)
    gate = lambda br: _dot_nt(jnp.where(cc == STEP_COL_G + br * H + hh, 1.0, 0.0), gl, HI)[:, 0:1]
    o_ref[0] = (gate(0) * o_c + gate(1) * o_s + gate(2) * o_w).astype(o_ref.dtype)

    wout_ref[0, :(n_win - 1) * HS] = win_ref[0, HS:]
    wout_ref[0, (n_win - 1) * HS:] = wnew


def nsa_step(q, new_rows, win_new, cache, page_table, win_state, u_small, pe2, w1c, w2c):
    Bd = q.shape[0]
    n_pool, page = cache.shape[:2]
    n_pages = page_table.shape[1]
    G, H, D = NSA_KV_HEADS, NSA_HEADS, HEAD_DIM
    n_past = n_pages * page
    n_chunk = n_past // CMP_STRIDE
    n_win = win_state.shape[1]
    assert n_past % SLC_LEN == 0 and n_win == WINDOW <= n_past and page % (4 * CMP_STRIDE) == 0 and 2 * G == SUBLANES
    n_slc = -(-(n_past + 1) // SLC_LEN)
    J = -(-n_slc // (2 * SUBLANES)) * (2 * SUBLANES)
    ovl = np.zeros((J, n_chunk * G), np.float32)
    ovl[:n_slc] = np.repeat(_overlap_t(n_chunk, n_slc), G, axis=1)
    ovl = jnp.asarray(ovl, BF16)
    body = functools.partial(_nsa_step_body, n_pages=n_pages, page=page, n_past=n_past)
    half_spec = lambda j, half: pl.BlockSpec((1, page, None, 2 * G, D), lambda b, pt: (pt[b, j], 0, half, 0, 0))
    const = lambda a: pl.BlockSpec(a.shape, lambda b, pt: (0,) * a.ndim, pipeline_mode=pl.Buffered(1))
    per_b = lambda shape: pl.BlockSpec((1,) + shape, lambda b, pt: (b,) + (0,) * len(shape))
    cache5 = cache.reshape(n_pool, page, 2, 2 * G, D)
    out, win_out = pl.pallas_call(
        body,
        grid_spec=pltpu.PrefetchScalarGridSpec(
            num_scalar_prefetch=1,
            grid=(Bd,),
            in_specs=[half_spec(j, 0) for j in range(n_pages)] + [half_spec(j, 1) for j in range(n_pages)]
                     + [per_b((H, D)), per_b((4 * G, D)), per_b((n_win * 2 * G, D)), per_b((2 * G, D)), per_b((1, LANES)),
                        const(pe2), const(w1c), const(w2c), const(ovl)],
            out_specs=[per_b((H, D)), per_b((n_win * 2 * G, D))],
            scratch_shapes=[pltpu.VMEM((G * n_chunk, CMP_FEAT), BF16)] * 2,
        ),
        out_shape=[jax.ShapeDtypeStruct((Bd, H, D), BF16), jax.ShapeDtypeStruct((Bd, n_win * 2 * G, D), F32)],
        compiler_params=_cparams("arbitrary"),
        name="nsa_step",
    )(page_table, *([cache5] * (2 * n_pages)), q.reshape(Bd, H, D), new_rows.reshape(Bd, 4 * G, D),
      win_state.reshape(Bd, n_win * 2 * G, D), win_new.reshape(Bd, 2 * G, D), u_small.reshape(Bd, 1, LANES),
      pe2, w1c, w2c, ovl)
    return out.reshape(Bd, H * D), win_out.reshape(win_state.shape)


def rope_tables(pos):
    half = HEAD_DIM // 2
    inv = ROPE_THETA ** (-jnp.arange(half, dtype=F32) / half)
    ang = pos.astype(F32)[:, None] * inv[None, :]
    cos, sin = jnp.cos(ang), jnp.sin(ang)
    return jnp.concatenate([cos, cos], axis=-1), jnp.concatenate([-sin, sin], axis=-1)


def _arrange_w_in(w_in):
    H, G = NSA_HEADS, NSA_KV_HEADS
    sizes = (H * HEAD_DIM, 6 * G * HEAD_DIM, 3 * H, GDN_HEADS * (2 * GDN_DK + GDN_DV), GDN_HEADS * GDN_DV, 2 * GDN_HEADS)
    o = np.cumsum((0,) + sizes).tolist()
    w_t = w_in.T
    K = w_in.shape[0]
    wide = ((o[0], o[2] - o[0]), (o[3], o[4] - o[3]), (o[4], o[5] - o[4]), (o[6], w_t.shape[0] - o[6]))
    w_ng = w_t[o[2]:o[3]].T
    w_ba = w_t[o[5]:o[6]].T
    nhb = GDN_HEADS // GDN_HB
    ng = w_ng.reshape(K, 3, G, NSA_GROUP).transpose(0, 2, 1, 3).reshape(K, G, 3 * NSA_GROUP)
    pieces = ((GDN_COL_B, w_ba[:, :GDN_HEADS].reshape(K, nhb, GDN_HB)),
              (GDN_COL_A, w_ba[:, GDN_HEADS:].reshape(K, nhb, GDN_HB)), (NSA_COL_G, ng))
    w_small_seq = _place_lanes(pieces, (K, nhb)).reshape(K, nhb * LANES).astype(BF16)
    w_small_step = _place_lanes(((STEP_COL_G, w_ng), (STEP_COL_B, w_ba)), (K,)).astype(BF16)
    return w_t.astype(BF16), wide, w_small_seq, w_small_step


def _place_lanes(pieces, lead):
    out, at = [], 0
    for lane, vals in pieces:
        out += [jnp.zeros(lead + (lane - at,), F32), vals.astype(F32)]
        at = lane + vals.shape[-1]
    return jnp.concatenate(out + [jnp.zeros(lead + (LANES - at,), F32)], axis=-1)


class _Tiles(NamedTuple):
    rows: int
    wide_rows: int
    cols: int
    ffn_cols: int
    norm_rows: int


def _tiles(n_rows):
    if n_rows >= 2048:
        return _Tiles(rows=1024, wide_rows=2048, cols=512, ffn_cols=256, norm_rows=256)
    return _Tiles(rows=n_rows, wide_rows=n_rows, cols=1024, ffn_cols=256, norm_rows=n_rows)


GDN_SEQ_ROWS = 256


def kernel(x_prompt, x_sample, cache_nsa_kv, page_table, state_win_kv, state_gdn, state_gdn_conv, state_ffn_conv, norm_mix, w_in, cmp_pe, cmp_w1, cmp_w2, gdn_conv_w, gdn_a_log, gdn_dt_bias, gdn_norm, w_nsa_out, w_gdn_out, w_o, norm_ffn, w_up, ffn_conv_w, w_down, norm_final):
    B, T, D = x_prompt.shape
    Bd = x_sample.shape[0]
    G = NSA_KV_HEADS
    n_past = page_table.shape[1] * cache_nsa_kv.shape[1]

    w_t, wide, w_small_seq, w_small_step = _arrange_w_in(w_in)
    w_nsa_b, w_gdn_b, w_o_b = w_nsa_out.astype(BF16), w_gdn_out.astype(BF16), w_o.astype(BF16)
    w_up_b, w_down_b = w_up.astype(BF16), w_down.astype(BF16)
    pe2, w1c, w2c = compress_params(cmp_pe, cmp_w1, cmp_w2)

    def project(x2d, t):
        xn = rmsnorm(x2d, norm_mix, BF16, t.norm_rows)
        return xn, [matmul_nt(xn, w_t, t.wide_rows, t.cols, r) for r in wide]

    def tail(x, mixed_nsa, mixed_gdn, u_mg, ffn_up, t):
        mixin = matmul_mix(mixed_nsa, mixed_gdn, w_nsa_b, w_gdn_b, u_mg, t.rows, t.cols)
        h = matmul_residual(mixin, w_o_b, x, t.rows, t.cols)
        hn = rmsnorm(h, norm_ffn, BF16, t.norm_rows)
        act, extra = ffn_up(hn)
        h2 = matmul_residual(act, w_down_b, h, t.rows, t.ffn_cols, x_buffers=1)
        return rmsnorm(h2, norm_final, F32, t.norm_rows), extra

    t = _tiles(B * T)
    xp = x_prompt.reshape(B * T, D)
    xn, (u_qkv, u_gqkv, u_z, u_mg) = project(xp, t)
    u_small = matmul(xn, w_small_seq, t.rows, w_small_seq.shape[1])
    cos, sin = rope_tables(jnp.arange(T, dtype=jnp.int32))
    q, kv_rows, win_rows = rope_split(u_qkv, cos, sin, t.norm_rows)
    cmp_kv = compress_seq(kv_rows, pe2, w1c, w2c, B)
    o_nsa = nsa_seq(q, kv_rows, win_rows, cmp_kv, u_small, B)
    a_rows, d_rows = gdn_param_rows(gdn_a_log, gdn_dt_bias)
    o_gdn, s_p = gdn_seq(u_gqkv, u_z, u_small, gdn_conv_w, a_rows, d_rows, gdn_norm, B, GDN_SEQ_ROWS)
    s_p = s_p.reshape(B, GDN_HEADS, GDN_DK, GDN_DV)

    def ffn_up_p(hn):
        act, tg, tv = ffn_up_seq(hn, w_up_b, ffn_conv_w, B, t.wide_rows, t.ffn_cols)
        n_t = T // t.wide_rows
        last = lambda a: a[n_t - 1::n_t, SUBLANES - (FFN_CONV - 1):]
        return act, jnp.concatenate([last(tg), last(tv)], axis=-1)

    y_p, ffn_p = tail(xp, o_nsa, o_gdn, u_mg, ffn_up_p, t)
    n_win_p = min(WINDOW, T)
    kv_p = kv_rows.reshape(B, T, 4, G, HEAD_DIM)
    win_p = win_rows.reshape(B, T, 2, G, HEAD_DIM)[:, T - n_win_p:]
    conv_p = u_gqkv.reshape(B, T, -1)[:, T - (GDN_CONV - 1):]

    t = _tiles(Bd)
    xs = x_sample.reshape(Bd, D)
    xn, (u_qkv, u_gqkv, u_z, u_mg) = project(xs, t)
    u_small = matmul(xn, w_small_step, t.rows, LANES)
    cos, sin = rope_tables(jnp.full((Bd,), n_past, jnp.int32))
    q, new_rows, win_new = rope_split(u_qkv, cos, sin, t.norm_rows)
    o_nsa, win_s = nsa_step(q, new_rows, win_new, cache_nsa_kv, page_table, state_win_kv, u_small, pe2, w1c, w2c)
    o_gdn, s_s = gdn_step(u_gqkv, u_z, u_small, state_gdn_conv, state_gdn, gdn_conv_w, gdn_a_log, gdn_dt_bias, gdn_norm)

    def ffn_up_s(hn):
        act, ug, uv = ffn_up_step(hn, w_up_b, ffn_conv_w, state_ffn_conv, t.ffn_cols)
        return act, jnp.concatenate([ug, uv], axis=-1)

    y_s, up_new = tail(xs, o_nsa, o_gdn, u_mg, ffn_up_s, t)
    kv_s = new_rows.reshape(Bd, 1, 4, G, HEAD_DIM)
    conv_s = jnp.concatenate([state_gdn_conv[:, 1:], u_gqkv[:, None]], axis=1)
    ffn_s = jnp.concatenate([state_ffn_conv[:, 1:], up_new[:, None]], axis=1)

    return (y_p.reshape(B, T, D), y_s.reshape(Bd, 1, D), kv_p, win_p, s_p, conv_p, ffn_p,
            kv_s, win_s, s_s, conv_s, ffn_s)
```

```python
import functools
import math
from typing import NamedTuple

import jax
import jax.numpy as jnp
import numpy as np
from jax import lax
from jax.experimental import pallas as pl
from jax.experimental.pallas import tpu as pltpu

F32 = jnp.float32
BF16 = jnp.bfloat16
HI = lax.Precision.HIGHEST

LANES = 128
SUBLANES = 8
VMEM_LIMIT_BYTES = 56 * 1024 * 1024

HEAD_DIM = 128
NSA_HEADS = 16
NSA_KV_HEADS = 4
NSA_GROUP = NSA_HEADS // NSA_KV_HEADS
CMP_LEN = 32
CMP_STRIDE = 16
CMP_HIDDEN = 256
SLC_LEN = 64
SLC_TOP_N = 16
SLC_LOCAL = 2
WINDOW = 512
GDN_HEADS = 16
GDN_DK = 128
GDN_DV = 128
GDN_CONV = 4
GDN_CHUNK = 64
FFN_CONV = 3
ROPE_THETA = 10000.0
EPS = 1e-6


def _cparams(*sem):
    return pltpu.CompilerParams(dimension_semantics=sem, vmem_limit_bytes=VMEM_LIMIT_BYTES)


def _dot(a, b, precision=None):
    return jnp.dot(a, b, preferred_element_type=F32, precision=precision)


def _dot_nt(a, b, precision=None):
    return lax.dot_general(a, b, (((1,), (1,)), ((), ())), preferred_element_type=F32, precision=precision)


def _dot_tn(a, b, precision=None):
    return lax.dot_general(a, b, (((0,), (0,)), ((), ())), preferred_element_type=F32, precision=precision)


def _sigmoid(x):
    return 1.0 / (1.0 + jnp.exp(-x))


def _silu(x):
    return x * _sigmoid(x)


def _rmsnorm_body(x_ref, w_ref, o_ref):
    x = x_ref[...]
    y = x * lax.rsqrt(jnp.mean(x * x, axis=-1, keepdims=True) + EPS)
    o_ref[...] = (y * w_ref[...]).astype(o_ref.dtype)


def rmsnorm(x, w, out_dtype, tm):
    M, D = x.shape
    return pl.pallas_call(
        _rmsnorm_body,
        grid=(M // tm,),
        in_specs=[pl.BlockSpec((tm, D), lambda i: (i, 0)), pl.BlockSpec((1, D), lambda i: (0, 0))],
        out_specs=pl.BlockSpec((tm, D), lambda i: (i, 0)),
        out_shape=jax.ShapeDtypeStruct((M, D), out_dtype),
        compiler_params=_cparams("parallel"),
        name="rmsnorm",
    )(x, w.reshape(1, D))


def _mm_body(x_ref, w_ref, o_ref):
    o_ref[...] = _dot(x_ref[...], w_ref[...]).astype(o_ref.dtype)


def matmul(x, w, tm, tn, out_dtype=F32):
    M, K = x.shape
    N = w.shape[1]
    return pl.pallas_call(
        _mm_body,
        grid=(M // tm, N // tn),
        in_specs=[pl.BlockSpec((tm, K), lambda i, j: (i, 0)), pl.BlockSpec((K, tn), lambda i, j: (0, j))],
        out_specs=pl.BlockSpec((tm, tn), lambda i, j: (i, j)),
        out_shape=jax.ShapeDtypeStruct((M, N), out_dtype),
        compiler_params=_cparams("parallel", "arbitrary"),
        name="matmul",
    )(x, w)


def _mm_nt_body(x_ref, w_ref, o_ref):
    o_ref[...] = _dot_nt(x_ref[...], w_ref[...]).astype(o_ref.dtype)


def matmul_nt(x, w_t, tm, tn, rows, out_dtype=F32):
    M, K = x.shape
    r0, N = rows
    assert r0 % (2 * SUBLANES) == 0 and N % tn == 0
    return pl.pallas_call(
        _mm_nt_body,
        grid=(M // tm, N // tn),
        in_specs=[pl.BlockSpec((tm, K), lambda i, j: (i, 0)),
                  pl.BlockSpec((pl.Element(tn), pl.Element(K)), lambda i, j: (pl.multiple_of(r0 + j * tn, 2 * SUBLANES), 0))],
        out_specs=pl.BlockSpec((tm, tn), lambda i, j: (i, j)),
        out_shape=jax.ShapeDtypeStruct((M, N), out_dtype),
        compiler_params=_cparams("parallel", "arbitrary"),
        name="matmul_nt",
    )(x, w_t)


def _mm_res_body(x_ref, w_ref, r_ref, o_ref):
    o_ref[...] = r_ref[...] + _dot(x_ref[...], w_ref[...])


def matmul_residual(x, w, res, tm, tn, x_buffers=2):
    M, K = x.shape
    N = w.shape[1]
    return pl.pallas_call(
        _mm_res_body,
        grid=(M // tm, N // tn),
        in_specs=[pl.BlockSpec((tm, K), lambda i, j: (i, 0), pipeline_mode=pl.Buffered(x_buffers)),
                  pl.BlockSpec((K, tn), lambda i, j: (0, j)),
                  pl.BlockSpec((tm, tn), lambda i, j: (i, j))],
        out_specs=pl.BlockSpec((tm, tn), lambda i, j: (i, j)),
        out_shape=jax.ShapeDtypeStruct((M, N), F32),
        compiler_params=_cparams("parallel", "arbitrary"),
        name="matmul_residual",
    )(x, w, res)


def _mm_mix_body(a_ref, b_ref, wa_ref, wb_ref, ga_ref, gb_ref, o_ref):
    ua = _dot(a_ref[...], wa_ref[...])
    ub = _dot(b_ref[...], wb_ref[...])
    o_ref[...] = (_sigmoid(ga_ref[...]) * ua + _sigmoid(gb_ref[...]) * ub).astype(o_ref.dtype)


def matmul_mix(a, b, wa, wb, gates, tm, tn):
    M, K = a.shape
    N = wa.shape[1]
    nj = N // tn
    return pl.pallas_call(
        _mm_mix_body,
        grid=(M // tm, nj),
        in_specs=[pl.BlockSpec((tm, K), lambda i, j: (i, 0)), pl.BlockSpec((tm, K), lambda i, j: (i, 0)),
                  pl.BlockSpec((K, tn), lambda i, j: (0, j)), pl.BlockSpec((K, tn), lambda i, j: (0, j)),
                  pl.BlockSpec((tm, tn), lambda i, j: (i, j)), pl.BlockSpec((tm, tn), lambda i, j: (i, j + nj))],
        out_specs=pl.BlockSpec((tm, tn), lambda i, j: (i, j)),
        out_shape=jax.ShapeDtypeStruct((M, N), BF16),
        compiler_params=_cparams("parallel", "arbitrary"),
        name="matmul_mix",
    )(a, b, wa, wb, gates, gates)


def _shift_rows(x, hist, sh):
    s = pltpu.roll(x, sh, axis=0)
    head = s[:SUBLANES]
    row8 = lax.broadcasted_iota(jnp.int32, head.shape, 0)
    for r in range(sh):
        head = jnp.where(row8 == r, hist[SUBLANES - sh + r:SUBLANES - sh + r + 1], head)
    return jnp.concatenate([head, s[SUBLANES:]], axis=0)


def _ffn_up_seq_body(x_ref, wg_ref, wv_ref, cg_ref, cv_ref, act_ref, tg_ref, tv_ref, hg_ref, hv_ref, *, tiles_per_seq):
    i = pl.program_id(0)
    j = pl.program_id(1)
    first = (i % tiles_per_seq) == 0
    x = x_ref[...]

    def branch(w_ref, c_ref, hist_ref, tail_ref):
        up = _dot(x, w_ref[...])
        tm = up.shape[0]
        hist = jnp.where(first, 0.0, hist_ref[j])
        c = c_ref[...]
        out = up * c[2:3] + _shift_rows(up, hist, 2) * c[0:1] + _shift_rows(up, hist, 1) * c[1:2]
        tail = up[tm - SUBLANES:tm]
        hist_ref[j] = tail
        tail_ref[0] = tail
        return out

    gate = branch(wg_ref, cg_ref, hg_ref, tg_ref)
    val = branch(wv_ref, cv_ref, hv_ref, tv_ref)
    act_ref[...] = (_silu(gate) * val).astype(act_ref.dtype)


def ffn_up_seq(x, w_up, conv_w, n_seq, tm, tn):
    M, K = x.shape
    F = w_up.shape[1] // 2
    nj = F // tn
    tiles_per_seq = M // n_seq // tm
    body = functools.partial(_ffn_up_seq_body, tiles_per_seq=tiles_per_seq)
    return pl.pallas_call(
        body,
        grid=(M // tm, nj),
        in_specs=[pl.BlockSpec((tm, K), lambda i, j: (i, 0)),
                  pl.BlockSpec((K, tn), lambda i, j: (0, j)), pl.BlockSpec((K, tn), lambda i, j: (0, j + nj)),
                  pl.BlockSpec((FFN_CONV, tn), lambda i, j: (0, j)), pl.BlockSpec((FFN_CONV, tn), lambda i, j: (0, j + nj))],
        out_specs=[pl.BlockSpec((tm, tn), lambda i, j: (i, j)),
                   pl.BlockSpec((1, SUBLANES, tn), lambda i, j: (i, 0, j)),
                   pl.BlockSpec((1, SUBLANES, tn), lambda i, j: (i, 0, j))],
        out_shape=[jax.ShapeDtypeStruct((M, F), BF16),
                   jax.ShapeDtypeStruct((M // tm, SUBLANES, F), F32), jax.ShapeDtypeStruct((M // tm, SUBLANES, F), F32)],
        scratch_shapes=[pltpu.VMEM((nj, SUBLANES, tn), F32), pltpu.VMEM((nj, SUBLANES, tn), F32)],
        compiler_params=_cparams("arbitrary", "arbitrary"),
        name="ffn_up_seq",
    )(x, w_up, w_up, conv_w, conv_w)


def _ffn_up_step_body(x_ref, wg_ref, wv_ref, cg_ref, cv_ref, g0_ref, g1_ref, v0_ref, v1_ref, act_ref, ug_ref, uv_ref):
    x = x_ref[...]

    def branch(w_ref, c_ref, h0_ref, h1_ref, up_ref):
        up = _dot(x, w_ref[...])
        up_ref[...] = up
        c = c_ref[...]
        return up * c[2:3] + h0_ref[...] * c[0:1] + h1_ref[...] * c[1:2]

    gate = branch(wg_ref, cg_ref, g0_ref, g1_ref, ug_ref)
    val = branch(wv_ref, cv_ref, v0_ref, v1_ref, uv_ref)
    act_ref[...] = (_silu(gate) * val).astype(act_ref.dtype)


def ffn_up_step(x, w_up, conv_w, hist, tn):
    M, K = x.shape
    F2 = w_up.shape[1]
    F = F2 // 2
    nj = F // tn
    h2 = hist.reshape(M, 2 * F2)
    return pl.pallas_call(
        _ffn_up_step_body,
        grid=(nj,),
        in_specs=[pl.BlockSpec((M, K), lambda j: (0, 0)),
                  pl.BlockSpec((K, tn), lambda j: (0, j)), pl.BlockSpec((K, tn), lambda j: (0, j + nj)),
                  pl.BlockSpec((FFN_CONV, tn), lambda j: (0, j)), pl.BlockSpec((FFN_CONV, tn), lambda j: (0, j + nj)),
                  pl.BlockSpec((M, tn), lambda j: (0, j)), pl.BlockSpec((M, tn), lambda j: (0, j + 2 * nj)),
                  pl.BlockSpec((M, tn), lambda j: (0, j + nj)), pl.BlockSpec((M, tn), lambda j: (0, j + 3 * nj))],
        out_specs=[pl.BlockSpec((M, tn), lambda j: (0, j))] * 3,
        out_shape=[jax.ShapeDtypeStruct((M, F), BF16), jax.ShapeDtypeStruct((M, F), F32), jax.ShapeDtypeStruct((M, F), F32)],
        compiler_params=_cparams("arbitrary"),
        name="ffn_up_step",
    )(x, w_up, w_up, conv_w, conv_w, h2, h2, h2, h2)


def _rope_body(u_ref, cos_ref, sin_ref, q_ref, kv_ref, win_ref):
    cos = cos_ref[...]
    sin = sin_ref[...]
    scale = HEAD_DIM ** -0.5

    def rot(x):
        return x * cos + pltpu.roll(x, HEAD_DIM // 2, axis=1) * sin

    for h in range(NSA_HEADS):
        sl = slice(h * HEAD_DIM, (h + 1) * HEAD_DIM)
        q_ref[:, sl] = (rot(u_ref[:, sl]) * scale).astype(q_ref.dtype)
    base = NSA_HEADS * HEAD_DIM
    n_glob = 4 * NSA_KV_HEADS
    for slot in range(6 * NSA_KV_HEADS):
        x = u_ref[:, base + slot * HEAD_DIM:base + (slot + 1) * HEAD_DIM]
        if (slot // NSA_KV_HEADS) % 2 == 0:
            x = rot(x)
        if slot < n_glob:
            kv_ref[:, slot * HEAD_DIM:(slot + 1) * HEAD_DIM] = x
        else:
            win_ref[:, (slot - n_glob) * HEAD_DIM:(slot - n_glob + 1) * HEAD_DIM] = x


def rope_split(u, cos, sin, tm):
    M, W = u.shape
    nt = cos.shape[0] // tm
    nq = NSA_HEADS * HEAD_DIM
    ng = 4 * NSA_KV_HEADS * HEAD_DIM
    nw = 2 * NSA_KV_HEADS * HEAD_DIM
    return pl.pallas_call(
        _rope_body,
        grid=(M // tm,),
        in_specs=[pl.BlockSpec((tm, W), lambda i: (i, 0)),
                  pl.BlockSpec((tm, HEAD_DIM), lambda i: (i % nt, 0)), pl.BlockSpec((tm, HEAD_DIM), lambda i: (i % nt, 0))],
        out_specs=[pl.BlockSpec((tm, nq), lambda i: (i, 0)), pl.BlockSpec((tm, ng), lambda i: (i, 0)),
                   pl.BlockSpec((tm, nw), lambda i: (i, 0))],
        out_shape=[jax.ShapeDtypeStruct((M, nq), BF16), jax.ShapeDtypeStruct((M, ng), F32), jax.ShapeDtypeStruct((M, nw), F32)],
        compiler_params=_cparams("parallel"),
        name="rope_split",
    )(u, cos, sin)


GDN_HB = 4
GDN_COL_B = 0
GDN_COL_A = 8
NSA_COL_G = 16
GDN_STEP_GROUPS = 2


def _softplus(x):
    return jnp.maximum(x, 0.0) + jnp.log1p(jnp.exp(-jnp.abs(x)))


def _l2norm(x):
    return x * lax.rsqrt(jnp.sum(x * x, axis=-1, keepdims=True) + EPS)


def _solve_unit_lower(a_list, r_list, order):
    steps = int(math.log2(order))
    a_list, r_list = list(a_list), list(r_list)
    m, n = a_list[0].shape[1], r_list[0].shape[1]
    for i in range(steps):
        last = i == steps - 1
        for c in range(len(a_list)):
            a_b = a_list[c].astype(BF16)
            r_hi, r_lo = _split_bf16(r_list[c])
            prod = _dot(a_b, jnp.concatenate(([] if last else [a_b]) + [r_hi, r_lo], axis=1))
            if not last:
                a_list[c], prod = prod[:, :m], prod[:, m:]
            r_list[c] = r_list[c] + (prod[:, :n] + prod[:, n:])
    return r_list


def _block_diag(blocks):
    z = jnp.zeros_like(blocks[0])
    nb = len(blocks)
    return jnp.concatenate([jnp.concatenate([blocks[h] if j == h else z for j in range(nb)], axis=1)
                            for h in range(nb)], axis=0)


def _gdn_chunks_groups(groups, states):
    nh = len(groups[0][0][0])
    C, d = groups[0][0][2][0].shape
    W = nh * C
    stack = lambda xs: jnp.concatenate(xs, axis=0)
    ri = lax.broadcasted_iota(jnp.int32, (W, W), 0)
    ci = lax.broadcasted_iota(jnp.int32, (W, W), 1)
    same = (ri // C) == (ci // C)

    a_list, r_list, pre = [], [], []
    for chunks in groups:
        for q, k, v, beta, gc, gcr in chunks:
            dmat = jnp.where(same & (ri >= ci), jnp.exp(jnp.minimum(stack(gc) - jnp.concatenate(gcr, axis=1), 0.0)), 0.0)
            eg = [jnp.exp(g) for g in gc]
            kb = [k[h] * beta[h] for h in range(nh)]
            xk = [_dot_nt(jnp.concatenate([kb[h], q[h]], axis=0).astype(BF16), k[h].astype(BF16)) for h in range(nh)]
            kk = _block_diag([x[:C] for x in xk])
            qk = _block_diag([x[C:] for x in xk])
            a_list.append(jnp.where(ri > ci, -(kk * dmat), 0.0))
            r_list.append(jnp.concatenate([stack([v[h] * beta[h] for h in range(nh)]),
                                           stack([kb[h] * eg[h] for h in range(nh)])], axis=1))
            g_last = [g[C - 1:C, :] for g in gc]
            pre.append(((qk * dmat).astype(BF16), [(q[h] * eg[h]).astype(BF16) for h in range(nh)],
                        [(k[h] * jnp.exp(g_last[h] - gc[h])).astype(BF16) for h in range(nh)],
                        stack([jnp.broadcast_to(jnp.exp(g_last[h]), (d, 1)) for h in range(nh)])))
    w_list = _solve_unit_lower(a_list, r_list, C)

    n_chunks = len(groups[0])
    states = list(states)
    outs = [[] for _ in groups]
    for c in range(n_chunks):
        for g in range(len(groups)):
            w, (aqk, q_eg, kd, keep) = w_list[g * n_chunks + c], pre[g * n_chunks + c]
            s_b = states[g].astype(BF16)
            xs = [_dot(jnp.concatenate([w[h * C:(h + 1) * C, d:].astype(BF16), q_eg[h]], axis=0), s_b[h * d:(h + 1) * d])
                  for h in range(nh)]
            v_new = (w[:, :d] - stack([x[:C] for x in xs])).astype(BF16)
            outs[g].append(stack([x[C:] for x in xs]) + _dot(aqk, v_new))
            upd = stack([_dot_tn(kd[h], v_new[h * C:(h + 1) * C]) for h in range(nh)])
            states[g] = states[g] * keep + upd
    return outs, states


def _gdn_seq_body(q_ref, k_ref, v_ref, z_ref, ba_ref, cq_ref, ck_ref, cv_ref, alog_ref, dtb_ref, nw_ref,
                  o_ref, s_ref, tq_ref, tk_ref, tv_ref, qs_ref, ks_ref, vs_ref, gs_ref, bs_ref):
    t_idx = pl.program_id(2)
    tt = q_ref.shape[0]
    hb = q_ref.shape[1] // GDN_DK
    C = GDN_CHUNK

    @pl.when(t_idx == 0)
    def _():
        s_ref[...] = jnp.zeros_like(s_ref)
        tq_ref[...] = jnp.zeros_like(tq_ref)
        tk_ref[...] = jnp.zeros_like(tk_ref)
        tv_ref[...] = jnp.zeros_like(tv_ref)

    def conv(x_ref, c_ref, tail_ref):
        x = x_ref[...]
        hist = tail_ref[...]
        row = lax.broadcasted_iota(jnp.int32, x.shape, 0)
        c = c_ref[...]
        out = x * c[GDN_CONV - 1:GDN_CONV]
        for j in range(GDN_CONV - 1):
            out = out + _shift_rows(x, hist, GDN_CONV - 1 - j) * c[j:j + 1]
        tail_ref[...] = x[tt - SUBLANES:tt]
        return _silu(out)

    qa = conv(q_ref, cq_ref, tq_ref)
    ka = conv(k_ref, ck_ref, tk_ref)
    vs_ref[...] = conv(v_ref, cv_ref, tv_ref)
    for h in range(hb):
        sl = slice(h * GDN_DK, (h + 1) * GDN_DK)
        qs_ref[:, sl] = _l2norm(qa[:, sl]) * GDN_DK ** -0.5
        ks_ref[:, sl] = _l2norm(ka[:, sl])
    ba = ba_ref[...]
    bs_ref[...] = _sigmoid(ba)
    n_groups = hb // GDN_HB
    par_row = lambda ref: jnp.concatenate([ref[g, 0:1, :] for g in range(n_groups)], axis=1)
    gs_ref[...] = -jnp.exp(par_row(alog_ref)) * _softplus(ba + par_row(dtb_ref))

    tri = (lax.broadcasted_iota(jnp.int32, (C, C), 0) >= lax.broadcasted_iota(jnp.int32, (C, C), 1)).astype(F32)
    sel = (lax.broadcasted_iota(jnp.int32, (2 * SUBLANES, LANES), 0)
           == lax.broadcasted_iota(jnp.int32, (2 * SUBLANES, LANES), 1)).astype(F32)
    nw = nw_ref[...]

    chunk_rows = [slice(c * C, (c + 1) * C) for c in range(tt // C)]
    gc_alls = [_dot(tri, gs_ref[rows, :], HI) for rows in chunk_rows]
    groups, group_heads = [], []
    for g in range(n_groups):
        heads = [slice((g * GDN_HB + h) * GDN_DK, (g * GDN_HB + h + 1) * GDN_DK) for h in range(GDN_HB)]
        lanes = slice(g * LANES, (g + 1) * LANES)
        chunks = []
        for rows, gc_full in zip(chunk_rows, gc_alls):
            gc_all = gc_full[:, lanes]
            gc_t = _dot_nt(sel, gc_all, HI)
            beta_all = bs_ref[rows, lanes]
            chunks.append((
                [qs_ref[rows, sl] for sl in heads], [ks_ref[rows, sl] for sl in heads], [vs_ref[rows, sl] for sl in heads],
                [beta_all[:, GDN_COL_B + h:GDN_COL_B + h + 1] for h in range(GDN_HB)],
                [gc_all[:, GDN_COL_A + h:GDN_COL_A + h + 1] for h in range(GDN_HB)],
                [gc_t[GDN_COL_A + h:GDN_COL_A + h + 1, :] for h in range(GDN_HB)]))
        groups.append(chunks)
        group_heads.append(heads)
    gdk = GDN_HB * GDN_DK
    outs, states = _gdn_chunks_groups(groups, [s_ref[0, g * gdk:(g + 1) * gdk] for g in range(n_groups)])
    for g in range(n_groups):
        s_ref[0, g * gdk:(g + 1) * gdk] = states[g]
        for rows, o in zip(chunk_rows, outs[g]):
            on = o * lax.rsqrt(jnp.mean(o * o, axis=-1, keepdims=True) + EPS) * nw
            for h, sl in enumerate(group_heads[g]):
                o_ref[rows, sl] = (on[h * C:(h + 1) * C] * _silu(z_ref[rows, sl])).astype(o_ref.dtype)


def gdn_seq(u_qkv, u_z, u_small, conv_w, alog_rows, dtb_rows, norm_w, n_seq, tt):
    M = u_qkv.shape[0]
    T = M // n_seq
    nt = T // tt
    hb = GDN_HB * GDN_STEP_GROUPS
    nhb = GDN_HEADS // hb
    wb = hb * GDN_DK
    gl = GDN_STEP_GROUPS * LANES
    row_blk = lambda off: pl.BlockSpec((tt, wb), lambda b, h, t: (b * nt + t, h + off))
    cw_blk = lambda off: pl.BlockSpec((GDN_CONV, wb), lambda b, h, t: (0, h + off))
    par_blk = pl.BlockSpec((GDN_STEP_GROUPS, SUBLANES, LANES), lambda b, h, t: (h, 0, 0))
    return pl.pallas_call(
        _gdn_seq_body,
        grid=(n_seq, nhb, nt),
        in_specs=[row_blk(0), row_blk(nhb), row_blk(2 * nhb), row_blk(0),
                  pl.BlockSpec((tt, gl), lambda b, h, t: (b * nt + t, h)),
                  cw_blk(0), cw_blk(nhb), cw_blk(2 * nhb), par_blk, par_blk,
                  pl.BlockSpec((1, GDN_DV), lambda b, h, t: (0, 0))],
        out_specs=[row_blk(0), pl.BlockSpec((1, hb * GDN_DK, GDN_DV), lambda b, h, t: (b, h, 0))],
        out_shape=[jax.ShapeDtypeStruct((M, GDN_HEADS * GDN_DV), BF16),
                   jax.ShapeDtypeStruct((n_seq, GDN_HEADS * GDN_DK, GDN_DV), F32)],
        scratch_shapes=[pltpu.VMEM((SUBLANES, wb), F32)] * 3 + [pltpu.VMEM((tt, wb), F32)] * 3
                       + [pltpu.VMEM((tt, gl), F32)] * 2,
        compiler_params=_cparams("parallel", "parallel", "arbitrary"),
        name="gdn_seq",
    )(u_qkv, u_qkv, u_qkv, u_z, u_small, conv_w, conv_w, conv_w, alog_rows, dtb_rows, norm_w.reshape(1, GDN_DV))


def gdn_param_rows(gdn_a_log, gdn_dt_bias):
    nhb = GDN_HEADS // GDN_HB

    def rows(p):
        r = _place_lanes(((GDN_COL_A, p.reshape(nhb, GDN_HB)),), (nhb,))
        return jnp.broadcast_to(r[:, None, :], (nhb, SUBLANES, LANES))

    return rows(gdn_a_log), rows(gdn_dt_bias)


CMP_RATIO = CMP_LEN // CMP_STRIDE
CMP_FEAT = CMP_STRIDE * HEAD_DIM
MASKED = -1e30


def _compress_mlp(part, pe_ref, w1_ref, w2_ref, chunk_step=1):
    n = part.shape[0]
    pe_part = _dot(pe_ref[...], w1_ref[...])
    hid0 = pe_part[0:1, :CMP_HIDDEN] + pe_part[1:2, CMP_HIDDEN:]
    hid = hid0 + part[:, :CMP_HIDDEN]
    hid = hid + pltpu.roll(part[:, CMP_HIDDEN:], n - chunk_step, axis=0)
    return _dot(_silu(hid).astype(BF16), w2_ref[...])


def _compress_seq_body(x_ref, pe_ref, w1_ref, w2_ref, o_ref, xc_ref):
    n = o_ref.shape[3]
    for s in range(CMP_STRIDE):
        xc_ref[:, s * HEAD_DIM:(s + 1) * HEAD_DIM] = x_ref[pl.ds(s, n, stride=CMP_STRIDE), :].astype(BF16)
    part = _dot(xc_ref[...], w1_ref[0])
    o_ref[0, 0, 0] = _compress_mlp(part, pe_ref.at[0], w1_ref.at[0], w2_ref.at[0])


def compress_seq(kv_rows, pe2, w1c, w2c, n_seq):
    M = kv_rows.shape[0]
    T = M // n_seq
    n = T // CMP_STRIDE
    G = NSA_KV_HEADS
    return pl.pallas_call(
        _compress_seq_body,
        grid=(n_seq, 2, G),
        in_specs=[pl.BlockSpec((T, HEAD_DIM), lambda b, c, g: (b, c * G + g)),
                  pl.BlockSpec((1, SUBLANES, CMP_FEAT), lambda b, c, g: (c, 0, 0)),
                  pl.BlockSpec((1, CMP_FEAT, 2 * CMP_HIDDEN), lambda b, c, g: (c, 0, 0)),
                  pl.BlockSpec((1, CMP_HIDDEN, HEAD_DIM), lambda b, c, g: (c, 0, 0))],
        out_specs=pl.BlockSpec((1, 1, 1, n, HEAD_DIM), lambda b, c, g: (b, c, g, 0, 0)),
        out_shape=jax.ShapeDtypeStruct((n_seq, 2, G, n, HEAD_DIM), F32),
        scratch_shapes=[pltpu.VMEM((n, CMP_FEAT), BF16)],
        compiler_params=_cparams("parallel", "parallel", "parallel"),
        name="compress_seq",
    )(kv_rows, pe2, w1c, w2c)


def compress_params(cmp_pe, cmp_w1, cmp_w2):
    w1 = cmp_w1.reshape(2, CMP_RATIO, CMP_FEAT, CMP_HIDDEN)
    w1c = jnp.concatenate([w1[:, r] for r in range(CMP_RATIO)], axis=-1).astype(BF16)
    pe = cmp_pe.reshape(2, CMP_RATIO, CMP_FEAT)
    pe2 = jnp.zeros((2, SUBLANES, CMP_FEAT), F32).at[:, :CMP_RATIO].set(pe).astype(BF16)
    return pe2, w1c, cmp_w2.astype(BF16)


def _masked_softmax(s, mask, axis=-1):
    s = jnp.where(mask, s, -jnp.inf)
    m = jnp.max(s, axis=axis, keepdims=True)
    m = jnp.where(m > -jnp.inf, m, 0.0)
    e = jnp.exp(s - m)
    return e * (1.0 / jnp.maximum(jnp.sum(e, axis=axis, keepdims=True), 1e-30))


def _masked_softmax_heads(s, mask, n_heads):
    t = mask.shape[0]
    return jnp.concatenate([_masked_softmax(s[h * t:(h + 1) * t], mask) for h in range(n_heads)], axis=0)


def _split_bf16(x):
    hi = x.astype(BF16)
    return hi, (x - hi.astype(F32)).astype(BF16)


def _select_blocks_t(imp_t, pos_row, n_blocks):
    J = imp_t.shape[0]
    j = lax.broadcasted_iota(jnp.int32, imp_t.shape, 0)
    cur = pos_row // SLC_LEN
    forced = (j == 0) | ((j <= cur) & (j > cur - SLC_LOCAL))
    score = jnp.where(j * SLC_LEN > pos_row, -jnp.inf, jnp.where(forced, jnp.inf, imp_t))
    score = jnp.where(j < n_blocks, score, -jnp.inf)
    tiles = [score[r:r + SUBLANES] for r in range(0, J, SUBLANES)]
    ranks = [jnp.zeros(t.shape, F32) for t in tiles]
    for jp in range(n_blocks):
        row = score[jp:jp + 1, :]
        for i, t in enumerate(tiles):
            lo_j = i * SUBLANES
            ge, gt = jnp.where(row >= t, 1.0, 0.0), jnp.where(row > t, 1.0, 0.0)
            if lo_j > jp:
                ahead = ge
            elif lo_j + SUBLANES - 1 <= jp:
                ahead = gt
            else:
                ahead = jnp.where(lax.broadcasted_iota(jnp.int32, t.shape, 0) + lo_j > jp, ge, gt)
            ranks[i] = ranks[i] + ahead
    rank = jnp.concatenate(ranks, axis=0)
    keep = (rank < float(min(SLC_TOP_N, n_blocks))) & (j < n_blocks)
    return jnp.where(keep, 1.0, 0.0)


def _nsa_seq_body(q_ref, kc_ref, vc_ref, ks_ref, vs_ref, kw_ref, vw_ref, gl_ref, ovl_ref, o_ref, m_ref, l_ref, acc_ref, *, kblk):
    i = pl.program_id(2)
    tq = q_ref.shape[0]
    T = ks_ref.shape[0]
    H = NSA_GROUP
    D = HEAD_DIM
    n_slc = T // SLC_LEN
    qb = q_ref[...]
    q4 = jnp.concatenate([qb[:, h * D:(h + 1) * D] for h in range(H)], axis=0)
    t0 = i * tq
    pos1 = t0 + lax.broadcasted_iota(jnp.int32, (tq, 1), 0)

    kc = kc_ref[0, 0, 0].astype(BF16)
    vc = vc_ref[0, 0, 0].astype(BF16)
    n_c = kc.shape[0]
    s = _dot_nt(q4, kc)
    cmp_end = lax.broadcasted_iota(jnp.int32, (1, n_c), 1) * CMP_STRIDE + (CMP_LEN - 1)
    p = _masked_softmax_heads(s, cmp_end <= pos1, H)
    o_c = _dot(p.astype(BF16), vc)

    psum = p[0:tq]
    for h in range(1, H):
        psum = psum + p[h * tq:(h + 1) * tq]
    p_hi, p_lo = _split_bf16(psum)
    ovl = ovl_ref[...]
    imp_t = _dot_nt(ovl, p_hi) + _dot_nt(ovl, p_lo)
    pos_row = t0 + lax.broadcasted_iota(jnp.int32, (1, tq), 1)
    sel_t = _select_blocks_t(imp_t, pos_row, n_slc)
    eye = (lax.broadcasted_iota(jnp.int32, (tq, tq), 0) == lax.broadcasted_iota(jnp.int32, (tq, tq), 1))
    sel = _dot_nt(jnp.where(eye, 1.0, 0.0).astype(BF16), sel_t.astype(BF16)).astype(BF16)

    n_kb = (t0 + tq + kblk - 1) // kblk

    m_ref[...] = jnp.full(m_ref.shape, MASKED, F32)
    l_ref[...] = jnp.zeros(l_ref.shape, F32)
    acc_ref[...] = jnp.zeros(acc_ref.shape, F32)

    def kv_step(kb, carry):
        k0 = pl.multiple_of(kb * kblk, kblk)
        kk = ks_ref[pl.ds(k0, kblk), :].astype(BF16)
        vv = vs_ref[pl.ds(k0, kblk), :].astype(BF16)
        s2 = _dot_nt(q4, kk)
        kpos = k0 + lax.broadcasted_iota(jnp.int32, (1, kblk), 1)
        expand = (lax.broadcasted_iota(jnp.int32, (n_slc, kblk), 0) == kpos // SLC_LEN)
        chosen = _dot(sel, jnp.where(expand, 1.0, 0.0).astype(BF16))
        bias = jnp.where((chosen > 0.5) & (kpos <= pos1), 0.0, MASKED)
        es = []
        for h in range(H):
            rows = slice(h * tq, (h + 1) * tq)
            sh = s2[rows] + bias
            m_old = m_ref[rows]
            m_new = jnp.maximum(m_old, jnp.max(sh, axis=-1, keepdims=True))
            alpha = jnp.exp(m_old - m_new)
            e = jnp.exp(sh - m_new)
            m_ref[rows] = m_new
            l_ref[rows] = alpha * l_ref[rows] + jnp.sum(e, axis=-1, keepdims=True)
            acc_ref[rows] = alpha * acc_ref[rows]
            es.append(e.astype(BF16))
        acc_ref[...] += _dot(jnp.concatenate(es, axis=0), vv)
        return carry

    lax.fori_loop(0, n_kb, kv_step, 0)
    o_s = acc_ref[...] * (1.0 / jnp.maximum(l_ref[...], 1e-30))

    band = WINDOW + tq
    w0 = pl.multiple_of(jnp.maximum(t0 - WINDOW, 0), tq)
    kw = kw_ref[pl.ds(w0, band), :].astype(BF16)
    vw = vw_ref[pl.ds(w0, band), :].astype(BF16)
    sw = _dot_nt(q4, kw)
    diff = pos1 - (w0 + lax.broadcasted_iota(jnp.int32, (1, band), 1))
    pw = _masked_softmax_heads(sw, (diff >= 0) & (diff < WINDOW), H)
    o_w = _dot(pw.astype(BF16), vw)

    gates = _sigmoid(gl_ref[...])
    for h in range(H):
        rows = slice(h * tq, (h + 1) * tq)
        g = lambda br: gates[:, NSA_COL_G + H * br + h:NSA_COL_G + H * br + h + 1]
        o_ref[:, h * D:(h + 1) * D] = (g(0) * o_c[rows] + g(1) * o_s[rows] + g(2) * o_w[rows]).astype(o_ref.dtype)


def _overlap_t(n_cmp_rows, n_slc):
    start = np.arange(n_cmp_rows)[None, :] * CMP_STRIDE
    blk = np.arange(n_slc)[:, None] * SLC_LEN
    return ((start < blk + SLC_LEN) & (start + CMP_LEN > blk)).astype(np.float32)


def nsa_seq(q, kv_rows, win_rows, cmp_kv, u_small, n_seq, tq=256, kblk=1024):
    M = q.shape[0]
    T = M // n_seq
    nt = T // tq
    G = NSA_KV_HEADS
    gw = NSA_GROUP * HEAD_DIM
    n_c = cmp_kv.shape[3]
    ovl = jnp.asarray(_overlap_t(n_c, T // SLC_LEN), BF16)
    body = functools.partial(_nsa_seq_body, kblk=kblk)
    seq_blk = lambda off: pl.BlockSpec((T, HEAD_DIM), lambda b, g, i: (b, g + off))
    cmp_blk = lambda c: pl.BlockSpec((1, 1, 1, n_c, HEAD_DIM), lambda b, g, i: (b, c, g, 0, 0))
    return pl.pallas_call(
        body,
        grid=(n_seq, G, nt),
        in_specs=[pl.BlockSpec((tq, gw), lambda b, g, i: (b * nt + i, g)),
                  cmp_blk(0), cmp_blk(1), seq_blk(2 * G), seq_blk(3 * G), seq_blk(0), seq_blk(G),
                  pl.BlockSpec((tq, LANES), lambda b, g, i: (b * nt + i, g)),
                  pl.BlockSpec(ovl.shape, lambda b, g, i: (0, 0))],
        out_specs=pl.BlockSpec((tq, gw), lambda b, g, i: (b * nt + i, g)),
        out_shape=jax.ShapeDtypeStruct((M, NSA_HEADS * HEAD_DIM), BF16),
        scratch_shapes=[pltpu.VMEM((NSA_GROUP * tq, 1), F32)] * 2 + [pltpu.VMEM((NSA_GROUP * tq, HEAD_DIM), F32)],
        compiler_params=_cparams("parallel", "parallel", "arbitrary"),
        name="nsa_seq",
    )(q, cmp_kv, cmp_kv, kv_rows, kv_rows, win_rows, win_rows, u_small, ovl)


STEP_COL_G = 0
STEP_COL_B = 3 * NSA_HEADS
STEP_COL_A = STEP_COL_B + GDN_HEADS


def _gdn_step_prep_body(x_ref, h_ref, c_ref, us_ref, alog_ref, dtb_ref, q_ref, k_ref, v_ref, b_ref, e_ref):
    W = x_ref.shape[1]
    c = c_ref[...]
    x = x_ref[...] * c[GDN_CONV - 1:GDN_CONV]
    for j in range(GDN_CONV - 1):
        x = x + h_ref[:, j * W:(j + 1) * W] * c[j:j + 1]
    x = _silu(x)
    us = us_ref[...]
    beta = _sigmoid(us)
    eg = jnp.exp(-jnp.exp(alog_ref[...]) * _softplus(us + dtb_ref[...]))
    nq = GDN_HEADS * GDN_DK
    for h in range(GDN_HEADS):
        sl = slice(h * GDN_DK, (h + 1) * GDN_DK)
        q_ref[:, sl] = _l2norm(x[:, sl]) * GDN_DK ** -0.5
        k_ref[:, sl] = _l2norm(x[:, nq + h * GDN_DK:nq + (h + 1) * GDN_DK])
        b_ref[:, sl] = jnp.broadcast_to(beta[:, STEP_COL_B + h:STEP_COL_B + h + 1], (x.shape[0], GDN_DK))
        e_ref[:, sl] = jnp.broadcast_to(eg[:, STEP_COL_A + h:STEP_COL_A + h + 1], (x.shape[0], GDN_DK))
    v_ref[...] = x[:, 2 * nq:]


def _gdn_step_body(q_ref, k_ref, v_ref, b_ref, e_ref, z_ref, nw_ref, s_ref, o_ref, so_ref):
    bb = q_ref.shape[0]
    eye = jnp.where(lax.broadcasted_iota(jnp.int32, (GDN_DK, GDN_DK), 0)
                    == lax.broadcasted_iota(jnp.int32, (GDN_DK, GDN_DK), 1), 1.0, 0.0)
    nw = nw_ref[...]

    def per_seq(bi, carry):
        q, k, v, beta, eg, z = q_ref[bi], k_ref[bi], v_ref[bi], b_ref[bi], e_ref[bi], z_ref[bi]
        k_t = _dot_nt(eye, k, HI)
        q_t = _dot_nt(eye, q, HI)
        outs = []
        for h in range(GDN_HEADS):
            r = slice(h, h + 1)
            s = s_ref[bi, h]
            kcol = k_t[:, h:h + 1]
            k_s = jnp.sum(kcol * s, axis=0, keepdims=True)
            q_s = jnp.sum(q_t[:, h:h + 1] * s, axis=0, keepdims=True)
            v_new = v[r] * beta[r] - (beta[r] * eg[r]) * k_s
            qk = jnp.sum(q[r] * k[r], axis=-1, keepdims=True)
            o = eg[r] * q_s + qk * v_new
            so_ref[bi, h] = s * eg[r] + kcol * v_new
            on = o * lax.rsqrt(jnp.mean(o * o, axis=-1, keepdims=True) + EPS) * nw
            outs.append(on * _silu(z[r]))
        o_ref[bi] = jnp.concatenate(outs, axis=0).astype(o_ref.dtype)
        return carry

    lax.fori_loop(0, bb, per_seq, 0)


def gdn_step(u_qkv, u_z, u_small, hist, state, conv_w, a_log, dt_bias, norm_w, bb=4):
    Bd, W = u_qkv.shape
    H = GDN_HEADS
    nq = H * GDN_DK
    row = lambda p: _place_lanes(((STEP_COL_A, p.reshape(1, H)),), (1,))
    full = lambda shape: pl.BlockSpec(shape, lambda i: (0,) * len(shape))
    outs = pl.pallas_call(
        _gdn_step_prep_body,
        grid=(1,),
        in_specs=[full((Bd, W)), full((Bd, (GDN_CONV - 1) * W)), full((GDN_CONV, W)), full((Bd, LANES)),
                  full((1, LANES)), full((1, LANES))],
        out_specs=[full((Bd, nq))] * 5,
        out_shape=[jax.ShapeDtypeStruct((Bd, nq), F32)] * 5,
        compiler_params=_cparams("arbitrary"),
        name="gdn_step_prep",
    )(u_qkv, hist.reshape(Bd, (GDN_CONV - 1) * W), conv_w, u_small, row(a_log), row(dt_bias))
    heads = lambda a: a.reshape(Bd, H, GDN_DK)
    vec_blk = pl.BlockSpec((bb, H, GDN_DK), lambda i: (i, 0, 0))
    st_blk = pl.BlockSpec((bb, H, GDN_DK, GDN_DV), lambda i: (i, 0, 0, 0))
    o, s_new = pl.pallas_call(
        _gdn_step_body,
        grid=(Bd // bb,),
        in_specs=[vec_blk] * 6 + [pl.BlockSpec((1, GDN_DV), lambda i: (0, 0)), st_blk],
        out_specs=[vec_blk, st_blk],
        out_shape=[jax.ShapeDtypeStruct((Bd, H, GDN_DV), BF16), jax.ShapeDtypeStruct(state.shape, F32)],
        compiler_params=_cparams("parallel"),
        name="gdn_step",
    )(*[heads(a) for a in outs], heads(u_z), norm_w.reshape(1, GDN_DV), state)
    return o.reshape(Bd, H * GDN_DV), s_new


def _softmax_with_new(s_past, valid, s_new):
    s_past = jnp.where(valid, s_past, -jnp.inf)
    m = jnp.maximum(jnp.max(s_past, axis=-1, keepdims=True), s_new)
    e = jnp.exp(s_past - m)
    e_new = jnp.exp(s_new - m)
    return e, e_new, 1.0 / (jnp.sum(e, axis=-1, keepdims=True) + e_new)


def _nsa_step_body(pt_ref, *refs, n_pages, page, n_past):
    lo = refs[:n_pages]
    hi = refs[n_pages:2 * n_pages]
    (q_ref, new_ref, win_ref, wnew_ref, gl_ref, pe_ref, w1_ref, w2_ref, ovl_ref, o_ref, wout_ref, xk_ref, xv_ref) = refs[2 * n_pages:]
    G, H, D = NSA_KV_HEADS, NSA_HEADS, HEAD_DIM
    HS = 2 * G
    pos = n_past
    cpp = page // CMP_STRIDE
    n_chunk = n_pages * cpp
    L = n_pages * page
    q = q_ref[0]
    qf = q.astype(F32)
    new = new_ref[0]
    row_g = lax.broadcasted_iota(jnp.int32, (H, 1), 0) // NSA_GROUP

    def per_head_rows(rows):
        return jnp.concatenate([jnp.broadcast_to(rows[g:g + 1], (NSA_GROUP, rows.shape[1])) for g in range(G)], axis=0)

    first_half = lax.broadcasted_iota(jnp.int32, (SUBLANES, D), 0) < G
    for j in range(n_pages):
        for s in range(CMP_STRIDE):
            for m in range(cpp // 4):
                ks, vs = [], []
                for pair in range(2):
                    ta = lo[j][0, (4 * m + 2 * pair) * CMP_STRIDE + s]
                    tb = lo[j][0, (4 * m + 2 * pair + 1) * CMP_STRIDE + s]
                    ks.append(jnp.where(first_half, ta, pltpu.roll(tb, G, axis=0)))
                    vs.append(jnp.where(first_half, pltpu.roll(ta, G, axis=0), tb))
                r0 = (j * cpp + 4 * m) * G
                xk_ref[r0:r0 + 4 * G, s * D:(s + 1) * D] = jnp.concatenate(ks, axis=0).astype(BF16)
                xv_ref[r0:r0 + 4 * G, s * D:(s + 1) * D] = jnp.concatenate(vs, axis=0).astype(BF16)
    ckv = []
    for c, x_ref in enumerate((xk_ref, xv_ref)):
        part = _dot(x_ref[...], w1_ref[c])
        ckv.append(_compress_mlp(part, pe_ref.at[c], w1_ref.at[c], w2_ref.at[c], G).astype(BF16))

    n_all = G * n_chunk
    s = _dot_nt(q, ckv[0])
    col = lax.broadcasted_iota(jnp.int32, (1, n_all), 1)
    blk = col // G
    ok = (col % G == row_g) & (blk * CMP_STRIDE + (CMP_LEN - 1) <= pos) & (blk < n_chunk - CMP_RATIO + 1)
    p = _masked_softmax(s, ok)
    o_c = _dot(p.astype(BF16), ckv[1])

    gsum = jnp.where(lax.broadcasted_iota(jnp.int32, (LANES, H), 0) == lax.broadcasted_iota(jnp.int32, (LANES, H), 1) // NSA_GROUP, 1.0, 0.0)
    psum = _dot(gsum, p, HI)
    p_hi, p_lo = _split_bf16(psum)
    ovl = ovl_ref[...]
    imp_t = _dot_nt(ovl, p_hi) + _dot_nt(ovl, p_lo)
    n_slc = -(-(L + 1) // SLC_LEN)
    sel_t = _select_blocks_t(imp_t, jnp.full((1, LANES), pos, jnp.int32), n_slc)
    J = sel_t.shape[0]
    eye8 = jnp.where(lax.broadcasted_iota(jnp.int32, (SUBLANES, LANES), 0) == lax.broadcasted_iota(jnp.int32, (SUBLANES, LANES), 1), 1.0, 0.0)
    sel = _dot_nt(eye8.astype(BF16), sel_t.astype(BF16))

    def slot_softmax(sc, tok_ok, s_new):
        slot = lax.broadcasted_iota(jnp.int32, (1, sc.shape[1]), 1) % HS
        return _softmax_with_new(sc, (slot == row_g) & tok_ok, s_new)

    tok = lax.broadcasted_iota(jnp.int32, (1, L * HS), 1) // HS
    chosen = per_head_rows(jnp.concatenate(
        [jnp.broadcast_to(sel[:, b:b + 1], (SUBLANES, SLC_LEN * HS)) for b in range(L // SLC_LEN)], axis=1))
    kv_hi = [hi[j][0].reshape(page * HS, D).astype(BF16) for j in range(n_pages)]
    sc = jnp.concatenate([_dot_nt(q, kv) for kv in kv_hi], axis=1)
    s_new = jnp.sum(qf * per_head_rows(new[2 * G:3 * G]), axis=-1, keepdims=True)
    e, e_new, inv = slot_softmax(sc, (chosen > 0.5) & (tok <= pos), s_new)
    e = pltpu.roll(e, G, axis=1).astype(BF16)
    acc = jnp.zeros((H, D), F32)
    for j, kv in enumerate(kv_hi):
        acc = acc + _dot(e[:, j * page * HS:(j + 1) * page * HS], kv)
    o_s = (acc + e_new * per_head_rows(new[3 * G:4 * G])) * inv

    n_win = win_ref.shape[1] // HS
    wnew = wnew_ref[0]
    win = win_ref[0].astype(BF16)
    wtok = pos - n_win + lax.broadcasted_iota(jnp.int32, (1, n_win * HS), 1) // HS
    sw_new = jnp.sum(qf * per_head_rows(wnew[0:G]), axis=-1, keepdims=True)
    ew, ew_new, winv = slot_softmax(_dot_nt(q, win), (pos - wtok < WINDOW) & (wtok >= 0), sw_new)
    o_w = (_dot(pltpu.roll(ew, G, axis=1).astype(BF16), win) + ew_new * per_head_rows(wnew[G:2 * G])) * winv

    gl = jnp.broadcast_to(_sigmoid(gl_ref[0]), (SUBLANES, LANES))
    hh = lax.broadcasted_iota(jnp.int32, (H, LANES), 0)
    cc = lax.broadcasted_iota(jnp.int32, (H, LANES), 1)
    gate = lambda br: _dot_nt(jnp.where(cc == STEP_COL_G + br * H + hh, 1.0, 0.0), gl, HI)[:, 0:1]
    o_ref[0] = (gate(0) * o_c + gate(1) * o_s + gate(2) * o_w).astype(o_ref.dtype)

    wout_ref[0, :(n_win - 1) * HS] = win_ref[0, HS:]
    wout_ref[0, (n_win - 1) * HS:] = wnew


def nsa_step(q, new_rows, win_new, cache, page_table, win_state, u_small, pe2, w1c, w2c):
    Bd = q.shape[0]
    n_pool, page = cache.shape[:2]
    n_pages = page_table.shape[1]
    G, H, D = NSA_KV_HEADS, NSA_HEADS, HEAD_DIM
    n_past = n_pages * page
    n_chunk = n_past // CMP_STRIDE
    n_win = win_state.shape[1]
    assert n_past % SLC_LEN == 0 and n_win == WINDOW <= n_past and page % (4 * CMP_STRIDE) == 0 and 2 * G == SUBLANES
    n_slc = -(-(n_past + 1) // SLC_LEN)
    J = -(-n_slc // (2 * SUBLANES)) * (2 * SUBLANES)
    ovl = np.zeros((J, n_chunk * G), np.float32)
    ovl[:n_slc] = np.repeat(_overlap_t(n_chunk, n_slc), G, axis=1)
    ovl = jnp.asarray(ovl, BF16)
    body = functools.partial(_nsa_step_body, n_pages=n_pages, page=page, n_past=n_past)
    half_spec = lambda j, half: pl.BlockSpec((1, page, None, 2 * G, D), lambda b, pt: (pt[b, j], 0, half, 0, 0))
    const = lambda a: pl.BlockSpec(a.shape, lambda b, pt: (0,) * a.ndim, pipeline_mode=pl.Buffered(1))
    per_b = lambda shape: pl.BlockSpec((1,) + shape, lambda b, pt: (b,) + (0,) * len(shape))
    cache5 = cache.reshape(n_pool, page, 2, 2 * G, D)
    out, win_out = pl.pallas_call(
        body,
        grid_spec=pltpu.PrefetchScalarGridSpec(
            num_scalar_prefetch=1,
            grid=(Bd,),
            in_specs=[half_spec(j, 0) for j in range(n_pages)] + [half_spec(j, 1) for j in range(n_pages)]
                     + [per_b((H, D)), per_b((4 * G, D)), per_b((n_win * 2 * G, D)), per_b((2 * G, D)), per_b((1, LANES)),
                        const(pe2), const(w1c), const(w2c), const(ovl)],
            out_specs=[per_b((H, D)), per_b((n_win * 2 * G, D))],
            scratch_shapes=[pltpu.VMEM((G * n_chunk, CMP_FEAT), BF16)] * 2,
        ),
        out_shape=[jax.ShapeDtypeStruct((Bd, H, D), BF16), jax.ShapeDtypeStruct((Bd, n_win * 2 * G, D), F32)],
        compiler_params=_cparams("arbitrary"),
        name="nsa_step",
    )(page_table, *([cache5] * (2 * n_pages)), q.reshape(Bd, H, D), new_rows.reshape(Bd, 4 * G, D),
      win_state.reshape(Bd, n_win * 2 * G, D), win_new.reshape(Bd, 2 * G, D), u_small.reshape(Bd, 1, LANES),
      pe2, w1c, w2c, ovl)
    return out.reshape(Bd, H * D), win_out.reshape(win_state.shape)


def rope_tables(pos):
    half = HEAD_DIM // 2
    inv = ROPE_THETA ** (-jnp.arange(half, dtype=F32) / half)
    ang = pos.astype(F32)[:, None] * inv[None, :]
    cos, sin = jnp.cos(ang), jnp.sin(ang)
    return jnp.concatenate([cos, cos], axis=-1), jnp.concatenate([-sin, sin], axis=-1)


def _arrange_w_in(w_in):
    H, G = NSA_HEADS, NSA_KV_HEADS
    sizes = (H * HEAD_DIM, 6 * G * HEAD_DIM, 3 * H, GDN_HEADS * (2 * GDN_DK + GDN_DV), GDN_HEADS * GDN_DV, 2 * GDN_HEADS)
    o = np.cumsum((0,) + sizes).tolist()
    w_t = w_in.T
    K = w_in.shape[0]
    wide = ((o[0], o[2] - o[0]), (o[3], o[4] - o[3]), (o[4], o[5] - o[4]), (o[6], w_t.shape[0] - o[6]))
    w_ng = w_t[o[2]:o[3]].T
    w_ba = w_t[o[5]:o[6]].T
    nhb = GDN_HEADS // GDN_HB
    ng = w_ng.reshape(K, 3, G, NSA_GROUP).transpose(0, 2, 1, 3).reshape(K, G, 3 * NSA_GROUP)
    pieces = ((GDN_COL_B, w_ba[:, :GDN_HEADS].reshape(K, nhb, GDN_HB)),
              (GDN_COL_A, w_ba[:, GDN_HEADS:].reshape(K, nhb, GDN_HB)), (NSA_COL_G, ng))
    w_small_seq = _place_lanes(pieces, (K, nhb)).reshape(K, nhb * LANES).astype(BF16)
    w_small_step = _place_lanes(((STEP_COL_G, w_ng), (STEP_COL_B, w_ba)), (K,)).astype(BF16)
    return w_t.astype(BF16), wide, w_small_seq, w_small_step


def _place_lanes(pieces, lead):
    out, at = [], 0
    for lane, vals in pieces:
        out += [jnp.zeros(lead + (lane - at,), F32), vals.astype(F32)]
        at = lane + vals.shape[-1]
    return jnp.concatenate(out + [jnp.zeros(lead + (LANES - at,), F32)], axis=-1)


class _Tiles(NamedTuple):
    rows: int
    wide_rows: int
    cols: int
    ffn_cols: int
    norm_rows: int


def _tiles(n_rows):
    if n_rows >= 2048:
        return _Tiles(rows=1024, wide_rows=2048, cols=512, ffn_cols=256, norm_rows=256)
    return _Tiles(rows=n_rows, wide_rows=n_rows, cols=1024, ffn_cols=256, norm_rows=n_rows)


GDN_SEQ_ROWS = 256


def kernel(x_prompt, x_sample, cache_nsa_kv, page_table, state_win_kv, state_gdn, state_gdn_conv, state_ffn_conv, norm_mix, w_in, cmp_pe, cmp_w1, cmp_w2, gdn_conv_w, gdn_a_log, gdn_dt_bias, gdn_norm, w_nsa_out, w_gdn_out, w_o, norm_ffn, w_up, ffn_conv_w, w_down, norm_final):
    B, T, D = x_prompt.shape
    Bd = x_sample.shape[0]
    G = NSA_KV_HEADS
    n_past = page_table.shape[1] * cache_nsa_kv.shape[1]

    w_t, wide, w_small_seq, w_small_step = _arrange_w_in(w_in)
    w_nsa_b, w_gdn_b, w_o_b = w_nsa_out.astype(BF16), w_gdn_out.astype(BF16), w_o.astype(BF16)
    w_up_b, w_down_b = w_up.astype(BF16), w_down.astype(BF16)
    pe2, w1c, w2c = compress_params(cmp_pe, cmp_w1, cmp_w2)

    def project(x2d, t):
        xn = rmsnorm(x2d, norm_mix, BF16, t.norm_rows)
        return xn, [matmul_nt(xn, w_t, t.wide_rows, t.cols, r) for r in wide]

    def tail(x, mixed_nsa, mixed_gdn, u_mg, ffn_up, t):
        mixin = matmul_mix(mixed_nsa, mixed_gdn, w_nsa_b, w_gdn_b, u_mg, t.rows, t.cols)
        h = matmul_residual(mixin, w_o_b, x, t.rows, t.cols)
        hn = rmsnorm(h, norm_ffn, BF16, t.norm_rows)
        act, extra = ffn_up(hn)
        h2 = matmul_residual(act, w_down_b, h, t.rows, t.ffn_cols, x_buffers=1)
        return rmsnorm(h2, norm_final, F32, t.norm_rows), extra

    t = _tiles(B * T)
    xp = x_prompt.reshape(B * T, D)
    xn, (u_qkv, u_gqkv, u_z, u_mg) = project(xp, t)
    u_small = matmul(xn, w_small_seq, t.rows, w_small_seq.shape[1])
    cos, sin = rope_tables(jnp.arange(T, dtype=jnp.int32))
    q, kv_rows, win_rows = rope_split(u_qkv, cos, sin, t.norm_rows)
    cmp_kv = compress_seq(kv_rows, pe2, w1c, w2c, B)
    o_nsa = nsa_seq(q, kv_rows, win_rows, cmp_kv, u_small, B)
    a_rows, d_rows = gdn_param_rows(gdn_a_log, gdn_dt_bias)
    o_gdn, s_p = gdn_seq(u_gqkv, u_z, u_small, gdn_conv_w, a_rows, d_rows, gdn_norm, B, GDN_SEQ_ROWS)
    s_p = s_p.reshape(B, GDN_HEADS, GDN_DK, GDN_DV)

    def ffn_up_p(hn):
        act, tg, tv = ffn_up_seq(hn, w_up_b, ffn_conv_w, B, t.rows, t.ffn_cols)
        n_t = T // t.rows
        last = lambda a: a[n_t - 1::n_t, SUBLANES - (FFN_CONV - 1):]
        return act, jnp.concatenate([last(tg), last(tv)], axis=-1)

    y_p, ffn_p = tail(xp, o_nsa, o_gdn, u_mg, ffn_up_p, t)
    n_win_p = min(WINDOW, T)
    kv_p = kv_rows.reshape(B, T, 4, G, HEAD_DIM)
    win_p = win_rows.reshape(B, T, 2, G, HEAD_DIM)[:, T - n_win_p:]
    conv_p = u_gqkv.reshape(B, T, -1)[:, T - (GDN_CONV - 1):]

    t = _tiles(Bd)
    xs = x_sample.reshape(Bd, D)
    xn, (u_qkv, u_gqkv, u_z, u_mg) = project(xs, t)
    u_small = matmul(xn, w_small_step, t.rows, LANES)
    cos, sin = rope_tables(jnp.full((Bd,), n_past, jnp.int32))
    q, new_rows, win_new = rope_split(u_qkv, cos, sin, t.norm_rows)
    o_nsa, win_s = nsa_step(q, new_rows, win_new, cache_nsa_kv, page_table, state_win_kv, u_small, pe2, w1c, w2c)
    o_gdn, s_s = gdn_step(u_gqkv, u_z, u_small, state_gdn_conv, state_gdn, gdn_conv_w, gdn_a_log, gdn_dt_bias, gdn_norm)

    def ffn_up_s(hn):
        act, ug, uv = ffn_up_step(hn, w_up_b, ffn_conv_w, state_ffn_conv, t.ffn_cols)
        return act, jnp.concatenate([ug, uv], axis=-1)

    y_s, up_new = tail(xs, o_nsa, o_gdn, u_mg, ffn_up_s, t)
    kv_s = new_rows.reshape(Bd, 1, 4, G, HEAD_DIM)
    conv_s = jnp.concatenate([state_gdn_conv[:, 1:], u_gqkv[:, None]], axis=1)
    ffn_s = jnp.concatenate([state_ffn_conv[:, 1:], up_new[:, None]], axis=1)

    return (y_p.reshape(B, T, D), y_s.reshape(Bd, 1, D), kv_p, win_p, s_p, conv_p, ffn_p,
            kv_s, win_s, s_s, conv_s, ffn_s)
```

```python
import functools
import math
from typing import NamedTuple

import jax
import jax.numpy as jnp
import numpy as np
from jax import lax
from jax.experimental import pallas as pl
from jax.experimental.pallas import tpu as pltpu

F32 = jnp.float32
BF16 = jnp.bfloat16
HI = lax.Precision.HIGHEST

LANES = 128
SUBLANES = 8
VMEM_LIMIT_BYTES = 56 * 1024 * 1024

HEAD_DIM = 128
NSA_HEADS = 16
NSA_KV_HEADS = 4
NSA_GROUP = NSA_HEADS // NSA_KV_HEADS
CMP_LEN = 32
CMP_STRIDE = 16
CMP_HIDDEN = 256
SLC_LEN = 64
SLC_TOP_N = 16
SLC_LOCAL = 2
WINDOW = 512
GDN_HEADS = 16
GDN_DK = 128
GDN_DV = 128
GDN_CONV = 4
GDN_CHUNK = 64
FFN_CONV = 3
ROPE_THETA = 10000.0
EPS = 1e-6


def _cparams(*sem):
    return pltpu.CompilerParams(dimension_semantics=sem, vmem_limit_bytes=VMEM_LIMIT_BYTES)


def _dot(a, b, precision=None):
    return jnp.dot(a, b, preferred_element_type=F32, precision=precision)


def _dot_nt(a, b, precision=None):
    return lax.dot_general(a, b, (((1,), (1,)), ((), ())), preferred_element_type=F32, precision=precision)


def _dot_tn(a, b, precision=None):
    return lax.dot_general(a, b, (((0,), (0,)), ((), ())), preferred_element_type=F32, precision=precision)


def _sigmoid(x):
    return 1.0 / (1.0 + jnp.exp(-x))


def _silu(x):
    return x * _sigmoid(x)


def _rmsnorm_body(x_ref, w_ref, o_ref):
    x = x_ref[...]
    y = x * lax.rsqrt(jnp.mean(x * x, axis=-1, keepdims=True) + EPS)
    o_ref[...] = (y * w_ref[...]).astype(o_ref.dtype)


def rmsnorm(x, w, out_dtype, tm):
    M, D = x.shape
    return pl.pallas_call(
        _rmsnorm_body,
        grid=(M // tm,),
        in_specs=[pl.BlockSpec((tm, D), lambda i: (i, 0)), pl.BlockSpec((1, D), lambda i: (0, 0))],
        out_specs=pl.BlockSpec((tm, D), lambda i: (i, 0)),
        out_shape=jax.ShapeDtypeStruct((M, D), out_dtype),
        compiler_params=_cparams("parallel"),
        name="rmsnorm",
    )(x, w.reshape(1, D))


def _mm_body(x_ref, w_ref, o_ref):
    o_ref[...] = _dot(x_ref[...], w_ref[...]).astype(o_ref.dtype)


def matmul(x, w, tm, tn, out_dtype=F32):
    M, K = x.shape
    N = w.shape[1]
    return pl.pallas_call(
        _mm_body,
        grid=(M // tm, N // tn),
        in_specs=[pl.BlockSpec((tm, K), lambda i, j: (i, 0)), pl.BlockSpec((K, tn), lambda i, j: (0, j))],
        out_specs=pl.BlockSpec((tm, tn), lambda i, j: (i, j)),
        out_shape=jax.ShapeDtypeStruct((M, N), out_dtype),
        compiler_params=_cparams("parallel", "arbitrary"),
        name="matmul",
    )(x, w)


def _mm_nt_body(x_ref, w_ref, o_ref):
    o_ref[...] = _dot_nt(x_ref[...], w_ref[...]).astype(o_ref.dtype)


def matmul_nt(x, w_t, tm, tn, rows, out_dtype=F32):
    M, K = x.shape
    r0, N = rows
    assert r0 % (2 * SUBLANES) == 0 and N % tn == 0
    return pl.pallas_call(
        _mm_nt_body,
        grid=(M // tm, N // tn),
        in_specs=[pl.BlockSpec((tm, K), lambda i, j: (i, 0)),
                  pl.BlockSpec((pl.Element(tn), pl.Element(K)), lambda i, j: (pl.multiple_of(r0 + j * tn, 2 * SUBLANES), 0))],
        out_specs=pl.BlockSpec((tm, tn), lambda i, j: (i, j)),
        out_shape=jax.ShapeDtypeStruct((M, N), out_dtype),
        compiler_params=_cparams("parallel", "arbitrary"),
        name="matmul_nt",
    )(x, w_t)


def _mm_res_body(x_ref, w_ref, r_ref, o_ref):
    o_ref[...] = r_ref[...] + _dot(x_ref[...], w_ref[...])


def matmul_residual(x, w, res, tm, tn, x_buffers=2):
    M, K = x.shape
    N = w.shape[1]
    return pl.pallas_call(
        _mm_res_body,
        grid=(M // tm, N // tn),
        in_specs=[pl.BlockSpec((tm, K), lambda i, j: (i, 0), pipeline_mode=pl.Buffered(x_buffers)),
                  pl.BlockSpec((K, tn), lambda i, j: (0, j)),
                  pl.BlockSpec((tm, tn), lambda i, j: (i, j))],
        out_specs=pl.BlockSpec((tm, tn), lambda i, j: (i, j)),
        out_shape=jax.ShapeDtypeStruct((M, N), F32),
        compiler_params=_cparams("parallel", "arbitrary"),
        name="matmul_residual",
    )(x, w, res)


def _mm_mix_body(a_ref, b_ref, wa_ref, wb_ref, ga_ref, gb_ref, o_ref):
    ua = _dot(a_ref[...], wa_ref[...])
    ub = _dot(b_ref[...], wb_ref[...])
    gate = lambda g_ref: _sigmoid(g_ref[...].astype(F32))
    o_ref[...] = (gate(ga_ref) * ua + gate(gb_ref) * ub).astype(o_ref.dtype)


def matmul_mix(a, b, wa, wb, gates, tm, tn):
    M, K = a.shape
    N = wa.shape[1]
    nj = N // tn
    return pl.pallas_call(
        _mm_mix_body,
        grid=(M // tm, nj),
        in_specs=[pl.BlockSpec((tm, K), lambda i, j: (i, 0)), pl.BlockSpec((tm, K), lambda i, j: (i, 0)),
                  pl.BlockSpec((K, tn), lambda i, j: (0, j)), pl.BlockSpec((K, tn), lambda i, j: (0, j)),
                  pl.BlockSpec((tm, tn), lambda i, j: (i, j)), pl.BlockSpec((tm, tn), lambda i, j: (i, j + nj))],
        out_specs=pl.BlockSpec((tm, tn), lambda i, j: (i, j)),
        out_shape=jax.ShapeDtypeStruct((M, N), BF16),
        compiler_params=_cparams("parallel", "arbitrary"),
        name="matmul_mix",
    )(a, b, wa, wb, gates, gates)


def _shift_rows(x, hist, sh):
    s = pltpu.roll(x, sh, axis=0)
    head = s[:SUBLANES]
    row8 = lax.broadcasted_iota(jnp.int32, head.shape, 0)
    for r in range(sh):
        head = jnp.where(row8 == r, hist[SUBLANES - sh + r:SUBLANES - sh + r + 1], head)
    return jnp.concatenate([head, s[SUBLANES:]], axis=0)


def _ffn_up_seq_body(x_ref, wg_ref, wv_ref, cg_ref, cv_ref, act_ref, tg_ref, tv_ref, hg_ref, hv_ref, *, tiles_per_seq):
    i = pl.program_id(0)
    j = pl.program_id(1)
    first = (i % tiles_per_seq) == 0
    x = x_ref[...]

    def branch(w_ref, c_ref, hist_ref, tail_ref):
        up = _dot(x, w_ref[...])
        tm = up.shape[0]
        hist = jnp.where(first, 0.0, hist_ref[j])
        c = c_ref[...]
        out = up * c[2:3] + _shift_rows(up, hist, 2) * c[0:1] + _shift_rows(up, hist, 1) * c[1:2]
        tail = up[tm - SUBLANES:tm]
        hist_ref[j] = tail
        tail_ref[0] = tail
        return out

    gate = branch(wg_ref, cg_ref, hg_ref, tg_ref)
    val = branch(wv_ref, cv_ref, hv_ref, tv_ref)
    act_ref[...] = (_silu(gate) * val).astype(act_ref.dtype)


def ffn_up_seq(x, w_up, conv_w, n_seq, tm, tn):
    M, K = x.shape
    F = w_up.shape[1] // 2
    nj = F // tn
    tiles_per_seq = M // n_seq // tm
    body = functools.partial(_ffn_up_seq_body, tiles_per_seq=tiles_per_seq)
    return pl.pallas_call(
        body,
        grid=(M // tm, nj),
        in_specs=[pl.BlockSpec((tm, K), lambda i, j: (i, 0)),
                  pl.BlockSpec((K, tn), lambda i, j: (0, j)), pl.BlockSpec((K, tn), lambda i, j: (0, j + nj)),
                  pl.BlockSpec((FFN_CONV, tn), lambda i, j: (0, j)), pl.BlockSpec((FFN_CONV, tn), lambda i, j: (0, j + nj))],
        out_specs=[pl.BlockSpec((tm, tn), lambda i, j: (i, j)),
                   pl.BlockSpec((1, SUBLANES, tn), lambda i, j: (i, 0, j)),
                   pl.BlockSpec((1, SUBLANES, tn), lambda i, j: (i, 0, j))],
        out_shape=[jax.ShapeDtypeStruct((M, F), BF16),
                   jax.ShapeDtypeStruct((M // tm, SUBLANES, F), F32), jax.ShapeDtypeStruct((M // tm, SUBLANES, F), F32)],
        scratch_shapes=[pltpu.VMEM((nj, SUBLANES, tn), F32), pltpu.VMEM((nj, SUBLANES, tn), F32)],
        compiler_params=_cparams("arbitrary", "arbitrary"),
        name="ffn_up_seq",
    )(x, w_up, w_up, conv_w, conv_w)


def _ffn_up_step_body(x_ref, wg_ref, wv_ref, cg_ref, cv_ref, g0_ref, g1_ref, v0_ref, v1_ref, act_ref, ug_ref, uv_ref):
    x = x_ref[...]

    def branch(w_ref, c_ref, h0_ref, h1_ref, up_ref):
        up = _dot(x, w_ref[...])
        up_ref[...] = up
        c = c_ref[...]
        return up * c[2:3] + h0_ref[...] * c[0:1] + h1_ref[...] * c[1:2]

    gate = branch(wg_ref, cg_ref, g0_ref, g1_ref, ug_ref)
    val = branch(wv_ref, cv_ref, v0_ref, v1_ref, uv_ref)
    act_ref[...] = (_silu(gate) * val).astype(act_ref.dtype)


def ffn_up_step(x, w_up, conv_w, hist, tn):
    M, K = x.shape
    F2 = w_up.shape[1]
    F = F2 // 2
    nj = F // tn
    h2 = hist.reshape(M, 2 * F2)
    return pl.pallas_call(
        _ffn_up_step_body,
        grid=(nj,),
        in_specs=[pl.BlockSpec((M, K), lambda j: (0, 0)),
                  pl.BlockSpec((K, tn), lambda j: (0, j)), pl.BlockSpec((K, tn), lambda j: (0, j + nj)),
                  pl.BlockSpec((FFN_CONV, tn), lambda j: (0, j)), pl.BlockSpec((FFN_CONV, tn), lambda j: (0, j + nj)),
                  pl.BlockSpec((M, tn), lambda j: (0, j)), pl.BlockSpec((M, tn), lambda j: (0, j + 2 * nj)),
                  pl.BlockSpec((M, tn), lambda j: (0, j + nj)), pl.BlockSpec((M, tn), lambda j: (0, j + 3 * nj))],
        out_specs=[pl.BlockSpec((M, tn), lambda j: (0, j))] * 3,
        out_shape=[jax.ShapeDtypeStruct((M, F), BF16), jax.ShapeDtypeStruct((M, F), F32), jax.ShapeDtypeStruct((M, F), F32)],
        compiler_params=_cparams("arbitrary"),
        name="ffn_up_step",
    )(x, w_up, w_up, conv_w, conv_w, h2, h2, h2, h2)


def _rope_body(u_ref, cos_ref, sin_ref, q_ref, kv_ref, win_ref):
    cos = cos_ref[...]
    sin = sin_ref[...]
    scale = HEAD_DIM ** -0.5

    def rot(x):
        return x * cos + pltpu.roll(x, HEAD_DIM // 2, axis=1) * sin

    for h in range(NSA_HEADS):
        sl = slice(h * HEAD_DIM, (h + 1) * HEAD_DIM)
        q_ref[:, sl] = (rot(u_ref[:, sl]) * scale).astype(q_ref.dtype)
    base = NSA_HEADS * HEAD_DIM
    n_glob = 4 * NSA_KV_HEADS
    for slot in range(6 * NSA_KV_HEADS):
        x = u_ref[:, base + slot * HEAD_DIM:base + (slot + 1) * HEAD_DIM]
        if (slot // NSA_KV_HEADS) % 2 == 0:
            x = rot(x)
        if slot < n_glob:
            kv_ref[:, slot * HEAD_DIM:(slot + 1) * HEAD_DIM] = x
        else:
            win_ref[:, (slot - n_glob) * HEAD_DIM:(slot - n_glob + 1) * HEAD_DIM] = x


def rope_split(u, cos, sin, tm):
    M, W = u.shape
    nt = cos.shape[0] // tm
    nq = NSA_HEADS * HEAD_DIM
    ng = 4 * NSA_KV_HEADS * HEAD_DIM
    nw = 2 * NSA_KV_HEADS * HEAD_DIM
    return pl.pallas_call(
        _rope_body,
        grid=(M // tm,),
        in_specs=[pl.BlockSpec((tm, W), lambda i: (i, 0)),
                  pl.BlockSpec((tm, HEAD_DIM), lambda i: (i % nt, 0)), pl.BlockSpec((tm, HEAD_DIM), lambda i: (i % nt, 0))],
        out_specs=[pl.BlockSpec((tm, nq), lambda i: (i, 0)), pl.BlockSpec((tm, ng), lambda i: (i, 0)),
                   pl.BlockSpec((tm, nw), lambda i: (i, 0))],
        out_shape=[jax.ShapeDtypeStruct((M, nq), BF16), jax.ShapeDtypeStruct((M, ng), F32), jax.ShapeDtypeStruct((M, nw), F32)],
        compiler_params=_cparams("parallel"),
        name="rope_split",
    )(u, cos, sin)


GDN_HB = 4
GDN_COL_B = 0
GDN_COL_A = 8
NSA_COL_G = 16
GDN_STEP_GROUPS = 2


def _softplus(x):
    return jnp.maximum(x, 0.0) + jnp.log1p(jnp.exp(-jnp.abs(x)))


def _l2norm(x):
    return x * lax.rsqrt(jnp.sum(x * x, axis=-1, keepdims=True) + EPS)


def _solve_unit_lower(a_list, r_list, order):
    steps = int(math.log2(order))
    a_list, r_list = list(a_list), list(r_list)
    m, n = a_list[0].shape[1], r_list[0].shape[1]
    for i in range(steps):
        last = i == steps - 1
        for c in range(len(a_list)):
            a_b = a_list[c].astype(BF16)
            r_hi, r_lo = _split_bf16(r_list[c])
            prod = _dot(a_b, jnp.concatenate(([] if last else [a_b]) + [r_hi, r_lo], axis=1))
            if not last:
                a_list[c], prod = prod[:, :m], prod[:, m:]
            r_list[c] = r_list[c] + (prod[:, :n] + prod[:, n:])
    return r_list


def _block_diag(blocks):
    z = jnp.zeros_like(blocks[0])
    nb = len(blocks)
    return jnp.concatenate([jnp.concatenate([blocks[h] if j == h else z for j in range(nb)], axis=1)
                            for h in range(nb)], axis=0)


def _gdn_chunks_groups(groups, states):
    nh = len(groups[0][0][0])
    C, d = groups[0][0][2][0].shape
    W = nh * C
    stack = lambda xs: jnp.concatenate(xs, axis=0)
    ri = lax.broadcasted_iota(jnp.int32, (W, W), 0)
    ci = lax.broadcasted_iota(jnp.int32, (W, W), 1)
    same = (ri // C) == (ci // C)

    a_list, r_list, pre = [], [], []
    for chunks in groups:
        for q, k, v, beta, gc, gcr in chunks:
            dmat = jnp.where(same & (ri >= ci), jnp.exp(jnp.minimum(stack(gc) - jnp.concatenate(gcr, axis=1), 0.0)), 0.0)
            eg = [jnp.exp(g) for g in gc]
            kb = [k[h] * beta[h] for h in range(nh)]
            xk = [_dot_nt(jnp.concatenate([kb[h], q[h]], axis=0).astype(BF16), k[h].astype(BF16)) for h in range(nh)]
            kk = _block_diag([x[:C] for x in xk])
            qk = _block_diag([x[C:] for x in xk])
            a_list.append(jnp.where(ri > ci, -(kk * dmat), 0.0))
            r_list.append(jnp.concatenate([stack([v[h] * beta[h] for h in range(nh)]),
                                           stack([kb[h] * eg[h] for h in range(nh)])], axis=1))
            g_last = [g[C - 1:C, :] for g in gc]
            pre.append(((qk * dmat).astype(BF16), [(q[h] * eg[h]).astype(BF16) for h in range(nh)],
                        [(k[h] * jnp.exp(g_last[h] - gc[h])).astype(BF16) for h in range(nh)],
                        stack([jnp.broadcast_to(jnp.exp(g_last[h]), (d, 1)) for h in range(nh)])))
    w_list = _solve_unit_lower(a_list, r_list, C)

    n_chunks = len(groups[0])
    states = list(states)
    outs = [[] for _ in groups]
    for c in range(n_chunks):
        for g in range(len(groups)):
            w, (aqk, q_eg, kd, keep) = w_list[g * n_chunks + c], pre[g * n_chunks + c]
            s_b = states[g].astype(BF16)
            xs = [_dot(jnp.concatenate([w[h * C:(h + 1) * C, d:].astype(BF16), q_eg[h]], axis=0), s_b[h * d:(h + 1) * d])
                  for h in range(nh)]
            v_new = (w[:, :d] - stack([x[:C] for x in xs])).astype(BF16)
            outs[g].append(stack([x[C:] for x in xs]) + _dot(aqk, v_new))
            upd = stack([_dot_tn(kd[h], v_new[h * C:(h + 1) * C]) for h in range(nh)])
            states[g] = states[g] * keep + upd
    return outs, states


def _gdn_seq_body(q_ref, k_ref, v_ref, z_ref, ba_ref, cq_ref, ck_ref, cv_ref, alog_ref, dtb_ref, nw_ref,
                  o_ref, s_ref, tq_ref, tk_ref, tv_ref, qs_ref, ks_ref, vs_ref, gs_ref, bs_ref):
    t_idx = pl.program_id(2)
    tt = q_ref.shape[0]
    hb = q_ref.shape[1] // GDN_DK
    C = GDN_CHUNK

    @pl.when(t_idx == 0)
    def _():
        s_ref[...] = jnp.zeros_like(s_ref)
        tq_ref[...] = jnp.zeros_like(tq_ref)
        tk_ref[...] = jnp.zeros_like(tk_ref)
        tv_ref[...] = jnp.zeros_like(tv_ref)

    def conv(x_ref, c_ref, tail_ref):
        x = x_ref[...]
        hist = tail_ref[...]
        row = lax.broadcasted_iota(jnp.int32, x.shape, 0)
        c = c_ref[...]
        out = x * c[GDN_CONV - 1:GDN_CONV]
        for j in range(GDN_CONV - 1):
            out = out + _shift_rows(x, hist, GDN_CONV - 1 - j) * c[j:j + 1]
        tail_ref[...] = x[tt - SUBLANES:tt]
        return _silu(out)

    qa = conv(q_ref, cq_ref, tq_ref)
    ka = conv(k_ref, ck_ref, tk_ref)
    vs_ref[...] = conv(v_ref, cv_ref, tv_ref)
    for h in range(hb):
        sl = slice(h * GDN_DK, (h + 1) * GDN_DK)
        qs_ref[:, sl] = _l2norm(qa[:, sl]) * GDN_DK ** -0.5
        ks_ref[:, sl] = _l2norm(ka[:, sl])
    ba = ba_ref[...]
    bs_ref[...] = _sigmoid(ba)
    n_groups = hb // GDN_HB
    par_row = lambda ref: jnp.concatenate([ref[g, 0:1, :] for g in range(n_groups)], axis=1)
    gs_ref[...] = -jnp.exp(par_row(alog_ref)) * _softplus(ba + par_row(dtb_ref))

    tri = (lax.broadcasted_iota(jnp.int32, (C, C), 0) >= lax.broadcasted_iota(jnp.int32, (C, C), 1)).astype(F32)
    sel = (lax.broadcasted_iota(jnp.int32, (2 * SUBLANES, LANES), 0)
           == lax.broadcasted_iota(jnp.int32, (2 * SUBLANES, LANES), 1)).astype(F32)
    nw = nw_ref[...]

    chunk_rows = [slice(c * C, (c + 1) * C) for c in range(tt // C)]
    gc_alls = [_dot(tri, gs_ref[rows, :], HI) for rows in chunk_rows]
    groups, group_heads = [], []
    for g in range(n_groups):
        heads = [slice((g * GDN_HB + h) * GDN_DK, (g * GDN_HB + h + 1) * GDN_DK) for h in range(GDN_HB)]
        lanes = slice(g * LANES, (g + 1) * LANES)
        chunks = []
        for rows, gc_full in zip(chunk_rows, gc_alls):
            gc_all = gc_full[:, lanes]
            gc_t = _dot_nt(sel, gc_all, HI)
            beta_all = bs_ref[rows, lanes]
            chunks.append((
                [qs_ref[rows, sl] for sl in heads], [ks_ref[rows, sl] for sl in heads], [vs_ref[rows, sl] for sl in heads],
                [beta_all[:, GDN_COL_B + h:GDN_COL_B + h + 1] for h in range(GDN_HB)],
                [gc_all[:, GDN_COL_A + h:GDN_COL_A + h + 1] for h in range(GDN_HB)],
                [gc_t[GDN_COL_A + h:GDN_COL_A + h + 1, :] for h in range(GDN_HB)]))
        groups.append(chunks)
        group_heads.append(heads)
    gdk = GDN_HB * GDN_DK
    outs, states = _gdn_chunks_groups(groups, [s_ref[0, g * gdk:(g + 1) * gdk] for g in range(n_groups)])
    for g in range(n_groups):
        s_ref[0, g * gdk:(g + 1) * gdk] = states[g]
        for rows, o in zip(chunk_rows, outs[g]):
            on = o * lax.rsqrt(jnp.mean(o * o, axis=-1, keepdims=True) + EPS) * nw
            for h, sl in enumerate(group_heads[g]):
                o_ref[rows, sl] = (on[h * C:(h + 1) * C] * _silu(z_ref[rows, sl])).astype(o_ref.dtype)


def gdn_seq(u_qkv, u_z, u_small, conv_w, alog_rows, dtb_rows, norm_w, n_seq, tt):
    M = u_qkv.shape[0]
    T = M // n_seq
    nt = T // tt
    hb = GDN_HB * GDN_STEP_GROUPS
    nhb = GDN_HEADS // hb
    wb = hb * GDN_DK
    gl = GDN_STEP_GROUPS * LANES
    row_blk = lambda off: pl.BlockSpec((tt, wb), lambda b, h, t: (b * nt + t, h + off))
    cw_blk = lambda off: pl.BlockSpec((GDN_CONV, wb), lambda b, h, t: (0, h + off))
    par_blk = pl.BlockSpec((GDN_STEP_GROUPS, SUBLANES, LANES), lambda b, h, t: (h, 0, 0))
    return pl.pallas_call(
        _gdn_seq_body,
        grid=(n_seq, nhb, nt),
        in_specs=[row_blk(0), row_blk(nhb), row_blk(2 * nhb), row_blk(0),
                  pl.BlockSpec((tt, gl), lambda b, h, t: (b * nt + t, h)),
                  cw_blk(0), cw_blk(nhb), cw_blk(2 * nhb), par_blk, par_blk,
                  pl.BlockSpec((1, GDN_DV), lambda b, h, t: (0, 0))],
        out_specs=[row_blk(0), pl.BlockSpec((1, hb * GDN_DK, GDN_DV), lambda b, h, t: (b, h, 0))],
        out_shape=[jax.ShapeDtypeStruct((M, GDN_HEADS * GDN_DV), BF16),
                   jax.ShapeDtypeStruct((n_seq, GDN_HEADS * GDN_DK, GDN_DV), F32)],
        scratch_shapes=[pltpu.VMEM((SUBLANES, wb), F32)] * 3 + [pltpu.VMEM((tt, wb), F32)] * 3
                       + [pltpu.VMEM((tt, gl), F32)] * 2,
        compiler_params=_cparams("parallel", "parallel", "arbitrary"),
        name="gdn_seq",
    )(u_qkv, u_qkv, u_qkv, u_z, u_small, conv_w, conv_w, conv_w, alog_rows, dtb_rows, norm_w.reshape(1, GDN_DV))


def gdn_param_rows(gdn_a_log, gdn_dt_bias):
    nhb = GDN_HEADS // GDN_HB

    def rows(p):
        r = _place_lanes(((GDN_COL_A, p.reshape(nhb, GDN_HB)),), (nhb,))
        return jnp.broadcast_to(r[:, None, :], (nhb, SUBLANES, LANES))

    return rows(gdn_a_log), rows(gdn_dt_bias)


CMP_RATIO = CMP_LEN // CMP_STRIDE
CMP_FEAT = CMP_STRIDE * HEAD_DIM
MASKED = -1e30


def _compress_mlp(part, pe_ref, w1_ref, w2_ref, chunk_step=1):
    n = part.shape[0]
    pe_part = _dot(pe_ref[...], w1_ref[...])
    hid0 = pe_part[0:1, :CMP_HIDDEN] + pe_part[1:2, CMP_HIDDEN:]
    hid = hid0 + part[:, :CMP_HIDDEN]
    hid = hid + pltpu.roll(part[:, CMP_HIDDEN:], n - chunk_step, axis=0)
    return _dot(_silu(hid).astype(BF16), w2_ref[...])


def _compress_seq_body(x_ref, pe_ref, w1_ref, w2_ref, o_ref, xc_ref):
    n = o_ref.shape[3]
    for s in range(CMP_STRIDE):
        xc_ref[:, s * HEAD_DIM:(s + 1) * HEAD_DIM] = x_ref[pl.ds(s, n, stride=CMP_STRIDE), :].astype(BF16)
    part = _dot(xc_ref[...], w1_ref[0])
    o_ref[0, 0, 0] = _compress_mlp(part, pe_ref.at[0], w1_ref.at[0], w2_ref.at[0])


def compress_seq(kv_rows, pe2, w1c, w2c, n_seq):
    M = kv_rows.shape[0]
    T = M // n_seq
    n = T // CMP_STRIDE
    G = NSA_KV_HEADS
    return pl.pallas_call(
        _compress_seq_body,
        grid=(n_seq, 2, G),
        in_specs=[pl.BlockSpec((T, HEAD_DIM), lambda b, c, g: (b, c * G + g)),
                  pl.BlockSpec((1, SUBLANES, CMP_FEAT), lambda b, c, g: (c, 0, 0)),
                  pl.BlockSpec((1, CMP_FEAT, 2 * CMP_HIDDEN), lambda b, c, g: (c, 0, 0)),
                  pl.BlockSpec((1, CMP_HIDDEN, HEAD_DIM), lambda b, c, g: (c, 0, 0))],
        out_specs=pl.BlockSpec((1, 1, 1, n, HEAD_DIM), lambda b, c, g: (b, c, g, 0, 0)),
        out_shape=jax.ShapeDtypeStruct((n_seq, 2, G, n, HEAD_DIM), F32),
        scratch_shapes=[pltpu.VMEM((n, CMP_FEAT), BF16)],
        compiler_params=_cparams("parallel", "parallel", "parallel"),
        name="compress_seq",
    )(kv_rows, pe2, w1c, w2c)


def compress_params(cmp_pe, cmp_w1, cmp_w2):
    w1 = cmp_w1.reshape(2, CMP_RATIO, CMP_FEAT, CMP_HIDDEN)
    w1c = jnp.concatenate([w1[:, r] for r in range(CMP_RATIO)], axis=-1).astype(BF16)
    pe = cmp_pe.reshape(2, CMP_RATIO, CMP_FEAT)
    pe2 = jnp.zeros((2, SUBLANES, CMP_FEAT), F32).at[:, :CMP_RATIO].set(pe).astype(BF16)
    return pe2, w1c, cmp_w2.astype(BF16)


def _masked_softmax(s, mask, axis=-1):
    s = jnp.where(mask, s, -jnp.inf)
    m = jnp.max(s, axis=axis, keepdims=True)
    m = jnp.where(m > -jnp.inf, m, 0.0)
    e = jnp.exp(s - m)
    return e * (1.0 / jnp.maximum(jnp.sum(e, axis=axis, keepdims=True), 1e-30))


def _masked_softmax_heads(s, mask, n_heads):
    t = mask.shape[0]
    return jnp.concatenate([_masked_softmax(s[h * t:(h + 1) * t], mask) for h in range(n_heads)], axis=0)


def _split_bf16(x):
    hi = x.astype(BF16)
    return hi, (x - hi.astype(F32)).astype(BF16)


def _select_blocks_t(imp_t, pos_row, n_blocks):
    J = imp_t.shape[0]
    j = lax.broadcasted_iota(jnp.int32, imp_t.shape, 0)
    cur = pos_row // SLC_LEN
    forced = (j == 0) | ((j <= cur) & (j > cur - SLC_LOCAL))
    score = jnp.where(j * SLC_LEN > pos_row, -jnp.inf, jnp.where(forced, jnp.inf, imp_t))
    score = jnp.where(j < n_blocks, score, -jnp.inf)
    tiles = [score[r:r + SUBLANES] for r in range(0, J, SUBLANES)]
    ranks = [jnp.zeros(t.shape, F32) for t in tiles]
    for jp in range(n_blocks):
        row = score[jp:jp + 1, :]
        for i, t in enumerate(tiles):
            lo_j = i * SUBLANES
            ge, gt = jnp.where(row >= t, 1.0, 0.0), jnp.where(row > t, 1.0, 0.0)
            if lo_j > jp:
                ahead = ge
            elif lo_j + SUBLANES - 1 <= jp:
                ahead = gt
            else:
                ahead = jnp.where(lax.broadcasted_iota(jnp.int32, t.shape, 0) + lo_j > jp, ge, gt)
            ranks[i] = ranks[i] + ahead
    rank = jnp.concatenate(ranks, axis=0)
    keep = (rank < float(min(SLC_TOP_N, n_blocks))) & (j < n_blocks)
    return jnp.where(keep, 1.0, 0.0)


def _nsa_seq_body(q_ref, kc_ref, vc_ref, ks_ref, vs_ref, kw_ref, vw_ref, gl_ref, ovl_ref, o_ref, m_ref, l_ref, acc_ref, *, kblk):
    i = pl.program_id(2)
    tq = q_ref.shape[0]
    T = ks_ref.shape[0]
    H = NSA_GROUP
    D = HEAD_DIM
    n_slc = T // SLC_LEN
    qb = q_ref[...]
    q4 = jnp.concatenate([qb[:, h * D:(h + 1) * D] for h in range(H)], axis=0)
    t0 = i * tq
    pos1 = t0 + lax.broadcasted_iota(jnp.int32, (tq, 1), 0)

    kc = kc_ref[0, 0, 0].astype(BF16)
    vc = vc_ref[0, 0, 0].astype(BF16)
    n_c = kc.shape[0]
    s = _dot_nt(q4, kc)
    cmp_end = lax.broadcasted_iota(jnp.int32, (1, n_c), 1) * CMP_STRIDE + (CMP_LEN - 1)
    p = _masked_softmax_heads(s, cmp_end <= pos1, H)
    o_c = _dot(p.astype(BF16), vc)

    psum = p[0:tq]
    for h in range(1, H):
        psum = psum + p[h * tq:(h + 1) * tq]
    p_hi, p_lo = _split_bf16(psum)
    ovl = ovl_ref[...]
    imp_t = _dot_nt(ovl, p_hi) + _dot_nt(ovl, p_lo)
    pos_row = t0 + lax.broadcasted_iota(jnp.int32, (1, tq), 1)
    sel_t = _select_blocks_t(imp_t, pos_row, n_slc)
    eye = (lax.broadcasted_iota(jnp.int32, (tq, tq), 0) == lax.broadcasted_iota(jnp.int32, (tq, tq), 1))
    sel = _dot_nt(jnp.where(eye, 1.0, 0.0).astype(BF16), sel_t.astype(BF16)).astype(BF16)

    n_kb = (t0 + tq + kblk - 1) // kblk

    m_ref[...] = jnp.full(m_ref.shape, MASKED, F32)
    l_ref[...] = jnp.zeros(l_ref.shape, F32)
    acc_ref[...] = jnp.zeros(acc_ref.shape, F32)

    def kv_step(kb, carry):
        k0 = pl.multiple_of(kb * kblk, kblk)
        kk = ks_ref[pl.ds(k0, kblk), :].astype(BF16)
        vv = vs_ref[pl.ds(k0, kblk), :].astype(BF16)
        s2 = _dot_nt(q4, kk)
        kpos = k0 + lax.broadcasted_iota(jnp.int32, (1, kblk), 1)
        expand = (lax.broadcasted_iota(jnp.int32, (n_slc, kblk), 0) == kpos // SLC_LEN)
        chosen = _dot(sel, jnp.where(expand, 1.0, 0.0).astype(BF16))
        bias = jnp.where((chosen > 0.5) & (kpos <= pos1), 0.0, MASKED)
        es = []
        for h in range(H):
            rows = slice(h * tq, (h + 1) * tq)
            sh = s2[rows] + bias
            m_old = m_ref[rows]
            m_new = jnp.maximum(m_old, jnp.max(sh, axis=-1, keepdims=True))
            alpha = jnp.exp(m_old - m_new)
            e = jnp.exp(sh - m_new)
            m_ref[rows] = m_new
            l_ref[rows] = alpha * l_ref[rows] + jnp.sum(e, axis=-1, keepdims=True)
            acc_ref[rows] = alpha * acc_ref[rows]
            es.append(e.astype(BF16))
        acc_ref[...] += _dot(jnp.concatenate(es, axis=0), vv)
        return carry

    lax.fori_loop(0, n_kb, kv_step, 0)
    o_s = acc_ref[...] * (1.0 / jnp.maximum(l_ref[...], 1e-30))

    band = WINDOW + tq
    w0 = pl.multiple_of(jnp.maximum(t0 - WINDOW, 0), tq)
    kw = kw_ref[pl.ds(w0, band), :].astype(BF16)
    vw = vw_ref[pl.ds(w0, band), :].astype(BF16)
    sw = _dot_nt(q4, kw)
    diff = pos1 - (w0 + lax.broadcasted_iota(jnp.int32, (1, band), 1))
    pw = _masked_softmax_heads(sw, (diff >= 0) & (diff < WINDOW), H)
    o_w = _dot(pw.astype(BF16), vw)

    gates = _sigmoid(gl_ref[...])
    for h in range(H):
        rows = slice(h * tq, (h + 1) * tq)
        g = lambda br: gates[:, NSA_COL_G + H * br + h:NSA_COL_G + H * br + h + 1]
        o_ref[:, h * D:(h + 1) * D] = (g(0) * o_c[rows] + g(1) * o_s[rows] + g(2) * o_w[rows]).astype(o_ref.dtype)


def _overlap_t(n_cmp_rows, n_slc):
    start = np.arange(n_cmp_rows)[None, :] * CMP_STRIDE
    blk = np.arange(n_slc)[:, None] * SLC_LEN
    return ((start < blk + SLC_LEN) & (start + CMP_LEN > blk)).astype(np.float32)


def nsa_seq(q, kv_rows, win_rows, cmp_kv, u_small, n_seq, tq=256, kblk=1024):
    M = q.shape[0]
    T = M // n_seq
    nt = T // tq
    G = NSA_KV_HEADS
    gw = NSA_GROUP * HEAD_DIM
    n_c = cmp_kv.shape[3]
    ovl = jnp.asarray(_overlap_t(n_c, T // SLC_LEN), BF16)
    body = functools.partial(_nsa_seq_body, kblk=kblk)
    seq_blk = lambda off: pl.BlockSpec((T, HEAD_DIM), lambda b, g, i: (b, g + off))
    cmp_blk = lambda c: pl.BlockSpec((1, 1, 1, n_c, HEAD_DIM), lambda b, g, i: (b, c, g, 0, 0))
    return pl.pallas_call(
        body,
        grid=(n_seq, G, nt),
        in_specs=[pl.BlockSpec((tq, gw), lambda b, g, i: (b * nt + i, g)),
                  cmp_blk(0), cmp_blk(1), seq_blk(2 * G), seq_blk(3 * G), seq_blk(0), seq_blk(G),
                  pl.BlockSpec((tq, LANES), lambda b, g, i: (b * nt + i, g)),
                  pl.BlockSpec(ovl.shape, lambda b, g, i: (0, 0))],
        out_specs=pl.BlockSpec((tq, gw), lambda b, g, i: (b * nt + i, g)),
        out_shape=jax.ShapeDtypeStruct((M, NSA_HEADS * HEAD_DIM), BF16),
        scratch_shapes=[pltpu.VMEM((NSA_GROUP * tq, 1), F32)] * 2 + [pltpu.VMEM((NSA_GROUP * tq, HEAD_DIM), F32)],
        compiler_params=_cparams("parallel", "parallel", "arbitrary"),
        name="nsa_seq",
    )(q, cmp_kv, cmp_kv, kv_rows, kv_rows, win_rows, win_rows, u_small, ovl)


STEP_COL_G = 0
STEP_COL_B = 3 * NSA_HEADS
STEP_COL_A = STEP_COL_B + GDN_HEADS


def _gdn_step_prep_body(x_ref, h_ref, c_ref, us_ref, alog_ref, dtb_ref, q_ref, k_ref, v_ref, b_ref, e_ref):
    W = x_ref.shape[1]
    c = c_ref[...]
    x = x_ref[...] * c[GDN_CONV - 1:GDN_CONV]
    for j in range(GDN_CONV - 1):
        x = x + h_ref[:, j * W:(j + 1) * W] * c[j:j + 1]
    x = _silu(x)
    us = us_ref[...]
    beta = _sigmoid(us)
    eg = jnp.exp(-jnp.exp(alog_ref[...]) * _softplus(us + dtb_ref[...]))
    nq = GDN_HEADS * GDN_DK
    for h in range(GDN_HEADS):
        sl = slice(h * GDN_DK, (h + 1) * GDN_DK)
        q_ref[:, sl] = _l2norm(x[:, sl]) * GDN_DK ** -0.5
        k_ref[:, sl] = _l2norm(x[:, nq + h * GDN_DK:nq + (h + 1) * GDN_DK])
        b_ref[:, sl] = jnp.broadcast_to(beta[:, STEP_COL_B + h:STEP_COL_B + h + 1], (x.shape[0], GDN_DK))
        e_ref[:, sl] = jnp.broadcast_to(eg[:, STEP_COL_A + h:STEP_COL_A + h + 1], (x.shape[0], GDN_DK))
    v_ref[...] = x[:, 2 * nq:]


def _gdn_step_body(q_ref, k_ref, v_ref, b_ref, e_ref, z_ref, nw_ref, s_ref, o_ref, so_ref):
    bb = q_ref.shape[0]
    eye = jnp.where(lax.broadcasted_iota(jnp.int32, (GDN_DK, GDN_DK), 0)
                    == lax.broadcasted_iota(jnp.int32, (GDN_DK, GDN_DK), 1), 1.0, 0.0)
    nw = nw_ref[...]

    def per_seq(bi, carry):
        q, k, v, beta, eg, z = q_ref[bi], k_ref[bi], v_ref[bi], b_ref[bi], e_ref[bi], z_ref[bi]
        k_t = _dot_nt(eye, k, HI)
        q_t = _dot_nt(eye, q, HI)
        outs = []
        for h in range(GDN_HEADS):
            r = slice(h, h + 1)
            s = s_ref[bi, h]
            kcol = k_t[:, h:h + 1]
            k_s = jnp.sum(kcol * s, axis=0, keepdims=True)
            q_s = jnp.sum(q_t[:, h:h + 1] * s, axis=0, keepdims=True)
            v_new = v[r] * beta[r] - (beta[r] * eg[r]) * k_s
            qk = jnp.sum(q[r] * k[r], axis=-1, keepdims=True)
            o = eg[r] * q_s + qk * v_new
            so_ref[bi, h] = s * eg[r] + kcol * v_new
            on = o * lax.rsqrt(jnp.mean(o * o, axis=-1, keepdims=True) + EPS) * nw
            outs.append(on * _silu(z[r]))
        o_ref[bi] = jnp.concatenate(outs, axis=0).astype(o_ref.dtype)
        return carry

    lax.fori_loop(0, bb, per_seq, 0)


def gdn_step(u_qkv, u_z, u_small, hist, state, conv_w, a_log, dt_bias, norm_w, bb=4):
    Bd, W = u_qkv.shape
    H = GDN_HEADS
    nq = H * GDN_DK
    row = lambda p: _place_lanes(((STEP_COL_A, p.reshape(1, H)),), (1,))
    full = lambda shape: pl.BlockSpec(shape, lambda i: (0,) * len(shape))
    outs = pl.pallas_call(
        _gdn_step_prep_body,
        grid=(1,),
        in_specs=[full((Bd, W)), full((Bd, (GDN_CONV - 1) * W)), full((GDN_CONV, W)), full((Bd, LANES)),
                  full((1, LANES)), full((1, LANES))],
        out_specs=[full((Bd, nq))] * 5,
        out_shape=[jax.ShapeDtypeStruct((Bd, nq), F32)] * 5,
        compiler_params=_cparams("arbitrary"),
        name="gdn_step_prep",
    )(u_qkv, hist.reshape(Bd, (GDN_CONV - 1) * W), conv_w, u_small, row(a_log), row(dt_bias))
    heads = lambda a: a.reshape(Bd, H, GDN_DK)
    vec_blk = pl.BlockSpec((bb, H, GDN_DK), lambda i: (i, 0, 0))
    st_blk = pl.BlockSpec((bb, H, GDN_DK, GDN_DV), lambda i: (i, 0, 0, 0))
    o, s_new = pl.pallas_call(
        _gdn_step_body,
        grid=(Bd // bb,),
        in_specs=[vec_blk] * 6 + [pl.BlockSpec((1, GDN_DV), lambda i: (0, 0)), st_blk],
        out_specs=[vec_blk, st_blk],
        out_shape=[jax.ShapeDtypeStruct((Bd, H, GDN_DV), BF16), jax.ShapeDtypeStruct(state.shape, F32)],
        compiler_params=_cparams("parallel"),
        name="gdn_step",
    )(*[heads(a) for a in outs], heads(u_z), norm_w.reshape(1, GDN_DV), state)
    return o.reshape(Bd, H * GDN_DV), s_new


def _softmax_with_new(s_past, valid, s_new):
    s_past = jnp.where(valid, s_past, -jnp.inf)
    m = jnp.maximum(jnp.max(s_past, axis=-1, keepdims=True), s_new)
    e = jnp.exp(s_past - m)
    e_new = jnp.exp(s_new - m)
    return e, e_new, 1.0 / (jnp.sum(e, axis=-1, keepdims=True) + e_new)


def _nsa_step_body(pt_ref, *refs, n_pages, page, n_past):
    lo = refs[:n_pages]
    hi = refs[n_pages:2 * n_pages]
    (q_ref, new_ref, win_ref, wnew_ref, gl_ref, pe_ref, w1_ref, w2_ref, ovl_ref, o_ref, wout_ref, xk_ref, xv_ref) = refs[2 * n_pages:]
    G, H, D = NSA_KV_HEADS, NSA_HEADS, HEAD_DIM
    HS = 2 * G
    pos = n_past
    cpp = page // CMP_STRIDE
    n_chunk = n_pages * cpp
    L = n_pages * page
    q = q_ref[0]
    qf = q.astype(F32)
    new = new_ref[0]
    row_g = lax.broadcasted_iota(jnp.int32, (H, 1), 0) // NSA_GROUP

    def per_head_rows(rows):
        return jnp.concatenate([jnp.broadcast_to(rows[g:g + 1], (NSA_GROUP, rows.shape[1])) for g in range(G)], axis=0)

    first_half = lax.broadcasted_iota(jnp.int32, (SUBLANES, D), 0) < G
    for j in range(n_pages):
        for s in range(CMP_STRIDE):
            for m in range(cpp // 4):
                ks, vs = [], []
                for pair in range(2):
                    ta = lo[j][0, (4 * m + 2 * pair) * CMP_STRIDE + s]
                    tb = lo[j][0, (4 * m + 2 * pair + 1) * CMP_STRIDE + s]
                    ks.append(jnp.where(first_half, ta, pltpu.roll(tb, G, axis=0)))
                    vs.append(jnp.where(first_half, pltpu.roll(ta, G, axis=0), tb))
                r0 = (j * cpp + 4 * m) * G
                xk_ref[r0:r0 + 4 * G, s * D:(s + 1) * D] = jnp.concatenate(ks, axis=0).astype(BF16)
                xv_ref[r0:r0 + 4 * G, s * D:(s + 1) * D] = jnp.concatenate(vs, axis=0).astype(BF16)
    ckv = []
    for c, x_ref in enumerate((xk_ref, xv_ref)):
        part = _dot(x_ref[...], w1_ref[c])
        ckv.append(_compress_mlp(part, pe_ref.at[c], w1_ref.at[c], w2_ref.at[c], G).astype(BF16))

    n_all = G * n_chunk
    s = _dot_nt(q, ckv[0])
    col = lax.broadcasted_iota(jnp.int32, (1, n_all), 1)
    blk = col // G
    ok = (col % G == row_g) & (blk * CMP_STRIDE + (CMP_LEN - 1) <= pos) & (blk < n_chunk - CMP_RATIO + 1)
    p = _masked_softmax(s, ok)
    o_c = _dot(p.astype(BF16), ckv[1])

    gsum = jnp.where(lax.broadcasted_iota(jnp.int32, (LANES, H), 0) == lax.broadcasted_iota(jnp.int32, (LANES, H), 1) // NSA_GROUP, 1.0, 0.0)
    psum = _dot(gsum, p, HI)
    p_hi, p_lo = _split_bf16(psum)
    ovl = ovl_ref[...]
    imp_t = _dot_nt(ovl, p_hi) + _dot_nt(ovl, p_lo)
    n_slc = -(-(L + 1) // SLC_LEN)
    sel_t = _select_blocks_t(imp_t, jnp.full((1, LANES), pos, jnp.int32), n_slc)
    J = sel_t.shape[0]
    eye8 = jnp.where(lax.broadcasted_iota(jnp.int32, (SUBLANES, LANES), 0) == lax.broadcasted_iota(jnp.int32, (SUBLANES, LANES), 1), 1.0, 0.0)
    sel = _dot_nt(eye8.astype(BF16), sel_t.astype(BF16))

    def slot_softmax(sc, tok_ok, s_new):
        slot = lax.broadcasted_iota(jnp.int32, (1, sc.shape[1]), 1) % HS
        return _softmax_with_new(sc, (slot == row_g) & tok_ok, s_new)

    tok = lax.broadcasted_iota(jnp.int32, (1, L * HS), 1) // HS
    chosen = per_head_rows(jnp.concatenate(
        [jnp.broadcast_to(sel[:, b:b + 1], (SUBLANES, SLC_LEN * HS)) for b in range(L // SLC_LEN)], axis=1))
    kv_hi = [hi[j][0].reshape(page * HS, D).astype(BF16) for j in range(n_pages)]
    sc = jnp.concatenate([_dot_nt(q, kv) for kv in kv_hi], axis=1)
    s_new = jnp.sum(qf * per_head_rows(new[2 * G:3 * G]), axis=-1, keepdims=True)
    e, e_new, inv = slot_softmax(sc, (chosen > 0.5) & (tok <= pos), s_new)
    e = pltpu.roll(e, G, axis=1).astype(BF16)
    acc = jnp.zeros((H, D), F32)
    for j, kv in enumerate(kv_hi):
        acc = acc + _dot(e[:, j * page * HS:(j + 1) * page * HS], kv)
    o_s = (acc + e_new * per_head_rows(new[3 * G:4 * G])) * inv

    n_win = win_ref.shape[1] // HS
    wnew = wnew_ref[0]
    win = win_ref[0].astype(BF16)
    wtok = pos - n_win + lax.broadcasted_iota(jnp.int32, (1, n_win * HS), 1) // HS
    sw_new = jnp.sum(qf * per_head_rows(wnew[0:G]), axis=-1, keepdims=True)
    ew, ew_new, winv = slot_softmax(_dot_nt(q, win), (pos - wtok < WINDOW) & (wtok >= 0), sw_new)
    o_w = (_dot(pltpu.roll(ew, G, axis=1).astype(BF16), win) + ew_new * per_head_rows(wnew[G:2 * G])) * winv

    gl = jnp.broadcast_to(_sigmoid(gl_ref[0]), (SUBLANES, LANES))
    hh = lax.broadcasted_iota(jnp.int32, (H, LANES), 0)
    cc = lax.broadcasted_iota(jnp.int32, (H, LANES), 1)
    gate = lambda br: _dot_nt(jnp.where(cc == STEP_COL_G + br * H + hh, 1.0, 0.0), gl, HI)[:, 0:1]
    o_ref[0] = (gate(0) * o_c + gate(1) * o_s + gate(2) * o_w).astype(o_ref.dtype)

    wout_ref[0, :(n_win - 1) * HS] = win_ref[0, HS:]
    wout_ref[0, (n_win - 1) * HS:] = wnew


def nsa_step(q, new_rows, win_new, cache, page_table, win_state, u_small, pe2, w1c, w2c):
    Bd = q.shape[0]
    n_pool, page = cache.shape[:2]
    n_pages = page_table.shape[1]
    G, H, D = NSA_KV_HEADS, NSA_HEADS, HEAD_DIM
    n_past = n_pages * page
    n_chunk = n_past // CMP_STRIDE
    n_win = win_state.shape[1]
    assert n_past % SLC_LEN == 0 and n_win == WINDOW <= n_past and page % (4 * CMP_STRIDE) == 0 and 2 * G == SUBLANES
    n_slc = -(-(n_past + 1) // SLC_LEN)
    J = -(-n_slc // (2 * SUBLANES)) * (2 * SUBLANES)
    ovl = np.zeros((J, n_chunk * G), np.float32)
    ovl[:n_slc] = np.repeat(_overlap_t(n_chunk, n_slc), G, axis=1)
    ovl = jnp.asarray(ovl, BF16)
    body = functools.partial(_nsa_step_body, n_pages=n_pages, page=page, n_past=n_past)
    half_spec = lambda j, half: pl.BlockSpec((1, page, None, 2 * G, D), lambda b, pt: (pt[b, j], 0, half, 0, 0))
    const = lambda a: pl.BlockSpec(a.shape, lambda b, pt: (0,) * a.ndim, pipeline_mode=pl.Buffered(1))
    per_b = lambda shape: pl.BlockSpec((1,) + shape, lambda b, pt: (b,) + (0,) * len(shape))
    cache5 = cache.reshape(n_pool, page, 2, 2 * G, D)
    out, win_out = pl.pallas_call(
        body,
        grid_spec=pltpu.PrefetchScalarGridSpec(
            num_scalar_prefetch=1,
            grid=(Bd,),
            in_specs=[half_spec(j, 0) for j in range(n_pages)] + [half_spec(j, 1) for j in range(n_pages)]
                     + [per_b((H, D)), per_b((4 * G, D)), per_b((n_win * 2 * G, D)), per_b((2 * G, D)), per_b((1, LANES)),
                        const(pe2), const(w1c), const(w2c), const(ovl)],
            out_specs=[per_b((H, D)), per_b((n_win * 2 * G, D))],
            scratch_shapes=[pltpu.VMEM((G * n_chunk, CMP_FEAT), BF16)] * 2,
        ),
        out_shape=[jax.ShapeDtypeStruct((Bd, H, D), BF16), jax.ShapeDtypeStruct((Bd, n_win * 2 * G, D), F32)],
        compiler_params=_cparams("arbitrary"),
        name="nsa_step",
    )(page_table, *([cache5] * (2 * n_pages)), q.reshape(Bd, H, D), new_rows.reshape(Bd, 4 * G, D),
      win_state.reshape(Bd, n_win * 2 * G, D), win_new.reshape(Bd, 2 * G, D), u_small.reshape(Bd, 1, LANES),
      pe2, w1c, w2c, ovl)
    return out.reshape(Bd, H * D), win_out.reshape(win_state.shape)


def rope_tables(pos):
    half = HEAD_DIM // 2
    inv = ROPE_THETA ** (-jnp.arange(half, dtype=F32) / half)
    ang = pos.astype(F32)[:, None] * inv[None, :]
    cos, sin = jnp.cos(ang), jnp.sin(ang)
    return jnp.concatenate([cos, cos], axis=-1), jnp.concatenate([-sin, sin], axis=-1)


def _arrange_w_in(w_in):
    H, G = NSA_HEADS, NSA_KV_HEADS
    sizes = (H * HEAD_DIM, 6 * G * HEAD_DIM, 3 * H, GDN_HEADS * (2 * GDN_DK + GDN_DV), GDN_HEADS * GDN_DV, 2 * GDN_HEADS)
    o = np.cumsum((0,) + sizes).tolist()
    w_t = w_in.T
    K = w_in.shape[0]
    wide = ((o[0], o[2] - o[0]), (o[3], o[4] - o[3]), (o[4], o[5] - o[4]), (o[6], w_t.shape[0] - o[6]))
    w_ng = w_t[o[2]:o[3]].T
    w_ba = w_t[o[5]:o[6]].T
    nhb = GDN_HEADS // GDN_HB
    ng = w_ng.reshape(K, 3, G, NSA_GROUP).transpose(0, 2, 1, 3).reshape(K, G, 3 * NSA_GROUP)
    pieces = ((GDN_COL_B, w_ba[:, :GDN_HEADS].reshape(K, nhb, GDN_HB)),
              (GDN_COL_A, w_ba[:, GDN_HEADS:].reshape(K, nhb, GDN_HB)), (NSA_COL_G, ng))
    w_small_seq = _place_lanes(pieces, (K, nhb)).reshape(K, nhb * LANES).astype(BF16)
    w_small_step = _place_lanes(((STEP_COL_G, w_ng), (STEP_COL_B, w_ba)), (K,)).astype(BF16)
    return w_t.astype(BF16), wide, w_small_seq, w_small_step


def _place_lanes(pieces, lead):
    out, at = [], 0
    for lane, vals in pieces:
        out += [jnp.zeros(lead + (lane - at,), F32), vals.astype(F32)]
        at = lane + vals.shape[-1]
    return jnp.concatenate(out + [jnp.zeros(lead + (LANES - at,), F32)], axis=-1)


class _Tiles(NamedTuple):
    rows: int
    wide_rows: int
    cols: int
    ffn_cols: int
    norm_rows: int


def _tiles(n_rows):
    if n_rows >= 2048:
        return _Tiles(rows=1024, wide_rows=2048, cols=512, ffn_cols=256, norm_rows=256)
    return _Tiles(rows=n_rows, wide_rows=n_rows, cols=1024, ffn_cols=256, norm_rows=n_rows)


GDN_SEQ_ROWS = 256


def kernel(x_prompt, x_sample, cache_nsa_kv, page_table, state_win_kv, state_gdn, state_gdn_conv, state_ffn_conv, norm_mix, w_in, cmp_pe, cmp_w1, cmp_w2, gdn_conv_w, gdn_a_log, gdn_dt_bias, gdn_norm, w_nsa_out, w_gdn_out, w_o, norm_ffn, w_up, ffn_conv_w, w_down, norm_final):
    B, T, D = x_prompt.shape
    Bd = x_sample.shape[0]
    G = NSA_KV_HEADS
    n_past = page_table.shape[1] * cache_nsa_kv.shape[1]

    w_t, wide, w_small_seq, w_small_step = _arrange_w_in(w_in)
    w_nsa_b, w_gdn_b, w_o_b = w_nsa_out.astype(BF16), w_gdn_out.astype(BF16), w_o.astype(BF16)
    w_up_b, w_down_b = w_up.astype(BF16), w_down.astype(BF16)
    pe2, w1c, w2c = compress_params(cmp_pe, cmp_w1, cmp_w2)

    def project(x2d, t):
        xn = rmsnorm(x2d, norm_mix, BF16, t.norm_rows)
        dtypes = (F32, F32, F32, BF16)
        return xn, [matmul_nt(xn, w_t, t.wide_rows, t.cols, r, dt) for r, dt in zip(wide, dtypes)]

    def tail(x, mixed_nsa, mixed_gdn, u_mg, ffn_up, t):
        mixin = matmul_mix(mixed_nsa, mixed_gdn, w_nsa_b, w_gdn_b, u_mg, t.rows, t.cols)
        h = matmul_residual(mixin, w_o_b, x, t.rows, t.cols)
        hn = rmsnorm(h, norm_ffn, BF16, t.norm_rows)
        act, extra = ffn_up(hn)
        h2 = matmul_residual(act, w_down_b, h, t.rows, t.ffn_cols, x_buffers=1)
        return rmsnorm(h2, norm_final, F32, t.norm_rows), extra

    t = _tiles(B * T)
    xp = x_prompt.reshape(B * T, D)
    xn, (u_qkv, u_gqkv, u_z, u_mg) = project(xp, t)
    u_small = matmul(xn, w_small_seq, t.rows, w_small_seq.shape[1])
    cos, sin = rope_tables(jnp.arange(T, dtype=jnp.int32))
    q, kv_rows, win_rows = rope_split(u_qkv, cos, sin, t.norm_rows)
    cmp_kv = compress_seq(kv_rows, pe2, w1c, w2c, B)
    o_nsa = nsa_seq(q, kv_rows, win_rows, cmp_kv, u_small, B)
    a_rows, d_rows = gdn_param_rows(gdn_a_log, gdn_dt_bias)
    o_gdn, s_p = gdn_seq(u_gqkv, u_z, u_small, gdn_conv_w, a_rows, d_rows, gdn_norm, B, GDN_SEQ_ROWS)
    s_p = s_p.reshape(B, GDN_HEADS, GDN_DK, GDN_DV)

    def ffn_up_p(hn):
        act, tg, tv = ffn_up_seq(hn, w_up_b, ffn_conv_w, B, t.rows, t.ffn_cols)
        n_t = T // t.rows
        last = lambda a: a[n_t - 1::n_t, SUBLANES - (FFN_CONV - 1):]
        return act, jnp.concatenate([last(tg), last(tv)], axis=-1)

    y_p, ffn_p = tail(xp, o_nsa, o_gdn, u_mg, ffn_up_p, t)
    n_win_p = min(WINDOW, T)
    kv_p = kv_rows.reshape(B, T, 4, G, HEAD_DIM)
    win_p = win_rows.reshape(B, T, 2, G, HEAD_DIM)[:, T - n_win_p:]
    conv_p = u_gqkv.reshape(B, T, -1)[:, T - (GDN_CONV - 1):]

    t = _tiles(Bd)
    xs = x_sample.reshape(Bd, D)
    xn, (u_qkv, u_gqkv, u_z, u_mg) = project(xs, t)
    u_small = matmul(xn, w_small_step, t.rows, LANES)
    cos, sin = rope_tables(jnp.full((Bd,), n_past, jnp.int32))
    q, new_rows, win_new = rope_split(u_qkv, cos, sin, t.norm_rows)
    o_nsa, win_s = nsa_step(q, new_rows, win_new, cache_nsa_kv, page_table, state_win_kv, u_small, pe2, w1c, w2c)
    o_gdn, s_s = gdn_step(u_gqkv, u_z, u_small, state_gdn_conv, state_gdn, gdn_conv_w, gdn_a_log, gdn_dt_bias, gdn_norm)

    def ffn_up_s(hn):
        act, ug, uv = ffn_up_step(hn, w_up_b, ffn_conv_w, state_ffn_conv, t.ffn_cols)
        return act, jnp.concatenate([ug, uv], axis=-1)

    y_s, up_new = tail(xs, o_nsa, o_gdn, u_mg, ffn_up_s, t)
    kv_s = new_rows.reshape(Bd, 1, 4, G, HEAD_DIM)
    conv_s = jnp.concatenate([state_gdn_conv[:, 1:], u_gqkv[:, None]], axis=1)
    ffn_s = jnp.concatenate([state_ffn_conv[:, 1:], up_new[:, None]], axis=1)

    return (y_p.reshape(B, T, D), y_s.reshape(Bd, 1, D), kv_p, win_p, s_p, conv_p, ffn_p,
            kv_s, win_s, s_s, conv_s, ffn_s)
```
